```python
import math
import jax, jax.numpy as jnp
from jax import lax
import numpy as np

D_MODEL = 1024
BATCH = 8
SEQ = 8192
DEPTH = 4

CHUNK = 64
MIX_WIDTH = D_MODEL
CONV_WIDTH = MIX_WIDTH // 2
CONV_GROUPS = 8
CONV_K = 31
RET_HEADS = 4
RET_DIM = (MIX_WIDTH - CONV_WIDTH) // RET_HEADS
RET_WIDTH = RET_HEADS * RET_DIM
IN_WIDTH = 2 * CONV_WIDTH + 4 * RET_WIDTH
D_FF = int(math.ceil(8 * D_MODEL / 3 / 256) * 256)
ROPE_BASE = 10000.0
EPS = 1e-6

kernel_name = 'hybrid_conformer_retention_encoder'


def rms_norm(x, g):
    xf = x.astype(jnp.float32)
    y = xf * lax.rsqrt(jnp.mean(xf * xf, axis=-1, keepdims=True) + EPS)
    return y.astype(x.dtype) * g


def layer_norm(x, g, b):
    xf = x.astype(jnp.float32)
    mu = jnp.mean(xf, axis=-1, keepdims=True)
    var = jnp.mean(jnp.square(xf - mu), axis=-1, keepdims=True)
    return ((xf - mu) * lax.rsqrt(var + EPS)).astype(x.dtype) * g + b


def causal_depthwise_conv(u, w, b):
    C = u.shape[-1]
    up = jnp.pad(u, ((0, 0), (CONV_K - 1, 0), (0, 0)))
    y = lax.conv_general_dilated(up, w[:, None, :].astype(u.dtype), window_strides=(1,),
                                 padding='VALID', dimension_numbers=('NWC', 'WIO', 'NWC'),
                                 feature_group_count=C)
    return y + b


def rotary(t, pos):
    half = t.shape[-1] // 2
    freqs = ROPE_BASE ** (-jnp.arange(half, dtype=jnp.float32) / half)
    ang = pos[:, None] * freqs[None, :]
    cos = jnp.cos(ang)[None, :, None, :]
    sin = jnp.sin(ang)[None, :, None, :]
    t1, t2 = t[..., :half], t[..., half:]
    return jnp.concatenate([t1 * cos - t2 * sin, t1 * sin + t2 * cos], axis=-1)


def chunk_retention(q, k, v):
    Bsz, S, H, Dk = q.shape
    Dv = v.shape[-1]
    NC = S // CHUNK

    def blk(t):
        return t.reshape(Bsz, NC, CHUNK, H, t.shape[-1]).transpose(0, 3, 1, 2, 4)

    q, k, v = blk(q), blk(k), blk(v)
    log_g = jnp.log(1.0 - 2.0 ** (-5.0 - jnp.arange(H, dtype=jnp.float32)))
    idx = jnp.arange(CHUNK, dtype=jnp.float32)
    intra_decay = jnp.exp(log_g[:, None, None] * jnp.abs(idx[:, None] - idx[None, :]))
    scores = jnp.einsum('bhnid,bhnjd->bhnij', q, k) * intra_decay[None, :, None]
    intra = jnp.einsum('bhnij,bhnje->bhnie', scores, v)

    k_dec = k * jnp.exp(log_g[:, None] * (CHUNK - 1 - idx))[None, :, None, :, None]
    kv = jnp.einsum('bhnjd,bhnje->nbhde', k_dec, v)
    chunk_decay = jnp.exp(log_g * CHUNK)[None, :, None, None]

    def step(state, kv_n):
        return chunk_decay * state + kv_n, state

    _, prev = lax.scan(step, jnp.zeros((Bsz, H, Dk, Dv), jnp.float32), kv)
    q_dec = q * jnp.exp(log_g[:, None] * (idx + 1.0))[None, :, None, :, None]
    cross = jnp.einsum('bhnid,nbhde->bhnie', q_dec, prev)
    return (intra + cross).transpose(0, 2, 3, 1, 4).reshape(Bsz, S, H, Dv)


def _fwd_setup_inputs(seed: int = 0) -> dict:
    key = jax.random.key(seed)
    ks = jax.random.split(key, 16)
    f32 = jnp.float32
    n = lambda k, shape, s: jax.random.normal(k, shape, f32) * s
    return {
        'x': n(ks[0], (BATCH, SEQ, D_MODEL), 1.0),
        'norm1_g': 1.0 + n(ks[1], (DEPTH, D_MODEL), 0.02),
        'w_in': n(ks[2], (DEPTH, D_MODEL, IN_WIDTH), D_MODEL ** -0.5),
        'conv_w': n(ks[3], (DEPTH, CONV_K, CONV_WIDTH), CONV_K ** -0.5),
        'conv_b': n(ks[4], (DEPTH, CONV_WIDTH), 0.01),
        'conv_ln_g': 1.0 + n(ks[5], (DEPTH, CONV_WIDTH), 0.02),
        'conv_ln_b': n(ks[6], (DEPTH, CONV_WIDTH), 0.01),
        'ret_gn_g': 1.0 + n(ks[7], (DEPTH, RET_WIDTH), 0.02),
        'w_out': n(ks[8], (DEPTH, MIX_WIDTH, D_MODEL), (MIX_WIDTH * 2 * DEPTH) ** -0.5),
        'norm2_g': 1.0 + n(ks[9], (DEPTH, D_MODEL), 0.02),
        'w_gate': n(ks[10], (DEPTH, D_MODEL, D_FF), D_MODEL ** -0.5),
        'w_up': n(ks[11], (DEPTH, D_MODEL, D_FF), D_MODEL ** -0.5),
        'w_down': n(ks[12], (DEPTH, D_FF, D_MODEL), (D_FF * 2 * DEPTH) ** -0.5),
        'final_g': 1.0 + n(ks[13], (D_MODEL,), 0.02),
    }


def _fwd_reference(x, norm1_g, w_in, conv_w, conv_b, conv_ln_g, conv_ln_b, ret_gn_g, w_out,
              norm2_g, w_gate, w_up, w_down, final_g):
    Bsz, S, _ = x.shape
    pos = jnp.arange(S, dtype=jnp.float32)
    cw, rw = CONV_WIDTH, RET_WIDTH
    for l in range(DEPTH):
        h = rms_norm(x, norm1_g[l])
        proj = h @ w_in[l]
        a = proj[..., :cw]
        b = proj[..., cw:2 * cw]
        q = proj[..., 2 * cw:2 * cw + rw]
        k = proj[..., 2 * cw + rw:2 * cw + 2 * rw]
        v = proj[..., 2 * cw + 2 * rw:2 * cw + 3 * rw]
        g = proj[..., 2 * cw + 3 * rw:]

        u = a * jax.nn.sigmoid(b)
        u = causal_depthwise_conv(u, conv_w[l], conv_b[l])
        u = jax.nn.silu(layer_norm(u, conv_ln_g[l], conv_ln_b[l]))

        qh = rotary(q.reshape(Bsz, S, RET_HEADS, RET_DIM).astype(jnp.float32), pos)
        kh = rotary(k.reshape(Bsz, S, RET_HEADS, RET_DIM).astype(jnp.float32), pos) * (RET_DIM ** -0.5)
        vh = v.reshape(Bsz, S, RET_HEADS, RET_DIM).astype(jnp.float32)
        r = chunk_retention(qh, kh, vh)
        mu = jnp.mean(r, axis=-1, keepdims=True)
        var = jnp.mean(jnp.square(r - mu), axis=-1, keepdims=True)
        r = ((r - mu) * lax.rsqrt(var + EPS)).reshape(Bsz, S, rw).astype(x.dtype)
        r = r * ret_gn_g[l] * jax.nn.silu(g)

        mixed = jnp.concatenate([u, r], axis=-1)
        x = x + mixed @ w_out[l]

        h2 = rms_norm(x, norm2_g[l])
        x = x + (jax.nn.silu(h2 @ w_gate[l]) * (h2 @ w_up[l])) @ w_down[l]
    return rms_norm(x, final_g)


import jax as _jax
import jax.numpy as _jnp

TWIN_FORMAT = 'train_step'
FWD_PARAMS = ['x', 'norm1_g', 'w_in', 'conv_w', 'conv_b', 'conv_ln_g', 'conv_ln_b', 'ret_gn_g', 'w_out', 'norm2_g', 'w_gate', 'w_up', 'w_down', 'final_g']
TWIN_WEIGHTS = ['norm1_g', 'w_in', 'conv_w', 'conv_b', 'conv_ln_g', 'conv_ln_b', 'ret_gn_g', 'w_out', 'norm2_g', 'w_gate', 'w_up', 'w_down', 'final_g']
TWIN_DIFF_INPUT = 'x'
TWIN_INPUTS = ['x', 'norm1_g', 'w_in', 'conv_w', 'conv_b', 'conv_ln_g', 'conv_ln_b', 'ret_gn_g', 'w_out', 'norm2_g', 'w_gate', 'w_up', 'w_down', 'final_g', 'loss_target', 'm_norm1_g', 'm_w_in', 'm_conv_w', 'm_conv_b', 'm_conv_ln_g', 'm_conv_ln_b', 'm_ret_gn_g', 'm_w_out', 'm_norm2_g', 'm_w_gate', 'm_w_up', 'm_w_down', 'm_final_g', 'v_norm1_g', 'v_w_in', 'v_conv_w', 'v_conv_b', 'v_conv_ln_g', 'v_conv_ln_b', 'v_ret_gn_g', 'v_w_out', 'v_norm2_g', 'v_w_gate', 'v_w_up', 'v_w_down', 'v_final_g']
TWIN_OUTPUTS = ['loss', 'grad_x', 'grad_norm1_g', 'grad_w_in', 'grad_conv_w', 'grad_conv_b', 'grad_conv_ln_g', 'grad_conv_ln_b', 'grad_ret_gn_g', 'grad_w_out', 'grad_norm2_g', 'grad_w_gate', 'grad_w_up', 'grad_w_down', 'grad_final_g', 'delta_norm1_g', 'delta_w_in', 'delta_conv_w', 'delta_conv_b', 'delta_conv_ln_g', 'delta_conv_ln_b', 'delta_ret_gn_g', 'delta_w_out', 'delta_norm2_g', 'delta_w_gate', 'delta_w_up', 'delta_w_down', 'delta_final_g', 'new_m_norm1_g', 'new_m_w_in', 'new_m_conv_w', 'new_m_conv_b', 'new_m_conv_ln_g', 'new_m_conv_ln_b', 'new_m_ret_gn_g', 'new_m_w_out', 'new_m_norm2_g', 'new_m_w_gate', 'new_m_w_up', 'new_m_w_down', 'new_m_final_g', 'new_v_norm1_g', 'new_v_w_in', 'new_v_conv_w', 'new_v_conv_b', 'new_v_conv_ln_g', 'new_v_conv_ln_b', 'new_v_ret_gn_g', 'new_v_w_out', 'new_v_norm2_g', 'new_v_w_gate', 'new_v_w_up', 'new_v_w_down', 'new_v_final_g']
TWIN_LEAF_KINDS = {'loss': 'loss', 'grad_x': 'grad_x', 'grad_norm1_g': 'grad_w', 'grad_w_in': 'grad_w', 'grad_conv_w': 'grad_w', 'grad_conv_b': 'grad_w', 'grad_conv_ln_g': 'grad_w', 'grad_conv_ln_b': 'grad_w', 'grad_ret_gn_g': 'grad_w', 'grad_w_out': 'grad_w', 'grad_norm2_g': 'grad_w', 'grad_w_gate': 'grad_w', 'grad_w_up': 'grad_w', 'grad_w_down': 'grad_w', 'grad_final_g': 'grad_w', 'delta_norm1_g': 'delta_w', 'delta_w_in': 'delta_w', 'delta_conv_w': 'delta_w', 'delta_conv_b': 'delta_w', 'delta_conv_ln_g': 'delta_w', 'delta_conv_ln_b': 'delta_w', 'delta_ret_gn_g': 'delta_w', 'delta_w_out': 'delta_w', 'delta_norm2_g': 'delta_w', 'delta_w_gate': 'delta_w', 'delta_w_up': 'delta_w', 'delta_w_down': 'delta_w', 'delta_final_g': 'delta_w', 'new_m_norm1_g': 'new_m', 'new_m_w_in': 'new_m', 'new_m_conv_w': 'new_m', 'new_m_conv_b': 'new_m', 'new_m_conv_ln_g': 'new_m', 'new_m_conv_ln_b': 'new_m', 'new_m_ret_gn_g': 'new_m', 'new_m_w_out': 'new_m', 'new_m_norm2_g': 'new_m', 'new_m_w_gate': 'new_m', 'new_m_w_up': 'new_m', 'new_m_w_down': 'new_m', 'new_m_final_g': 'new_m', 'new_v_norm1_g': 'new_v', 'new_v_w_in': 'new_v', 'new_v_conv_w': 'new_v', 'new_v_conv_b': 'new_v', 'new_v_conv_ln_g': 'new_v', 'new_v_conv_ln_b': 'new_v', 'new_v_ret_gn_g': 'new_v', 'new_v_w_out': 'new_v', 'new_v_norm2_g': 'new_v', 'new_v_w_gate': 'new_v', 'new_v_w_up': 'new_v', 'new_v_w_down': 'new_v', 'new_v_final_g': 'new_v'}


def _forward(args):
    return _fwd_reference(*[args[k] for k in FWD_PARAMS])


def _output_shape():
    def fwd():
        inp = _fwd_setup_inputs(0)
        return _fwd_reference(*[inp[k] for k in FWD_PARAMS])
    out = _jax.eval_shape(fwd)
    return out.shape, out.dtype

N_MICROBATCH = 1
ADAM_LR = 0.001
ADAM_B1 = 0.9
ADAM_B2 = 0.999
ADAM_EPS = 1e-08
ADAM_WD = 0.01
ADAM_STEP = 10
PER_EXAMPLE_BATCH_AXIS = {'x': 0, 'loss_target': 0}
SHARED_INPUTS = []
_WEIGHT_DTYPES = {'norm1_g': _jnp.float32, 'w_in': _jnp.float32, 'conv_w': _jnp.float32, 'conv_b': _jnp.float32, 'conv_ln_g': _jnp.float32, 'conv_ln_b': _jnp.float32, 'ret_gn_g': _jnp.float32, 'w_out': _jnp.float32, 'norm2_g': _jnp.float32, 'w_gate': _jnp.float32, 'w_up': _jnp.float32, 'w_down': _jnp.float32, 'final_g': _jnp.float32}
MOMENT_SCALE = {'norm1_g': 8.954249e-02, 'w_in': 5.089247e-02, 'conv_w': 5.609157e-02, 'conv_b': 1.176944e-01, 'conv_ln_g': 6.614616e-02, 'conv_ln_b': 6.263035e-02, 'ret_gn_g': 5.418225e-02, 'w_out': 1.536790e-01, 'norm2_g': 7.669208e-02, 'w_gate': 3.224475e-02, 'w_up': 3.122976e-02, 'w_down': 1.465245e-01, 'final_g': 6.402398e+01}


def _to_microbatches(a, axis):
    t = _jnp.moveaxis(a, axis, 0)
    t = t.reshape((N_MICROBATCH, t.shape[0] // N_MICROBATCH) + t.shape[1:])
    return _jnp.moveaxis(t, 1, axis + 1)


def setup_inputs(seed: int = 0) -> dict:
    inp = _fwd_setup_inputs(seed)
    key = _jax.random.fold_in(_jax.random.key(seed), 7919)
    shape, _ = _output_shape()
    out = dict(inp)
    out["loss_target"] = _jax.random.normal(_jax.random.fold_in(key, 0), shape, _jnp.float32)
    for i, name in enumerate(TWIN_WEIGHTS):
        w = inp[name].astype(_jnp.float32)
        if MOMENT_SCALE is None:
            s = _jnp.sqrt(_jnp.mean(_jnp.square(w)) + 1e-30)
        else:
            s = MOMENT_SCALE[name]
        km, kv = _jax.random.split(_jax.random.fold_in(key, i + 1))
        out[name] = w
        out["m_" + name] = s * _jax.random.normal(km, w.shape, _jnp.float32)
        out["v_" + name] = (s * s) * _jax.random.uniform(kv, w.shape, _jnp.float32, 0.5, 1.5)
    if N_MICROBATCH > 1:
        for name, axis in PER_EXAMPLE_BATCH_AXIS.items():
            out[name] = _to_microbatches(out[name], axis)
    return {'x': out['x'], 'norm1_g': out['norm1_g'], 'w_in': out['w_in'], 'conv_w': out['conv_w'], 'conv_b': out['conv_b'], 'conv_ln_g': out['conv_ln_g'], 'conv_ln_b': out['conv_ln_b'], 'ret_gn_g': out['ret_gn_g'], 'w_out': out['w_out'], 'norm2_g': out['norm2_g'], 'w_gate': out['w_gate'], 'w_up': out['w_up'], 'w_down': out['w_down'], 'final_g': out['final_g'], 'loss_target': out['loss_target'], 'm_norm1_g': out['m_norm1_g'], 'm_w_in': out['m_w_in'], 'm_conv_w': out['m_conv_w'], 'm_conv_b': out['m_conv_b'], 'm_conv_ln_g': out['m_conv_ln_g'], 'm_conv_ln_b': out['m_conv_ln_b'], 'm_ret_gn_g': out['m_ret_gn_g'], 'm_w_out': out['m_w_out'], 'm_norm2_g': out['m_norm2_g'], 'm_w_gate': out['m_w_gate'], 'm_w_up': out['m_w_up'], 'm_w_down': out['m_w_down'], 'm_final_g': out['m_final_g'], 'v_norm1_g': out['v_norm1_g'], 'v_w_in': out['v_w_in'], 'v_conv_w': out['v_conv_w'], 'v_conv_b': out['v_conv_b'], 'v_conv_ln_g': out['v_conv_ln_g'], 'v_conv_ln_b': out['v_conv_ln_b'], 'v_ret_gn_g': out['v_ret_gn_g'], 'v_w_out': out['v_w_out'], 'v_norm2_g': out['v_norm2_g'], 'v_w_gate': out['v_w_gate'], 'v_w_up': out['v_w_up'], 'v_w_down': out['v_w_down'], 'v_final_g': out['v_final_g']}


def _loss(weights, diff, rest, loss_target):
    with _jax.named_scope("forward"):
        args = {**rest, TWIN_DIFF_INPUT: diff, **{k: w.astype(_WEIGHT_DTYPES[k]) for k, w in weights.items()}}
        y = _forward(args)
    with _jax.named_scope("loss_head"):
        err = _jnp.square(y.astype(_jnp.float32) - loss_target)
        return 0.5 * _jnp.sum(_jnp.mean(err, axis=-1)) if err.ndim else 0.5 * err


def _adamw(w, g, m, v):
    m = ADAM_B1 * m + (1.0 - ADAM_B1) * g
    v = ADAM_B2 * v + (1.0 - ADAM_B2) * _jnp.square(g)
    m_hat = m / (1.0 - ADAM_B1 ** ADAM_STEP)
    v_hat = v / (1.0 - ADAM_B2 ** ADAM_STEP)
    delta = -ADAM_LR * (m_hat / (_jnp.sqrt(v_hat) + ADAM_EPS) + ADAM_WD * w)
    return delta, m, v


def reference(x, norm1_g, w_in, conv_w, conv_b, conv_ln_g, conv_ln_b, ret_gn_g, w_out, norm2_g, w_gate, w_up, w_down, final_g, loss_target, m_norm1_g, m_w_in, m_conv_w, m_conv_b, m_conv_ln_g, m_conv_ln_b, m_ret_gn_g, m_w_out, m_norm2_g, m_w_gate, m_w_up, m_w_down, m_final_g, v_norm1_g, v_w_in, v_conv_w, v_conv_b, v_conv_ln_g, v_conv_ln_b, v_ret_gn_g, v_w_out, v_norm2_g, v_w_gate, v_w_up, v_w_down, v_final_g):
    given = dict(x=x, norm1_g=norm1_g, w_in=w_in, conv_w=conv_w, conv_b=conv_b, conv_ln_g=conv_ln_g, conv_ln_b=conv_ln_b, ret_gn_g=ret_gn_g, w_out=w_out, norm2_g=norm2_g, w_gate=w_gate, w_up=w_up, w_down=w_down, final_g=final_g, loss_target=loss_target, m_norm1_g=m_norm1_g, m_w_in=m_w_in, m_conv_w=m_conv_w, m_conv_b=m_conv_b, m_conv_ln_g=m_conv_ln_g, m_conv_ln_b=m_conv_ln_b, m_ret_gn_g=m_ret_gn_g, m_w_out=m_w_out, m_norm2_g=m_norm2_g, m_w_gate=m_w_gate, m_w_up=m_w_up, m_w_down=m_w_down, m_final_g=m_final_g, v_norm1_g=v_norm1_g, v_w_in=v_w_in, v_conv_w=v_conv_w, v_conv_b=v_conv_b, v_conv_ln_g=v_conv_ln_g, v_conv_ln_b=v_conv_ln_b, v_ret_gn_g=v_ret_gn_g, v_w_out=v_w_out, v_norm2_g=v_norm2_g, v_w_gate=v_w_gate, v_w_up=v_w_up, v_w_down=v_w_down, v_final_g=v_final_g)
    weights = {n: given[n] for n in TWIN_WEIGHTS}
    shared = {n: given[n] for n in SHARED_INPUTS}
    per_example = {n: given[n] for n in ['x']}
    grad_fn = _jax.value_and_grad(_loss, argnums=(0, 1))

    def one_microbatch(ex, loss_target):
        ex = dict(ex)
        diff = ex.pop(TWIN_DIFF_INPUT)
        return grad_fn(weights, diff, {**shared, **ex}, loss_target)

    if N_MICROBATCH == 1:
        loss, (grad_w, grad_x) = one_microbatch(per_example, given["loss_target"])
    else:
        def body(carry, xs):
            loss_sum, grad_sum = carry
            l_k, (gw_k, gx_k) = one_microbatch(xs[0], xs[1])
            with _jax.named_scope("update"):
                return (loss_sum + l_k, _jax.tree.map(_jnp.add, grad_sum, gw_k)), gx_k

        init = (_jnp.zeros((), _jnp.float32), _jax.tree.map(_jnp.zeros_like, weights))
        (loss, grad_w), grad_x = _jax.lax.scan(body, init, (per_example, given["loss_target"]))
    with _jax.named_scope("update"):
        delta_w, new_m, new_v = {}, {}, {}
        for n in TWIN_WEIGHTS:
            delta_w[n], new_m[n], new_v[n] = _adamw(weights[n], grad_w[n], given["m_" + n], given["v_" + n])
    return (loss, grad_x, *[grad_w[n] for n in TWIN_WEIGHTS], *[delta_w[n] for n in TWIN_WEIGHTS],
            *[new_m[n] for n in TWIN_WEIGHTS], *[new_v[n] for n in TWIN_WEIGHTS])
```

```python
import math

import jax
import jax.numpy as jnp
from jax import lax
from jax.experimental import pallas as pl
from jax.experimental.pallas import tpu as pltpu

D = 1024
L = 4
CW = 512
RW = 512
NH = 4
HD = 128
CK = 31
CHUNK = 64
INW = 3072
FF = 2816
NCHIP = 4
EPS = 1e-6
ROPE_BASE = 10000.0
SEQ_TILE = 512
MLP_ROWS = 1024
MLP_COLS = 256
HALO = 32
CONV_ROWS = 32

ADAM_LR = 0.001
ADAM_B1 = 0.9
ADAM_B2 = 0.999
ADAM_EPS = 1e-08
ADAM_WD = 0.01
ADAM_STEP = 10

BF = jnp.bfloat16
F32 = jnp.float32
MESH = pl.DeviceIdType.MESH
ANY = pl.BlockSpec(memory_space=pl.ANY)
VMEM_SPEC = pl.BlockSpec(memory_space=pltpu.VMEM)


def _params(n_grid, vmem_mb):
    return pltpu.CompilerParams(dimension_semantics=("arbitrary",) * n_grid,
                                vmem_limit_bytes=vmem_mb << 20)


def _dot(a, b):
    return jnp.dot(a, b, preferred_element_type=F32)


def _dot_nt(a, b):
    return lax.dot_general(a, b, (((1,), (1,)), ((), ())), preferred_element_type=F32)


def _dot_tn(a, b):
    return lax.dot_general(a, b, (((0,), (0,)), ((), ())), preferred_element_type=F32)


def _sigmoid(x):
    return 1.0 / (1.0 + jnp.exp(-x))


def _mean(x):
    return jnp.mean(x, axis=-1, keepdims=True)


def _rot(t, cs, sn):
    return t * cs + pltpu.roll(t, HD // 2, 1) * sn


def _rot_t(dy, cs, sn):
    return dy * cs + pltpu.roll(dy * sn, HD // 2, 1)


def _rms_bwd(x, g, dh, dx_in):
    r = lax.rsqrt(_mean(x * x) + EPS)
    xh = x * r
    dxh = dh * g
    dx = dx_in + r * (dxh - xh * _mean(dxh * xh))
    return dx, jnp.sum(dh * xh, axis=0, keepdims=True), (xh * g).astype(BF)


def in_proj(x, g, win, l):
    S = x.shape[0]
    tm, tn = min(S, SEQ_TILE), 768

    def body(x_ref, g_ref, w_ref, o_ref, h_scr):
        @pl.when(pl.program_id(1) == 0)
        def _():
            xv = x_ref[...]
            h_scr[...] = (xv * lax.rsqrt(_mean(xv * xv) + EPS) * g_ref[...]).astype(BF)
        o_ref[...] = _dot(h_scr[...], w_ref[...])

    return pl.pallas_call(
        body, name=f"in_proj_{l}", grid=(S // tm, INW // tn),
        in_specs=[pl.BlockSpec((tm, D), lambda i, j: (i, 0)),
                  pl.BlockSpec((1, D), lambda i, j: (0, 0)),
                  pl.BlockSpec((None, D, tn), lambda i, j: (l, 0, j))],
        out_specs=pl.BlockSpec((tm, tn), lambda i, j: (i, j)),
        out_shape=jax.ShapeDtypeStruct((S, INW), F32),
        scratch_shapes=[pltpu.VMEM((tm, D), BF)],
        compiler_params=_params(2, 40),
    )(x, g, win)


def conv_fwd(proj, cw, cb, lg, lb, l):
    S = proj.shape[0]
    tc = min(S, SEQ_TILE)

    def body(a_ref, b_ref, w_ref, cb_ref, lg_ref, lb_ref, u1_ref, u_ref, buf):
        i = pl.program_id(0)

        @pl.when(i == 0)
        def _():
            buf[0:HALO, :] = jnp.zeros((HALO, CW), F32)

        @pl.when(i > 0)
        def _():
            buf[0:HALO, :] = buf[tc:tc + HALO, :]

        buf[HALO:HALO + tc, :] = a_ref[...] * _sigmoid(b_ref[...])
        off = HALO - (CK - 1)
        for r0 in range(0, tc, CONV_ROWS):
            acc = jnp.broadcast_to(cb_ref[...], (CONV_ROWS, CW))
            for k in range(CK):
                acc = acc + w_ref[k:k + 1, :] * buf[r0 + off + k:r0 + off + k + CONV_ROWS, :]
            u1_ref[r0:r0 + CONV_ROWS, :] = acc
            d = acc - _mean(acc)
            u2 = d * lax.rsqrt(_mean(d * d) + EPS) * lg_ref[...] + lb_ref[...]
            u_ref[r0:r0 + CONV_ROWS, :] = (u2 * _sigmoid(u2)).astype(BF)

    vec = pl.BlockSpec((1, CW), lambda i: (0, 0))
    return pl.pallas_call(
        body, name=f"conv_fwd_{l}", grid=(S // tc,),
        in_specs=[pl.BlockSpec((tc, CW), lambda i: (i, 0)),
                  pl.BlockSpec((tc, CW), lambda i: (i, 1)),
                  pl.BlockSpec((CK, CW), lambda i: (0, 0)), vec, vec, vec],
        out_specs=[pl.BlockSpec((tc, CW), lambda i: (i, 0)),
                   pl.BlockSpec((tc, CW), lambda i: (i, 0))],
        out_shape=[jax.ShapeDtypeStruct((S, CW), F32), jax.ShapeDtypeStruct((S, CW), BF)],
        scratch_shapes=[pltpu.VMEM((tc + HALO, CW), F32)],
        compiler_params=_params(1, 32),
    )(proj, proj, cw, cb, lg, lb)


def _ret_tables(S):
    half = HD // 2
    pos = jnp.arange(S, dtype=F32)
    freqs = ROPE_BASE ** (-jnp.arange(half, dtype=F32) / half)
    ang = pos[:, None] * freqs[None, :]
    cos, sin = jnp.cos(ang), jnp.sin(ang)
    cosf = jnp.concatenate([cos, cos], axis=-1)
    sinf = jnp.concatenate([-sin, sin], axis=-1)
    log_g = jnp.log(1.0 - 2.0 ** (-5.0 - jnp.arange(NH, dtype=F32)))
    idx = jnp.arange(CHUNK, dtype=F32)
    dmat = jnp.exp(log_g[:, None, None] * jnp.abs(idx[:, None] - idx[None, :]))
    qdec = jnp.broadcast_to(jnp.exp(log_g[:, None] * (idx + 1.0))[:, :, None], (NH, CHUNK, HD))
    kdec = jnp.broadcast_to(jnp.exp(log_g[:, None] * (CHUNK - 1 - idx))[:, :, None], (NH, CHUNK, HD))
    cdec = jnp.broadcast_to(jnp.exp(log_g * CHUNK)[:, None, None], (NH, HD, HD))
    return cosf, sinf, dmat, qdec, kdec, cdec


def _ret_specs(tr, tmap):
    q0, k0, v0 = (2 * CW) // HD, (2 * CW + RW) // HD, (2 * CW + 2 * RW) // HD
    return [pl.BlockSpec((tr, HD), lambda h, t: (tmap(t), q0 + h)),
            pl.BlockSpec((tr, HD), lambda h, t: (tmap(t), k0 + h)),
            pl.BlockSpec((tr, HD), lambda h, t: (tmap(t), v0 + h)),
            pl.BlockSpec((tr, HD), lambda h, t: (tmap(t), 0)),
            pl.BlockSpec((tr, HD), lambda h, t: (tmap(t), 0)),
            pl.BlockSpec((None, CHUNK, CHUNK), lambda h, t: (h, 0, 0)),
            pl.BlockSpec((None, CHUNK, HD), lambda h, t: (h, 0, 0)),
            pl.BlockSpec((None, CHUNK, HD), lambda h, t: (h, 0, 0)),
            pl.BlockSpec((None, HD, HD), lambda h, t: (h, 0, 0))]


def ret_fwd(proj, tables, l):
    S = proj.shape[0]
    tr = min(S, SEQ_TILE)
    cpb = tr // CHUNK
    scale = HD ** -0.5

    def body(q_ref, k_ref, v_ref, cos_ref, sin_ref, dm_ref, qd_ref, kd_ref, cd_ref,
             r_ref, st_ref, st):
        @pl.when(pl.program_id(1) == 0)
        def _():
            st[...] = jnp.zeros((HD, HD), F32)

        for c in range(cpb):
            rows = slice(c * CHUNK, (c + 1) * CHUNK)
            cs, sn = cos_ref[rows, :], sin_ref[rows, :]
            qr = _rot(q_ref[rows, :], cs, sn)
            kr = _rot(k_ref[rows, :], cs, sn) * scale
            vb = v_ref[rows, :].astype(BF)
            s = st[...]
            st_ref[c] = s
            sc = _dot_nt(qr.astype(BF), kr.astype(BF)) * dm_ref[...]
            out = _dot(sc.astype(BF), vb) + _dot((qr * qd_ref[...]).astype(BF), s.astype(BF))
            r_ref[rows, :] = out
            st[...] = cd_ref[...] * s + _dot_tn((kr * kd_ref[...]).astype(BF), vb)

    return pl.pallas_call(
        body, name=f"ret_fwd_{l}", grid=(NH, S // tr),
        in_specs=_ret_specs(tr, lambda t: t),
        out_specs=[pl.BlockSpec((tr, HD), lambda h, t: (t, h)),
                   pl.BlockSpec((None, cpb, HD, HD), lambda h, t: (h, t, 0, 0))],
        out_shape=[jax.ShapeDtypeStruct((S, RW), F32),
                   jax.ShapeDtypeStruct((NH, S // CHUNK, HD, HD), F32)],
        scratch_shapes=[pltpu.VMEM((HD, HD), F32)],
        compiler_params=_params(2, 32),
    )(proj, proj, proj, *tables)


def out_proj(u, r_raw, proj, gn, wout, x, l):
    S = x.shape[0]
    tm = min(S, SEQ_TILE)
    gate_blk = (2 * CW + 3 * RW) // RW

    def body(u_ref, r_ref, gate_ref, gn_ref, w_ref, x_ref, x2_ref, mix_ref):
        mix_ref[:, 0:CW] = u_ref[...]
        gt = gate_ref[...]
        sil = gt * _sigmoid(gt) * gn_ref[...]
        for h in range(NH):
            cols = slice(h * HD, (h + 1) * HD)
            rh = r_ref[:, cols]
            d = rh - _mean(rh)
            rn = d * lax.rsqrt(_mean(d * d) + EPS)
            mix_ref[:, CW + h * HD:CW + (h + 1) * HD] = (rn * sil[:, cols]).astype(BF)
        x2_ref[...] = x_ref[...] + _dot(mix_ref[...], w_ref[...])

    return pl.pallas_call(
        body, name=f"out_proj_{l}", grid=(S // tm,),
        in_specs=[pl.BlockSpec((tm, CW), lambda i: (i, 0)),
                  pl.BlockSpec((tm, RW), lambda i: (i, 0)),
                  pl.BlockSpec((tm, RW), lambda i: (i, gate_blk)),
                  pl.BlockSpec((1, RW), lambda i: (0, 0)),
                  pl.BlockSpec((None, D, D), lambda i: (l, 0, 0)),
                  pl.BlockSpec((tm, D), lambda i: (i, 0))],
        out_specs=[pl.BlockSpec((tm, D), lambda i: (i, 0)),
                   pl.BlockSpec((tm, D), lambda i: (i, 0))],
        out_shape=[jax.ShapeDtypeStruct((S, D), F32), jax.ShapeDtypeStruct((S, D), BF)],
        compiler_params=_params(1, 40),
    )(u, r_raw, proj, gn, wout, x)


def mlp_fwd(x2, g2, wgt, wut, wd, l):
    S = x2.shape[0]
    tm, tf = min(S, MLP_ROWS), MLP_COLS
    nk = FF // tf

    def body(x_ref, g_ref, wg_ref, wu_ref, wd_ref, o_ref, gs_ref, us_ref, h_scr, acc):
        k = pl.program_id(1)

        @pl.when(k == 0)
        def _():
            xv = x_ref[...]
            h_scr[...] = (xv * lax.rsqrt(_mean(xv * xv) + EPS) * g_ref[...]).astype(BF)
            acc[...] = jnp.zeros((tm, D), F32)

        gv = _dot_nt(h_scr[...], wg_ref[...])
        uv = _dot_nt(h_scr[...], wu_ref[...])
        gs_ref[...] = gv.astype(BF)
        us_ref[...] = uv.astype(BF)
        acc[...] += _dot((gv * _sigmoid(gv) * uv).astype(BF), wd_ref[...])

        @pl.when(k == nk - 1)
        def _():
            o_ref[...] = x_ref[...] + acc[...]

    wspec = pl.BlockSpec((None, tf, D), lambda i, k: (l, k, 0))
    return pl.pallas_call(
        body, name=f"mlp_fwd_{l}", grid=(S // tm, nk),
        in_specs=[pl.BlockSpec((tm, D), lambda i, k: (i, 0)),
                  pl.BlockSpec((1, D), lambda i, k: (0, 0)), wspec, wspec, wspec],
        out_specs=[pl.BlockSpec((tm, D), lambda i, k: (i, 0)),
                   pl.BlockSpec((tm, tf), lambda i, k: (i, k)),
                   pl.BlockSpec((tm, tf), lambda i, k: (i, k))],
        out_shape=[jax.ShapeDtypeStruct((S, D), F32), jax.ShapeDtypeStruct((S, FF), BF),
                   jax.ShapeDtypeStruct((S, FF), BF)],
        scratch_shapes=[pltpu.VMEM((tm, D), BF), pltpu.VMEM((tm, D), F32)],
        compiler_params=_params(2, 56),
    )(x2, g2, wgt, wut, wd)


def final_loss(x, gf, tgt):
    S = x.shape[0]
    tm = min(S, SEQ_TILE)

    def body(x_ref, g_ref, t_ref, dx_ref, loss_ref, dg_ref):
        @pl.when(pl.program_id(0) == 0)
        def _():
            loss_ref[...] = jnp.zeros((8, 128), F32)
            dg_ref[...] = jnp.zeros((1, D), F32)

        xv = x_ref[...]
        r = lax.rsqrt(_mean(xv * xv) + EPS)
        xh = xv * r
        diff = xh * g_ref[...] - t_ref[...]
        loss_ref[...] += jnp.sum(jnp.sum(diff * diff, axis=-1, keepdims=True), axis=0, keepdims=True)
        dy = diff * (1.0 / D)
        dg_ref[...] += jnp.sum(dy * xh, axis=0, keepdims=True)
        dxh = dy * g_ref[...]
        dx_ref[...] = r * (dxh - xh * _mean(dxh * xh))

    return pl.pallas_call(
        body, name="final_loss", grid=(S // tm,),
        in_specs=[pl.BlockSpec((tm, D), lambda i: (i, 0)),
                  pl.BlockSpec((1, D), lambda i: (0, 0)),
                  pl.BlockSpec((tm, D), lambda i: (i, 0))],
        out_specs=[pl.BlockSpec((tm, D), lambda i: (i, 0)),
                   pl.BlockSpec((8, 128), lambda i: (0, 0)),
                   pl.BlockSpec((1, D), lambda i: (0, 0))],
        out_shape=[jax.ShapeDtypeStruct((S, D), F32), jax.ShapeDtypeStruct((8, 128), F32),
                   jax.ShapeDtypeStruct((1, D), F32)],
        compiler_params=_params(1, 40),
    )(x, gf, tgt)


def mlp_bwd(dx3, x2, g2, gs, us, wgt, wut, wd, l):
    S = x2.shape[0]
    tm, tf = min(S, MLP_ROWS), MLP_COLS
    nk = FF // tf

    def body(dx_ref, x_ref, g_ref, gs_ref, us_ref, wg_ref, wu_ref, wd_ref,
             dx2_ref, dg_ref, du_ref, a_ref, h_ref, dgain_ref, dxb, acc):
        i, k = pl.program_id(0), pl.program_id(1)

        @pl.when(k == 0)
        def _():
            dxb[...] = dx_ref[...].astype(BF)
            acc[...] = jnp.zeros((tm, D), F32)

        @pl.when((k == 0) & (i == 0))
        def _():
            dgain_ref[...] = jnp.zeros((1, D), F32)

        da = _dot_nt(dxb[...], wd_ref[...])
        gv = gs_ref[...].astype(F32)
        uv = us_ref[...].astype(F32)
        sg = _sigmoid(gv)
        sil = gv * sg
        dgv = (da * uv * (sg * (1.0 + gv * (1.0 - sg)))).astype(BF)
        duv = (da * sil).astype(BF)
        a_ref[...] = (sil * uv).astype(BF)
        dg_ref[...] = dgv
        du_ref[...] = duv
        acc[...] += _dot(dgv, wg_ref[...]) + _dot(duv, wu_ref[...])

        @pl.when(k == nk - 1)
        def _():
            dx2, dgain, hb = _rms_bwd(x_ref[...], g_ref[...], acc[...], dx_ref[...])
            dx2_ref[...] = dx2
            dgain_ref[...] += dgain
            h_ref[...] = hb

    wspec = pl.BlockSpec((None, tf, D), lambda i, k: (l, k, 0))
    row = pl.BlockSpec((tm, D), lambda i, k: (i, 0))
    wide = pl.BlockSpec((tm, tf), lambda i, k: (i, k))
    return pl.pallas_call(
        body, name=f"mlp_bwd_{l}", grid=(S // tm, nk),
        in_specs=[row, row, pl.BlockSpec((1, D), lambda i, k: (0, 0)), wide, wide, wspec, wspec, wspec],
        out_specs=[row, wide, wide, wide, row, pl.BlockSpec((1, D), lambda i, k: (0, 0))],
        out_shape=[jax.ShapeDtypeStruct((S, D), F32), jax.ShapeDtypeStruct((S, FF), BF),
                   jax.ShapeDtypeStruct((S, FF), BF), jax.ShapeDtypeStruct((S, FF), BF),
                   jax.ShapeDtypeStruct((S, D), BF), jax.ShapeDtypeStruct((1, D), F32)],
        scratch_shapes=[pltpu.VMEM((tm, D), BF), pltpu.VMEM((tm, D), F32)],
        compiler_params=_params(2, 56),
    )(dx3, x2, g2, gs, us, wgt, wut, wd)


def wgrad(a, b, stack, l, name):
    S, K = a.shape
    N = b.shape[1]
    tk = 1408 if K == FF else min(K, 1024)
    tn = 768 if N == INW else min(N, 1024)
    ts = min(S, SEQ_TILE)
    ns = S // ts

    def body(a_ref, b_ref, stack_ref, o_ref, acc):
        s = pl.program_id(2)

        @pl.when(s == 0)
        def _():
            acc[...] = jnp.zeros((tk, tn), F32)

        acc[...] += _dot_tn(a_ref[...], b_ref[...].astype(BF))

        @pl.when(s == ns - 1)
        def _():
            o_ref[...] = acc[...].astype(BF)

    return pl.pallas_call(
        body, name=f"{name}_{l}", grid=(K // tk, N // tn, ns),
        in_specs=[pl.BlockSpec((ts, tk), lambda i, j, s: (s, i)),
                  pl.BlockSpec((ts, tn), lambda i, j, s: (s, j)), ANY],
        out_specs=pl.BlockSpec((None, tk, tn), lambda i, j, s: (l, i, j)),
        out_shape=jax.ShapeDtypeStruct(stack.shape, BF),
        scratch_shapes=[pltpu.VMEM((tk, tn), F32)],
        input_output_aliases={2: 0},
        compiler_params=_params(3, 48),
    )(a, b, stack)


def out_proj_bwd(dx2, wout, r_raw, proj, gn, u1, lg, lb, l):
    S = dx2.shape[0]
    tm = min(S, SEQ_TILE)
    gate_blk = (2 * CW + 3 * RW) // RW

    def body(dx_ref, w_ref, r_ref, gate_ref, gn_ref, u1_ref, lg_ref, lb_ref,
             dgate_ref, dr_ref, du1_ref, sums_ref):
        @pl.when(pl.program_id(0) == 0)
        def _():
            sums_ref[...] = jnp.zeros((8, CW), F32)

        dmix = _dot_nt(dx_ref[...].astype(BF), w_ref[...])
        gt = gate_ref[...]
        sg = _sigmoid(gt)
        sil = gt * sg
        dsil = sg * (1.0 + gt * (1.0 - sg))
        for h in range(NH):
            cols = slice(h * HD, (h + 1) * HD)
            rh = r_ref[:, cols]
            d = rh - _mean(rh)
            rs = lax.rsqrt(_mean(d * d) + EPS)
            rn = d * rs
            drr = dmix[:, CW + h * HD:CW + (h + 1) * HD]
            gnh = gn_ref[:, cols]
            sums_ref[0:1, cols] += jnp.sum(drr * rn * sil[:, cols], axis=0, keepdims=True)
            dgate_ref[:, cols] = (drr * rn * gnh * dsil[:, cols]).astype(BF)
            drn = drr * gnh * sil[:, cols]
            dr_ref[:, cols] = rs * (drn - _mean(drn) - rn * _mean(drn * rn))
        du = dmix[:, 0:CW]
        u1 = u1_ref[...]
        d = u1 - _mean(u1)
        rs = lax.rsqrt(_mean(d * d) + EPS)
        xh = d * rs
        u2 = xh * lg_ref[...] + lb_ref[...]
        sg2 = _sigmoid(u2)
        du2 = du * (sg2 * (1.0 + u2 * (1.0 - sg2)))
        sums_ref[1:2, :] += jnp.sum(du2 * xh, axis=0, keepdims=True)
        sums_ref[2:3, :] += jnp.sum(du2, axis=0, keepdims=True)
        dxh = du2 * lg_ref[...]
        du1_ref[...] = rs * (dxh - _mean(dxh) - xh * _mean(dxh * xh))

    vec = pl.BlockSpec((1, CW), lambda i: (0, 0))
    half = pl.BlockSpec((tm, CW), lambda i: (i, 0))
    return pl.pallas_call(
        body, name=f"out_proj_bwd_{l}", grid=(S // tm,),
        in_specs=[pl.BlockSpec((tm, D), lambda i: (i, 0)),
                  pl.BlockSpec((None, D, D), lambda i: (l, 0, 0)),
                  half, pl.BlockSpec((tm, RW), lambda i: (i, gate_blk)), vec, half, vec, vec],
        out_specs=[half, half, half, pl.BlockSpec((8, CW), lambda i: (0, 0))],
        out_shape=[jax.ShapeDtypeStruct((S, RW), BF), jax.ShapeDtypeStruct((S, RW), F32),
                   jax.ShapeDtypeStruct((S, CW), F32), jax.ShapeDtypeStruct((8, CW), F32)],
        compiler_params=_params(1, 40),
    )(dx2, wout, r_raw, proj, gn, u1, lg, lb)


def conv_bwd(du1, proj, cw, l):
    S = proj.shape[0]
    tc = min(S, SEQ_TILE)
    nt = S // tc

    def body(du1_ref, a_ref, b_ref, w_ref, dab_ref, dwb_ref, buf):
        i = pl.program_id(0)

        @pl.when(i == 0)
        def _():
            buf[tc:tc + HALO, :] = jnp.zeros((HALO, CW), F32)
            dwb_ref[...] = jnp.zeros((CK + 1, CW), F32)

        @pl.when(i > 0)
        def _():
            buf[tc:tc + HALO, :] = buf[0:HALO, :]

        buf[0:tc, :] = du1_ref[...]
        parts = [jnp.zeros((8, CW), F32) for _ in range(CK)]
        for r0 in range(0, tc, CONV_ROWS):
            av = a_ref[r0:r0 + CONV_ROWS, :]
            sgb = _sigmoid(b_ref[r0:r0 + CONV_ROWS, :])
            u0 = av * sgb
            acc = jnp.zeros((CONV_ROWS, CW), F32)
            for j in range(CK):
                sl = buf[r0 + j:r0 + j + CONV_ROWS, :]
                acc = acc + w_ref[CK - 1 - j:CK - j, :] * sl
                pr = u0 * sl
                red = pr[0:8, :]
                for q in range(1, CONV_ROWS // 8):
                    red = red + pr[8 * q:8 * q + 8, :]
                parts[CK - 1 - j] = parts[CK - 1 - j] + red
            dab_ref[r0:r0 + CONV_ROWS, 0:CW] = (acc * sgb).astype(BF)
            dab_ref[r0:r0 + CONV_ROWS, CW:2 * CW] = (acc * av * sgb * (1.0 - sgb)).astype(BF)
        for k in range(CK):
            dwb_ref[k:k + 1, :] += jnp.sum(parts[k], axis=0, keepdims=True)
        dwb_ref[CK:CK + 1, :] += jnp.sum(du1_ref[...], axis=0, keepdims=True)

    return pl.pallas_call(
        body, name=f"conv_bwd_{l}", grid=(nt,),
        in_specs=[pl.BlockSpec((tc, CW), lambda i: (nt - 1 - i, 0)),
                  pl.BlockSpec((tc, CW), lambda i: (nt - 1 - i, 0)),
                  pl.BlockSpec((tc, CW), lambda i: (nt - 1 - i, 1)),
                  pl.BlockSpec((CK, CW), lambda i: (0, 0))],
        out_specs=[pl.BlockSpec((tc, 2 * CW), lambda i: (nt - 1 - i, 0)),
                   pl.BlockSpec((CK + 1, CW), lambda i: (0, 0))],
        out_shape=[jax.ShapeDtypeStruct((S, 2 * CW), BF), jax.ShapeDtypeStruct((CK + 1, CW), F32)],
        scratch_shapes=[pltpu.VMEM((tc + HALO, CW), F32)],
        compiler_params=_params(1, 32),
    )(du1, proj, proj, cw)


def ret_bwd(dr, proj, states, tables, l):
    S = proj.shape[0]
    tr = min(S, SEQ_TILE)
    cpb = tr // CHUNK
    nt = S // tr
    scale = HD ** -0.5

    def body(q_ref, k_ref, v_ref, cos_ref, sin_ref, dm_ref, qd_ref, kd_ref, cd_ref, dr_ref, st_ref,
             dq_ref, dk_ref, dv_ref, gst):
        @pl.when(pl.program_id(1) == 0)
        def _():
            gst[...] = jnp.zeros((HD, HD), F32)

        for c in reversed(range(cpb)):
            rows = slice(c * CHUNK, (c + 1) * CHUNK)
            cs, sn = cos_ref[rows, :], sin_ref[rows, :]
            qr = _rot(q_ref[rows, :], cs, sn)
            kr = _rot(k_ref[rows, :], cs, sn) * scale
            qb, kb = qr.astype(BF), kr.astype(BF)
            vb = v_ref[rows, :].astype(BF)
            dob = dr_ref[rows, :].astype(BF)
            sb = st_ref[c].astype(BF)
            gn1 = gst[...]
            gb = gn1.astype(BF)
            sc = (_dot_nt(qb, kb) * dm_ref[...]).astype(BF)
            dsc = (_dot_nt(dob, vb) * dm_ref[...]).astype(BF)
            dqr = _dot(dsc, kb) + _dot_nt(dob, sb) * qd_ref[...]
            dkr = _dot_tn(dsc, qb) + _dot_nt(vb, gb) * kd_ref[...]
            dvv = _dot_tn(sc, dob) + _dot((kr * kd_ref[...]).astype(BF), gb)
            gst[...] = cd_ref[...] * gn1 + _dot_tn((qr * qd_ref[...]).astype(BF), dob)
            dq_ref[rows, :] = _rot_t(dqr, cs, sn).astype(BF)
            dk_ref[rows, :] = _rot_t(dkr * scale, cs, sn).astype(BF)
            dv_ref[rows, :] = dvv.astype(BF)

    rev = lambda t: nt - 1 - t
    hblk = pl.BlockSpec((tr, HD), lambda h, t: (rev(t), h))
    return pl.pallas_call(
        body, name=f"ret_bwd_{l}", grid=(NH, nt),
        in_specs=_ret_specs(tr, rev) + [hblk, pl.BlockSpec((None, cpb, HD, HD), lambda h, t: (h, rev(t), 0, 0))],
        out_specs=[hblk, hblk, hblk],
        out_shape=[jax.ShapeDtypeStruct((S, RW), BF)] * 3,
        scratch_shapes=[pltpu.VMEM((HD, HD), F32)],
        compiler_params=_params(2, 32),
    )(proj, proj, proj, *tables, dr, states)


def in_proj_bwd(dproj, win, x, g, dx2, l):
    S = x.shape[0]
    tm, tk = min(S, SEQ_TILE), 768
    nk = INW // tk

    def body(dp_ref, w_ref, x_ref, g_ref, dx2_ref, dx_ref, h_ref, dgain_ref, acc):
        i, k = pl.program_id(0), pl.program_id(1)

        @pl.when(k == 0)
        def _():
            acc[...] = jnp.zeros((tm, D), F32)

        @pl.when((k == 0) & (i == 0))
        def _():
            dgain_ref[...] = jnp.zeros((1, D), F32)

        acc[...] += _dot_nt(dp_ref[...], w_ref[...])

        @pl.when(k == nk - 1)
        def _():
            dx, dgain, hb = _rms_bwd(x_ref[...], g_ref[...], acc[...], dx2_ref[...])
            dx_ref[...] = dx
            dgain_ref[...] += dgain
            h_ref[...] = hb

    row = pl.BlockSpec((tm, D), lambda i, k: (i, 0))
    vec = pl.BlockSpec((1, D), lambda i, k: (0, 0))
    return pl.pallas_call(
        body, name=f"in_proj_bwd_{l}", grid=(S // tm, nk),
        in_specs=[pl.BlockSpec((tm, tk), lambda i, k: (i, k)),
                  pl.BlockSpec((None, D, tk), lambda i, k: (l, 0, k)), row, vec, row],
        out_specs=[row, row, vec],
        out_shape=[jax.ShapeDtypeStruct((S, D), F32), jax.ShapeDtypeStruct((S, D), BF),
                   jax.ShapeDtypeStruct((1, D), F32)],
        scratch_shapes=[pltpu.VMEM((tm, D), F32)],
        compiler_params=_params(2, 40),
    )(dproj, win, x, g, dx2)


def sum_slots(recv, name):
    _, R, C = recv.shape
    tr = 256 if R % 256 == 0 else R

    def body(r_ref, o_ref):
        acc = r_ref[0].astype(F32)
        for k in range(1, NCHIP):
            acc = acc + r_ref[k].astype(F32)
        o_ref[...] = acc

    return pl.pallas_call(
        body, name=name, grid=(R // tr,),
        in_specs=[pl.BlockSpec((NCHIP, tr, C), lambda i: (0, i, 0))],
        out_specs=pl.BlockSpec((tr, C), lambda i: (i, 0)),
        out_shape=jax.ShapeDtypeStruct((R, C), F32),
        compiler_params=_params(1, 32),
    )(recv)


def adamw(w, ga, gb, m, v, name):
    R, C = w.shape
    tr = 256 if R % 256 == 0 else R
    c1 = 1.0 - ADAM_B1 ** ADAM_STEP
    c2 = 1.0 - ADAM_B2 ** ADAM_STEP

    def body(w_ref, ga_ref, gb_ref, m_ref, v_ref, g_out, d_out, m_out, v_out):
        g = ga_ref[...] + gb_ref[...]
        mn = ADAM_B1 * m_ref[...] + (1.0 - ADAM_B1) * g
        vn = ADAM_B2 * v_ref[...] + (1.0 - ADAM_B2) * (g * g)
        g_out[...] = g
        m_out[...] = mn
        v_out[...] = vn
        d_out[...] = -ADAM_LR * ((mn / c1) / (jnp.sqrt(vn / c2) + ADAM_EPS) + ADAM_WD * w_ref[...])

    blk = pl.BlockSpec((tr, C), lambda i: (i, 0))
    return pl.pallas_call(
        body, name=name, grid=(R // tr,),
        in_specs=[blk] * 5, out_specs=[blk] * 4,
        out_shape=[jax.ShapeDtypeStruct((R, C), F32)] * 4,
        compiler_params=_params(1, 40),
    )(w, ga, gb, m, v)


def _place():
    x, y, c = lax.axis_index("x"), lax.axis_index("y"), lax.axis_index("c")
    chips = [(1 - x, y), (x, 1 - y), (1 - x, 1 - y)]
    return x, y, c, chips


def _window(ref, axis, j, size):
    idx = [slice(None)] * len(ref.shape)
    idx[axis] = pl.ds(pl.multiple_of(j * size, 128 if axis == len(ref.shape) - 1 else 16), size)
    return ref.at[tuple(idx)]


def gather_weights(shards, axes):
    n = len(shards)
    out_shape = []
    for s, ax in zip(shards, axes):
        shp = list(s.shape)
        shp[ax] *= NCHIP
        out_shape.append(jax.ShapeDtypeStruct(tuple(shp), s.dtype))

    def body(*refs):
        ins, outs = refs[:n], refs[n:2 * n]
        send, recv, loc = refs[2 * n:]
        x, y, c, chips = _place()

        def spot(a, px, py):
            return _window(outs[a], axes[a], 2 * px + py, ins[a].shape[axes[a]])

        def remote(a, k, chip, dst):
            return pltpu.make_async_remote_copy(
                src_ref=ins[a], dst_ref=dst, send_sem=send.at[3 * a + k], recv_sem=recv.at[3 * a + k],
                device_id=(chip[0], chip[1], c), device_id_type=MESH)

        mine = [pltpu.make_async_copy(ins[a], spot(a, x, y), loc.at[a]) for a in range(n)]
        for cp in mine:
            cp.start()
        sends = [remote(a, k, chip, spot(a, x, y)) for a in range(n) for k, chip in enumerate(chips)]
        for cp in sends:
            cp.start()
        for a in range(n):
            for k, chip in enumerate(chips):
                remote(a, k, chip, spot(a, *chip)).wait_recv()
        for cp in sends:
            cp.wait_send()
        for cp in mine:
            cp.wait()

    return pl.pallas_call(
        body, name="gather_weights", in_specs=[ANY] * n, out_specs=[ANY] * n, out_shape=out_shape,
        scratch_shapes=[pltpu.SemaphoreType.DMA((3 * n,)), pltpu.SemaphoreType.DMA((3 * n,)),
                        pltpu.SemaphoreType.DMA((n,))],
    )(*shards)


def scatter_grads(grads, axes, sizes):
    n = len(grads)
    out_shape = []
    for g, ax, sz in zip(grads, axes, sizes):
        shp = list(g.shape)
        shp[ax] = sz
        out_shape.append(jax.ShapeDtypeStruct((NCHIP,) + tuple(shp), g.dtype))

    def body(*refs):
        ins, outs = refs[:n], refs[n:2 * n]
        send, recv, loc = refs[2 * n:]
        x, y, c, chips = _place()
        me = 2 * x + y

        def remote(a, k, chip, slot):
            return pltpu.make_async_remote_copy(
                src_ref=_window(ins[a], axes[a], 2 * chip[0] + chip[1], sizes[a]), dst_ref=outs[a].at[slot],
                send_sem=send.at[3 * a + k], recv_sem=recv.at[3 * a + k],
                device_id=(chip[0], chip[1], c), device_id_type=MESH)

        mine = [pltpu.make_async_copy(_window(ins[a], axes[a], me, sizes[a]), outs[a].at[me], loc.at[a])
                for a in range(n)]
        for cp in mine:
            cp.start()
        sends = [remote(a, k, chip, me) for a in range(n) for k, chip in enumerate(chips)]
        for cp in sends:
            cp.start()
        for a in range(n):
            for k, chip in enumerate(chips):
                remote(a, k, chip, 2 * chip[0] + chip[1]).wait_recv()
        for cp in sends:
            cp.wait_send()
        for cp in mine:
            cp.wait()

    return pl.pallas_call(
        body, name="scatter_grads", in_specs=[ANY] * n, out_specs=[ANY] * n, out_shape=out_shape,
        scratch_shapes=[pltpu.SemaphoreType.DMA((3 * n,)), pltpu.SemaphoreType.DMA((3 * n,)),
                        pltpu.SemaphoreType.DMA((n,))],
    )(*grads)


def sibling_swap(parts):
    n = len(parts)

    def body(*refs):
        ins, outs = refs[:n], refs[n:2 * n]
        send, recv = refs[2 * n:]
        x, y, c, _ = _place()
        cps = [pltpu.make_async_remote_copy(src_ref=ins[a], dst_ref=outs[a], send_sem=send.at[a],
                                            recv_sem=recv.at[a], device_id=(x, y, 1 - c), device_id_type=MESH)
               for a in range(n)]
        for cp in cps:
            cp.start()
        for cp in cps:
            cp.wait_recv()
        for cp in cps:
            cp.wait_send()

    return pl.pallas_call(
        body, name="sibling_swap", in_specs=[ANY] * n, out_specs=[ANY] * n,
        out_shape=[jax.ShapeDtypeStruct(p.shape, p.dtype) for p in parts],
        scratch_shapes=[pltpu.SemaphoreType.DMA((n,)), pltpu.SemaphoreType.DMA((n,))],
    )(*parts)


def small_allreduce(p):
    R, C = p.shape
    ndev = 8

    def body(p_ref, o_ref, buf, send, recv):
        x, y, c, _ = _place()
        me = 4 * x + 2 * y + c
        buf[me] = p_ref[...]

        def peer(d):
            px = 1 - x if d & 4 else x
            py = 1 - y if d & 2 else y
            pc = 1 - c if d & 1 else c
            return px, py, pc

        def copy(d, slot):
            return pltpu.make_async_remote_copy(src_ref=p_ref, dst_ref=buf.at[slot], send_sem=send.at[d - 1],
                                                recv_sem=recv.at[d - 1], device_id=peer(d), device_id_type=MESH)

        sends = [copy(d, me) for d in range(1, ndev)]
        for cp in sends:
            cp.start()
        for d in range(1, ndev):
            px, py, pc = peer(d)
            copy(d, 4 * px + 2 * py + pc).wait_recv()
        for cp in sends:
            cp.wait_send()
        acc = buf[0]
        for k in range(1, ndev):
            acc = acc + buf[k]
        o_ref[...] = acc

    return pl.pallas_call(
        body, name="small_allreduce", in_specs=[VMEM_SPEC], out_specs=VMEM_SPEC,
        out_shape=jax.ShapeDtypeStruct((R, C), F32),
        scratch_shapes=[pltpu.VMEM((ndev, R, C), F32), pltpu.SemaphoreType.DMA((ndev - 1,)),
                        pltpu.SemaphoreType.DMA((ndev - 1,))],
        compiler_params=pltpu.CompilerParams(vmem_limit_bytes=32 << 20),
    )(p)


def kernel(x, norm1_g, w_in, conv_w, conv_b, conv_ln_g, conv_ln_b, ret_gn_g, w_out, norm2_g, w_gate, w_up, w_down, final_g, loss_target, m_norm1_g, m_w_in, m_conv_w, m_conv_b, m_conv_ln_g, m_conv_ln_b, m_ret_gn_g, m_w_out, m_norm2_g, m_w_gate, m_w_up, m_w_down, m_final_g, v_norm1_g, v_w_in, v_conv_w, v_conv_b, v_conv_ln_g, v_conv_ln_b, v_ret_gn_g, v_w_out, v_norm2_g, v_w_gate, v_w_up, v_w_down, v_final_g):
    S = x.shape[1]
    xs = x.reshape(S, D)
    tgt = loss_target.reshape(S, D)
    fsh = FF // NCHIP

    win, wout, wgt, wut, wd, cw = gather_weights(
        [w_in.astype(BF), w_out.astype(BF), jnp.swapaxes(w_gate, 1, 2).astype(BF),
         jnp.swapaxes(w_up, 1, 2).astype(BF), w_down.astype(BF), conv_w],
        [2, 1, 1, 1, 1, 2])
    tables = _ret_tables(S)
    row = lambda a, l: a[l].reshape(1, -1)

    saved = []
    xc = xs
    for l in range(L):
        proj = in_proj(xc, row(norm1_g, l), win, l)
        u1, u = conv_fwd(proj, cw[l], row(conv_b, l), row(conv_ln_g, l), row(conv_ln_b, l), l)
        r_raw, states = ret_fwd(proj, tables, l)
        x2, mixed = out_proj(u, r_raw, proj, row(ret_gn_g, l), wout, xc, l)
        x3, gs, us = mlp_fwd(x2, row(norm2_g, l), wgt, wut, wd, l)
        saved.append((xc, proj, u1, r_raw, states, mixed, x2, gs, us))
        xc = x3

    dx, loss_acc, d_final = final_loss(xc, final_g.reshape(1, D), tgt)
    loss = lax.psum(loss_acc[0, 0] * (0.5 / D), ("x", "y", "c"))

    g_win = jnp.zeros((L, D, INW), BF)
    g_wout = jnp.zeros((L, D, D), BF)
    g_wgt = jnp.zeros((L, FF, D), BF)
    g_wut = jnp.zeros((L, FF, D), BF)
    g_wd = jnp.zeros((L, FF, D), BF)
    small = [None] * L
    for l in reversed(range(L)):
        xin, proj, u1, r_raw, states, mixed, x2, gs, us = saved[l]
        dx2, dgs, dus, act, h2, d_n2 = mlp_bwd(dx, x2, row(norm2_g, l), gs, us, wgt, wut, wd, l)
        g_wd = wgrad(act, dx, g_wd, l, "wgrad_down")
        g_wgt = wgrad(dgs, h2, g_wgt, l, "wgrad_gate")
        g_wut = wgrad(dus, h2, g_wut, l, "wgrad_up")
        dgate, dr, du1, sums = out_proj_bwd(dx2, wout, r_raw, proj, row(ret_gn_g, l), u1,
                                            row(conv_ln_g, l), row(conv_ln_b, l), l)
        g_wout = wgrad(mixed, dx2, g_wout, l, "wgrad_out")
        dab, dwb = conv_bwd(du1, proj, cw[l], l)
        dq, dk, dv = ret_bwd(dr, proj, states, tables, l)
        dproj = jnp.concatenate([dab, dq, dk, dv, dgate], axis=1)
        dx, h1, d_n1 = in_proj_bwd(dproj, win, xin, row(norm1_g, l), dx2, l)
        g_win = wgrad(h1, dproj, g_win, l, "wgrad_in")
        small[l] = jnp.concatenate([dwb, sums, d_n1.reshape(2, CW), d_n2.reshape(2, CW)], axis=0)
    grad_x = dx.reshape(1, S, D)

    per = CK + 1 + 8 + 4
    packed = jnp.concatenate(small + [d_final.reshape(2, CW), jnp.zeros((6, CW), F32)], axis=0)
    tot = small_allreduce(packed)
    lay = tot[:L * per].reshape(L, per, CW)
    g_conv_w_full = lay[:, 0:CK, :]
    j = 2 * lax.axis_index("x") + lax.axis_index("y")
    g_conv_w = lax.dynamic_slice_in_dim(g_conv_w_full, j * (CW // NCHIP), CW // NCHIP, axis=2)
    g_small = {
        "conv_b": lay[:, CK, :], "ret_gn_g": lay[:, CK + 1, :], "conv_ln_g": lay[:, CK + 2, :],
        "conv_ln_b": lay[:, CK + 3, :], "norm1_g": lay[:, CK + 9:CK + 11, :].reshape(L, D),
        "norm2_g": lay[:, CK + 11:CK + 13, :].reshape(L, D), "final_g": tot[L * per:L * per + 2].reshape(D),
    }

    recv = scatter_grads([g_win, g_wout, g_wgt, g_wut, g_wd], [2, 1, 1, 1, 1],
                         [INW // NCHIP, D // NCHIP, fsh, fsh, fsh])
    shard_shapes = [(L * D, INW // NCHIP), (L * D // NCHIP, D), (L * fsh, D), (L * fsh, D), (L * fsh, D)]
    names = ["w_in", "w_out", "w_gate", "w_up", "w_down"]
    parts = [sum_slots(r.reshape((NCHIP,) + shp), f"sum_{nm}") for r, shp, nm in zip(recv, shard_shapes, names)]
    theirs = sibling_swap(parts)

    def unT(a):
        return jnp.swapaxes(a.reshape(L, fsh, D), 1, 2).reshape(L * D, fsh)

    big = {}
    wmv = {"w_in": (w_in, m_w_in, v_w_in), "w_out": (w_out, m_w_out, v_w_out),
           "w_gate": (w_gate, m_w_gate, v_w_gate), "w_up": (w_up, m_w_up, v_w_up),
           "w_down": (w_down, m_w_down, v_w_down)}
    for nm, mine, other in zip(names, parts, theirs):
        w, m, v = wmv[nm]
        if nm in ("w_gate", "w_up"):
            mine, other = unT(mine), unT(other)
        shp2 = (w.shape[0] * w.shape[1], w.shape[2])
        outs = adamw(w.reshape(shp2), mine, other, m.reshape(shp2), v.reshape(shp2), f"adamw_{nm}")
        big[nm] = [o.reshape(w.shape) for o in outs]

    cshape = (L * CK, CW // NCHIP)
    zc = jnp.zeros(cshape, F32)
    big["conv_w"] = [o.reshape(conv_w.shape) for o in adamw(
        conv_w.reshape(cshape), g_conv_w.reshape(cshape), zc, m_conv_w.reshape(cshape),
        v_conv_w.reshape(cshape), "adamw_conv_w")]
    vec_names = ["norm1_g", "conv_b", "conv_ln_g", "conv_ln_b", "ret_gn_g", "norm2_g", "final_g"]
    vec_w = {"norm1_g": (norm1_g, m_norm1_g, v_norm1_g), "conv_b": (conv_b, m_conv_b, v_conv_b),
             "conv_ln_g": (conv_ln_g, m_conv_ln_g, v_conv_ln_g), "conv_ln_b": (conv_ln_b, m_conv_ln_b, v_conv_ln_b),
             "ret_gn_g": (ret_gn_g, m_ret_gn_g, v_ret_gn_g), "norm2_g": (norm2_g, m_norm2_g, v_norm2_g),
             "final_g": (final_g, m_final_g, v_final_g)}
    cat = lambda arrs: jnp.concatenate([a.reshape(-1, CW) for a in arrs], axis=0)
    vw = cat([vec_w[nm][0] for nm in vec_names])
    vm = cat([vec_w[nm][1] for nm in vec_names])
    vv = cat([vec_w[nm][2] for nm in vec_names])
    vg = cat([g_small[nm] for nm in vec_names])
    vouts = adamw(vw, vg, jnp.zeros_like(vg), vm, vv, "adamw_vectors")
    off = 0
    for nm in vec_names:
        w = vec_w[nm][0]
        nrow = w.size // CW
        big[nm] = [o[off:off + nrow].reshape(w.shape) for o in vouts]
        off += nrow

    order = ["norm1_g", "w_in", "conv_w", "conv_b", "conv_ln_g", "conv_ln_b", "ret_gn_g", "w_out", "norm2_g",
             "w_gate", "w_up", "w_down", "final_g"]
    return (loss, grad_x, *[big[nm][0] for nm in order], *[big[nm][1] for nm in order],
            *[big[nm][2] for nm in order], *[big[nm][3] for nm in order])
```

```python
import math

import jax
import jax.numpy as jnp
from jax import lax
from jax.experimental import pallas as pl
from jax.experimental.pallas import tpu as pltpu

D = 1024
L = 4
CW = 512
RW = 512
NH = 4
HD = 128
CK = 31
CHUNK = 64
INW = 3072
FF = 2816
NCHIP = 4
EPS = 1e-6
ROPE_BASE = 10000.0
SEQ_TILE = 512
MLP_ROWS = 1024
MLP_COLS = 256
HALO = 32
CONV_ROWS = 32

ADAM_LR = 0.001
ADAM_B1 = 0.9
ADAM_B2 = 0.999
ADAM_EPS = 1e-08
ADAM_WD = 0.01
ADAM_STEP = 10

BF = jnp.bfloat16
F32 = jnp.float32
MESH = pl.DeviceIdType.MESH
ANY = pl.BlockSpec(memory_space=pl.ANY)
VMEM_SPEC = pl.BlockSpec(memory_space=pltpu.VMEM)
HBM_SPEC = pl.BlockSpec(memory_space=pltpu.HBM)
SEM_SPEC = pl.BlockSpec(memory_space=pltpu.SEMAPHORE)
DATAFLOW = pltpu.SideEffectType.DATAFLOW_SIDE_EFFECTING


def _params(n_grid, vmem_mb):
    return pltpu.CompilerParams(dimension_semantics=("arbitrary",) * n_grid,
                                vmem_limit_bytes=vmem_mb << 20)


def _dot(a, b):
    return jnp.dot(a, b, preferred_element_type=F32)


def _dot_nt(a, b):
    return lax.dot_general(a, b, (((1,), (1,)), ((), ())), preferred_element_type=F32)


def _dot_tn(a, b):
    return lax.dot_general(a, b, (((0,), (0,)), ((), ())), preferred_element_type=F32)


def _sigmoid(x):
    return 0.5 * jnp.tanh(0.5 * x) + 0.5


def _mean(x):
    return jnp.mean(x, axis=-1, keepdims=True)


def _rot(t, cs, sn):
    return t * cs + pltpu.roll(t, HD // 2, 1) * sn


def _rot_t(dy, cs, sn):
    return dy * cs + pltpu.roll(dy * sn, HD // 2, 1)


def _rms_bwd(x, g, dh, dx_in):
    r = lax.rsqrt(_mean(x * x) + EPS)
    xh = x * r
    dxh = dh * g
    dx = dx_in + r * (dxh - xh * _mean(dxh * xh))
    return dx, jnp.sum(dh * xh, axis=0, keepdims=True), (xh * g).astype(BF)


def in_proj(x, g, win, l):
    S = x.shape[0]
    tm, tn = min(S, SEQ_TILE), 768

    def body(x_ref, g_ref, w_ref, o_ref, h_scr):
        @pl.when(pl.program_id(1) == 0)
        def _():
            xv = x_ref[...]
            h_scr[...] = (xv * lax.rsqrt(_mean(xv * xv) + EPS) * g_ref[...]).astype(BF)
        o_ref[...] = _dot(h_scr[...], w_ref[...])

    return pl.pallas_call(
        body, name=f"in_proj_{l}", grid=(S // tm, INW // tn),
        in_specs=[pl.BlockSpec((tm, D), lambda i, j: (i, 0)),
                  pl.BlockSpec((1, D), lambda i, j: (0, 0)),
                  pl.BlockSpec((D, tn), lambda i, j: (0, j))],
        out_specs=pl.BlockSpec((tm, tn), lambda i, j: (i, j)),
        out_shape=jax.ShapeDtypeStruct((S, INW), F32),
        scratch_shapes=[pltpu.VMEM((tm, D), BF)],
        compiler_params=_params(2, 40),
    )(x, g, win)


def conv_fwd(proj, cw, cb, lg, lb, l):
    S = proj.shape[0]
    tc = min(S, SEQ_TILE)

    def body(a_ref, b_ref, w_ref, cb_ref, lg_ref, lb_ref, u1_ref, u_ref, buf):
        i = pl.program_id(0)

        @pl.when(i == 0)
        def _():
            buf[0:HALO, :] = jnp.zeros((HALO, CW), F32)

        @pl.when(i > 0)
        def _():
            buf[0:HALO, :] = buf[tc:tc + HALO, :]

        buf[HALO:HALO + tc, :] = a_ref[...] * _sigmoid(b_ref[...])
        off = HALO - (CK - 1)
        for r0 in range(0, tc, CONV_ROWS):
            acc = jnp.broadcast_to(cb_ref[...], (CONV_ROWS, CW))
            for k in range(CK):
                acc = acc + w_ref[k:k + 1, :] * buf[r0 + off + k:r0 + off + k + CONV_ROWS, :]
            u1_ref[r0:r0 + CONV_ROWS, :] = acc
            d = acc - _mean(acc)
            u2 = d * lax.rsqrt(_mean(d * d) + EPS) * lg_ref[...] + lb_ref[...]
            u_ref[r0:r0 + CONV_ROWS, :] = (u2 * _sigmoid(u2)).astype(BF)

    vec = pl.BlockSpec((1, CW), lambda i: (0, 0))
    return pl.pallas_call(
        body, name=f"conv_fwd_{l}", grid=(S // tc,),
        in_specs=[pl.BlockSpec((tc, CW), lambda i: (i, 0)),
                  pl.BlockSpec((tc, CW), lambda i: (i, 1)),
                  pl.BlockSpec((CK, CW), lambda i: (0, 0)), vec, vec, vec],
        out_specs=[pl.BlockSpec((tc, CW), lambda i: (i, 0)),
                   pl.BlockSpec((tc, CW), lambda i: (i, 0))],
        out_shape=[jax.ShapeDtypeStruct((S, CW), F32), jax.ShapeDtypeStruct((S, CW), BF)],
        scratch_shapes=[pltpu.VMEM((tc + HALO, CW), F32)],
        compiler_params=_params(1, 32),
    )(proj, proj, cw, cb, lg, lb)


def _ret_tables(S):
    half = HD // 2
    pos = jnp.arange(S, dtype=F32)
    freqs = ROPE_BASE ** (-jnp.arange(half, dtype=F32) / half)
    ang = pos[:, None] * freqs[None, :]
    cos, sin = jnp.cos(ang), jnp.sin(ang)
    cosf = jnp.concatenate([cos, cos], axis=-1)
    sinf = jnp.concatenate([-sin, sin], axis=-1)
    log_g = jnp.log(1.0 - 2.0 ** (-5.0 - jnp.arange(NH, dtype=F32)))
    idx = jnp.arange(CHUNK, dtype=F32)
    dmat = jnp.exp(log_g[:, None, None] * jnp.abs(idx[:, None] - idx[None, :]))
    qdec = jnp.broadcast_to(jnp.exp(log_g[:, None] * (idx + 1.0))[:, :, None], (NH, CHUNK, HD))
    kdec = jnp.broadcast_to(jnp.exp(log_g[:, None] * (CHUNK - 1 - idx))[:, :, None], (NH, CHUNK, HD))
    cdec = jnp.broadcast_to(jnp.exp(log_g * CHUNK)[:, None, None], (NH, HD, HD))
    return cosf, sinf, dmat, qdec, kdec, cdec


def _ret_specs(tr, tmap):
    q0, k0, v0 = (2 * CW) // HD, (2 * CW + RW) // HD, (2 * CW + 2 * RW) // HD
    return [pl.BlockSpec((tr, HD), lambda h, t: (tmap(t), q0 + h)),
            pl.BlockSpec((tr, HD), lambda h, t: (tmap(t), k0 + h)),
            pl.BlockSpec((tr, HD), lambda h, t: (tmap(t), v0 + h)),
            pl.BlockSpec((tr, HD), lambda h, t: (tmap(t), 0)),
            pl.BlockSpec((tr, HD), lambda h, t: (tmap(t), 0)),
            pl.BlockSpec((None, CHUNK, CHUNK), lambda h, t: (h, 0, 0)),
            pl.BlockSpec((None, CHUNK, HD), lambda h, t: (h, 0, 0)),
            pl.BlockSpec((None, CHUNK, HD), lambda h, t: (h, 0, 0)),
            pl.BlockSpec((None, HD, HD), lambda h, t: (h, 0, 0))]


def ret_fwd(proj, tables, l):
    S = proj.shape[0]
    tr = min(S, SEQ_TILE)
    cpb = tr // CHUNK
    scale = HD ** -0.5

    def body(q_ref, k_ref, v_ref, cos_ref, sin_ref, dm_ref, qd_ref, kd_ref, cd_ref,
             r_ref, st_ref, st):
        @pl.when(pl.program_id(1) == 0)
        def _():
            st[...] = jnp.zeros((HD, HD), F32)

        for c in range(cpb):
            rows = slice(c * CHUNK, (c + 1) * CHUNK)
            cs, sn = cos_ref[rows, :], sin_ref[rows, :]
            qr = _rot(q_ref[rows, :], cs, sn)
            kr = _rot(k_ref[rows, :], cs, sn) * scale
            vb = v_ref[rows, :].astype(BF)
            s = st[...]
            st_ref[c] = s
            sc = _dot_nt(qr.astype(BF), kr.astype(BF)) * dm_ref[...]
            out = _dot(sc.astype(BF), vb) + _dot((qr * qd_ref[...]).astype(BF), s.astype(BF))
            r_ref[rows, :] = out
            st[...] = cd_ref[...] * s + _dot_tn((kr * kd_ref[...]).astype(BF), vb)

    return pl.pallas_call(
        body, name=f"ret_fwd_{l}", grid=(NH, S // tr),
        in_specs=_ret_specs(tr, lambda t: t),
        out_specs=[pl.BlockSpec((tr, HD), lambda h, t: (t, h)),
                   pl.BlockSpec((None, cpb, HD, HD), lambda h, t: (h, t, 0, 0))],
        out_shape=[jax.ShapeDtypeStruct((S, RW), F32),
                   jax.ShapeDtypeStruct((NH, S // CHUNK, HD, HD), F32)],
        scratch_shapes=[pltpu.VMEM((HD, HD), F32)],
        compiler_params=_params(2, 32),
    )(proj, proj, proj, *tables)


def out_proj(u, r_raw, proj, gn, wout, x, l):
    S = x.shape[0]
    tm = min(S, SEQ_TILE)
    gate_blk = (2 * CW + 3 * RW) // RW

    def body(u_ref, r_ref, gate_ref, gn_ref, w_ref, x_ref, x2_ref, mix_ref):
        mix_ref[:, 0:CW] = u_ref[...]
        gt = gate_ref[...]
        sil = gt * _sigmoid(gt) * gn_ref[...]
        for h in range(NH):
            cols = slice(h * HD, (h + 1) * HD)
            rh = r_ref[:, cols]
            d = rh - _mean(rh)
            rn = d * lax.rsqrt(_mean(d * d) + EPS)
            mix_ref[:, CW + h * HD:CW + (h + 1) * HD] = (rn * sil[:, cols]).astype(BF)
        x2_ref[...] = x_ref[...] + _dot(mix_ref[...], w_ref[...])

    return pl.pallas_call(
        body, name=f"out_proj_{l}", grid=(S // tm,),
        in_specs=[pl.BlockSpec((tm, CW), lambda i: (i, 0)),
                  pl.BlockSpec((tm, RW), lambda i: (i, 0)),
                  pl.BlockSpec((tm, RW), lambda i: (i, gate_blk)),
                  pl.BlockSpec((1, RW), lambda i: (0, 0)),
                  pl.BlockSpec((D, D), lambda i: (0, 0)),
                  pl.BlockSpec((tm, D), lambda i: (i, 0))],
        out_specs=[pl.BlockSpec((tm, D), lambda i: (i, 0)),
                   pl.BlockSpec((tm, D), lambda i: (i, 0))],
        out_shape=[jax.ShapeDtypeStruct((S, D), F32), jax.ShapeDtypeStruct((S, D), BF)],
        compiler_params=_params(1, 40),
    )(u, r_raw, proj, gn, wout, x)


def mlp_fwd(x2, g2, wgt, wut, wd, l):
    S = x2.shape[0]
    tm, tf = min(S, MLP_ROWS), MLP_COLS
    nk = FF // tf

    def body(x_ref, g_ref, wg_ref, wu_ref, wd_ref, o_ref, gs_ref, us_ref, h_scr, acc):
        k = pl.program_id(1)

        @pl.when(k == 0)
        def _():
            xv = x_ref[...]
            h_scr[...] = (xv * lax.rsqrt(_mean(xv * xv) + EPS) * g_ref[...]).astype(BF)
            acc[...] = jnp.zeros((tm, D), F32)

        gv = _dot_nt(h_scr[...], wg_ref[...])
        uv = _dot_nt(h_scr[...], wu_ref[...])
        gs_ref[...] = gv.astype(BF)
        us_ref[...] = uv.astype(BF)
        acc[...] += _dot((gv * _sigmoid(gv) * uv).astype(BF), wd_ref[...])

        @pl.when(k == nk - 1)
        def _():
            o_ref[...] = x_ref[...] + acc[...]

    wspec = pl.BlockSpec((tf, D), lambda i, k: (k, 0))
    return pl.pallas_call(
        body, name=f"mlp_fwd_{l}", grid=(S // tm, nk),
        in_specs=[pl.BlockSpec((tm, D), lambda i, k: (i, 0)),
                  pl.BlockSpec((1, D), lambda i, k: (0, 0)), wspec, wspec, wspec],
        out_specs=[pl.BlockSpec((tm, D), lambda i, k: (i, 0)),
                   pl.BlockSpec((tm, tf), lambda i, k: (i, k)),
                   pl.BlockSpec((tm, tf), lambda i, k: (i, k))],
        out_shape=[jax.ShapeDtypeStruct((S, D), F32), jax.ShapeDtypeStruct((S, FF), BF),
                   jax.ShapeDtypeStruct((S, FF), BF)],
        scratch_shapes=[pltpu.VMEM((tm, D), BF), pltpu.VMEM((tm, D), F32)],
        compiler_params=_params(2, 56),
    )(x2, g2, wgt, wut, wd)


def final_loss(x, gf, tgt):
    S = x.shape[0]
    tm = min(S, SEQ_TILE)

    def body(x_ref, g_ref, t_ref, dx_ref, loss_ref, dg_ref):
        @pl.when(pl.program_id(0) == 0)
        def _():
            loss_ref[...] = jnp.zeros((8, 128), F32)
            dg_ref[...] = jnp.zeros((1, D), F32)

        xv = x_ref[...]
        r = lax.rsqrt(_mean(xv * xv) + EPS)
        xh = xv * r
        diff = xh * g_ref[...] - t_ref[...]
        loss_ref[...] += jnp.sum(jnp.sum(diff * diff, axis=-1, keepdims=True), axis=0, keepdims=True)
        dy = diff * (1.0 / D)
        dg_ref[...] += jnp.sum(dy * xh, axis=0, keepdims=True)
        dxh = dy * g_ref[...]
        dx_ref[...] = r * (dxh - xh * _mean(dxh * xh))

    return pl.pallas_call(
        body, name="final_loss", grid=(S // tm,),
        in_specs=[pl.BlockSpec((tm, D), lambda i: (i, 0)),
                  pl.BlockSpec((1, D), lambda i: (0, 0)),
                  pl.BlockSpec((tm, D), lambda i: (i, 0))],
        out_specs=[pl.BlockSpec((tm, D), lambda i: (i, 0)),
                   pl.BlockSpec((8, 128), lambda i: (0, 0)),
                   pl.BlockSpec((1, D), lambda i: (0, 0))],
        out_shape=[jax.ShapeDtypeStruct((S, D), F32), jax.ShapeDtypeStruct((8, 128), F32),
                   jax.ShapeDtypeStruct((1, D), F32)],
        compiler_params=_params(1, 40),
    )(x, gf, tgt)


def mlp_bwd(dx3, x2, g2, gs, us, wgt, wut, wd, l):
    S = x2.shape[0]
    tm, tf = min(S, MLP_ROWS), MLP_COLS
    nk = FF // tf

    def body(dx_ref, x_ref, g_ref, gs_ref, us_ref, wg_ref, wu_ref, wd_ref,
             dx2_ref, dg_ref, du_ref, a_ref, h_ref, dgain_ref, dxb, acc):
        i, k = pl.program_id(0), pl.program_id(1)

        @pl.when(k == 0)
        def _():
            dxb[...] = dx_ref[...].astype(BF)
            acc[...] = jnp.zeros((tm, D), F32)

        @pl.when((k == 0) & (i == 0))
        def _():
            dgain_ref[...] = jnp.zeros((1, D), F32)

        da = _dot_nt(dxb[...], wd_ref[...])
        gv = gs_ref[...].astype(F32)
        uv = us_ref[...].astype(F32)
        sg = _sigmoid(gv)
        sil = gv * sg
        dgv = (da * uv * (sg * (1.0 + gv * (1.0 - sg)))).astype(BF)
        duv = (da * sil).astype(BF)
        a_ref[...] = (sil * uv).astype(BF)
        dg_ref[...] = dgv
        du_ref[...] = duv
        acc[...] += _dot(dgv, wg_ref[...]) + _dot(duv, wu_ref[...])

        @pl.when(k == nk - 1)
        def _():
            dx2, dgain, hb = _rms_bwd(x_ref[...], g_ref[...], acc[...], dx_ref[...])
            dx2_ref[...] = dx2
            dgain_ref[...] += dgain
            h_ref[...] = hb

    wspec = pl.BlockSpec((tf, D), lambda i, k: (k, 0))
    row = pl.BlockSpec((tm, D), lambda i, k: (i, 0))
    wide = pl.BlockSpec((tm, tf), lambda i, k: (i, k))
    return pl.pallas_call(
        body, name=f"mlp_bwd_{l}", grid=(S // tm, nk),
        in_specs=[row, row, pl.BlockSpec((1, D), lambda i, k: (0, 0)), wide, wide, wspec, wspec, wspec],
        out_specs=[row, wide, wide, wide, row, pl.BlockSpec((1, D), lambda i, k: (0, 0))],
        out_shape=[jax.ShapeDtypeStruct((S, D), F32), jax.ShapeDtypeStruct((S, FF), BF),
                   jax.ShapeDtypeStruct((S, FF), BF), jax.ShapeDtypeStruct((S, FF), BF),
                   jax.ShapeDtypeStruct((S, D), BF), jax.ShapeDtypeStruct((1, D), F32)],
        scratch_shapes=[pltpu.VMEM((tm, D), BF), pltpu.VMEM((tm, D), F32)],
        compiler_params=_params(2, 56),
    )(dx3, x2, g2, gs, us, wgt, wut, wd)


def wgrad(a, b, l, name):
    S, K = a.shape
    N = b.shape[1]
    tk = 1408 if K == FF else min(K, 1024)
    tn = 768 if N == INW else min(N, 1024)
    ts = min(S, SEQ_TILE)
    ns = S // ts

    def body(a_ref, b_ref, o_ref, acc):
        s = pl.program_id(2)

        @pl.when(s == 0)
        def _():
            acc[...] = jnp.zeros((tk, tn), F32)

        acc[...] += _dot_tn(a_ref[...], b_ref[...].astype(BF))

        @pl.when(s == ns - 1)
        def _():
            o_ref[...] = acc[...].astype(BF)

    return pl.pallas_call(
        body, name=f"{name}_{l}", grid=(K // tk, N // tn, ns),
        in_specs=[pl.BlockSpec((ts, tk), lambda i, j, s: (s, i)),
                  pl.BlockSpec((ts, tn), lambda i, j, s: (s, j))],
        out_specs=pl.BlockSpec((tk, tn), lambda i, j, s: (i, j)),
        out_shape=jax.ShapeDtypeStruct((K, N), BF),
        scratch_shapes=[pltpu.VMEM((tk, tn), F32)],
        compiler_params=_params(3, 48),
    )(a, b)


def out_proj_bwd(dx2, wout, r_raw, proj, gn, u1, lg, lb, l):
    S = dx2.shape[0]
    tm = min(S, SEQ_TILE)
    gate_blk = (2 * CW + 3 * RW) // RW

    def body(dx_ref, w_ref, r_ref, gate_ref, gn_ref, u1_ref, lg_ref, lb_ref,
             dgate_ref, dr_ref, du1_ref, sums_ref):
        @pl.when(pl.program_id(0) == 0)
        def _():
            sums_ref[...] = jnp.zeros((8, CW), F32)

        dmix = _dot_nt(dx_ref[...].astype(BF), w_ref[...])
        gt = gate_ref[...]
        sg = _sigmoid(gt)
        sil = gt * sg
        dsil = sg * (1.0 + gt * (1.0 - sg))
        for h in range(NH):
            cols = slice(h * HD, (h + 1) * HD)
            rh = r_ref[:, cols]
            d = rh - _mean(rh)
            rs = lax.rsqrt(_mean(d * d) + EPS)
            rn = d * rs
            drr = dmix[:, CW + h * HD:CW + (h + 1) * HD]
            gnh = gn_ref[:, cols]
            sums_ref[0:1, cols] += jnp.sum(drr * rn * sil[:, cols], axis=0, keepdims=True)
            dgate_ref[:, cols] = (drr * rn * gnh * dsil[:, cols]).astype(BF)
            drn = drr * gnh * sil[:, cols]
            dr_ref[:, cols] = rs * (drn - _mean(drn) - rn * _mean(drn * rn))
        du = dmix[:, 0:CW]
        u1 = u1_ref[...]
        d = u1 - _mean(u1)
        rs = lax.rsqrt(_mean(d * d) + EPS)
        xh = d * rs
        u2 = xh * lg_ref[...] + lb_ref[...]
        sg2 = _sigmoid(u2)
        du2 = du * (sg2 * (1.0 + u2 * (1.0 - sg2)))
        sums_ref[1:2, :] += jnp.sum(du2 * xh, axis=0, keepdims=True)
        sums_ref[2:3, :] += jnp.sum(du2, axis=0, keepdims=True)
        dxh = du2 * lg_ref[...]
        du1_ref[...] = rs * (dxh - _mean(dxh) - xh * _mean(dxh * xh))

    vec = pl.BlockSpec((1, CW), lambda i: (0, 0))
    half = pl.BlockSpec((tm, CW), lambda i: (i, 0))
    return pl.pallas_call(
        body, name=f"out_proj_bwd_{l}", grid=(S // tm,),
        in_specs=[pl.BlockSpec((tm, D), lambda i: (i, 0)),
                  pl.BlockSpec((D, D), lambda i: (0, 0)),
                  half, pl.BlockSpec((tm, RW), lambda i: (i, gate_blk)), vec, half, vec, vec],
        out_specs=[half, half, half, pl.BlockSpec((8, CW), lambda i: (0, 0))],
        out_shape=[jax.ShapeDtypeStruct((S, RW), BF), jax.ShapeDtypeStruct((S, RW), F32),
                   jax.ShapeDtypeStruct((S, CW), F32), jax.ShapeDtypeStruct((8, CW), F32)],
        compiler_params=_params(1, 40),
    )(dx2, wout, r_raw, proj, gn, u1, lg, lb)


def conv_bwd(du1, proj, cw, l):
    S = proj.shape[0]
    tc = min(S, SEQ_TILE)
    nt = S // tc

    def body(du1_ref, a_ref, b_ref, w_ref, dab_ref, dwb_ref, buf):
        i = pl.program_id(0)

        @pl.when(i == 0)
        def _():
            buf[tc:tc + HALO, :] = jnp.zeros((HALO, CW), F32)
            dwb_ref[...] = jnp.zeros((CK + 1, CW), F32)

        @pl.when(i > 0)
        def _():
            buf[tc:tc + HALO, :] = buf[0:HALO, :]

        buf[0:tc, :] = du1_ref[...]
        parts = [jnp.zeros((8, CW), F32) for _ in range(CK)]
        for r0 in range(0, tc, CONV_ROWS):
            av = a_ref[r0:r0 + CONV_ROWS, :]
            sgb = _sigmoid(b_ref[r0:r0 + CONV_ROWS, :])
            u0 = av * sgb
            acc = jnp.zeros((CONV_ROWS, CW), F32)
            for j in range(CK):
                sl = buf[r0 + j:r0 + j + CONV_ROWS, :]
                acc = acc + w_ref[CK - 1 - j:CK - j, :] * sl
                pr = u0 * sl
                red = pr[0:8, :]
                for q in range(1, CONV_ROWS // 8):
                    red = red + pr[8 * q:8 * q + 8, :]
                parts[CK - 1 - j] = parts[CK - 1 - j] + red
            dab_ref[r0:r0 + CONV_ROWS, 0:CW] = (acc * sgb).astype(BF)
            dab_ref[r0:r0 + CONV_ROWS, CW:2 * CW] = (acc * av * sgb * (1.0 - sgb)).astype(BF)
        for k in range(CK):
            dwb_ref[k:k + 1, :] += jnp.sum(parts[k], axis=0, keepdims=True)
        dwb_ref[CK:CK + 1, :] += jnp.sum(du1_ref[...], axis=0, keepdims=True)

    return pl.pallas_call(
        body, name=f"conv_bwd_{l}", grid=(nt,),
        in_specs=[pl.BlockSpec((tc, CW), lambda i: (nt - 1 - i, 0)),
                  pl.BlockSpec((tc, CW), lambda i: (nt - 1 - i, 0)),
                  pl.BlockSpec((tc, CW), lambda i: (nt - 1 - i, 1)),
                  pl.BlockSpec((CK, CW), lambda i: (0, 0))],
        out_specs=[pl.BlockSpec((tc, 2 * CW), lambda i: (nt - 1 - i, 0)),
                   pl.BlockSpec((CK + 1, CW), lambda i: (0, 0))],
        out_shape=[jax.ShapeDtypeStruct((S, 2 * CW), BF), jax.ShapeDtypeStruct((CK + 1, CW), F32)],
        scratch_shapes=[pltpu.VMEM((tc + HALO, CW), F32)],
        compiler_params=_params(1, 32),
    )(du1, proj, proj, cw)


def ret_bwd(dr, proj, states, tables, l):
    S = proj.shape[0]
    tr = min(S, SEQ_TILE)
    cpb = tr // CHUNK
    nt = S // tr
    scale = HD ** -0.5

    def body(q_ref, k_ref, v_ref, cos_ref, sin_ref, dm_ref, qd_ref, kd_ref, cd_ref, dr_ref, st_ref,
             dq_ref, dk_ref, dv_ref, gst):
        @pl.when(pl.program_id(1) == 0)
        def _():
            gst[...] = jnp.zeros((HD, HD), F32)

        for c in reversed(range(cpb)):
            rows = slice(c * CHUNK, (c + 1) * CHUNK)
            cs, sn = cos_ref[rows, :], sin_ref[rows, :]
            qr = _rot(q_ref[rows, :], cs, sn)
            kr = _rot(k_ref[rows, :], cs, sn) * scale
            qb, kb = qr.astype(BF), kr.astype(BF)
            vb = v_ref[rows, :].astype(BF)
            dob = dr_ref[rows, :].astype(BF)
            sb = st_ref[c].astype(BF)
            gn1 = gst[...]
            gb = gn1.astype(BF)
            sc = (_dot_nt(qb, kb) * dm_ref[...]).astype(BF)
            dsc = (_dot_nt(dob, vb) * dm_ref[...]).astype(BF)
            dqr = _dot(dsc, kb) + _dot_nt(dob, sb) * qd_ref[...]
            dkr = _dot_tn(dsc, qb) + _dot_nt(vb, gb) * kd_ref[...]
            dvv = _dot_tn(sc, dob) + _dot((kr * kd_ref[...]).astype(BF), gb)
            gst[...] = cd_ref[...] * gn1 + _dot_tn((qr * qd_ref[...]).astype(BF), dob)
            dq_ref[rows, :] = _rot_t(dqr, cs, sn).astype(BF)
            dk_ref[rows, :] = _rot_t(dkr * scale, cs, sn).astype(BF)
            dv_ref[rows, :] = dvv.astype(BF)

    rev = lambda t: nt - 1 - t
    hblk = pl.BlockSpec((tr, HD), lambda h, t: (rev(t), h))
    return pl.pallas_call(
        body, name=f"ret_bwd_{l}", grid=(NH, nt),
        in_specs=_ret_specs(tr, rev) + [hblk, pl.BlockSpec((None, cpb, HD, HD), lambda h, t: (h, rev(t), 0, 0))],
        out_specs=[hblk, hblk, hblk],
        out_shape=[jax.ShapeDtypeStruct((S, RW), BF)] * 3,
        scratch_shapes=[pltpu.VMEM((HD, HD), F32)],
        compiler_params=_params(2, 32),
    )(proj, proj, proj, *tables, dr, states)


def in_proj_bwd(dproj, win, x, g, dx2, l):
    S = x.shape[0]
    tm, tk = min(S, SEQ_TILE), 768
    nk = INW // tk

    def body(dp_ref, w_ref, x_ref, g_ref, dx2_ref, dx_ref, h_ref, dgain_ref, acc):
        i, k = pl.program_id(0), pl.program_id(1)

        @pl.when(k == 0)
        def _():
            acc[...] = jnp.zeros((tm, D), F32)

        @pl.when((k == 0) & (i == 0))
        def _():
            dgain_ref[...] = jnp.zeros((1, D), F32)

        acc[...] += _dot_nt(dp_ref[...], w_ref[...])

        @pl.when(k == nk - 1)
        def _():
            dx, dgain, hb = _rms_bwd(x_ref[...], g_ref[...], acc[...], dx2_ref[...])
            dx_ref[...] = dx
            dgain_ref[...] += dgain
            h_ref[...] = hb

    row = pl.BlockSpec((tm, D), lambda i, k: (i, 0))
    vec = pl.BlockSpec((1, D), lambda i, k: (0, 0))
    return pl.pallas_call(
        body, name=f"in_proj_bwd_{l}", grid=(S // tm, nk),
        in_specs=[pl.BlockSpec((tm, tk), lambda i, k: (i, k)),
                  pl.BlockSpec((D, tk), lambda i, k: (0, k)), row, vec, row],
        out_specs=[row, row, vec],
        out_shape=[jax.ShapeDtypeStruct((S, D), F32), jax.ShapeDtypeStruct((S, D), BF),
                   jax.ShapeDtypeStruct((1, D), F32)],
        scratch_shapes=[pltpu.VMEM((tm, D), F32)],
        compiler_params=_params(2, 40),
    )(dproj, win, x, g, dx2)


def sum_slots(recv, name):
    _, R, C = recv.shape
    tr = 256 if R % 256 == 0 else R

    def body(r_ref, o_ref):
        acc = r_ref[0].astype(F32)
        for k in range(1, NCHIP):
            acc = acc + r_ref[k].astype(F32)
        o_ref[...] = acc

    return pl.pallas_call(
        body, name=name, grid=(R // tr,),
        in_specs=[pl.BlockSpec((NCHIP, tr, C), lambda i: (0, i, 0))],
        out_specs=pl.BlockSpec((tr, C), lambda i: (i, 0)),
        out_shape=jax.ShapeDtypeStruct((R, C), F32),
        compiler_params=_params(1, 32),
    )(recv)


def adamw(w, ga, gb, m, v, name):
    R, C = w.shape
    tr = 256 if R % 256 == 0 else R
    c1 = 1.0 - ADAM_B1 ** ADAM_STEP
    c2 = 1.0 - ADAM_B2 ** ADAM_STEP

    def body(w_ref, ga_ref, gb_ref, m_ref, v_ref, g_out, d_out, m_out, v_out):
        g = ga_ref[...] + gb_ref[...]
        mn = ADAM_B1 * m_ref[...] + (1.0 - ADAM_B1) * g
        vn = ADAM_B2 * v_ref[...] + (1.0 - ADAM_B2) * (g * g)
        g_out[...] = g
        m_out[...] = mn
        v_out[...] = vn
        d_out[...] = -ADAM_LR * ((mn / c1) / (jnp.sqrt(vn / c2) + ADAM_EPS) + ADAM_WD * w_ref[...])

    blk = pl.BlockSpec((tr, C), lambda i: (i, 0))
    return pl.pallas_call(
        body, name=name, grid=(R // tr,),
        in_specs=[blk] * 5, out_specs=[blk] * 4,
        out_shape=[jax.ShapeDtypeStruct((R, C), F32)] * 4,
        compiler_params=_params(1, 40),
    )(w, ga, gb, m, v)


def _place():
    x, y, c = lax.axis_index("x"), lax.axis_index("y"), lax.axis_index("c")
    chips = [(1 - x, y), (x, 1 - y), (1 - x, 1 - y)]
    return x, y, c, chips


def _window(ref, axis, j, size):
    idx = [slice(None)] * len(ref.shape)
    idx[axis] = pl.ds(pl.multiple_of(j * size, 128 if axis == len(ref.shape) - 1 else 16), size)
    return ref.at[tuple(idx)]


def _hbm(a):
    return pltpu.with_memory_space_constraint(a, pltpu.HBM)


def _hbm_like(arrs):
    return [pltpu.HBM(a.shape, a.dtype) for a in arrs]


def gather_start(shards, axes, after, l):
    n = len(shards)
    lands = []
    for s, ax in zip(shards, axes):
        shp = list(s.shape)
        shp[ax] *= NCHIP
        lands.append(lax.empty(tuple(shp), s.dtype))
    extra = [] if after is None else [after]

    def body(*refs):
        ins, land = refs[:n], refs[n:2 * n]
        send, recv = refs[2 * n + len(extra)], refs[2 * n + len(extra) + 1]
        token, loc = refs[-2], refs[-1]
        x, y, c, chips = _place()

        def spot(a):
            return _window(land[a], axes[a], 2 * x + y, ins[a].shape[axes[a]])

        mine = [pltpu.make_async_copy(ins[a], spot(a), loc.at[a]) for a in range(n)]
        for cp in mine:
            cp.start()
        for a in range(n):
            for k, chip in enumerate(chips):
                pltpu.make_async_remote_copy(
                    src_ref=ins[a], dst_ref=spot(a), send_sem=send.at[3 * a + k], recv_sem=recv.at[3 * a + k],
                    device_id=(chip[0], chip[1], c), device_id_type=MESH).start()
        for cp in mine:
            cp.wait()
        token[...] = jnp.zeros_like(token)

    outs = pl.pallas_call(
        body, name=f"gather_start_{l}",
        in_specs=[HBM_SPEC] * (2 * n) + [ANY] * len(extra),
        out_specs=(SEM_SPEC, SEM_SPEC, *[HBM_SPEC] * (2 * n), VMEM_SPEC),
        out_shape=(pltpu.SemaphoreType.DMA((3 * n,)), pltpu.SemaphoreType.DMA((3 * n,)),
                   *_hbm_like(shards), *_hbm_like(lands), jax.ShapeDtypeStruct((8, 128), F32)),
        input_output_aliases={a: 2 + a for a in range(2 * n)},
        scratch_shapes=[pltpu.SemaphoreType.DMA((n,))],
        compiler_params=pltpu.CompilerParams(has_side_effects=DATAFLOW),
    )(*[_hbm(s) for s in shards], *[_hbm(b) for b in lands], *extra)
    return outs[0], outs[1], list(outs[2:2 + n]), list(outs[2 + n:2 + 2 * n]), outs[-1]


def gather_wait(send, recv, shards, lands, axes, after, l):
    n = len(shards)

    def body(*refs):
        ins, land = refs[:n], refs[n:2 * n]
        send_ref, recv_ref = refs[2 * n], refs[2 * n + 1]
        x, y, c, chips = _place()
        for a in range(n):
            for k, chip in enumerate(chips):
                cp = pltpu.make_async_remote_copy(
                    src_ref=ins[a], dst_ref=_window(land[a], axes[a], 2 * chip[0] + chip[1], ins[a].shape[axes[a]]),
                    send_sem=send_ref.at[3 * a + k], recv_sem=recv_ref.at[3 * a + k],
                    device_id=(chip[0], chip[1], c), device_id_type=MESH)
                cp.wait_send()
                cp.wait_recv()

    outs = pl.pallas_call(
        body, name=f"gather_wait_{l}",
        in_specs=[HBM_SPEC] * (2 * n) + [SEM_SPEC, SEM_SPEC, ANY],
        out_specs=[HBM_SPEC] * (2 * n),
        out_shape=(*_hbm_like(shards), *_hbm_like(lands)),
        input_output_aliases={a: a for a in range(2 * n)},
        compiler_params=pltpu.CompilerParams(has_side_effects=DATAFLOW),
    )(*shards, *lands, send, recv, after)
    return list(outs[n:])


def scatter_start(grads, axes, sizes, lands, l):
    n = len(grads)

    def body(*refs):
        ins, land = refs[:n], refs[n:2 * n]
        send, recv = refs[2 * n], refs[2 * n + 1]
        token, loc = refs[-2], refs[-1]
        x, y, c, chips = _place()
        me = 2 * x + y
        mine = [pltpu.make_async_copy(_window(ins[a], axes[a], me, sizes[a]), land[a].at[me, l], loc.at[a])
                for a in range(n)]
        for cp in mine:
            cp.start()
        for a in range(n):
            for k, chip in enumerate(chips):
                pltpu.make_async_remote_copy(
                    src_ref=_window(ins[a], axes[a], 2 * chip[0] + chip[1], sizes[a]), dst_ref=land[a].at[me, l],
                    send_sem=send.at[3 * a + k], recv_sem=recv.at[3 * a + k],
                    device_id=(chip[0], chip[1], c), device_id_type=MESH).start()
        for cp in mine:
            cp.wait()
        token[...] = jnp.zeros_like(token)

    outs = pl.pallas_call(
        body, name=f"scatter_start_{l}",
        in_specs=[HBM_SPEC] * (2 * n),
        out_specs=(SEM_SPEC, SEM_SPEC, *[HBM_SPEC] * (2 * n), VMEM_SPEC),
        out_shape=(pltpu.SemaphoreType.DMA((3 * n,)), pltpu.SemaphoreType.DMA((3 * n,)),
                   *_hbm_like(grads), *_hbm_like(lands), jax.ShapeDtypeStruct((8, 128), F32)),
        input_output_aliases={a: 2 + a for a in range(2 * n)},
        scratch_shapes=[pltpu.SemaphoreType.DMA((n,))],
        compiler_params=pltpu.CompilerParams(has_side_effects=DATAFLOW),
    )(*[_hbm(g) for g in grads], *[_hbm(b) for b in lands])
    return outs[0], outs[1], list(outs[2:2 + n]), list(outs[2 + n:2 + 2 * n]), outs[-1]


def scatter_wait(started, lands, axes, sizes):
    n = len(lands)

    def body(*refs):
        land = refs[:n]
        x, y, c, chips = _place()
        for l in range(L):
            base = n + l * (n + 2)
            ins = refs[base:base + n]
            send_ref, recv_ref = refs[base + n], refs[base + n + 1]
            for a in range(n):
                for k, chip in enumerate(chips):
                    jp = 2 * chip[0] + chip[1]
                    cp = pltpu.make_async_remote_copy(
                        src_ref=_window(ins[a], axes[a], jp, sizes[a]), dst_ref=land[a].at[jp, l],
                        send_sem=send_ref.at[3 * a + k], recv_sem=recv_ref.at[3 * a + k],
                        device_id=(chip[0], chip[1], c), device_id_type=MESH)
                    cp.wait_send()
                    cp.wait_recv()

    operands = list(lands)
    specs = [HBM_SPEC] * n
    for l in range(L):
        send, recv, grads = started[l]
        operands += list(grads) + [send, recv]
        specs += [HBM_SPEC] * n + [SEM_SPEC, SEM_SPEC]
    outs = pl.pallas_call(
        body, name="scatter_wait", in_specs=specs, out_specs=[HBM_SPEC] * n, out_shape=tuple(_hbm_like(lands)),
        input_output_aliases={a: a for a in range(n)},
        compiler_params=pltpu.CompilerParams(has_side_effects=DATAFLOW),
    )(*operands)
    return list(outs)


def sibling_swap(parts):
    n = len(parts)

    def body(*refs):
        ins, outs = refs[:n], refs[n:2 * n]
        send, recv = refs[2 * n:]
        x, y, c, _ = _place()
        cps = [pltpu.make_async_remote_copy(src_ref=ins[a], dst_ref=outs[a], send_sem=send.at[a],
                                            recv_sem=recv.at[a], device_id=(x, y, 1 - c), device_id_type=MESH)
               for a in range(n)]
        for cp in cps:
            cp.start()
        for cp in cps:
            cp.wait_recv()
        for cp in cps:
            cp.wait_send()

    return pl.pallas_call(
        body, name="sibling_swap", in_specs=[ANY] * n, out_specs=[ANY] * n,
        out_shape=[jax.ShapeDtypeStruct(p.shape, p.dtype) for p in parts],
        scratch_shapes=[pltpu.SemaphoreType.DMA((n,)), pltpu.SemaphoreType.DMA((n,))],
    )(*parts)


def small_allreduce(p):
    R, C = p.shape
    ndev = 8

    def body(p_ref, o_ref, buf, send, recv):
        x, y, c, _ = _place()
        me = 4 * x + 2 * y + c
        buf[me] = p_ref[...]

        def peer(d):
            px = 1 - x if d & 4 else x
            py = 1 - y if d & 2 else y
            pc = 1 - c if d & 1 else c
            return px, py, pc

        def copy(d, slot):
            return pltpu.make_async_remote_copy(src_ref=p_ref, dst_ref=buf.at[slot], send_sem=send.at[d - 1],
                                                recv_sem=recv.at[d - 1], device_id=peer(d), device_id_type=MESH)

        sends = [copy(d, me) for d in range(1, ndev)]
        for cp in sends:
            cp.start()
        for d in range(1, ndev):
            px, py, pc = peer(d)
            copy(d, 4 * px + 2 * py + pc).wait_recv()
        for cp in sends:
            cp.wait_send()
        acc = buf[0]
        for k in range(1, ndev):
            acc = acc + buf[k]
        o_ref[...] = acc

    return pl.pallas_call(
        body, name="small_allreduce", in_specs=[VMEM_SPEC], out_specs=VMEM_SPEC,
        out_shape=jax.ShapeDtypeStruct((R, C), F32),
        scratch_shapes=[pltpu.VMEM((ndev, R, C), F32), pltpu.SemaphoreType.DMA((ndev - 1,)),
                        pltpu.SemaphoreType.DMA((ndev - 1,))],
        compiler_params=pltpu.CompilerParams(vmem_limit_bytes=32 << 20),
    )(p)


def kernel(x, norm1_g, w_in, conv_w, conv_b, conv_ln_g, conv_ln_b, ret_gn_g, w_out, norm2_g, w_gate, w_up, w_down, final_g, loss_target, m_norm1_g, m_w_in, m_conv_w, m_conv_b, m_conv_ln_g, m_conv_ln_b, m_ret_gn_g, m_w_out, m_norm2_g, m_w_gate, m_w_up, m_w_down, m_final_g, v_norm1_g, v_w_in, v_conv_w, v_conv_b, v_conv_ln_g, v_conv_ln_b, v_ret_gn_g, v_w_out, v_norm2_g, v_w_gate, v_w_up, v_w_down, v_final_g):
    S = x.shape[1]
    xs = x.reshape(S, D)
    tgt = loss_target.reshape(S, D)
    fsh = FF // NCHIP

    def shards_of(l):
        return [w_in[l].astype(BF), w_out[l].astype(BF), w_gate[l].T.astype(BF), w_up[l].T.astype(BF),
                w_down[l].astype(BF), conv_w[l]]

    gather_axes = [1, 0, 0, 0, 0, 1]
    tables = _ret_tables(S)
    row = lambda a, l: a[l].reshape(1, -1)
    after_token = lambda a, token: a + token[0:1, 0:1]

    started = gather_start(shards_of(0), gather_axes, None, 0)
    weights = [None] * L
    saved = []
    xc = xs
    after = started[4]
    for l in range(L):
        weights[l] = gather_wait(*started[:4], gather_axes, after, l)
        win, wout, wgt, wut, wd, cw = weights[l]
        g1 = row(norm1_g, l)
        if l + 1 < L:
            started = gather_start(shards_of(l + 1), gather_axes, win, l + 1)
            g1 = after_token(g1, started[4])
        proj = in_proj(xc, g1, win, l)
        u1, u = conv_fwd(proj, cw, row(conv_b, l), row(conv_ln_g, l), row(conv_ln_b, l), l)
        r_raw, states = ret_fwd(proj, tables, l)
        x2, mixed = out_proj(u, r_raw, proj, row(ret_gn_g, l), wout, xc, l)
        x3, gs, us = mlp_fwd(x2, row(norm2_g, l), wgt, wut, wd, l)
        saved.append((xc, proj, u1, r_raw, states, mixed, x2, gs, us))
        xc = x3
        after = x3

    dx, loss_acc, d_final = final_loss(xc, final_g.reshape(1, D), tgt)
    loss = lax.psum(loss_acc[0, 0] * (0.5 / D), ("x", "y", "c"))

    scatter_axes = [1, 0, 0, 0, 0]
    scatter_sizes = [INW // NCHIP, D // NCHIP, fsh, fsh, fsh]
    lands = [lax.empty((NCHIP, L, D, INW // NCHIP), BF), lax.empty((NCHIP, L, D // NCHIP, D), BF),
             lax.empty((NCHIP, L, fsh, D), BF), lax.empty((NCHIP, L, fsh, D), BF), lax.empty((NCHIP, L, fsh, D), BF)]
    sent = [None] * L
    token = None
    small = [None] * L
    for l in reversed(range(L)):
        xin, proj, u1, r_raw, states, mixed, x2, gs, us = saved[l]
        win, wout, wgt, wut, wd, cw = weights[l]
        g2 = row(norm2_g, l) if token is None else after_token(row(norm2_g, l), token)
        dx2, dgs, dus, act, h2, d_n2 = mlp_bwd(dx, x2, g2, gs, us, wgt, wut, wd, l)
        g_wd = wgrad(act, dx, l, "wgrad_down")
        g_wgt = wgrad(dgs, h2, l, "wgrad_gate")
        g_wut = wgrad(dus, h2, l, "wgrad_up")
        dgate, dr, du1, sums = out_proj_bwd(dx2, wout, r_raw, proj, row(ret_gn_g, l), u1,
                                            row(conv_ln_g, l), row(conv_ln_b, l), l)
        g_wout = wgrad(mixed, dx2, l, "wgrad_out")
        dab, dwb = conv_bwd(du1, proj, cw, l)
        dq, dk, dv = ret_bwd(dr, proj, states, tables, l)
        dproj = jnp.concatenate([dab, dq, dk, dv, dgate], axis=1)
        dx, h1, d_n1 = in_proj_bwd(dproj, win, xin, row(norm1_g, l), dx2, l)
        g_win = wgrad(h1, dproj, l, "wgrad_in")
        send, recv, grads, lands, token = scatter_start([g_win, g_wout, g_wgt, g_wut, g_wd], scatter_axes,
                                                        scatter_sizes, lands, l)
        sent[l] = (send, recv, grads)
        small[l] = jnp.concatenate([dwb, sums, d_n1.reshape(2, CW), d_n2.reshape(2, CW)], axis=0)
    grad_x = dx.reshape(1, S, D)

    per = CK + 1 + 8 + 4
    packed = jnp.concatenate(small + [d_final.reshape(2, CW), jnp.zeros((6, CW), F32)], axis=0)
    tot = small_allreduce(packed)
    lay = tot[:L * per].reshape(L, per, CW)
    g_conv_w_full = lay[:, 0:CK, :]
    j = 2 * lax.axis_index("x") + lax.axis_index("y")
    g_conv_w = lax.dynamic_slice_in_dim(g_conv_w_full, j * (CW // NCHIP), CW // NCHIP, axis=2)
    g_small = {
        "conv_b": lay[:, CK, :], "ret_gn_g": lay[:, CK + 1, :], "conv_ln_g": lay[:, CK + 2, :],
        "conv_ln_b": lay[:, CK + 3, :], "norm1_g": lay[:, CK + 9:CK + 11, :].reshape(L, D),
        "norm2_g": lay[:, CK + 11:CK + 13, :].reshape(L, D), "final_g": tot[L * per:L * per + 2].reshape(D),
    }

    recv = scatter_wait(sent, lands, scatter_axes, scatter_sizes)
    shard_shapes = [(L * D, INW // NCHIP), (L * D // NCHIP, D), (L * fsh, D), (L * fsh, D), (L * fsh, D)]
    names = ["w_in", "w_out", "w_gate", "w_up", "w_down"]
    parts = [sum_slots(r.reshape((NCHIP,) + shp), f"sum_{nm}") for r, shp, nm in zip(recv, shard_shapes, names)]
    theirs = sibling_swap(parts)

    def unT(a):
        return jnp.swapaxes(a.reshape(L, fsh, D), 1, 2).reshape(L * D, fsh)

    big = {}
    wmv = {"w_in": (w_in, m_w_in, v_w_in), "w_out": (w_out, m_w_out, v_w_out),
           "w_gate": (w_gate, m_w_gate, v_w_gate), "w_up": (w_up, m_w_up, v_w_up),
           "w_down": (w_down, m_w_down, v_w_down)}
    for nm, mine, other in zip(names, parts, theirs):
        w, m, v = wmv[nm]
        if nm in ("w_gate", "w_up"):
            mine, other = unT(mine), unT(other)
        shp2 = (w.shape[0] * w.shape[1], w.shape[2])
        outs = adamw(w.reshape(shp2), mine, other, m.reshape(shp2), v.reshape(shp2), f"adamw_{nm}")
        big[nm] = [o.reshape(w.shape) for o in outs]

    cshape = (L * CK, CW // NCHIP)
    zc = jnp.zeros(cshape, F32)
    big["conv_w"] = [o.reshape(conv_w.shape) for o in adamw(
        conv_w.reshape(cshape), g_conv_w.reshape(cshape), zc, m_conv_w.reshape(cshape),
        v_conv_w.reshape(cshape), "adamw_conv_w")]
    vec_names = ["norm1_g", "conv_b", "conv_ln_g", "conv_ln_b", "ret_gn_g", "norm2_g", "final_g"]
    vec_w = {"norm1_g": (norm1_g, m_norm1_g, v_norm1_g), "conv_b": (conv_b, m_conv_b, v_conv_b),
             "conv_ln_g": (conv_ln_g, m_conv_ln_g, v_conv_ln_g), "conv_ln_b": (conv_ln_b, m_conv_ln_b, v_conv_ln_b),
             "ret_gn_g": (ret_gn_g, m_ret_gn_g, v_ret_gn_g), "norm2_g": (norm2_g, m_norm2_g, v_norm2_g),
             "final_g": (final_g, m_final_g, v_final_g)}
    cat = lambda arrs: jnp.concatenate([a.reshape(-1, CW) for a in arrs], axis=0)
    vw = cat([vec_w[nm][0] for nm in vec_names])
    vm = cat([vec_w[nm][1] for nm in vec_names])
    vv = cat([vec_w[nm][2] for nm in vec_names])
    vg = cat([g_small[nm] for nm in vec_names])
    vouts = adamw(vw, vg, jnp.zeros_like(vg), vm, vv, "adamw_vectors")
    off = 0
    for nm in vec_names:
        w = vec_w[nm][0]
        nrow = w.size // CW
        big[nm] = [o[off:off + nrow].reshape(w.shape) for o in vouts]
        off += nrow

    order = ["norm1_g", "w_in", "conv_w", "conv_b", "conv_ln_g", "conv_ln_b", "ret_gn_g", "w_out", "norm2_g",
             "w_gate", "w_up", "w_down", "final_g"]
    return (loss, grad_x, *[big[nm][0] for nm in order], *[big[nm][1] for nm in order],
            *[big[nm][2] for nm in order], *[big[nm][3] for nm in order])
```

```python
import math

import jax
import jax.numpy as jnp
from jax import lax
from jax.experimental import pallas as pl
from jax.experimental.pallas import tpu as pltpu

D = 1024
L = 4
CW = 512
RW = 512
NH = 4
HD = 128
CK = 31
CHUNK = 64
INW = 3072
FF = 2816
NCHIP = 4
EPS = 1e-6
ROPE_BASE = 10000.0
SEQ_TILE = 512
MLP_ROWS = 1024
MLP_COLS = 256
HALO = 32
CONV_ROWS = 32

ADAM_LR = 0.001
ADAM_B1 = 0.9
ADAM_B2 = 0.999
ADAM_EPS = 1e-08
ADAM_WD = 0.01
ADAM_STEP = 10

BF = jnp.bfloat16
F32 = jnp.float32
MESH = pl.DeviceIdType.MESH
ANY = pl.BlockSpec(memory_space=pl.ANY)
VMEM_SPEC = pl.BlockSpec(memory_space=pltpu.VMEM)
HBM_SPEC = pl.BlockSpec(memory_space=pltpu.HBM)
SEM_SPEC = pl.BlockSpec(memory_space=pltpu.SEMAPHORE)
DATAFLOW = pltpu.SideEffectType.DATAFLOW_SIDE_EFFECTING


def _params(n_grid, vmem_mb):
    return pltpu.CompilerParams(dimension_semantics=("arbitrary",) * n_grid,
                                vmem_limit_bytes=vmem_mb << 20)


def _ordered_call(body, deps, *, in_specs, **kw):
    n, nd = len(in_specs), len(deps)

    def with_deps(*refs):
        body(*refs[:n], *refs[n + nd:])

    return pl.pallas_call(with_deps, in_specs=list(in_specs) + [ANY] * nd, **kw)


def _dot(a, b):
    return jnp.dot(a, b, preferred_element_type=F32)


def _dot_nt(a, b):
    return lax.dot_general(a, b, (((1,), (1,)), ((), ())), preferred_element_type=F32)


def _dot_tn(a, b):
    return lax.dot_general(a, b, (((0,), (0,)), ((), ())), preferred_element_type=F32)


def _sigmoid(x):
    return 0.5 * jnp.tanh(0.5 * x) + 0.5


def _mean(x):
    return jnp.mean(x, axis=-1, keepdims=True)


def _rot(t, cs, sn):
    return t * cs + pltpu.roll(t, HD // 2, 1) * sn


def _rot_t(dy, cs, sn):
    return dy * cs + pltpu.roll(dy * sn, HD // 2, 1)


def _rms_bwd(x, g, dh, dx_in):
    r = lax.rsqrt(_mean(x * x) + EPS)
    xh = x * r
    dxh = dh * g
    dx = dx_in + r * (dxh - xh * _mean(dxh * xh))
    return dx, jnp.sum(dh * xh, axis=0, keepdims=True), (xh * g).astype(BF)


def in_proj(x, g, win, l, deps=()):
    S = x.shape[0]
    tm, tn = min(S, SEQ_TILE), 768

    def body(x_ref, g_ref, w_ref, o_ref, h_scr):
        @pl.when(pl.program_id(1) == 0)
        def _():
            xv = x_ref[...]
            h_scr[...] = (xv * lax.rsqrt(_mean(xv * xv) + EPS) * g_ref[...]).astype(BF)
        o_ref[...] = _dot(h_scr[...], w_ref[...])

    return _ordered_call(
        body, deps, name=f"in_proj_{l}", grid=(S // tm, INW // tn),
        in_specs=[pl.BlockSpec((tm, D), lambda i, j: (i, 0)),
                  pl.BlockSpec((1, D), lambda i, j: (0, 0)),
                  pl.BlockSpec((D, tn), lambda i, j: (0, j))],
        out_specs=pl.BlockSpec((tm, tn), lambda i, j: (i, j)),
        out_shape=jax.ShapeDtypeStruct((S, INW), F32),
        scratch_shapes=[pltpu.VMEM((tm, D), BF)],
        compiler_params=_params(2, 40),
    )(x, g, win, *deps)


def conv_fwd(proj, cw, cb, lg, lb, l, deps=()):
    S = proj.shape[0]
    tc = min(S, SEQ_TILE)

    def body(a_ref, b_ref, w_ref, cb_ref, lg_ref, lb_ref, u1_ref, u_ref, buf):
        i = pl.program_id(0)

        @pl.when(i == 0)
        def _():
            buf[0:HALO, :] = jnp.zeros((HALO, CW), F32)

        @pl.when(i > 0)
        def _():
            buf[0:HALO, :] = buf[tc:tc + HALO, :]

        buf[HALO:HALO + tc, :] = a_ref[...] * _sigmoid(b_ref[...])
        off = HALO - (CK - 1)
        for r0 in range(0, tc, CONV_ROWS):
            acc = jnp.broadcast_to(cb_ref[...], (CONV_ROWS, CW))
            for k in range(CK):
                acc = acc + w_ref[k:k + 1, :] * buf[r0 + off + k:r0 + off + k + CONV_ROWS, :]
            u1_ref[r0:r0 + CONV_ROWS, :] = acc
            d = acc - _mean(acc)
            u2 = d * lax.rsqrt(_mean(d * d) + EPS) * lg_ref[...] + lb_ref[...]
            u_ref[r0:r0 + CONV_ROWS, :] = (u2 * _sigmoid(u2)).astype(BF)

    vec = pl.BlockSpec((1, CW), lambda i: (0, 0))
    return _ordered_call(
        body, deps, name=f"conv_fwd_{l}", grid=(S // tc,),
        in_specs=[pl.BlockSpec((tc, CW), lambda i: (i, 0)),
                  pl.BlockSpec((tc, CW), lambda i: (i, 1)),
                  pl.BlockSpec((CK, CW), lambda i: (0, 0)), vec, vec, vec],
        out_specs=[pl.BlockSpec((tc, CW), lambda i: (i, 0)),
                   pl.BlockSpec((tc, CW), lambda i: (i, 0))],
        out_shape=[jax.ShapeDtypeStruct((S, CW), F32), jax.ShapeDtypeStruct((S, CW), BF)],
        scratch_shapes=[pltpu.VMEM((tc + HALO, CW), F32)],
        compiler_params=_params(1, 32),
    )(proj, proj, cw, cb, lg, lb, *deps)


def _ret_tables(S):
    half = HD // 2
    pos = jnp.arange(S, dtype=F32)
    freqs = ROPE_BASE ** (-jnp.arange(half, dtype=F32) / half)
    ang = pos[:, None] * freqs[None, :]
    cos, sin = jnp.cos(ang), jnp.sin(ang)
    cosf = jnp.concatenate([cos, cos], axis=-1)
    sinf = jnp.concatenate([-sin, sin], axis=-1)
    log_g = jnp.log(1.0 - 2.0 ** (-5.0 - jnp.arange(NH, dtype=F32)))
    idx = jnp.arange(CHUNK, dtype=F32)
    dmat = jnp.exp(log_g[:, None, None] * jnp.abs(idx[:, None] - idx[None, :]))
    qdec = jnp.broadcast_to(jnp.exp(log_g[:, None] * (idx + 1.0))[:, :, None], (NH, CHUNK, HD))
    kdec = jnp.broadcast_to(jnp.exp(log_g[:, None] * (CHUNK - 1 - idx))[:, :, None], (NH, CHUNK, HD))
    cdec = jnp.broadcast_to(jnp.exp(log_g * CHUNK)[:, None, None], (NH, HD, HD))
    return cosf, sinf, dmat, qdec, kdec, cdec


def _ret_specs(tr, tmap):
    q0, k0, v0 = (2 * CW) // HD, (2 * CW + RW) // HD, (2 * CW + 2 * RW) // HD
    return [pl.BlockSpec((tr, HD), lambda h, t: (tmap(t), q0 + h)),
            pl.BlockSpec((tr, HD), lambda h, t: (tmap(t), k0 + h)),
            pl.BlockSpec((tr, HD), lambda h, t: (tmap(t), v0 + h)),
            pl.BlockSpec((tr, HD), lambda h, t: (tmap(t), 0)),
            pl.BlockSpec((tr, HD), lambda h, t: (tmap(t), 0)),
            pl.BlockSpec((None, CHUNK, CHUNK), lambda h, t: (h, 0, 0)),
            pl.BlockSpec((None, CHUNK, HD), lambda h, t: (h, 0, 0)),
            pl.BlockSpec((None, CHUNK, HD), lambda h, t: (h, 0, 0)),
            pl.BlockSpec((None, HD, HD), lambda h, t: (h, 0, 0))]


def ret_fwd(proj, tables, l, deps=()):
    S = proj.shape[0]
    tr = min(S, SEQ_TILE)
    cpb = tr // CHUNK
    scale = HD ** -0.5

    def body(q_ref, k_ref, v_ref, cos_ref, sin_ref, dm_ref, qd_ref, kd_ref, cd_ref,
             r_ref, st_ref, st):
        @pl.when(pl.program_id(1) == 0)
        def _():
            st[...] = jnp.zeros((HD, HD), F32)

        for c in range(cpb):
            rows = slice(c * CHUNK, (c + 1) * CHUNK)
            cs, sn = cos_ref[rows, :], sin_ref[rows, :]
            qr = _rot(q_ref[rows, :], cs, sn)
            kr = _rot(k_ref[rows, :], cs, sn) * scale
            vb = v_ref[rows, :].astype(BF)
            s = st[...]
            st_ref[c] = s
            sc = _dot_nt(qr.astype(BF), kr.astype(BF)) * dm_ref[...]
            out = _dot(sc.astype(BF), vb) + _dot((qr * qd_ref[...]).astype(BF), s.astype(BF))
            r_ref[rows, :] = out
            st[...] = cd_ref[...] * s + _dot_tn((kr * kd_ref[...]).astype(BF), vb)

    return _ordered_call(
        body, deps, name=f"ret_fwd_{l}", grid=(NH, S // tr),
        in_specs=_ret_specs(tr, lambda t: t),
        out_specs=[pl.BlockSpec((tr, HD), lambda h, t: (t, h)),
                   pl.BlockSpec((None, cpb, HD, HD), lambda h, t: (h, t, 0, 0))],
        out_shape=[jax.ShapeDtypeStruct((S, RW), F32),
                   jax.ShapeDtypeStruct((NH, S // CHUNK, HD, HD), F32)],
        scratch_shapes=[pltpu.VMEM((HD, HD), F32)],
        compiler_params=_params(2, 32),
    )(proj, proj, proj, *tables, *deps)


def out_proj(u, r_raw, proj, gn, wout, x, l, deps=()):
    S = x.shape[0]
    tm = min(S, SEQ_TILE)
    gate_blk = (2 * CW + 3 * RW) // RW

    def body(u_ref, r_ref, gate_ref, gn_ref, w_ref, x_ref, x2_ref, mix_ref):
        mix_ref[:, 0:CW] = u_ref[...]
        gt = gate_ref[...]
        sil = gt * _sigmoid(gt) * gn_ref[...]
        for h in range(NH):
            cols = slice(h * HD, (h + 1) * HD)
            rh = r_ref[:, cols]
            d = rh - _mean(rh)
            rn = d * lax.rsqrt(_mean(d * d) + EPS)
            mix_ref[:, CW + h * HD:CW + (h + 1) * HD] = (rn * sil[:, cols]).astype(BF)
        x2_ref[...] = x_ref[...] + _dot(mix_ref[...], w_ref[...])

    return _ordered_call(
        body, deps, name=f"out_proj_{l}", grid=(S // tm,),
        in_specs=[pl.BlockSpec((tm, CW), lambda i: (i, 0)),
                  pl.BlockSpec((tm, RW), lambda i: (i, 0)),
                  pl.BlockSpec((tm, RW), lambda i: (i, gate_blk)),
                  pl.BlockSpec((1, RW), lambda i: (0, 0)),
                  pl.BlockSpec((D, D), lambda i: (0, 0)),
                  pl.BlockSpec((tm, D), lambda i: (i, 0))],
        out_specs=[pl.BlockSpec((tm, D), lambda i: (i, 0)),
                   pl.BlockSpec((tm, D), lambda i: (i, 0))],
        out_shape=[jax.ShapeDtypeStruct((S, D), F32), jax.ShapeDtypeStruct((S, D), BF)],
        compiler_params=_params(1, 40),
    )(u, r_raw, proj, gn, wout, x, *deps)


def mlp_fwd(x2, g2, wgt, wut, wd, l, deps=()):
    S = x2.shape[0]
    tm, tf = min(S, MLP_ROWS), MLP_COLS
    nk = FF // tf

    def body(x_ref, g_ref, wg_ref, wu_ref, wd_ref, o_ref, gs_ref, us_ref, h_scr, acc):
        k = pl.program_id(1)

        @pl.when(k == 0)
        def _():
            xv = x_ref[...]
            h_scr[...] = (xv * lax.rsqrt(_mean(xv * xv) + EPS) * g_ref[...]).astype(BF)
            acc[...] = jnp.zeros((tm, D), F32)

        gv = _dot_nt(h_scr[...], wg_ref[...])
        uv = _dot_nt(h_scr[...], wu_ref[...])
        gs_ref[...] = gv.astype(BF)
        us_ref[...] = uv.astype(BF)
        acc[...] += _dot((gv * _sigmoid(gv) * uv).astype(BF), wd_ref[...])

        @pl.when(k == nk - 1)
        def _():
            o_ref[...] = x_ref[...] + acc[...]

    wspec = pl.BlockSpec((tf, D), lambda i, k: (k, 0))
    return _ordered_call(
        body, deps, name=f"mlp_fwd_{l}", grid=(S // tm, nk),
        in_specs=[pl.BlockSpec((tm, D), lambda i, k: (i, 0)),
                  pl.BlockSpec((1, D), lambda i, k: (0, 0)), wspec, wspec, wspec],
        out_specs=[pl.BlockSpec((tm, D), lambda i, k: (i, 0)),
                   pl.BlockSpec((tm, tf), lambda i, k: (i, k)),
                   pl.BlockSpec((tm, tf), lambda i, k: (i, k))],
        out_shape=[jax.ShapeDtypeStruct((S, D), F32), jax.ShapeDtypeStruct((S, FF), BF),
                   jax.ShapeDtypeStruct((S, FF), BF)],
        scratch_shapes=[pltpu.VMEM((tm, D), BF), pltpu.VMEM((tm, D), F32)],
        compiler_params=_params(2, 56),
    )(x2, g2, wgt, wut, wd, *deps)


def final_loss(x, gf, tgt):
    S = x.shape[0]
    tm = min(S, SEQ_TILE)

    def body(x_ref, g_ref, t_ref, dx_ref, loss_ref, dg_ref):
        @pl.when(pl.program_id(0) == 0)
        def _():
            loss_ref[...] = jnp.zeros((8, 128), F32)
            dg_ref[...] = jnp.zeros((1, D), F32)

        xv = x_ref[...]
        r = lax.rsqrt(_mean(xv * xv) + EPS)
        xh = xv * r
        diff = xh * g_ref[...] - t_ref[...]
        loss_ref[...] += jnp.sum(jnp.sum(diff * diff, axis=-1, keepdims=True), axis=0, keepdims=True)
        dy = diff * (1.0 / D)
        dg_ref[...] += jnp.sum(dy * xh, axis=0, keepdims=True)
        dxh = dy * g_ref[...]
        dx_ref[...] = r * (dxh - xh * _mean(dxh * xh))

    return pl.pallas_call(
        body, name="final_loss", grid=(S // tm,),
        in_specs=[pl.BlockSpec((tm, D), lambda i: (i, 0)),
                  pl.BlockSpec((1, D), lambda i: (0, 0)),
                  pl.BlockSpec((tm, D), lambda i: (i, 0))],
        out_specs=[pl.BlockSpec((tm, D), lambda i: (i, 0)),
                   pl.BlockSpec((8, 128), lambda i: (0, 0)),
                   pl.BlockSpec((1, D), lambda i: (0, 0))],
        out_shape=[jax.ShapeDtypeStruct((S, D), F32), jax.ShapeDtypeStruct((8, 128), F32),
                   jax.ShapeDtypeStruct((1, D), F32)],
        compiler_params=_params(1, 40),
    )(x, gf, tgt)


def mlp_bwd(dx3, x2, g2, gs, us, wgt, wut, wd, l, deps=()):
    S = x2.shape[0]
    tm, tf = min(S, MLP_ROWS), MLP_COLS
    nk = FF // tf

    def body(dx_ref, x_ref, g_ref, gs_ref, us_ref, wg_ref, wu_ref, wd_ref,
             dx2_ref, dg_ref, du_ref, a_ref, h_ref, dgain_ref, dxb, acc):
        i, k = pl.program_id(0), pl.program_id(1)

        @pl.when(k == 0)
        def _():
            dxb[...] = dx_ref[...].astype(BF)
            acc[...] = jnp.zeros((tm, D), F32)

        @pl.when((k == 0) & (i == 0))
        def _():
            dgain_ref[...] = jnp.zeros((1, D), F32)

        da = _dot_nt(dxb[...], wd_ref[...])
        gv = gs_ref[...].astype(F32)
        uv = us_ref[...].astype(F32)
        sg = _sigmoid(gv)
        sil = gv * sg
        dgv = (da * uv * (sg * (1.0 + gv * (1.0 - sg)))).astype(BF)
        duv = (da * sil).astype(BF)
        a_ref[...] = (sil * uv).astype(BF)
        dg_ref[...] = dgv
        du_ref[...] = duv
        acc[...] += _dot(dgv, wg_ref[...]) + _dot(duv, wu_ref[...])

        @pl.when(k == nk - 1)
        def _():
            dx2, dgain, hb = _rms_bwd(x_ref[...], g_ref[...], acc[...], dx_ref[...])
            dx2_ref[...] = dx2
            dgain_ref[...] += dgain
            h_ref[...] = hb

    wspec = pl.BlockSpec((tf, D), lambda i, k: (k, 0))
    row = pl.BlockSpec((tm, D), lambda i, k: (i, 0))
    wide = pl.BlockSpec((tm, tf), lambda i, k: (i, k))
    return _ordered_call(
        body, deps, name=f"mlp_bwd_{l}", grid=(S // tm, nk),
        in_specs=[row, row, pl.BlockSpec((1, D), lambda i, k: (0, 0)), wide, wide, wspec, wspec, wspec],
        out_specs=[row, wide, wide, wide, row, pl.BlockSpec((1, D), lambda i, k: (0, 0))],
        out_shape=[jax.ShapeDtypeStruct((S, D), F32), jax.ShapeDtypeStruct((S, FF), BF),
                   jax.ShapeDtypeStruct((S, FF), BF), jax.ShapeDtypeStruct((S, FF), BF),
                   jax.ShapeDtypeStruct((S, D), BF), jax.ShapeDtypeStruct((1, D), F32)],
        scratch_shapes=[pltpu.VMEM((tm, D), BF), pltpu.VMEM((tm, D), F32)],
        compiler_params=_params(2, 56),
    )(dx3, x2, g2, gs, us, wgt, wut, wd, *deps)


def wgrad(a, b, l, name, deps=()):
    S, K = a.shape
    N = b.shape[1]
    tk = 1408 if K == FF else min(K, 1024)
    tn = 768 if N == INW else min(N, 1024)
    ts = min(S, SEQ_TILE)
    ns = S // ts

    def body(a_ref, b_ref, o_ref, acc):
        s = pl.program_id(2)

        @pl.when(s == 0)
        def _():
            acc[...] = jnp.zeros((tk, tn), F32)

        acc[...] += _dot_tn(a_ref[...], b_ref[...].astype(BF))

        @pl.when(s == ns - 1)
        def _():
            o_ref[...] = acc[...].astype(BF)

    return _ordered_call(
        body, deps, name=f"{name}_{l}", grid=(K // tk, N // tn, ns),
        in_specs=[pl.BlockSpec((ts, tk), lambda i, j, s: (s, i)),
                  pl.BlockSpec((ts, tn), lambda i, j, s: (s, j))],
        out_specs=pl.BlockSpec((tk, tn), lambda i, j, s: (i, j)),
        out_shape=jax.ShapeDtypeStruct((K, N), BF),
        scratch_shapes=[pltpu.VMEM((tk, tn), F32)],
        compiler_params=_params(3, 48),
    )(a, b, *deps)


def out_proj_bwd(dx2, wout, r_raw, proj, gn, u1, lg, lb, l, deps=()):
    S = dx2.shape[0]
    tm = min(S, SEQ_TILE)
    gate_blk = (2 * CW + 3 * RW) // RW

    def body(dx_ref, w_ref, r_ref, gate_ref, gn_ref, u1_ref, lg_ref, lb_ref,
             dgate_ref, dr_ref, du1_ref, sums_ref):
        @pl.when(pl.program_id(0) == 0)
        def _():
            sums_ref[...] = jnp.zeros((8, CW), F32)

        dmix = _dot_nt(dx_ref[...].astype(BF), w_ref[...])
        gt = gate_ref[...]
        sg = _sigmoid(gt)
        sil = gt * sg
        dsil = sg * (1.0 + gt * (1.0 - sg))
        for h in range(NH):
            cols = slice(h * HD, (h + 1) * HD)
            rh = r_ref[:, cols]
            d = rh - _mean(rh)
            rs = lax.rsqrt(_mean(d * d) + EPS)
            rn = d * rs
            drr = dmix[:, CW + h * HD:CW + (h + 1) * HD]
            gnh = gn_ref[:, cols]
            sums_ref[0:1, cols] += jnp.sum(drr * rn * sil[:, cols], axis=0, keepdims=True)
            dgate_ref[:, cols] = (drr * rn * gnh * dsil[:, cols]).astype(BF)
            drn = drr * gnh * sil[:, cols]
            dr_ref[:, cols] = rs * (drn - _mean(drn) - rn * _mean(drn * rn))
        du = dmix[:, 0:CW]
        u1 = u1_ref[...]
        d = u1 - _mean(u1)
        rs = lax.rsqrt(_mean(d * d) + EPS)
        xh = d * rs
        u2 = xh * lg_ref[...] + lb_ref[...]
        sg2 = _sigmoid(u2)
        du2 = du * (sg2 * (1.0 + u2 * (1.0 - sg2)))
        sums_ref[1:2, :] += jnp.sum(du2 * xh, axis=0, keepdims=True)
        sums_ref[2:3, :] += jnp.sum(du2, axis=0, keepdims=True)
        dxh = du2 * lg_ref[...]
        du1_ref[...] = rs * (dxh - _mean(dxh) - xh * _mean(dxh * xh))

    vec = pl.BlockSpec((1, CW), lambda i: (0, 0))
    half = pl.BlockSpec((tm, CW), lambda i: (i, 0))
    return _ordered_call(
        body, deps, name=f"out_proj_bwd_{l}", grid=(S // tm,),
        in_specs=[pl.BlockSpec((tm, D), lambda i: (i, 0)),
                  pl.BlockSpec((D, D), lambda i: (0, 0)),
                  half, pl.BlockSpec((tm, RW), lambda i: (i, gate_blk)), vec, half, vec, vec],
        out_specs=[half, half, half, pl.BlockSpec((8, CW), lambda i: (0, 0))],
        out_shape=[jax.ShapeDtypeStruct((S, RW), BF), jax.ShapeDtypeStruct((S, RW), F32),
                   jax.ShapeDtypeStruct((S, CW), F32), jax.ShapeDtypeStruct((8, CW), F32)],
        compiler_params=_params(1, 40),
    )(dx2, wout, r_raw, proj, gn, u1, lg, lb, *deps)


def conv_bwd(du1, proj, cw, l, deps=()):
    S = proj.shape[0]
    tc = min(S, SEQ_TILE)
    nt = S // tc

    def body(du1_ref, a_ref, b_ref, w_ref, dab_ref, dwb_ref, buf):
        i = pl.program_id(0)

        @pl.when(i == 0)
        def _():
            buf[tc:tc + HALO, :] = jnp.zeros((HALO, CW), F32)
            dwb_ref[...] = jnp.zeros((CK + 1, CW), F32)

        @pl.when(i > 0)
        def _():
            buf[tc:tc + HALO, :] = buf[0:HALO, :]

        buf[0:tc, :] = du1_ref[...]
        parts = [jnp.zeros((8, CW), F32) for _ in range(CK)]
        for r0 in range(0, tc, CONV_ROWS):
            av = a_ref[r0:r0 + CONV_ROWS, :]
            sgb = _sigmoid(b_ref[r0:r0 + CONV_ROWS, :])
            u0 = av * sgb
            acc = jnp.zeros((CONV_ROWS, CW), F32)
            for j in range(CK):
                sl = buf[r0 + j:r0 + j + CONV_ROWS, :]
                acc = acc + w_ref[CK - 1 - j:CK - j, :] * sl
                pr = u0 * sl
                red = pr[0:8, :]
                for q in range(1, CONV_ROWS // 8):
                    red = red + pr[8 * q:8 * q + 8, :]
                parts[CK - 1 - j] = parts[CK - 1 - j] + red
            dab_ref[r0:r0 + CONV_ROWS, 0:CW] = (acc * sgb).astype(BF)
            dab_ref[r0:r0 + CONV_ROWS, CW:2 * CW] = (acc * av * sgb * (1.0 - sgb)).astype(BF)
        for k in range(CK):
            dwb_ref[k:k + 1, :] += jnp.sum(parts[k], axis=0, keepdims=True)
        dwb_ref[CK:CK + 1, :] += jnp.sum(du1_ref[...], axis=0, keepdims=True)

    return _ordered_call(
        body, deps, name=f"conv_bwd_{l}", grid=(nt,),
        in_specs=[pl.BlockSpec((tc, CW), lambda i: (nt - 1 - i, 0)),
                  pl.BlockSpec((tc, CW), lambda i: (nt - 1 - i, 0)),
                  pl.BlockSpec((tc, CW), lambda i: (nt - 1 - i, 1)),
                  pl.BlockSpec((CK, CW), lambda i: (0, 0))],
        out_specs=[pl.BlockSpec((tc, 2 * CW), lambda i: (nt - 1 - i, 0)),
                   pl.BlockSpec((CK + 1, CW), lambda i: (0, 0))],
        out_shape=[jax.ShapeDtypeStruct((S, 2 * CW), BF), jax.ShapeDtypeStruct((CK + 1, CW), F32)],
        scratch_shapes=[pltpu.VMEM((tc + HALO, CW), F32)],
        compiler_params=_params(1, 32),
    )(du1, proj, proj, cw, *deps)


def ret_bwd(dr, proj, states, tables, l):
    S = proj.shape[0]
    tr = min(S, SEQ_TILE)
    cpb = tr // CHUNK
    nt = S // tr
    scale = HD ** -0.5

    def body(q_ref, k_ref, v_ref, cos_ref, sin_ref, dm_ref, qd_ref, kd_ref, cd_ref, dr_ref, st_ref,
             dq_ref, dk_ref, dv_ref, gst):
        @pl.when(pl.program_id(1) == 0)
        def _():
            gst[...] = jnp.zeros((HD, HD), F32)

        for c in reversed(range(cpb)):
            rows = slice(c * CHUNK, (c + 1) * CHUNK)
            cs, sn = cos_ref[rows, :], sin_ref[rows, :]
            qr = _rot(q_ref[rows, :], cs, sn)
            kr = _rot(k_ref[rows, :], cs, sn) * scale
            qb, kb = qr.astype(BF), kr.astype(BF)
            vb = v_ref[rows, :].astype(BF)
            dob = dr_ref[rows, :].astype(BF)
            sb = st_ref[c].astype(BF)
            gn1 = gst[...]
            gb = gn1.astype(BF)
            sc = (_dot_nt(qb, kb) * dm_ref[...]).astype(BF)
            dsc = (_dot_nt(dob, vb) * dm_ref[...]).astype(BF)
            dqr = _dot(dsc, kb) + _dot_nt(dob, sb) * qd_ref[...]
            dkr = _dot_tn(dsc, qb) + _dot_nt(vb, gb) * kd_ref[...]
            dvv = _dot_tn(sc, dob) + _dot((kr * kd_ref[...]).astype(BF), gb)
            gst[...] = cd_ref[...] * gn1 + _dot_tn((qr * qd_ref[...]).astype(BF), dob)
            dq_ref[rows, :] = _rot_t(dqr, cs, sn).astype(BF)
            dk_ref[rows, :] = _rot_t(dkr * scale, cs, sn).astype(BF)
            dv_ref[rows, :] = dvv.astype(BF)

    rev = lambda t: nt - 1 - t
    hblk = pl.BlockSpec((tr, HD), lambda h, t: (rev(t), h))
    return pl.pallas_call(
        body, name=f"ret_bwd_{l}", grid=(NH, nt),
        in_specs=_ret_specs(tr, rev) + [hblk, pl.BlockSpec((None, cpb, HD, HD), lambda h, t: (h, rev(t), 0, 0))],
        out_specs=[hblk, hblk, hblk],
        out_shape=[jax.ShapeDtypeStruct((S, RW), BF)] * 3,
        scratch_shapes=[pltpu.VMEM((HD, HD), F32)],
        compiler_params=_params(2, 32),
    )(proj, proj, proj, *tables, dr, states)


def in_proj_bwd(dproj, win, x, g, dx2, l):
    S = x.shape[0]
    tm, tk = min(S, SEQ_TILE), 768
    nk = INW // tk

    def body(dp_ref, w_ref, x_ref, g_ref, dx2_ref, dx_ref, h_ref, dgain_ref, acc):
        i, k = pl.program_id(0), pl.program_id(1)

        @pl.when(k == 0)
        def _():
            acc[...] = jnp.zeros((tm, D), F32)

        @pl.when((k == 0) & (i == 0))
        def _():
            dgain_ref[...] = jnp.zeros((1, D), F32)

        acc[...] += _dot_nt(dp_ref[...], w_ref[...])

        @pl.when(k == nk - 1)
        def _():
            dx, dgain, hb = _rms_bwd(x_ref[...], g_ref[...], acc[...], dx2_ref[...])
            dx_ref[...] = dx
            dgain_ref[...] += dgain
            h_ref[...] = hb

    row = pl.BlockSpec((tm, D), lambda i, k: (i, 0))
    vec = pl.BlockSpec((1, D), lambda i, k: (0, 0))
    return pl.pallas_call(
        body, name=f"in_proj_bwd_{l}", grid=(S // tm, nk),
        in_specs=[pl.BlockSpec((tm, tk), lambda i, k: (i, k)),
                  pl.BlockSpec((D, tk), lambda i, k: (0, k)), row, vec, row],
        out_specs=[row, row, vec],
        out_shape=[jax.ShapeDtypeStruct((S, D), F32), jax.ShapeDtypeStruct((S, D), BF),
                   jax.ShapeDtypeStruct((1, D), F32)],
        scratch_shapes=[pltpu.VMEM((tm, D), F32)],
        compiler_params=_params(2, 40),
    )(dproj, win, x, g, dx2)


def sum_slots(recv, name):
    _, R, C = recv.shape
    tr = 256 if R % 256 == 0 else R

    def body(r_ref, o_ref):
        acc = r_ref[0].astype(F32)
        for k in range(1, NCHIP):
            acc = acc + r_ref[k].astype(F32)
        o_ref[...] = acc

    return pl.pallas_call(
        body, name=name, grid=(R // tr,),
        in_specs=[pl.BlockSpec((NCHIP, tr, C), lambda i: (0, i, 0))],
        out_specs=pl.BlockSpec((tr, C), lambda i: (i, 0)),
        out_shape=jax.ShapeDtypeStruct((R, C), F32),
        compiler_params=_params(1, 32),
    )(recv)


def adamw(w, ga, gb, m, v, name):
    R, C = w.shape
    tr = 256 if R % 256 == 0 else R
    c1 = 1.0 - ADAM_B1 ** ADAM_STEP
    c2 = 1.0 - ADAM_B2 ** ADAM_STEP

    def body(w_ref, ga_ref, gb_ref, m_ref, v_ref, g_out, d_out, m_out, v_out):
        g = ga_ref[...] + gb_ref[...]
        mn = ADAM_B1 * m_ref[...] + (1.0 - ADAM_B1) * g
        vn = ADAM_B2 * v_ref[...] + (1.0 - ADAM_B2) * (g * g)
        g_out[...] = g
        m_out[...] = mn
        v_out[...] = vn
        d_out[...] = -ADAM_LR * ((mn / c1) / (jnp.sqrt(vn / c2) + ADAM_EPS) + ADAM_WD * w_ref[...])

    blk = pl.BlockSpec((tr, C), lambda i: (i, 0))
    return pl.pallas_call(
        body, name=name, grid=(R // tr,),
        in_specs=[blk] * 5, out_specs=[blk] * 4,
        out_shape=[jax.ShapeDtypeStruct((R, C), F32)] * 4,
        compiler_params=_params(1, 40),
    )(w, ga, gb, m, v)


def _place():
    x, y, c = lax.axis_index("x"), lax.axis_index("y"), lax.axis_index("c")
    chips = [(1 - x, y), (x, 1 - y), (1 - x, 1 - y)]
    return x, y, c, chips


def _window(ref, axis, j, size):
    idx = [slice(None)] * len(ref.shape)
    idx[axis] = pl.ds(pl.multiple_of(j * size, 128 if axis == len(ref.shape) - 1 else 16), size)
    return ref.at[tuple(idx)]


def _hbm(a):
    return pltpu.with_memory_space_constraint(a, pltpu.HBM)


def _hbm_like(arrs):
    return [pltpu.HBM(a.shape, a.dtype) for a in arrs]


def gather_start(shards, axes, after, tag):
    n = len(shards)
    na = len(after)
    lands = []
    for s, ax in zip(shards, axes):
        shp = list(s.shape)
        shp[ax] *= NCHIP
        lands.append(lax.empty(tuple(shp), s.dtype))

    def body(*refs):
        ins, land = refs[:n], refs[n:2 * n]
        send, recv = refs[2 * n + na], refs[2 * n + na + 1]
        token, loc = refs[-2], refs[-1]
        x, y, c, chips = _place()

        def spot(a):
            return _window(land[a], axes[a], 2 * x + y, ins[a].shape[axes[a]])

        mine = [pltpu.make_async_copy(ins[a], spot(a), loc.at[a]) for a in range(n)]
        for cp in mine:
            cp.start()
        for a in range(n):
            for k, chip in enumerate(chips):
                pltpu.make_async_remote_copy(
                    src_ref=ins[a], dst_ref=spot(a), send_sem=send.at[3 * a + k], recv_sem=recv.at[3 * a + k],
                    device_id=(chip[0], chip[1], c), device_id_type=MESH).start()
        for cp in mine:
            cp.wait()
        token[...] = jnp.zeros_like(token)

    outs = pl.pallas_call(
        body, name=f"gather_start_{tag}",
        in_specs=[HBM_SPEC] * (2 * n) + [ANY] * na,
        out_specs=(SEM_SPEC, SEM_SPEC, *[HBM_SPEC] * (2 * n), VMEM_SPEC),
        out_shape=(pltpu.SemaphoreType.DMA((3 * n,)), pltpu.SemaphoreType.DMA((3 * n,)),
                   *_hbm_like(shards), *_hbm_like(lands), jax.ShapeDtypeStruct((8, 128), F32)),
        input_output_aliases={a: 2 + a for a in range(2 * n)},
        scratch_shapes=[pltpu.SemaphoreType.DMA((n,))],
        compiler_params=pltpu.CompilerParams(has_side_effects=DATAFLOW),
    )(*[_hbm(s) for s in shards], *[_hbm(b) for b in lands], *after)
    return (outs[0], outs[1], list(outs[2:2 + n]), list(outs[2 + n:2 + 2 * n]), list(axes)), outs[-1]


def gather_wait(groups, after, tag):
    sizes = [len(g[2]) for g in groups]
    total = sum(sizes)

    def body(*refs):
        x, y, c, chips = _place()
        pos = 2 * total
        off = 0
        for g, n in zip(groups, sizes):
            ins, land = refs[off:off + n], refs[total + off:total + off + n]
            send_ref, recv_ref = refs[pos], refs[pos + 1]
            axes = g[4]
            for a in range(n):
                for k, chip in enumerate(chips):
                    cp = pltpu.make_async_remote_copy(
                        src_ref=ins[a],
                        dst_ref=_window(land[a], axes[a], 2 * chip[0] + chip[1], ins[a].shape[axes[a]]),
                        send_sem=send_ref.at[3 * a + k], recv_sem=recv_ref.at[3 * a + k],
                        device_id=(chip[0], chip[1], c), device_id_type=MESH)
                    cp.wait_send()
                    cp.wait_recv()
            pos += 2
            off += n

    shards = [s for g in groups for s in g[2]]
    lands = [b for g in groups for b in g[3]]
    sems = [s for g in groups for s in (g[0], g[1])]
    outs = pl.pallas_call(
        body, name=f"gather_wait_{tag}",
        in_specs=[HBM_SPEC] * (2 * total) + [SEM_SPEC] * len(sems) + [ANY],
        out_specs=[HBM_SPEC] * (2 * total),
        out_shape=(*_hbm_like(shards), *_hbm_like(lands)),
        input_output_aliases={a: a for a in range(2 * total)},
        compiler_params=pltpu.CompilerParams(has_side_effects=DATAFLOW),
    )(*shards, *lands, *sems, after)
    return list(outs[total:])


def scatter_start(grads, axes, sizes, lands, l, tag):
    n = len(grads)

    def body(*refs):
        ins, land = refs[:n], refs[n:2 * n]
        send, recv = refs[2 * n], refs[2 * n + 1]
        token, loc = refs[-2], refs[-1]
        x, y, c, chips = _place()
        me = 2 * x + y
        mine = [pltpu.make_async_copy(_window(ins[a], axes[a], me, sizes[a]), land[a].at[me, l], loc.at[a])
                for a in range(n)]
        for cp in mine:
            cp.start()
        for a in range(n):
            for k, chip in enumerate(chips):
                pltpu.make_async_remote_copy(
                    src_ref=_window(ins[a], axes[a], 2 * chip[0] + chip[1], sizes[a]), dst_ref=land[a].at[me, l],
                    send_sem=send.at[3 * a + k], recv_sem=recv.at[3 * a + k],
                    device_id=(chip[0], chip[1], c), device_id_type=MESH).start()
        for cp in mine:
            cp.wait()
        token[...] = jnp.zeros_like(token)

    outs = pl.pallas_call(
        body, name=f"scatter_start_{tag}",
        in_specs=[HBM_SPEC] * (2 * n),
        out_specs=(SEM_SPEC, SEM_SPEC, *[HBM_SPEC] * (2 * n), VMEM_SPEC),
        out_shape=(pltpu.SemaphoreType.DMA((3 * n,)), pltpu.SemaphoreType.DMA((3 * n,)),
                   *_hbm_like(grads), *_hbm_like(lands), jax.ShapeDtypeStruct((8, 128), F32)),
        input_output_aliases={a: 2 + a for a in range(2 * n)},
        scratch_shapes=[pltpu.SemaphoreType.DMA((n,))],
        compiler_params=pltpu.CompilerParams(has_side_effects=DATAFLOW),
    )(*[_hbm(g) for g in grads], *[_hbm(b) for b in lands])
    group = (outs[0], outs[1], list(outs[2:2 + n]), list(axes), list(sizes), l)
    return group, list(outs[2 + n:2 + 2 * n]), outs[-1]


def scatter_wait(groups, lands, which):
    nl = len(lands)

    def body(*refs):
        land = refs[:nl]
        x, y, c, chips = _place()
        pos = nl
        for g, wh in zip(groups, which):
            n = len(g[2])
            ins = refs[pos:pos + n]
            send_ref, recv_ref = refs[pos + n], refs[pos + n + 1]
            axes, sizes, l = g[3], g[4], g[5]
            for a in range(n):
                for k, chip in enumerate(chips):
                    jp = 2 * chip[0] + chip[1]
                    cp = pltpu.make_async_remote_copy(
                        src_ref=_window(ins[a], axes[a], jp, sizes[a]), dst_ref=land[wh[a]].at[jp, l],
                        send_sem=send_ref.at[3 * a + k], recv_sem=recv_ref.at[3 * a + k],
                        device_id=(chip[0], chip[1], c), device_id_type=MESH)
                    cp.wait_send()
                    cp.wait_recv()
            pos += n + 2

    operands = list(lands)
    specs = [HBM_SPEC] * nl
    for g in groups:
        operands += list(g[2]) + [g[0], g[1]]
        specs += [HBM_SPEC] * len(g[2]) + [SEM_SPEC, SEM_SPEC]
    outs = pl.pallas_call(
        body, name="scatter_wait", in_specs=specs, out_specs=[HBM_SPEC] * nl, out_shape=tuple(_hbm_like(lands)),
        input_output_aliases={a: a for a in range(nl)},
        compiler_params=pltpu.CompilerParams(has_side_effects=DATAFLOW),
    )(*operands)
    return list(outs)


def sibling_swap(parts):
    n = len(parts)

    def body(*refs):
        ins, outs = refs[:n], refs[n:2 * n]
        send, recv = refs[2 * n:]
        x, y, c, _ = _place()
        cps = [pltpu.make_async_remote_copy(src_ref=ins[a], dst_ref=outs[a], send_sem=send.at[a],
                                            recv_sem=recv.at[a], device_id=(x, y, 1 - c), device_id_type=MESH)
               for a in range(n)]
        for cp in cps:
            cp.start()
        for cp in cps:
            cp.wait_recv()
        for cp in cps:
            cp.wait_send()

    return pl.pallas_call(
        body, name="sibling_swap", in_specs=[ANY] * n, out_specs=[ANY] * n,
        out_shape=[jax.ShapeDtypeStruct(p.shape, p.dtype) for p in parts],
        scratch_shapes=[pltpu.SemaphoreType.DMA((n,)), pltpu.SemaphoreType.DMA((n,))],
    )(*parts)


def small_allreduce(p):
    R, C = p.shape
    ndev = 8

    def body(p_ref, o_ref, buf, send, recv):
        x, y, c, _ = _place()
        me = 4 * x + 2 * y + c
        buf[me] = p_ref[...]

        def peer(d):
            px = 1 - x if d & 4 else x
            py = 1 - y if d & 2 else y
            pc = 1 - c if d & 1 else c
            return px, py, pc

        def copy(d, slot):
            return pltpu.make_async_remote_copy(src_ref=p_ref, dst_ref=buf.at[slot], send_sem=send.at[d - 1],
                                                recv_sem=recv.at[d - 1], device_id=peer(d), device_id_type=MESH)

        sends = [copy(d, me) for d in range(1, ndev)]
        for cp in sends:
            cp.start()
        for d in range(1, ndev):
            px, py, pc = peer(d)
            copy(d, 4 * px + 2 * py + pc).wait_recv()
        for cp in sends:
            cp.wait_send()
        acc = buf[0]
        for k in range(1, ndev):
            acc = acc + buf[k]
        o_ref[...] = acc

    return pl.pallas_call(
        body, name="small_allreduce", in_specs=[VMEM_SPEC], out_specs=VMEM_SPEC,
        out_shape=jax.ShapeDtypeStruct((R, C), F32),
        scratch_shapes=[pltpu.VMEM((ndev, R, C), F32), pltpu.SemaphoreType.DMA((ndev - 1,)),
                        pltpu.SemaphoreType.DMA((ndev - 1,))],
        compiler_params=pltpu.CompilerParams(vmem_limit_bytes=32 << 20),
    )(p)


def kernel(x, norm1_g, w_in, conv_w, conv_b, conv_ln_g, conv_ln_b, ret_gn_g, w_out, norm2_g, w_gate, w_up, w_down, final_g, loss_target, m_norm1_g, m_w_in, m_conv_w, m_conv_b, m_conv_ln_g, m_conv_ln_b, m_ret_gn_g, m_w_out, m_norm2_g, m_w_gate, m_w_up, m_w_down, m_final_g, v_norm1_g, v_w_in, v_conv_w, v_conv_b, v_conv_ln_g, v_conv_ln_b, v_ret_gn_g, v_w_out, v_norm2_g, v_w_gate, v_w_up, v_w_down, v_final_g):
    S = x.shape[1]
    xs = x.reshape(S, D)
    tgt = loss_target.reshape(S, D)
    fsh = FF // NCHIP

    def shards_of(l):
        return [w_in[l].astype(BF), w_out[l].astype(BF), w_gate[l].T.astype(BF), w_up[l].T.astype(BF),
                w_down[l].astype(BF), conv_w[l]]

    gather_axes = [1, 0, 0, 0, 0, 1]
    shard_cache = [shards_of(l) for l in range(L)]
    tables = _ret_tables(S)
    row = lambda a, l: a[l].reshape(1, -1)
    groups = {}
    weights = [dict() for _ in range(L)]

    def begin(l, which, after):
        group, token = gather_start([shard_cache[l][i] for i in which], [gather_axes[i] for i in which],
                                    after, f"{l}_{which[0]}")
        groups[(l, which[0])] = (group, which)
        return token

    def finish(l, firsts, after, tag):
        gs = [groups[(l, f)] for f in firsts]
        outs = gather_wait([g for g, _ in gs], after, f"{l}_{tag}")
        k = 0
        for _, which in gs:
            for i in which:
                weights[l][i] = outs[k]
                k += 1

    def host(stage, after):
        tokens = []
        for gl, which in stage:
            tokens.append(begin(gl, which, [after] + tokens))
        return tokens

    def hosted(l):
        if l == 0:
            return {"in": [(0, [2])], "conv": [(0, [3])], "ret": [(0, [4])], "out": [(1, [0])],
                    "mlp": [(1, [1, 5]), (1, [2])]}
        nxt = {"ret": [(l + 1, [0])], "out": [(l + 1, [1, 5])], "mlp": [(l + 1, [2])]} if l + 1 < L else {}
        return {"in": [(l, [3])], "conv": [(l, [4])], **nxt}

    first = begin(0, [0], [])
    after = begin(0, [1, 5], [first])
    saved = []
    xc = xs
    for l in range(L):
        sched = hosted(l)
        finish(l, [0, 1], after, "a")
        win, wout, cw = weights[l][0], weights[l][1], weights[l][5]
        proj = in_proj(xc, row(norm1_g, l), win, l, host(sched.get("in", []), win))
        u1, u = conv_fwd(proj, cw, row(conv_b, l), row(conv_ln_g, l), row(conv_ln_b, l), l,
                         host(sched.get("conv", []), proj))
        r_raw, states = ret_fwd(proj, tables, l, host(sched.get("ret", []), u))
        x2, mixed = out_proj(u, r_raw, proj, row(ret_gn_g, l), wout, xc, l, host(sched.get("out", []), r_raw))
        finish(l, [2, 3, 4], x2, "b")
        wgt, wut, wd = weights[l][2], weights[l][3], weights[l][4]
        x3, gs, us = mlp_fwd(x2, row(norm2_g, l), wgt, wut, wd, l, host(sched.get("mlp", []), wd))
        saved.append((xc, proj, u1, r_raw, states, mixed, x2, gs, us))
        xc = x3
        after = x3

    dx, loss_acc, d_final = final_loss(xc, final_g.reshape(1, D), tgt)
    loss = lax.psum(loss_acc[0, 0] * (0.5 / D), ("x", "y", "c"))

    scatter_axes = [1, 0, 0, 0, 0]
    scatter_sizes = [INW // NCHIP, D // NCHIP, fsh, fsh, fsh]
    lands = [lax.empty((NCHIP, L, D, INW // NCHIP), BF), lax.empty((NCHIP, L, D // NCHIP, D), BF),
             lax.empty((NCHIP, L, fsh, D), BF), lax.empty((NCHIP, L, fsh, D), BF), lax.empty((NCHIP, L, fsh, D), BF)]
    sent, sent_which = [], []

    def send_grad(g, a, l):
        group, new_land, token = scatter_start([g], [scatter_axes[a]], [scatter_sizes[a]], [lands[a]], l, f"{l}_{a}")
        lands[a] = new_land[0]
        sent.append(group)
        sent_which.append([a])
        return [token]

    small = [None] * L
    dep = []
    for l in reversed(range(L)):
        xin, proj, u1, r_raw, states, mixed, x2, gs, us = saved[l]
        win, wout, wgt, wut, wd, cw = (weights[l][i] for i in range(6))
        dx2, dgs, dus, act, h2, d_n2 = mlp_bwd(dx, x2, row(norm2_g, l), gs, us, wgt, wut, wd, l, dep)
        g_wd = wgrad(act, dx, l, "wgrad_down")
        g_wgt = wgrad(dgs, h2, l, "wgrad_gate", send_grad(g_wd, 4, l))
        g_wut = wgrad(dus, h2, l, "wgrad_up", send_grad(g_wgt, 2, l))
        dgate, dr, du1, sums = out_proj_bwd(dx2, wout, r_raw, proj, row(ret_gn_g, l), u1,
                                            row(conv_ln_g, l), row(conv_ln_b, l), l, send_grad(g_wut, 3, l))
        g_wout = wgrad(mixed, dx2, l, "wgrad_out")
        dab, dwb = conv_bwd(du1, proj, cw, l, send_grad(g_wout, 1, l))
        dq, dk, dv = ret_bwd(dr, proj, states, tables, l)
        dproj = jnp.concatenate([dab, dq, dk, dv, dgate], axis=1)
        dx, h1, d_n1 = in_proj_bwd(dproj, win, xin, row(norm1_g, l), dx2, l)
        g_win = wgrad(h1, dproj, l, "wgrad_in")
        dep = send_grad(g_win, 0, l)
        small[l] = jnp.concatenate([dwb, sums, d_n1.reshape(2, CW), d_n2.reshape(2, CW)], axis=0)
    grad_x = dx.reshape(1, S, D)

    per = CK + 1 + 8 + 4
    packed = jnp.concatenate(small + [d_final.reshape(2, CW), jnp.zeros((6, CW), F32)], axis=0)
    tot = small_allreduce(packed)
    lay = tot[:L * per].reshape(L, per, CW)
    g_conv_w_full = lay[:, 0:CK, :]
    j = 2 * lax.axis_index("x") + lax.axis_index("y")
    g_conv_w = lax.dynamic_slice_in_dim(g_conv_w_full, j * (CW // NCHIP), CW // NCHIP, axis=2)
    g_small = {
        "conv_b": lay[:, CK, :], "ret_gn_g": lay[:, CK + 1, :], "conv_ln_g": lay[:, CK + 2, :],
        "conv_ln_b": lay[:, CK + 3, :], "norm1_g": lay[:, CK + 9:CK + 11, :].reshape(L, D),
        "norm2_g": lay[:, CK + 11:CK + 13, :].reshape(L, D), "final_g": tot[L * per:L * per + 2].reshape(D),
    }

    recv = scatter_wait(sent, lands, sent_which)
    shard_shapes = [(L * D, INW // NCHIP), (L * D // NCHIP, D), (L * fsh, D), (L * fsh, D), (L * fsh, D)]
    names = ["w_in", "w_out", "w_gate", "w_up", "w_down"]
    parts = [sum_slots(r.reshape((NCHIP,) + shp), f"sum_{nm}") for r, shp, nm in zip(recv, shard_shapes, names)]
    theirs = sibling_swap(parts)

    def unT(a):
        return jnp.swapaxes(a.reshape(L, fsh, D), 1, 2).reshape(L * D, fsh)

    big = {}
    wmv = {"w_in": (w_in, m_w_in, v_w_in), "w_out": (w_out, m_w_out, v_w_out),
           "w_gate": (w_gate, m_w_gate, v_w_gate), "w_up": (w_up, m_w_up, v_w_up),
           "w_down": (w_down, m_w_down, v_w_down)}
    for nm, mine, other in zip(names, parts, theirs):
        w, m, v = wmv[nm]
        if nm in ("w_gate", "w_up"):
            mine, other = unT(mine), unT(other)
        shp2 = (w.shape[0] * w.shape[1], w.shape[2])
        outs = adamw(w.reshape(shp2), mine, other, m.reshape(shp2), v.reshape(shp2), f"adamw_{nm}")
        big[nm] = [o.reshape(w.shape) for o in outs]

    cshape = (L * CK, CW // NCHIP)
    zc = jnp.zeros(cshape, F32)
    big["conv_w"] = [o.reshape(conv_w.shape) for o in adamw(
        conv_w.reshape(cshape), g_conv_w.reshape(cshape), zc, m_conv_w.reshape(cshape),
        v_conv_w.reshape(cshape), "adamw_conv_w")]
    vec_names = ["norm1_g", "conv_b", "conv_ln_g", "conv_ln_b", "ret_gn_g", "norm2_g", "final_g"]
    vec_w = {"norm1_g": (norm1_g, m_norm1_g, v_norm1_g), "conv_b": (conv_b, m_conv_b, v_conv_b),
             "conv_ln_g": (conv_ln_g, m_conv_ln_g, v_conv_ln_g), "conv_ln_b": (conv_ln_b, m_conv_ln_b, v_conv_ln_b),
             "ret_gn_g": (ret_gn_g, m_ret_gn_g, v_ret_gn_g), "norm2_g": (norm2_g, m_norm2_g, v_norm2_g),
             "final_g": (final_g, m_final_g, v_final_g)}
    cat = lambda arrs: jnp.concatenate([a.reshape(-1, CW) for a in arrs], axis=0)
    vw = cat([vec_w[nm][0] for nm in vec_names])
    vm = cat([vec_w[nm][1] for nm in vec_names])
    vv = cat([vec_w[nm][2] for nm in vec_names])
    vg = cat([g_small[nm] for nm in vec_names])
    vouts = adamw(vw, vg, jnp.zeros_like(vg), vm, vv, "adamw_vectors")
    off = 0
    for nm in vec_names:
        w = vec_w[nm][0]
        nrow = w.size // CW
        big[nm] = [o[off:off + nrow].reshape(w.shape) for o in vouts]
        off += nrow

    order = ["norm1_g", "w_in", "conv_w", "conv_b", "conv_ln_g", "conv_ln_b", "ret_gn_g", "w_out", "norm2_g",
             "w_gate", "w_up", "w_down", "final_g"]
    return (loss, grad_x, *[big[nm][0] for nm in order], *[big[nm][1] for nm in order],
            *[big[nm][2] for nm in order], *[big[nm][3] for nm in order])
```

```python
import math

import jax
import jax.numpy as jnp
from jax import lax
from jax.experimental import pallas as pl
from jax.experimental.pallas import tpu as pltpu

D = 1024
L = 4
CW = 512
RW = 512
NH = 4
HD = 128
CK = 31
CHUNK = 64
INW = 3072
FF = 2816
NCHIP = 4
EPS = 1e-6
ROPE_BASE = 10000.0
SEQ_TILE = 512
MLP_ROWS = 1024
MLP_COLS = 256
HALO = 32
CONV_ROWS = 32

ADAM_LR = 0.001
ADAM_B1 = 0.9
ADAM_B2 = 0.999
ADAM_EPS = 1e-08
ADAM_WD = 0.01
ADAM_STEP = 10

BF = jnp.bfloat16
F32 = jnp.float32
MESH = pl.DeviceIdType.MESH
ANY = pl.BlockSpec(memory_space=pl.ANY)
VMEM_SPEC = pl.BlockSpec(memory_space=pltpu.VMEM)
HBM_SPEC = pl.BlockSpec(memory_space=pltpu.HBM)
SEM_SPEC = pl.BlockSpec(memory_space=pltpu.SEMAPHORE)
DATAFLOW = pltpu.SideEffectType.DATAFLOW_SIDE_EFFECTING


def _params(n_grid, vmem_mb):
    return pltpu.CompilerParams(dimension_semantics=("arbitrary",) * n_grid,
                                vmem_limit_bytes=vmem_mb << 20)


def _ordered_call(body, deps, *, in_specs, **kw):
    n, nd = len(in_specs), len(deps)

    def with_deps(*refs):
        body(*refs[:n], *refs[n + nd:])

    return pl.pallas_call(with_deps, in_specs=list(in_specs) + [ANY] * nd, **kw)


def _dot(a, b):
    return jnp.dot(a, b, preferred_element_type=F32)


def _dot_nt(a, b):
    return lax.dot_general(a, b, (((1,), (1,)), ((), ())), preferred_element_type=F32)


def _dot_tn(a, b):
    return lax.dot_general(a, b, (((0,), (0,)), ((), ())), preferred_element_type=F32)


def _sigmoid(x):
    return 0.5 * jnp.tanh(0.5 * x) + 0.5


def _mean(x):
    return jnp.mean(x, axis=-1, keepdims=True)


def _rot(t, cs, sn):
    return t * cs + pltpu.roll(t, HD // 2, 1) * sn


def _rot_t(dy, cs, sn):
    return dy * cs + pltpu.roll(dy * sn, HD // 2, 1)


def _rms_bwd(x, g, dh, dx_in):
    r = lax.rsqrt(_mean(x * x) + EPS)
    xh = x * r
    dxh = dh * g
    dx = dx_in + r * (dxh - xh * _mean(dxh * xh))
    return dx, jnp.sum(dh * xh, axis=0, keepdims=True), (xh * g).astype(BF)


def in_proj(x, g, win, l, deps=()):
    S = x.shape[0]
    tm = min(S, SEQ_TILE)

    def body(x_ref, g_ref, w_ref, o_ref):
        xv = x_ref[...]
        h = (xv * lax.rsqrt(_mean(xv * xv) + EPS) * g_ref[...]).astype(BF)
        o_ref[...] = _dot(h, w_ref[...])

    return _ordered_call(
        body, deps, name=f"in_proj_{l}", grid=(S // tm,),
        in_specs=[pl.BlockSpec((tm, D), lambda i: (i, 0)),
                  pl.BlockSpec((1, D), lambda i: (0, 0)),
                  pl.BlockSpec((D, INW), lambda i: (0, 0))],
        out_specs=pl.BlockSpec((tm, INW), lambda i: (i, 0)),
        out_shape=jax.ShapeDtypeStruct((S, INW), F32),
        compiler_params=_params(1, 48),
    )(x, g, win, *deps)


def conv_fwd(proj, cw, cb, lg, lb, l, deps=()):
    S = proj.shape[0]
    tc = min(S, SEQ_TILE)

    def body(a_ref, b_ref, w_ref, cb_ref, lg_ref, lb_ref, u1_ref, u_ref, buf):
        i = pl.program_id(0)

        @pl.when(i == 0)
        def _():
            buf[0:HALO, :] = jnp.zeros((HALO, CW), F32)

        @pl.when(i > 0)
        def _():
            buf[0:HALO, :] = buf[tc:tc + HALO, :]

        buf[HALO:HALO + tc, :] = a_ref[...] * _sigmoid(b_ref[...])
        off = HALO - (CK - 1)
        for r0 in range(0, tc, CONV_ROWS):
            acc = jnp.broadcast_to(cb_ref[...], (CONV_ROWS, CW))
            for k in range(CK):
                acc = acc + w_ref[k:k + 1, :] * buf[r0 + off + k:r0 + off + k + CONV_ROWS, :]
            u1_ref[r0:r0 + CONV_ROWS, :] = acc
            d = acc - _mean(acc)
            u2 = d * lax.rsqrt(_mean(d * d) + EPS) * lg_ref[...] + lb_ref[...]
            u_ref[r0:r0 + CONV_ROWS, :] = (u2 * _sigmoid(u2)).astype(BF)

    vec = pl.BlockSpec((1, CW), lambda i: (0, 0))
    return _ordered_call(
        body, deps, name=f"conv_fwd_{l}", grid=(S // tc,),
        in_specs=[pl.BlockSpec((tc, CW), lambda i: (i, 0)),
                  pl.BlockSpec((tc, CW), lambda i: (i, 1)),
                  pl.BlockSpec((CK, CW), lambda i: (0, 0)), vec, vec, vec],
        out_specs=[pl.BlockSpec((tc, CW), lambda i: (i, 0)),
                   pl.BlockSpec((tc, CW), lambda i: (i, 0))],
        out_shape=[jax.ShapeDtypeStruct((S, CW), F32), jax.ShapeDtypeStruct((S, CW), BF)],
        scratch_shapes=[pltpu.VMEM((tc + HALO, CW), F32)],
        compiler_params=_params(1, 32),
    )(proj, proj, cw, cb, lg, lb, *deps)


def _ret_tables(S):
    half = HD // 2
    pos = jnp.arange(S, dtype=F32)
    freqs = ROPE_BASE ** (-jnp.arange(half, dtype=F32) / half)
    ang = pos[:, None] * freqs[None, :]
    cos, sin = jnp.cos(ang), jnp.sin(ang)
    cosf = jnp.concatenate([cos, cos], axis=-1)
    sinf = jnp.concatenate([-sin, sin], axis=-1)
    log_g = jnp.log(1.0 - 2.0 ** (-5.0 - jnp.arange(NH, dtype=F32)))
    idx = jnp.arange(CHUNK, dtype=F32)
    dmat = jnp.exp(log_g[:, None, None] * jnp.abs(idx[:, None] - idx[None, :]))
    qdec = jnp.broadcast_to(jnp.exp(log_g[:, None] * (idx + 1.0))[:, :, None], (NH, CHUNK, HD))
    kdec = jnp.broadcast_to(jnp.exp(log_g[:, None] * (CHUNK - 1 - idx))[:, :, None], (NH, CHUNK, HD))
    cdec = jnp.broadcast_to(jnp.exp(log_g * CHUNK)[:, None, None], (NH, HD, HD))
    return cosf, sinf, dmat, qdec, kdec, cdec


def _ret_specs(tr, tmap):
    q0, k0, v0 = (2 * CW) // HD, (2 * CW + RW) // HD, (2 * CW + 2 * RW) // HD
    return [pl.BlockSpec((tr, HD), lambda h, t: (tmap(t), q0 + h)),
            pl.BlockSpec((tr, HD), lambda h, t: (tmap(t), k0 + h)),
            pl.BlockSpec((tr, HD), lambda h, t: (tmap(t), v0 + h)),
            pl.BlockSpec((tr, HD), lambda h, t: (tmap(t), 0)),
            pl.BlockSpec((tr, HD), lambda h, t: (tmap(t), 0)),
            pl.BlockSpec((None, CHUNK, CHUNK), lambda h, t: (h, 0, 0)),
            pl.BlockSpec((None, CHUNK, HD), lambda h, t: (h, 0, 0)),
            pl.BlockSpec((None, CHUNK, HD), lambda h, t: (h, 0, 0)),
            pl.BlockSpec((None, HD, HD), lambda h, t: (h, 0, 0))]


def ret_fwd(proj, tables, l, deps=()):
    S = proj.shape[0]
    tr = min(S, SEQ_TILE)
    cpb = tr // CHUNK
    scale = HD ** -0.5

    def body(q_ref, k_ref, v_ref, cos_ref, sin_ref, dm_ref, qd_ref, kd_ref, cd_ref,
             r_ref, st_ref, st):
        @pl.when(pl.program_id(1) == 0)
        def _():
            st[...] = jnp.zeros((HD, HD), F32)

        for c in range(cpb):
            rows = slice(c * CHUNK, (c + 1) * CHUNK)
            cs, sn = cos_ref[rows, :], sin_ref[rows, :]
            qr = _rot(q_ref[rows, :], cs, sn)
            kr = _rot(k_ref[rows, :], cs, sn) * scale
            vb = v_ref[rows, :].astype(BF)
            s = st[...]
            st_ref[c] = s
            sc = _dot_nt(qr.astype(BF), kr.astype(BF)) * dm_ref[...]
            out = _dot(sc.astype(BF), vb) + _dot((qr * qd_ref[...]).astype(BF), s.astype(BF))
            r_ref[rows, :] = out
            st[...] = cd_ref[...] * s + _dot_tn((kr * kd_ref[...]).astype(BF), vb)

    return _ordered_call(
        body, deps, name=f"ret_fwd_{l}", grid=(NH, S // tr),
        in_specs=_ret_specs(tr, lambda t: t),
        out_specs=[pl.BlockSpec((tr, HD), lambda h, t: (t, h)),
                   pl.BlockSpec((None, cpb, HD, HD), lambda h, t: (h, t, 0, 0))],
        out_shape=[jax.ShapeDtypeStruct((S, RW), F32),
                   jax.ShapeDtypeStruct((NH, S // CHUNK, HD, HD), F32)],
        scratch_shapes=[pltpu.VMEM((HD, HD), F32)],
        compiler_params=_params(2, 32),
    )(proj, proj, proj, *tables, *deps)


def out_proj(u, r_raw, proj, gn, wout, x, l, deps=()):
    S = x.shape[0]
    tm = min(S, SEQ_TILE)
    gate_blk = (2 * CW + 3 * RW) // RW

    def body(u_ref, r_ref, gate_ref, gn_ref, w_ref, x_ref, x2_ref, mix_ref):
        mix_ref[:, 0:CW] = u_ref[...]
        gt = gate_ref[...]
        sil = gt * _sigmoid(gt) * gn_ref[...]
        for h in range(NH):
            cols = slice(h * HD, (h + 1) * HD)
            rh = r_ref[:, cols]
            d = rh - _mean(rh)
            rn = d * lax.rsqrt(_mean(d * d) + EPS)
            mix_ref[:, CW + h * HD:CW + (h + 1) * HD] = (rn * sil[:, cols]).astype(BF)
        x2_ref[...] = x_ref[...] + _dot(mix_ref[...], w_ref[...])

    return _ordered_call(
        body, deps, name=f"out_proj_{l}", grid=(S // tm,),
        in_specs=[pl.BlockSpec((tm, CW), lambda i: (i, 0)),
                  pl.BlockSpec((tm, RW), lambda i: (i, 0)),
                  pl.BlockSpec((tm, RW), lambda i: (i, gate_blk)),
                  pl.BlockSpec((1, RW), lambda i: (0, 0)),
                  pl.BlockSpec((D, D), lambda i: (0, 0)),
                  pl.BlockSpec((tm, D), lambda i: (i, 0))],
        out_specs=[pl.BlockSpec((tm, D), lambda i: (i, 0)),
                   pl.BlockSpec((tm, D), lambda i: (i, 0))],
        out_shape=[jax.ShapeDtypeStruct((S, D), F32), jax.ShapeDtypeStruct((S, D), BF)],
        compiler_params=_params(1, 40),
    )(u, r_raw, proj, gn, wout, x, *deps)


def mlp_fwd(x2, g2, wgt, wut, wd, l, deps=()):
    S = x2.shape[0]
    tm, tf = min(S, MLP_ROWS), MLP_COLS
    nk = FF // tf

    def body(x_ref, g_ref, wg_ref, wu_ref, wd_ref, o_ref, gs_ref, us_ref, h_scr, acc):
        k = pl.program_id(1)

        @pl.when(k == 0)
        def _():
            xv = x_ref[...]
            h_scr[...] = (xv * lax.rsqrt(_mean(xv * xv) + EPS) * g_ref[...]).astype(BF)
            acc[...] = jnp.zeros((tm, D), F32)

        gv = _dot_nt(h_scr[...], wg_ref[...])
        uv = _dot_nt(h_scr[...], wu_ref[...])
        gs_ref[...] = gv.astype(BF)
        us_ref[...] = uv.astype(BF)
        acc[...] += _dot((gv * _sigmoid(gv) * uv).astype(BF), wd_ref[...])

        @pl.when(k == nk - 1)
        def _():
            o_ref[...] = x_ref[...] + acc[...]

    wspec = pl.BlockSpec((tf, D), lambda i, k: (k, 0))
    return _ordered_call(
        body, deps, name=f"mlp_fwd_{l}", grid=(S // tm, nk),
        in_specs=[pl.BlockSpec((tm, D), lambda i, k: (i, 0)),
                  pl.BlockSpec((1, D), lambda i, k: (0, 0)), wspec, wspec, wspec],
        out_specs=[pl.BlockSpec((tm, D), lambda i, k: (i, 0)),
                   pl.BlockSpec((tm, tf), lambda i, k: (i, k)),
                   pl.BlockSpec((tm, tf), lambda i, k: (i, k))],
        out_shape=[jax.ShapeDtypeStruct((S, D), F32), jax.ShapeDtypeStruct((S, FF), BF),
                   jax.ShapeDtypeStruct((S, FF), BF)],
        scratch_shapes=[pltpu.VMEM((tm, D), BF), pltpu.VMEM((tm, D), F32)],
        compiler_params=_params(2, 56),
    )(x2, g2, wgt, wut, wd, *deps)


def final_loss(x, gf, tgt):
    S = x.shape[0]
    tm = min(S, SEQ_TILE)

    def body(x_ref, g_ref, t_ref, dx_ref, loss_ref, dg_ref):
        @pl.when(pl.program_id(0) == 0)
        def _():
            loss_ref[...] = jnp.zeros((8, 128), F32)
            dg_ref[...] = jnp.zeros((1, D), F32)

        xv = x_ref[...]
        r = lax.rsqrt(_mean(xv * xv) + EPS)
        xh = xv * r
        diff = xh * g_ref[...] - t_ref[...]
        loss_ref[...] += jnp.sum(jnp.sum(diff * diff, axis=-1, keepdims=True), axis=0, keepdims=True)
        dy = diff * (1.0 / D)
        dg_ref[...] += jnp.sum(dy * xh, axis=0, keepdims=True)
        dxh = dy * g_ref[...]
        dx_ref[...] = r * (dxh - xh * _mean(dxh * xh))

    return pl.pallas_call(
        body, name="final_loss", grid=(S // tm,),
        in_specs=[pl.BlockSpec((tm, D), lambda i: (i, 0)),
                  pl.BlockSpec((1, D), lambda i: (0, 0)),
                  pl.BlockSpec((tm, D), lambda i: (i, 0))],
        out_specs=[pl.BlockSpec((tm, D), lambda i: (i, 0)),
                   pl.BlockSpec((8, 128), lambda i: (0, 0)),
                   pl.BlockSpec((1, D), lambda i: (0, 0))],
        out_shape=[jax.ShapeDtypeStruct((S, D), F32), jax.ShapeDtypeStruct((8, 128), F32),
                   jax.ShapeDtypeStruct((1, D), F32)],
        compiler_params=_params(1, 40),
    )(x, gf, tgt)


def mlp_bwd(dx3, x2, g2, gs, us, wgt, wut, wd, l, deps=()):
    S = x2.shape[0]
    tm, tf = min(S, MLP_ROWS), MLP_COLS
    nk = FF // tf

    def body(dx_ref, x_ref, g_ref, gs_ref, us_ref, wg_ref, wu_ref, wd_ref,
             dx2_ref, dg_ref, du_ref, a_ref, h_ref, dgain_ref, dxb, acc):
        i, k = pl.program_id(0), pl.program_id(1)

        @pl.when(k == 0)
        def _():
            dxb[...] = dx_ref[...].astype(BF)
            acc[...] = jnp.zeros((tm, D), F32)

        @pl.when((k == 0) & (i == 0))
        def _():
            dgain_ref[...] = jnp.zeros((1, D), F32)

        da = _dot_nt(dxb[...], wd_ref[...])
        gv = gs_ref[...].astype(F32)
        uv = us_ref[...].astype(F32)
        sg = _sigmoid(gv)
        sil = gv * sg
        dgv = (da * uv * (sg * (1.0 + gv * (1.0 - sg)))).astype(BF)
        duv = (da * sil).astype(BF)
        a_ref[...] = (sil * uv).astype(BF)
        dg_ref[...] = dgv
        du_ref[...] = duv
        acc[...] += _dot(dgv, wg_ref[...]) + _dot(duv, wu_ref[...])

        @pl.when(k == nk - 1)
        def _():
            dx2, dgain, hb = _rms_bwd(x_ref[...], g_ref[...], acc[...], dx_ref[...])
            dx2_ref[...] = dx2
            dgain_ref[...] += dgain
            h_ref[...] = hb

    wspec = pl.BlockSpec((tf, D), lambda i, k: (k, 0))
    row = pl.BlockSpec((tm, D), lambda i, k: (i, 0))
    wide = pl.BlockSpec((tm, tf), lambda i, k: (i, k))
    return _ordered_call(
        body, deps, name=f"mlp_bwd_{l}", grid=(S // tm, nk),
        in_specs=[row, row, pl.BlockSpec((1, D), lambda i, k: (0, 0)), wide, wide, wspec, wspec, wspec],
        out_specs=[row, wide, wide, wide, row, pl.BlockSpec((1, D), lambda i, k: (0, 0))],
        out_shape=[jax.ShapeDtypeStruct((S, D), F32), jax.ShapeDtypeStruct((S, FF), BF),
                   jax.ShapeDtypeStruct((S, FF), BF), jax.ShapeDtypeStruct((S, FF), BF),
                   jax.ShapeDtypeStruct((S, D), BF), jax.ShapeDtypeStruct((1, D), F32)],
        scratch_shapes=[pltpu.VMEM((tm, D), BF), pltpu.VMEM((tm, D), F32)],
        compiler_params=_params(2, 56),
    )(dx3, x2, g2, gs, us, wgt, wut, wd, *deps)


def wgrad(a, b, l, name, deps=()):
    S, K = a.shape
    N = b.shape[1]
    tk = 1408 if K == FF else min(K, 1024)
    tn = 768 if N == INW else min(N, 1024)
    ts = min(S, SEQ_TILE)
    ns = S // ts

    def body(a_ref, b_ref, o_ref, acc):
        s = pl.program_id(2)

        @pl.when(s == 0)
        def _():
            acc[...] = jnp.zeros((tk, tn), F32)

        acc[...] += _dot_tn(a_ref[...], b_ref[...].astype(BF))

        @pl.when(s == ns - 1)
        def _():
            o_ref[...] = acc[...].astype(BF)

    return _ordered_call(
        body, deps, name=f"{name}_{l}", grid=(K // tk, N // tn, ns),
        in_specs=[pl.BlockSpec((ts, tk), lambda i, j, s: (s, i)),
                  pl.BlockSpec((ts, tn), lambda i, j, s: (s, j))],
        out_specs=pl.BlockSpec((tk, tn), lambda i, j, s: (i, j)),
        out_shape=jax.ShapeDtypeStruct((K, N), BF),
        scratch_shapes=[pltpu.VMEM((tk, tn), F32)],
        compiler_params=_params(3, 48),
    )(a, b, *deps)


def out_proj_bwd(dx2, wout, r_raw, proj, gn, u1, lg, lb, l, deps=()):
    S = dx2.shape[0]
    tm = min(S, SEQ_TILE)
    gate_blk = (2 * CW + 3 * RW) // RW

    def body(dx_ref, w_ref, r_ref, gate_ref, gn_ref, u1_ref, lg_ref, lb_ref,
             dgate_ref, dr_ref, du1_ref, sums_ref):
        @pl.when(pl.program_id(0) == 0)
        def _():
            sums_ref[...] = jnp.zeros((8, CW), F32)

        dmix = _dot_nt(dx_ref[...].astype(BF), w_ref[...])
        gt = gate_ref[...]
        sg = _sigmoid(gt)
        sil = gt * sg
        dsil = sg * (1.0 + gt * (1.0 - sg))
        for h in range(NH):
            cols = slice(h * HD, (h + 1) * HD)
            rh = r_ref[:, cols]
            d = rh - _mean(rh)
            rs = lax.rsqrt(_mean(d * d) + EPS)
            rn = d * rs
            drr = dmix[:, CW + h * HD:CW + (h + 1) * HD]
            gnh = gn_ref[:, cols]
            sums_ref[0:1, cols] += jnp.sum(drr * rn * sil[:, cols], axis=0, keepdims=True)
            dgate_ref[:, cols] = (drr * rn * gnh * dsil[:, cols]).astype(BF)
            drn = drr * gnh * sil[:, cols]
            dr_ref[:, cols] = rs * (drn - _mean(drn) - rn * _mean(drn * rn))
        du = dmix[:, 0:CW]
        u1 = u1_ref[...]
        d = u1 - _mean(u1)
        rs = lax.rsqrt(_mean(d * d) + EPS)
        xh = d * rs
        u2 = xh * lg_ref[...] + lb_ref[...]
        sg2 = _sigmoid(u2)
        du2 = du * (sg2 * (1.0 + u2 * (1.0 - sg2)))
        sums_ref[1:2, :] += jnp.sum(du2 * xh, axis=0, keepdims=True)
        sums_ref[2:3, :] += jnp.sum(du2, axis=0, keepdims=True)
        dxh = du2 * lg_ref[...]
        du1_ref[...] = rs * (dxh - _mean(dxh) - xh * _mean(dxh * xh))

    vec = pl.BlockSpec((1, CW), lambda i: (0, 0))
    half = pl.BlockSpec((tm, CW), lambda i: (i, 0))
    return _ordered_call(
        body, deps, name=f"out_proj_bwd_{l}", grid=(S // tm,),
        in_specs=[pl.BlockSpec((tm, D), lambda i: (i, 0)),
                  pl.BlockSpec((D, D), lambda i: (0, 0)),
                  half, pl.BlockSpec((tm, RW), lambda i: (i, gate_blk)), vec, half, vec, vec],
        out_specs=[half, half, half, pl.BlockSpec((8, CW), lambda i: (0, 0))],
        out_shape=[jax.ShapeDtypeStruct((S, RW), BF), jax.ShapeDtypeStruct((S, RW), F32),
                   jax.ShapeDtypeStruct((S, CW), F32), jax.ShapeDtypeStruct((8, CW), F32)],
        compiler_params=_params(1, 40),
    )(dx2, wout, r_raw, proj, gn, u1, lg, lb, *deps)


def conv_bwd(du1, proj, cw, l, deps=()):
    S = proj.shape[0]
    tc = min(S, SEQ_TILE)
    nt = S // tc

    def body(du1_ref, a_ref, b_ref, w_ref, dab_ref, dwb_ref, buf):
        i = pl.program_id(0)

        @pl.when(i == 0)
        def _():
            buf[tc:tc + HALO, :] = jnp.zeros((HALO, CW), F32)
            dwb_ref[...] = jnp.zeros((CK + 1, CW), F32)

        @pl.when(i > 0)
        def _():
            buf[tc:tc + HALO, :] = buf[0:HALO, :]

        buf[0:tc, :] = du1_ref[...]
        parts = [jnp.zeros((8, CW), F32) for _ in range(CK)]
        for r0 in range(0, tc, CONV_ROWS):
            av = a_ref[r0:r0 + CONV_ROWS, :]
            sgb = _sigmoid(b_ref[r0:r0 + CONV_ROWS, :])
            u0 = av * sgb
            acc = jnp.zeros((CONV_ROWS, CW), F32)
            for j in range(CK):
                sl = buf[r0 + j:r0 + j + CONV_ROWS, :]
                acc = acc + w_ref[CK - 1 - j:CK - j, :] * sl
                pr = u0 * sl
                red = pr[0:8, :]
                for q in range(1, CONV_ROWS // 8):
                    red = red + pr[8 * q:8 * q + 8, :]
                parts[CK - 1 - j] = parts[CK - 1 - j] + red
            dab_ref[r0:r0 + CONV_ROWS, 0:CW] = (acc * sgb).astype(BF)
            dab_ref[r0:r0 + CONV_ROWS, CW:2 * CW] = (acc * av * sgb * (1.0 - sgb)).astype(BF)
        for k in range(CK):
            dwb_ref[k:k + 1, :] += jnp.sum(parts[k], axis=0, keepdims=True)
        dwb_ref[CK:CK + 1, :] += jnp.sum(du1_ref[...], axis=0, keepdims=True)

    return _ordered_call(
        body, deps, name=f"conv_bwd_{l}", grid=(nt,),
        in_specs=[pl.BlockSpec((tc, CW), lambda i: (nt - 1 - i, 0)),
                  pl.BlockSpec((tc, CW), lambda i: (nt - 1 - i, 0)),
                  pl.BlockSpec((tc, CW), lambda i: (nt - 1 - i, 1)),
                  pl.BlockSpec((CK, CW), lambda i: (0, 0))],
        out_specs=[pl.BlockSpec((tc, 2 * CW), lambda i: (nt - 1 - i, 0)),
                   pl.BlockSpec((CK + 1, CW), lambda i: (0, 0))],
        out_shape=[jax.ShapeDtypeStruct((S, 2 * CW), BF), jax.ShapeDtypeStruct((CK + 1, CW), F32)],
        scratch_shapes=[pltpu.VMEM((tc + HALO, CW), F32)],
        compiler_params=_params(1, 32),
    )(du1, proj, proj, cw, *deps)


def ret_bwd(dr, proj, states, tables, l):
    S = proj.shape[0]
    tr = min(S, SEQ_TILE)
    cpb = tr // CHUNK
    nt = S // tr
    scale = HD ** -0.5

    def body(q_ref, k_ref, v_ref, cos_ref, sin_ref, dm_ref, qd_ref, kd_ref, cd_ref, dr_ref, st_ref,
             dq_ref, dk_ref, dv_ref, gst):
        @pl.when(pl.program_id(1) == 0)
        def _():
            gst[...] = jnp.zeros((HD, HD), F32)

        for c in reversed(range(cpb)):
            rows = slice(c * CHUNK, (c + 1) * CHUNK)
            cs, sn = cos_ref[rows, :], sin_ref[rows, :]
            qr = _rot(q_ref[rows, :], cs, sn)
            kr = _rot(k_ref[rows, :], cs, sn) * scale
            qb, kb = qr.astype(BF), kr.astype(BF)
            vb = v_ref[rows, :].astype(BF)
            dob = dr_ref[rows, :].astype(BF)
            sb = st_ref[c].astype(BF)
            gn1 = gst[...]
            gb = gn1.astype(BF)
            sc = (_dot_nt(qb, kb) * dm_ref[...]).astype(BF)
            dsc = (_dot_nt(dob, vb) * dm_ref[...]).astype(BF)
            dqr = _dot(dsc, kb) + _dot_nt(dob, sb) * qd_ref[...]
            dkr = _dot_tn(dsc, qb) + _dot_nt(vb, gb) * kd_ref[...]
            dvv = _dot_tn(sc, dob) + _dot((kr * kd_ref[...]).astype(BF), gb)
            gst[...] = cd_ref[...] * gn1 + _dot_tn((qr * qd_ref[...]).astype(BF), dob)
            dq_ref[rows, :] = _rot_t(dqr, cs, sn).astype(BF)
            dk_ref[rows, :] = _rot_t(dkr * scale, cs, sn).astype(BF)
            dv_ref[rows, :] = dvv.astype(BF)

    rev = lambda t: nt - 1 - t
    hblk = pl.BlockSpec((tr, HD), lambda h, t: (rev(t), h))
    return pl.pallas_call(
        body, name=f"ret_bwd_{l}", grid=(NH, nt),
        in_specs=_ret_specs(tr, rev) + [hblk, pl.BlockSpec((None, cpb, HD, HD), lambda h, t: (h, rev(t), 0, 0))],
        out_specs=[hblk, hblk, hblk],
        out_shape=[jax.ShapeDtypeStruct((S, RW), BF)] * 3,
        scratch_shapes=[pltpu.VMEM((HD, HD), F32)],
        compiler_params=_params(2, 32),
    )(proj, proj, proj, *tables, dr, states)


def in_proj_bwd(parts, win, x, g, dx2, l):
    S = x.shape[0]
    tm = min(S, SEQ_TILE)
    n = len(parts)

    def body(*refs):
        srcs = refs[:n]
        w_ref, x_ref, g_ref, dx2_ref, dx_ref, h_ref, dgain_ref = refs[n:]

        @pl.when(pl.program_id(0) == 0)
        def _():
            dgain_ref[...] = jnp.zeros((1, D), F32)

        dh, col = None, 0
        for r in srcs:
            width = r.shape[1]
            term = _dot_nt(r[...], w_ref[:, col:col + width])
            dh = term if dh is None else dh + term
            col += width
        dx, dgain, hb = _rms_bwd(x_ref[...], g_ref[...], dh, dx2_ref[...])
        dx_ref[...] = dx
        dgain_ref[...] += dgain
        h_ref[...] = hb

    row = pl.BlockSpec((tm, D), lambda i: (i, 0))
    vec = pl.BlockSpec((1, D), lambda i: (0, 0))
    return pl.pallas_call(
        body, name=f"in_proj_bwd_{l}", grid=(S // tm,),
        in_specs=[pl.BlockSpec((tm, p.shape[1]), lambda i: (i, 0)) for p in parts]
        + [pl.BlockSpec((D, INW), lambda i: (0, 0)), row, vec, row],
        out_specs=[row, row, vec],
        out_shape=[jax.ShapeDtypeStruct((S, D), F32), jax.ShapeDtypeStruct((S, D), BF),
                   jax.ShapeDtypeStruct((1, D), F32)],
        compiler_params=_params(1, 48),
    )(*parts, win, x, g, dx2)


def wgrad_in(h, parts, l):
    S = h.shape[0]
    ts = min(S, SEQ_TILE)
    ns = S // ts
    tn = 512
    tiles = [(p, c) for p, a in enumerate(parts) for c in range(a.shape[1] // tn)]
    first = [min(j for j, (q, _) in enumerate(tiles) if q == p) for p in range(len(parts))]
    count = [a.shape[1] // tn for a in parts]

    def body(*refs):
        h_ref, srcs = refs[0], refs[1:1 + len(parts)]
        o_ref, acc = refs[-2], refs[-1]
        j, s = pl.program_id(0), pl.program_id(1)

        @pl.when(s == 0)
        def _():
            acc[...] = jnp.zeros((D, tn), F32)

        for jj, (p, _) in enumerate(tiles):
            @pl.when(j == jj)
            def _(p=p):
                acc[...] += _dot_tn(h_ref[...], srcs[p][...])

        @pl.when(s == ns - 1)
        def _():
            o_ref[...] = acc[...].astype(BF)

    def part_spec(p):
        def index(j, s):
            mine = (j >= first[p]) & (j < first[p] + count[p])
            return jnp.where(mine, s, 0), jnp.clip(j - first[p], 0, count[p] - 1)
        return pl.BlockSpec((ts, tn), index)

    return pl.pallas_call(
        body, name=f"wgrad_in_{l}", grid=(len(tiles), ns),
        in_specs=[pl.BlockSpec((ts, D), lambda j, s: (s, 0))] + [part_spec(p) for p in range(len(parts))],
        out_specs=pl.BlockSpec((D, tn), lambda j, s: (0, j)),
        out_shape=jax.ShapeDtypeStruct((D, INW), BF),
        scratch_shapes=[pltpu.VMEM((D, tn), F32)],
        compiler_params=_params(2, 40),
    )(h, *parts)


def sum_slots(recv, name):
    _, R, C = recv.shape
    tr = 256 if R % 256 == 0 else R

    def body(r_ref, o_ref):
        acc = r_ref[0].astype(F32)
        for k in range(1, NCHIP):
            acc = acc + r_ref[k].astype(F32)
        o_ref[...] = acc

    return pl.pallas_call(
        body, name=name, grid=(R // tr,),
        in_specs=[pl.BlockSpec((NCHIP, tr, C), lambda i: (0, i, 0))],
        out_specs=pl.BlockSpec((tr, C), lambda i: (i, 0)),
        out_shape=jax.ShapeDtypeStruct((R, C), F32),
        compiler_params=_params(1, 32),
    )(recv)


def adamw(w, ga, gb, m, v, name):
    R, C = w.shape
    tr = 256 if R % 256 == 0 else R
    c1 = 1.0 - ADAM_B1 ** ADAM_STEP
    c2 = 1.0 - ADAM_B2 ** ADAM_STEP

    def body(w_ref, ga_ref, gb_ref, m_ref, v_ref, g_out, d_out, m_out, v_out):
        g = ga_ref[...] + gb_ref[...]
        mn = ADAM_B1 * m_ref[...] + (1.0 - ADAM_B1) * g
        vn = ADAM_B2 * v_ref[...] + (1.0 - ADAM_B2) * (g * g)
        g_out[...] = g
        m_out[...] = mn
        v_out[...] = vn
        d_out[...] = -ADAM_LR * ((mn / c1) / (jnp.sqrt(vn / c2) + ADAM_EPS) + ADAM_WD * w_ref[...])

    blk = pl.BlockSpec((tr, C), lambda i: (i, 0))
    return pl.pallas_call(
        body, name=name, grid=(R // tr,),
        in_specs=[blk] * 5, out_specs=[blk] * 4,
        out_shape=[jax.ShapeDtypeStruct((R, C), F32)] * 4,
        compiler_params=_params(1, 40),
    )(w, ga, gb, m, v)


def _place():
    x, y, c = lax.axis_index("x"), lax.axis_index("y"), lax.axis_index("c")
    chips = [(1 - x, y), (x, 1 - y), (1 - x, 1 - y)]
    return x, y, c, chips


def _window(ref, axis, j, size):
    idx = [slice(None)] * len(ref.shape)
    idx[axis] = pl.ds(pl.multiple_of(j * size, 128 if axis == len(ref.shape) - 1 else 16), size)
    return ref.at[tuple(idx)]


def _hbm(a):
    return pltpu.with_memory_space_constraint(a, pltpu.HBM)


def _hbm_like(arrs):
    return [pltpu.HBM(a.shape, a.dtype) for a in arrs]


def gather_start(shards, axes, after, tag):
    n = len(shards)
    na = len(after)
    lands = []
    for s, ax in zip(shards, axes):
        shp = list(s.shape)
        shp[ax] *= NCHIP
        lands.append(lax.empty(tuple(shp), s.dtype))

    def body(*refs):
        ins, land = refs[:n], refs[n:2 * n]
        send, recv = refs[2 * n + na], refs[2 * n + na + 1]
        token = refs[-1]
        x, y, c, chips = _place()
        for a in range(n):
            for k, chip in enumerate(chips):
                pltpu.make_async_remote_copy(
                    src_ref=ins[a], dst_ref=_window(land[a], axes[a], 2 * x + y, ins[a].shape[axes[a]]),
                    send_sem=send.at[3 * a + k], recv_sem=recv.at[3 * a + k],
                    device_id=(chip[0], chip[1], c), device_id_type=MESH).start()
        token[...] = jnp.zeros_like(token)

    outs = pl.pallas_call(
        body, name=f"gather_start_{tag}",
        in_specs=[HBM_SPEC] * (2 * n) + [ANY] * na,
        out_specs=(SEM_SPEC, SEM_SPEC, *[HBM_SPEC] * (2 * n), VMEM_SPEC),
        out_shape=(pltpu.SemaphoreType.DMA((3 * n,)), pltpu.SemaphoreType.DMA((3 * n,)),
                   *_hbm_like(shards), *_hbm_like(lands), jax.ShapeDtypeStruct((8, 128), F32)),
        input_output_aliases={a: 2 + a for a in range(2 * n)},
        compiler_params=pltpu.CompilerParams(has_side_effects=DATAFLOW),
    )(*[_hbm(s) for s in shards], *[_hbm(b) for b in lands], *after)
    return (outs[0], outs[1], list(outs[2:2 + n]), list(outs[2 + n:2 + 2 * n]), list(axes)), outs[-1]


def gather_wait(groups, after, tag):
    sizes = [len(g[2]) for g in groups]
    total = sum(sizes)

    def body(*refs):
        x, y, c, chips = _place()
        loc = refs[-1]
        pos = 2 * total
        off = 0
        mine = []
        for g, n in zip(groups, sizes):
            ins, land = refs[off:off + n], refs[total + off:total + off + n]
            send_ref, recv_ref = refs[pos], refs[pos + 1]
            axes = g[4]
            for a in range(n):
                own = pltpu.make_async_copy(
                    ins[a], _window(land[a], axes[a], 2 * x + y, ins[a].shape[axes[a]]), loc.at[off + a])
                own.start()
                mine.append(own)
                for k, chip in enumerate(chips):
                    cp = pltpu.make_async_remote_copy(
                        src_ref=ins[a],
                        dst_ref=_window(land[a], axes[a], 2 * chip[0] + chip[1], ins[a].shape[axes[a]]),
                        send_sem=send_ref.at[3 * a + k], recv_sem=recv_ref.at[3 * a + k],
                        device_id=(chip[0], chip[1], c), device_id_type=MESH)
                    cp.wait_send()
                    cp.wait_recv()
            pos += 2
            off += n
        for own in mine:
            own.wait()

    shards = [s for g in groups for s in g[2]]
    lands = [b for g in groups for b in g[3]]
    sems = [s for g in groups for s in (g[0], g[1])]
    outs = pl.pallas_call(
        body, name=f"gather_wait_{tag}",
        in_specs=[HBM_SPEC] * (2 * total) + [SEM_SPEC] * len(sems) + [ANY],
        out_specs=[HBM_SPEC] * (2 * total),
        out_shape=(*_hbm_like(shards), *_hbm_like(lands)),
        input_output_aliases={a: a for a in range(2 * total)},
        scratch_shapes=[pltpu.SemaphoreType.DMA((total,))],
        compiler_params=pltpu.CompilerParams(has_side_effects=DATAFLOW),
    )(*shards, *lands, *sems, after)
    return list(outs[total:])


def scatter_start(grads, axes, sizes, lands, l, tag):
    n = len(grads)

    def body(*refs):
        ins, land = refs[:n], refs[n:2 * n]
        send, recv = refs[2 * n], refs[2 * n + 1]
        token = refs[-1]
        x, y, c, chips = _place()
        me = 2 * x + y
        for a in range(n):
            for k, chip in enumerate(chips):
                pltpu.make_async_remote_copy(
                    src_ref=_window(ins[a], axes[a], 2 * chip[0] + chip[1], sizes[a]), dst_ref=land[a].at[me, l],
                    send_sem=send.at[3 * a + k], recv_sem=recv.at[3 * a + k],
                    device_id=(chip[0], chip[1], c), device_id_type=MESH).start()
        token[...] = jnp.zeros_like(token)

    outs = pl.pallas_call(
        body, name=f"scatter_start_{tag}",
        in_specs=[HBM_SPEC] * (2 * n),
        out_specs=(SEM_SPEC, SEM_SPEC, *[HBM_SPEC] * (2 * n), VMEM_SPEC),
        out_shape=(pltpu.SemaphoreType.DMA((3 * n,)), pltpu.SemaphoreType.DMA((3 * n,)),
                   *_hbm_like(grads), *_hbm_like(lands), jax.ShapeDtypeStruct((8, 128), F32)),
        input_output_aliases={a: 2 + a for a in range(2 * n)},
        compiler_params=pltpu.CompilerParams(has_side_effects=DATAFLOW),
    )(*[_hbm(g) for g in grads], *[_hbm(b) for b in lands])
    group = (outs[0], outs[1], list(outs[2:2 + n]), list(axes), list(sizes), l)
    return group, list(outs[2 + n:2 + 2 * n]), outs[-1]


def scatter_wait(groups, lands, which):
    nl = len(lands)

    def body(*refs):
        land = refs[:nl]
        loc = refs[-1]
        x, y, c, chips = _place()
        me = 2 * x + y
        pos = nl
        mine = []
        for g, wh in zip(groups, which):
            n = len(g[2])
            ins = refs[pos:pos + n]
            send_ref, recv_ref = refs[pos + n], refs[pos + n + 1]
            axes, sizes, l = g[3], g[4], g[5]
            for a in range(n):
                own = pltpu.make_async_copy(_window(ins[a], axes[a], me, sizes[a]), land[wh[a]].at[me, l],
                                            loc.at[len(mine)])
                own.start()
                mine.append(own)
                for k, chip in enumerate(chips):
                    jp = 2 * chip[0] + chip[1]
                    cp = pltpu.make_async_remote_copy(
                        src_ref=_window(ins[a], axes[a], jp, sizes[a]), dst_ref=land[wh[a]].at[jp, l],
                        send_sem=send_ref.at[3 * a + k], recv_sem=recv_ref.at[3 * a + k],
                        device_id=(chip[0], chip[1], c), device_id_type=MESH)
                    cp.wait_send()
                    cp.wait_recv()
            pos += n + 2
        for own in mine:
            own.wait()

    n_copies = sum(len(g[2]) for g in groups)
    operands = list(lands)
    specs = [HBM_SPEC] * nl
    for g in groups:
        operands += list(g[2]) + [g[0], g[1]]
        specs += [HBM_SPEC] * len(g[2]) + [SEM_SPEC, SEM_SPEC]
    outs = pl.pallas_call(
        body, name="scatter_wait", in_specs=specs, out_specs=[HBM_SPEC] * nl, out_shape=tuple(_hbm_like(lands)),
        input_output_aliases={a: a for a in range(nl)},
        scratch_shapes=[pltpu.SemaphoreType.DMA((n_copies,))],
        compiler_params=pltpu.CompilerParams(has_side_effects=DATAFLOW),
    )(*operands)
    return list(outs)


def sibling_swap(parts):
    n = len(parts)

    def body(*refs):
        ins, outs = refs[:n], refs[n:2 * n]
        send, recv = refs[2 * n:]
        x, y, c, _ = _place()
        cps = [pltpu.make_async_remote_copy(src_ref=ins[a], dst_ref=outs[a], send_sem=send.at[a],
                                            recv_sem=recv.at[a], device_id=(x, y, 1 - c), device_id_type=MESH)
               for a in range(n)]
        for cp in cps:
            cp.start()
        for cp in cps:
            cp.wait_recv()
        for cp in cps:
            cp.wait_send()

    return pl.pallas_call(
        body, name="sibling_swap", in_specs=[ANY] * n, out_specs=[ANY] * n,
        out_shape=[jax.ShapeDtypeStruct(p.shape, p.dtype) for p in parts],
        scratch_shapes=[pltpu.SemaphoreType.DMA((n,)), pltpu.SemaphoreType.DMA((n,))],
    )(*parts)


def small_allreduce(p):
    R, C = p.shape
    ndev = 8

    def body(p_ref, o_ref, buf, send, recv):
        x, y, c, _ = _place()
        me = 4 * x + 2 * y + c
        buf[me] = p_ref[...]

        def peer(d):
            px = 1 - x if d & 4 else x
            py = 1 - y if d & 2 else y
            pc = 1 - c if d & 1 else c
            return px, py, pc

        def copy(d, slot):
            return pltpu.make_async_remote_copy(src_ref=p_ref, dst_ref=buf.at[slot], send_sem=send.at[d - 1],
                                                recv_sem=recv.at[d - 1], device_id=peer(d), device_id_type=MESH)

        sends = [copy(d, me) for d in range(1, ndev)]
        for cp in sends:
            cp.start()
        for d in range(1, ndev):
            px, py, pc = peer(d)
            copy(d, 4 * px + 2 * py + pc).wait_recv()
        for cp in sends:
            cp.wait_send()
        acc = buf[0]
        for k in range(1, ndev):
            acc = acc + buf[k]
        o_ref[...] = acc

    return pl.pallas_call(
        body, name="small_allreduce", in_specs=[VMEM_SPEC], out_specs=VMEM_SPEC,
        out_shape=jax.ShapeDtypeStruct((R, C), F32),
        scratch_shapes=[pltpu.VMEM((ndev, R, C), F32), pltpu.SemaphoreType.DMA((ndev - 1,)),
                        pltpu.SemaphoreType.DMA((ndev - 1,))],
        compiler_params=pltpu.CompilerParams(vmem_limit_bytes=32 << 20),
    )(p)


def kernel(x, norm1_g, w_in, conv_w, conv_b, conv_ln_g, conv_ln_b, ret_gn_g, w_out, norm2_g, w_gate, w_up, w_down, final_g, loss_target, m_norm1_g, m_w_in, m_conv_w, m_conv_b, m_conv_ln_g, m_conv_ln_b, m_ret_gn_g, m_w_out, m_norm2_g, m_w_gate, m_w_up, m_w_down, m_final_g, v_norm1_g, v_w_in, v_conv_w, v_conv_b, v_conv_ln_g, v_conv_ln_b, v_ret_gn_g, v_w_out, v_norm2_g, v_w_gate, v_w_up, v_w_down, v_final_g):
    S = x.shape[1]
    xs = x.reshape(S, D)
    tgt = loss_target.reshape(S, D)
    fsh = FF // NCHIP

    def shards_of(l):
        return [w_in[l].astype(BF), w_out[l].astype(BF), w_gate[l].T.astype(BF), w_up[l].T.astype(BF),
                w_down[l].astype(BF), conv_w[l]]

    gather_axes = [1, 0, 0, 0, 0, 1]
    shard_cache = [shards_of(l) for l in range(L)]
    tables = _ret_tables(S)
    row = lambda a, l: a[l].reshape(1, -1)
    groups = {}
    weights = [dict() for _ in range(L)]

    def begin(l, which, after):
        group, token = gather_start([shard_cache[l][i] for i in which], [gather_axes[i] for i in which],
                                    after, f"{l}_{which[0]}")
        groups[(l, which[0])] = (group, which)
        return token

    def finish(l, firsts, after, tag):
        gs = [groups[(l, f)] for f in firsts]
        outs = gather_wait([g for g, _ in gs], after, f"{l}_{tag}")
        k = 0
        for _, which in gs:
            for i in which:
                weights[l][i] = outs[k]
                k += 1

    def host(stage, after):
        tokens = []
        for gl, which in stage:
            tokens.append(begin(gl, which, [after] + tokens))
        return tokens

    def hosted(l):
        if l == 0:
            return {"in": [(0, [2])], "conv": [(0, [3])], "ret": [(0, [4])], "out": [(1, [0])],
                    "mlp": [(1, [1, 5]), (1, [2])]}
        nxt = {"ret": [(l + 1, [0])], "out": [(l + 1, [1, 5])], "mlp": [(l + 1, [2])]} if l + 1 < L else {}
        return {"in": [(l, [3])], "conv": [(l, [4])], **nxt}

    first = begin(0, [0], [])
    after = begin(0, [1, 5], [first])
    saved = []
    xc = xs
    for l in range(L):
        sched = hosted(l)
        finish(l, [0, 1], after, "a")
        win, wout, cw = weights[l][0], weights[l][1], weights[l][5]
        proj = in_proj(xc, row(norm1_g, l), win, l, host(sched.get("in", []), win))
        u1, u = conv_fwd(proj, cw, row(conv_b, l), row(conv_ln_g, l), row(conv_ln_b, l), l,
                         host(sched.get("conv", []), proj))
        r_raw, states = ret_fwd(proj, tables, l, host(sched.get("ret", []), u))
        x2, mixed = out_proj(u, r_raw, proj, row(ret_gn_g, l), wout, xc, l, host(sched.get("out", []), r_raw))
        finish(l, [2, 3, 4], x2, "b")
        wgt, wut, wd = weights[l][2], weights[l][3], weights[l][4]
        x3, gs, us = mlp_fwd(x2, row(norm2_g, l), wgt, wut, wd, l, host(sched.get("mlp", []), wd))
        saved.append((xc, proj, u1, r_raw, states, mixed, x2, gs, us))
        xc = x3
        after = x3

    dx, loss_acc, d_final = final_loss(xc, final_g.reshape(1, D), tgt)
    loss = lax.psum(loss_acc[0, 0] * (0.5 / D), ("x", "y", "c"))

    scatter_axes = [1, 0, 0, 0, 0]
    scatter_sizes = [INW // NCHIP, D // NCHIP, fsh, fsh, fsh]
    lands = [lax.empty((NCHIP, L, D, INW // NCHIP), BF), lax.empty((NCHIP, L, D // NCHIP, D), BF),
             lax.empty((NCHIP, L, fsh, D), BF), lax.empty((NCHIP, L, fsh, D), BF), lax.empty((NCHIP, L, fsh, D), BF)]
    sent, sent_which = [], []

    def send_grad(g, a, l):
        group, new_land, token = scatter_start([g], [scatter_axes[a]], [scatter_sizes[a]], [lands[a]], l, f"{l}_{a}")
        lands[a] = new_land[0]
        sent.append(group)
        sent_which.append([a])
        return [token]

    small = [None] * L
    dep = []
    for l in reversed(range(L)):
        xin, proj, u1, r_raw, states, mixed, x2, gs, us = saved[l]
        win, wout, wgt, wut, wd, cw = (weights[l][i] for i in range(6))
        dx2, dgs, dus, act, h2, d_n2 = mlp_bwd(dx, x2, row(norm2_g, l), gs, us, wgt, wut, wd, l, dep)
        g_wd = wgrad(act, dx, l, "wgrad_down")
        g_wgt = wgrad(dgs, h2, l, "wgrad_gate", send_grad(g_wd, 4, l))
        g_wut = wgrad(dus, h2, l, "wgrad_up", send_grad(g_wgt, 2, l))
        dgate, dr, du1, sums = out_proj_bwd(dx2, wout, r_raw, proj, row(ret_gn_g, l), u1,
                                            row(conv_ln_g, l), row(conv_ln_b, l), l, send_grad(g_wut, 3, l))
        g_wout = wgrad(mixed, dx2, l, "wgrad_out")
        dab, dwb = conv_bwd(du1, proj, cw, l, send_grad(g_wout, 1, l))
        dq, dk, dv = ret_bwd(dr, proj, states, tables, l)
        dproj = [dab, dq, dk, dv, dgate]
        dx, h1, d_n1 = in_proj_bwd(dproj, win, xin, row(norm1_g, l), dx2, l)
        g_win = wgrad_in(h1, dproj, l)
        dep = send_grad(g_win, 0, l)
        small[l] = jnp.concatenate([dwb, sums, d_n1.reshape(2, CW), d_n2.reshape(2, CW)], axis=0)
    grad_x = dx.reshape(1, S, D)

    per = CK + 1 + 8 + 4
    packed = jnp.concatenate(small + [d_final.reshape(2, CW), jnp.zeros((6, CW), F32)], axis=0)
    tot = small_allreduce(packed)
    lay = tot[:L * per].reshape(L, per, CW)
    g_conv_w_full = lay[:, 0:CK, :]
    j = 2 * lax.axis_index("x") + lax.axis_index("y")
    g_conv_w = lax.dynamic_slice_in_dim(g_conv_w_full, j * (CW // NCHIP), CW // NCHIP, axis=2)
    g_small = {
        "conv_b": lay[:, CK, :], "ret_gn_g": lay[:, CK + 1, :], "conv_ln_g": lay[:, CK + 2, :],
        "conv_ln_b": lay[:, CK + 3, :], "norm1_g": lay[:, CK + 9:CK + 11, :].reshape(L, D),
        "norm2_g": lay[:, CK + 11:CK + 13, :].reshape(L, D), "final_g": tot[L * per:L * per + 2].reshape(D),
    }

    recv = scatter_wait(sent, lands, sent_which)
    shard_shapes = [(L * D, INW // NCHIP), (L * D // NCHIP, D), (L * fsh, D), (L * fsh, D), (L * fsh, D)]
    names = ["w_in", "w_out", "w_gate", "w_up", "w_down"]
    parts = [sum_slots(r.reshape((NCHIP,) + shp), f"sum_{nm}") for r, shp, nm in zip(recv, shard_shapes, names)]
    theirs = sibling_swap(parts)

    def unT(a):
        return jnp.swapaxes(a.reshape(L, fsh, D), 1, 2).reshape(L * D, fsh)

    big = {}
    wmv = {"w_in": (w_in, m_w_in, v_w_in), "w_out": (w_out, m_w_out, v_w_out),
           "w_gate": (w_gate, m_w_gate, v_w_gate), "w_up": (w_up, m_w_up, v_w_up),
           "w_down": (w_down, m_w_down, v_w_down)}
    for nm, mine, other in zip(names, parts, theirs):
        w, m, v = wmv[nm]
        if nm in ("w_gate", "w_up"):
            mine, other = unT(mine), unT(other)
        shp2 = (w.shape[0] * w.shape[1], w.shape[2])
        outs = adamw(w.reshape(shp2), mine, other, m.reshape(shp2), v.reshape(shp2), f"adamw_{nm}")
        big[nm] = [o.reshape(w.shape) for o in outs]

    cshape = (L * CK, CW // NCHIP)
    zc = jnp.zeros(cshape, F32)
    big["conv_w"] = [o.reshape(conv_w.shape) for o in adamw(
        conv_w.reshape(cshape), g_conv_w.reshape(cshape), zc, m_conv_w.reshape(cshape),
        v_conv_w.reshape(cshape), "adamw_conv_w")]
    vec_names = ["norm1_g", "conv_b", "conv_ln_g", "conv_ln_b", "ret_gn_g", "norm2_g", "final_g"]
    vec_w = {"norm1_g": (norm1_g, m_norm1_g, v_norm1_g), "conv_b": (conv_b, m_conv_b, v_conv_b),
             "conv_ln_g": (conv_ln_g, m_conv_ln_g, v_conv_ln_g), "conv_ln_b": (conv_ln_b, m_conv_ln_b, v_conv_ln_b),
             "ret_gn_g": (ret_gn_g, m_ret_gn_g, v_ret_gn_g), "norm2_g": (norm2_g, m_norm2_g, v_norm2_g),
             "final_g": (final_g, m_final_g, v_final_g)}
    cat = lambda arrs: jnp.concatenate([a.reshape(-1, CW) for a in arrs], axis=0)
    vw = cat([vec_w[nm][0] for nm in vec_names])
    vm = cat([vec_w[nm][1] for nm in vec_names])
    vv = cat([vec_w[nm][2] for nm in vec_names])
    vg = cat([g_small[nm] for nm in vec_names])
    vouts = adamw(vw, vg, jnp.zeros_like(vg), vm, vv, "adamw_vectors")
    off = 0
    for nm in vec_names:
        w = vec_w[nm][0]
        nrow = w.size // CW
        big[nm] = [o[off:off + nrow].reshape(w.shape) for o in vouts]
        off += nrow

    order = ["norm1_g", "w_in", "conv_w", "conv_b", "conv_ln_g", "conv_ln_b", "ret_gn_g", "w_out", "norm2_g",
             "w_gate", "w_up", "w_down", "final_g"]
    return (loss, grad_x, *[big[nm][0] for nm in order], *[big[nm][1] for nm in order],
            *[big[nm][2] for nm in order], *[big[nm][3] for nm in order])
```

```python
import math

import jax
import jax.numpy as jnp
from jax import lax
from jax.experimental import pallas as pl
from jax.experimental.pallas import tpu as pltpu

D = 1024
L = 4
CW = 512
RW = 512
NH = 4
HD = 128
CK = 31
CHUNK = 64
INW = 3072
FF = 2816
NCHIP = 4
EPS = 1e-6
ROPE_BASE = 10000.0
SEQ_TILE = 512
MLP_ROWS = 1024
MLP_COLS = 256
HALO = 32
CONV_ROWS = 32

ADAM_LR = 0.001
ADAM_B1 = 0.9
ADAM_B2 = 0.999
ADAM_EPS = 1e-08
ADAM_WD = 0.01
ADAM_STEP = 10

BF = jnp.bfloat16
F32 = jnp.float32
MESH = pl.DeviceIdType.MESH
ANY = pl.BlockSpec(memory_space=pl.ANY)
VMEM_SPEC = pl.BlockSpec(memory_space=pltpu.VMEM)
HBM_SPEC = pl.BlockSpec(memory_space=pltpu.HBM)
SEM_SPEC = pl.BlockSpec(memory_space=pltpu.SEMAPHORE)
DATAFLOW = pltpu.SideEffectType.DATAFLOW_SIDE_EFFECTING


def _params(n_grid, vmem_mb):
    return pltpu.CompilerParams(dimension_semantics=("arbitrary",) * n_grid,
                                vmem_limit_bytes=vmem_mb << 20)


def _ordered_call(body, deps, *, in_specs, **kw):
    n, nd = len(in_specs), len(deps)

    def with_deps(*refs):
        body(*refs[:n], *refs[n + nd:])

    return pl.pallas_call(with_deps, in_specs=list(in_specs) + [ANY] * nd, **kw)


def _dot(a, b):
    return jnp.dot(a, b, preferred_element_type=F32)


def _dot_nt(a, b):
    return lax.dot_general(a, b, (((1,), (1,)), ((), ())), preferred_element_type=F32)


def _dot_tn(a, b):
    return lax.dot_general(a, b, (((0,), (0,)), ((), ())), preferred_element_type=F32)


def _sigmoid(x):
    return 0.5 * jnp.tanh(0.5 * x) + 0.5


def _mean(x):
    return jnp.mean(x, axis=-1, keepdims=True)


def _rot(t, cs, sn):
    return t * cs + pltpu.roll(t, HD // 2, 1) * sn


def _rot_t(dy, cs, sn):
    return dy * cs + pltpu.roll(dy * sn, HD // 2, 1)


def _rms_bwd(x, g, dh, dx_in):
    r = lax.rsqrt(_mean(x * x) + EPS)
    xh = x * r
    dxh = dh * g
    dx = dx_in + r * (dxh - xh * _mean(dxh * xh))
    return dx, jnp.sum(dh * xh, axis=0, keepdims=True), (xh * g).astype(BF)


def in_proj(x, g, win, l, deps=()):
    S = x.shape[0]
    tm = min(S, SEQ_TILE)

    def body(x_ref, g_ref, w_ref, o_ref):
        xv = x_ref[...]
        h = (xv * lax.rsqrt(_mean(xv * xv) + EPS) * g_ref[...]).astype(BF)
        o_ref[...] = _dot(h, w_ref[...])

    return _ordered_call(
        body, deps, name=f"in_proj_{l}", grid=(S // tm,),
        in_specs=[pl.BlockSpec((tm, D), lambda i: (i, 0)),
                  pl.BlockSpec((1, D), lambda i: (0, 0)),
                  pl.BlockSpec((D, INW), lambda i: (0, 0))],
        out_specs=pl.BlockSpec((tm, INW), lambda i: (i, 0)),
        out_shape=jax.ShapeDtypeStruct((S, INW), F32),
        compiler_params=_params(1, 48),
    )(x, g, win, *deps)


def conv_fwd(proj, cw, cb, lg, lb, l, deps=()):
    S = proj.shape[0]
    tc = min(S, SEQ_TILE)

    def body(a_ref, b_ref, w_ref, cb_ref, lg_ref, lb_ref, u1_ref, u_ref, buf):
        i = pl.program_id(0)

        @pl.when(i == 0)
        def _():
            buf[0:HALO, :] = jnp.zeros((HALO, CW), F32)

        @pl.when(i > 0)
        def _():
            buf[0:HALO, :] = buf[tc:tc + HALO, :]

        buf[HALO:HALO + tc, :] = a_ref[...] * _sigmoid(b_ref[...])
        off = HALO - (CK - 1)
        for r0 in range(0, tc, CONV_ROWS):
            acc = jnp.broadcast_to(cb_ref[...], (CONV_ROWS, CW))
            for k in range(CK):
                acc = acc + w_ref[k:k + 1, :] * buf[r0 + off + k:r0 + off + k + CONV_ROWS, :]
            u1_ref[r0:r0 + CONV_ROWS, :] = acc
            d = acc - _mean(acc)
            u2 = d * lax.rsqrt(_mean(d * d) + EPS) * lg_ref[...] + lb_ref[...]
            u_ref[r0:r0 + CONV_ROWS, :] = (u2 * _sigmoid(u2)).astype(BF)

    vec = pl.BlockSpec((1, CW), lambda i: (0, 0))
    return _ordered_call(
        body, deps, name=f"conv_fwd_{l}", grid=(S // tc,),
        in_specs=[pl.BlockSpec((tc, CW), lambda i: (i, 0)),
                  pl.BlockSpec((tc, CW), lambda i: (i, 1)),
                  pl.BlockSpec((CK, CW), lambda i: (0, 0)), vec, vec, vec],
        out_specs=[pl.BlockSpec((tc, CW), lambda i: (i, 0)),
                   pl.BlockSpec((tc, CW), lambda i: (i, 0))],
        out_shape=[jax.ShapeDtypeStruct((S, CW), F32), jax.ShapeDtypeStruct((S, CW), BF)],
        scratch_shapes=[pltpu.VMEM((tc + HALO, CW), F32)],
        compiler_params=_params(1, 32),
    )(proj, proj, cw, cb, lg, lb, *deps)


def _ret_tables(S):
    half = HD // 2
    pos = jnp.arange(S, dtype=F32)
    freqs = ROPE_BASE ** (-jnp.arange(half, dtype=F32) / half)
    ang = pos[:, None] * freqs[None, :]
    cos, sin = jnp.cos(ang), jnp.sin(ang)
    cosf = jnp.concatenate([cos, cos], axis=-1)
    sinf = jnp.concatenate([-sin, sin], axis=-1)
    log_g = jnp.log(1.0 - 2.0 ** (-5.0 - jnp.arange(NH, dtype=F32)))
    idx = jnp.arange(CHUNK, dtype=F32)
    dmat = jnp.exp(log_g[:, None, None] * jnp.abs(idx[:, None] - idx[None, :]))
    qdec = jnp.broadcast_to(jnp.exp(log_g[:, None] * (idx + 1.0))[:, :, None], (NH, CHUNK, HD))
    kdec = jnp.broadcast_to(jnp.exp(log_g[:, None] * (CHUNK - 1 - idx))[:, :, None], (NH, CHUNK, HD))
    cdec = jnp.broadcast_to(jnp.exp(log_g * CHUNK)[:, None, None], (NH, HD, HD))
    return cosf, sinf, dmat, qdec, kdec, cdec


def _ret_specs(tr, tmap):
    q0, k0, v0 = (2 * CW) // HD, (2 * CW + RW) // HD, (2 * CW + 2 * RW) // HD
    return [pl.BlockSpec((tr, HD), lambda h, t: (tmap(t), q0 + h)),
            pl.BlockSpec((tr, HD), lambda h, t: (tmap(t), k0 + h)),
            pl.BlockSpec((tr, HD), lambda h, t: (tmap(t), v0 + h)),
            pl.BlockSpec((tr, HD), lambda h, t: (tmap(t), 0)),
            pl.BlockSpec((tr, HD), lambda h, t: (tmap(t), 0)),
            pl.BlockSpec((None, CHUNK, CHUNK), lambda h, t: (h, 0, 0)),
            pl.BlockSpec((None, CHUNK, HD), lambda h, t: (h, 0, 0)),
            pl.BlockSpec((None, CHUNK, HD), lambda h, t: (h, 0, 0)),
            pl.BlockSpec((None, HD, HD), lambda h, t: (h, 0, 0))]


def ret_fwd(proj, tables, l, deps=()):
    S = proj.shape[0]
    tr = min(S, SEQ_TILE)
    cpb = tr // CHUNK
    scale = HD ** -0.5

    def body(q_ref, k_ref, v_ref, cos_ref, sin_ref, dm_ref, qd_ref, kd_ref, cd_ref,
             r_ref, st_ref, st):
        @pl.when(pl.program_id(1) == 0)
        def _():
            st[...] = jnp.zeros((HD, HD), F32)

        for c in range(cpb):
            rows = slice(c * CHUNK, (c + 1) * CHUNK)
            cs, sn = cos_ref[rows, :], sin_ref[rows, :]
            qr = _rot(q_ref[rows, :], cs, sn)
            kr = _rot(k_ref[rows, :], cs, sn) * scale
            vb = v_ref[rows, :].astype(BF)
            s = st[...]
            st_ref[c] = s
            sc = _dot_nt(qr.astype(BF), kr.astype(BF)) * dm_ref[...]
            out = _dot(sc.astype(BF), vb) + _dot((qr * qd_ref[...]).astype(BF), s.astype(BF))
            r_ref[rows, :] = out
            st[...] = cd_ref[...] * s + _dot_tn((kr * kd_ref[...]).astype(BF), vb)

    return _ordered_call(
        body, deps, name=f"ret_fwd_{l}", grid=(NH, S // tr),
        in_specs=_ret_specs(tr, lambda t: t),
        out_specs=[pl.BlockSpec((tr, HD), lambda h, t: (t, h)),
                   pl.BlockSpec((None, cpb, HD, HD), lambda h, t: (h, t, 0, 0))],
        out_shape=[jax.ShapeDtypeStruct((S, RW), F32),
                   jax.ShapeDtypeStruct((NH, S // CHUNK, HD, HD), F32)],
        scratch_shapes=[pltpu.VMEM((HD, HD), F32)],
        compiler_params=_params(2, 32),
    )(proj, proj, proj, *tables, *deps)


def out_proj(u, r_raw, proj, gn, wout, x, l, deps=()):
    S = x.shape[0]
    tm = min(S, SEQ_TILE)
    gate_blk = (2 * CW + 3 * RW) // RW

    def body(u_ref, r_ref, gate_ref, gn_ref, w_ref, x_ref, x2_ref, mix_ref):
        mix_ref[:, 0:CW] = u_ref[...]
        gt = gate_ref[...]
        sil = gt * _sigmoid(gt) * gn_ref[...]
        for h in range(NH):
            cols = slice(h * HD, (h + 1) * HD)
            rh = r_ref[:, cols]
            d = rh - _mean(rh)
            rn = d * lax.rsqrt(_mean(d * d) + EPS)
            mix_ref[:, CW + h * HD:CW + (h + 1) * HD] = (rn * sil[:, cols]).astype(BF)
        x2_ref[...] = x_ref[...] + _dot(mix_ref[...], w_ref[...])

    return _ordered_call(
        body, deps, name=f"out_proj_{l}", grid=(S // tm,),
        in_specs=[pl.BlockSpec((tm, CW), lambda i: (i, 0)),
                  pl.BlockSpec((tm, RW), lambda i: (i, 0)),
                  pl.BlockSpec((tm, RW), lambda i: (i, gate_blk)),
                  pl.BlockSpec((1, RW), lambda i: (0, 0)),
                  pl.BlockSpec((D, D), lambda i: (0, 0)),
                  pl.BlockSpec((tm, D), lambda i: (i, 0))],
        out_specs=[pl.BlockSpec((tm, D), lambda i: (i, 0)),
                   pl.BlockSpec((tm, D), lambda i: (i, 0))],
        out_shape=[jax.ShapeDtypeStruct((S, D), F32), jax.ShapeDtypeStruct((S, D), BF)],
        compiler_params=_params(1, 40),
    )(u, r_raw, proj, gn, wout, x, *deps)


def mlp_fwd(x2, g2, wgt, wut, wd, l, deps=()):
    S = x2.shape[0]
    tm, tf = min(S, MLP_ROWS), MLP_COLS
    nk = FF // tf

    def body(x_ref, g_ref, wg_ref, wu_ref, wd_ref, o_ref, gs_ref, us_ref, h_scr, acc):
        k = pl.program_id(1)

        @pl.when(k == 0)
        def _():
            xv = x_ref[...]
            h_scr[...] = (xv * lax.rsqrt(_mean(xv * xv) + EPS) * g_ref[...]).astype(BF)
            acc[...] = jnp.zeros((tm, D), F32)

        gv = _dot_nt(h_scr[...], wg_ref[...])
        uv = _dot_nt(h_scr[...], wu_ref[...])
        gs_ref[...] = gv.astype(BF)
        us_ref[...] = uv.astype(BF)
        acc[...] += _dot((gv * _sigmoid(gv) * uv).astype(BF), wd_ref[...])

        @pl.when(k == nk - 1)
        def _():
            o_ref[...] = x_ref[...] + acc[...]

    wspec = pl.BlockSpec((tf, D), lambda i, k: (k, 0))
    return _ordered_call(
        body, deps, name=f"mlp_fwd_{l}", grid=(S // tm, nk),
        in_specs=[pl.BlockSpec((tm, D), lambda i, k: (i, 0)),
                  pl.BlockSpec((1, D), lambda i, k: (0, 0)), wspec, wspec, wspec],
        out_specs=[pl.BlockSpec((tm, D), lambda i, k: (i, 0)),
                   pl.BlockSpec((tm, tf), lambda i, k: (i, k)),
                   pl.BlockSpec((tm, tf), lambda i, k: (i, k))],
        out_shape=[jax.ShapeDtypeStruct((S, D), F32), jax.ShapeDtypeStruct((S, FF), BF),
                   jax.ShapeDtypeStruct((S, FF), BF)],
        scratch_shapes=[pltpu.VMEM((tm, D), BF), pltpu.VMEM((tm, D), F32)],
        compiler_params=_params(2, 56),
    )(x2, g2, wgt, wut, wd, *deps)


def final_loss(x, gf, tgt):
    S = x.shape[0]
    tm = min(S, SEQ_TILE)

    def body(x_ref, g_ref, t_ref, dx_ref, loss_ref, dg_ref):
        @pl.when(pl.program_id(0) == 0)
        def _():
            loss_ref[...] = jnp.zeros((8, 128), F32)
            dg_ref[...] = jnp.zeros((1, D), F32)

        xv = x_ref[...]
        r = lax.rsqrt(_mean(xv * xv) + EPS)
        xh = xv * r
        diff = xh * g_ref[...] - t_ref[...]
        loss_ref[...] += jnp.sum(jnp.sum(diff * diff, axis=-1, keepdims=True), axis=0, keepdims=True)
        dy = diff * (1.0 / D)
        dg_ref[...] += jnp.sum(dy * xh, axis=0, keepdims=True)
        dxh = dy * g_ref[...]
        dx_ref[...] = r * (dxh - xh * _mean(dxh * xh))

    return pl.pallas_call(
        body, name="final_loss", grid=(S // tm,),
        in_specs=[pl.BlockSpec((tm, D), lambda i: (i, 0)),
                  pl.BlockSpec((1, D), lambda i: (0, 0)),
                  pl.BlockSpec((tm, D), lambda i: (i, 0))],
        out_specs=[pl.BlockSpec((tm, D), lambda i: (i, 0)),
                   pl.BlockSpec((8, 128), lambda i: (0, 0)),
                   pl.BlockSpec((1, D), lambda i: (0, 0))],
        out_shape=[jax.ShapeDtypeStruct((S, D), F32), jax.ShapeDtypeStruct((8, 128), F32),
                   jax.ShapeDtypeStruct((1, D), F32)],
        compiler_params=_params(1, 40),
    )(x, gf, tgt)


def mlp_bwd(dx3, x2, g2, gs, us, wgt, wut, wd, l, deps=()):
    S = x2.shape[0]
    tm, tf = min(S, MLP_ROWS), MLP_COLS
    nk = FF // tf

    def body(dx_ref, x_ref, g_ref, gs_ref, us_ref, wg_ref, wu_ref, wd_ref,
             dx2_ref, dg_ref, du_ref, a_ref, h_ref, dgain_ref, dxb, acc):
        i, k = pl.program_id(0), pl.program_id(1)

        @pl.when(k == 0)
        def _():
            dxb[...] = dx_ref[...].astype(BF)
            acc[...] = jnp.zeros((tm, D), F32)

        @pl.when((k == 0) & (i == 0))
        def _():
            dgain_ref[...] = jnp.zeros((1, D), F32)

        da = _dot_nt(dxb[...], wd_ref[...])
        gv = gs_ref[...].astype(F32)
        uv = us_ref[...].astype(F32)
        sg = _sigmoid(gv)
        sil = gv * sg
        dgv = (da * uv * (sg * (1.0 + gv * (1.0 - sg)))).astype(BF)
        duv = (da * sil).astype(BF)
        a_ref[...] = (sil * uv).astype(BF)
        dg_ref[...] = dgv
        du_ref[...] = duv
        acc[...] += _dot(dgv, wg_ref[...]) + _dot(duv, wu_ref[...])

        @pl.when(k == nk - 1)
        def _():
            dx2, dgain, hb = _rms_bwd(x_ref[...], g_ref[...], acc[...], dx_ref[...])
            dx2_ref[...] = dx2
            dgain_ref[...] += dgain
            h_ref[...] = hb

    wspec = pl.BlockSpec((tf, D), lambda i, k: (k, 0))
    row = pl.BlockSpec((tm, D), lambda i, k: (i, 0))
    wide = pl.BlockSpec((tm, tf), lambda i, k: (i, k))
    return _ordered_call(
        body, deps, name=f"mlp_bwd_{l}", grid=(S // tm, nk),
        in_specs=[row, row, pl.BlockSpec((1, D), lambda i, k: (0, 0)), wide, wide, wspec, wspec, wspec],
        out_specs=[row, wide, wide, wide, row, pl.BlockSpec((1, D), lambda i, k: (0, 0))],
        out_shape=[jax.ShapeDtypeStruct((S, D), F32), jax.ShapeDtypeStruct((S, FF), BF),
                   jax.ShapeDtypeStruct((S, FF), BF), jax.ShapeDtypeStruct((S, FF), BF),
                   jax.ShapeDtypeStruct((S, D), BF), jax.ShapeDtypeStruct((1, D), F32)],
        scratch_shapes=[pltpu.VMEM((tm, D), BF), pltpu.VMEM((tm, D), F32)],
        compiler_params=_params(2, 56),
    )(dx3, x2, g2, gs, us, wgt, wut, wd, *deps)


def wgrad(a, b, l, name, deps=()):
    S, K = a.shape
    N = b.shape[1]
    tk = 1408 if K == FF else min(K, 1024)
    tn = 768 if N == INW else min(N, 1024)
    ts = min(S, SEQ_TILE)
    ns = S // ts

    def body(a_ref, b_ref, o_ref, acc):
        s = pl.program_id(2)

        @pl.when(s == 0)
        def _():
            acc[...] = jnp.zeros((tk, tn), F32)

        acc[...] += _dot_tn(a_ref[...], b_ref[...].astype(BF))

        @pl.when(s == ns - 1)
        def _():
            o_ref[...] = acc[...].astype(BF)

    return _ordered_call(
        body, deps, name=f"{name}_{l}", grid=(K // tk, N // tn, ns),
        in_specs=[pl.BlockSpec((ts, tk), lambda i, j, s: (s, i)),
                  pl.BlockSpec((ts, tn), lambda i, j, s: (s, j))],
        out_specs=pl.BlockSpec((tk, tn), lambda i, j, s: (i, j)),
        out_shape=jax.ShapeDtypeStruct((K, N), BF),
        scratch_shapes=[pltpu.VMEM((tk, tn), F32)],
        compiler_params=_params(3, 48),
    )(a, b, *deps)


def out_proj_bwd(dx2, wout, r_raw, proj, gn, u1, lg, lb, l, deps=()):
    S = dx2.shape[0]
    tm = min(S, SEQ_TILE)
    gate_blk = (2 * CW + 3 * RW) // RW

    def body(dx_ref, w_ref, r_ref, gate_ref, gn_ref, u1_ref, lg_ref, lb_ref,
             dgate_ref, dr_ref, du1_ref, sums_ref):
        @pl.when(pl.program_id(0) == 0)
        def _():
            sums_ref[...] = jnp.zeros((8, CW), F32)

        dmix = _dot_nt(dx_ref[...].astype(BF), w_ref[...])
        gt = gate_ref[...]
        sg = _sigmoid(gt)
        sil = gt * sg
        dsil = sg * (1.0 + gt * (1.0 - sg))
        for h in range(NH):
            cols = slice(h * HD, (h + 1) * HD)
            rh = r_ref[:, cols]
            d = rh - _mean(rh)
            rs = lax.rsqrt(_mean(d * d) + EPS)
            rn = d * rs
            drr = dmix[:, CW + h * HD:CW + (h + 1) * HD]
            gnh = gn_ref[:, cols]
            sums_ref[0:1, cols] += jnp.sum(drr * rn * sil[:, cols], axis=0, keepdims=True)
            dgate_ref[:, cols] = (drr * rn * gnh * dsil[:, cols]).astype(BF)
            drn = drr * gnh * sil[:, cols]
            dr_ref[:, cols] = rs * (drn - _mean(drn) - rn * _mean(drn * rn))
        du = dmix[:, 0:CW]
        u1 = u1_ref[...]
        d = u1 - _mean(u1)
        rs = lax.rsqrt(_mean(d * d) + EPS)
        xh = d * rs
        u2 = xh * lg_ref[...] + lb_ref[...]
        sg2 = _sigmoid(u2)
        du2 = du * (sg2 * (1.0 + u2 * (1.0 - sg2)))
        sums_ref[1:2, :] += jnp.sum(du2 * xh, axis=0, keepdims=True)
        sums_ref[2:3, :] += jnp.sum(du2, axis=0, keepdims=True)
        dxh = du2 * lg_ref[...]
        du1_ref[...] = rs * (dxh - _mean(dxh) - xh * _mean(dxh * xh))

    vec = pl.BlockSpec((1, CW), lambda i: (0, 0))
    half = pl.BlockSpec((tm, CW), lambda i: (i, 0))
    return _ordered_call(
        body, deps, name=f"out_proj_bwd_{l}", grid=(S // tm,),
        in_specs=[pl.BlockSpec((tm, D), lambda i: (i, 0)),
                  pl.BlockSpec((D, D), lambda i: (0, 0)),
                  half, pl.BlockSpec((tm, RW), lambda i: (i, gate_blk)), vec, half, vec, vec],
        out_specs=[half, half, half, pl.BlockSpec((8, CW), lambda i: (0, 0))],
        out_shape=[jax.ShapeDtypeStruct((S, RW), BF), jax.ShapeDtypeStruct((S, RW), F32),
                   jax.ShapeDtypeStruct((S, CW), F32), jax.ShapeDtypeStruct((8, CW), F32)],
        compiler_params=_params(1, 40),
    )(dx2, wout, r_raw, proj, gn, u1, lg, lb, *deps)


def conv_bwd(du1, proj, cw, l, deps=()):
    S = proj.shape[0]
    tc = min(S, SEQ_TILE)
    nt = S // tc

    def body(du1_ref, a_ref, b_ref, w_ref, dab_ref, dwb_ref, buf):
        i = pl.program_id(0)

        @pl.when(i == 0)
        def _():
            buf[tc:tc + HALO, :] = jnp.zeros((HALO, CW), F32)
            dwb_ref[...] = jnp.zeros((CK + 1, CW), F32)

        @pl.when(i > 0)
        def _():
            buf[tc:tc + HALO, :] = buf[0:HALO, :]

        buf[0:tc, :] = du1_ref[...]
        parts = [jnp.zeros((8, CW), F32) for _ in range(CK)]
        for r0 in range(0, tc, CONV_ROWS):
            av = a_ref[r0:r0 + CONV_ROWS, :]
            sgb = _sigmoid(b_ref[r0:r0 + CONV_ROWS, :])
            u0 = av * sgb
            acc = jnp.zeros((CONV_ROWS, CW), F32)
            for j in range(CK):
                sl = buf[r0 + j:r0 + j + CONV_ROWS, :]
                acc = acc + w_ref[CK - 1 - j:CK - j, :] * sl
                pr = u0 * sl
                red = pr[0:8, :]
                for q in range(1, CONV_ROWS // 8):
                    red = red + pr[8 * q:8 * q + 8, :]
                parts[CK - 1 - j] = parts[CK - 1 - j] + red
            dab_ref[r0:r0 + CONV_ROWS, 0:CW] = (acc * sgb).astype(BF)
            dab_ref[r0:r0 + CONV_ROWS, CW:2 * CW] = (acc * av * sgb * (1.0 - sgb)).astype(BF)
        for k in range(CK):
            dwb_ref[k:k + 1, :] += jnp.sum(parts[k], axis=0, keepdims=True)
        dwb_ref[CK:CK + 1, :] += jnp.sum(du1_ref[...], axis=0, keepdims=True)

    return _ordered_call(
        body, deps, name=f"conv_bwd_{l}", grid=(nt,),
        in_specs=[pl.BlockSpec((tc, CW), lambda i: (nt - 1 - i, 0)),
                  pl.BlockSpec((tc, CW), lambda i: (nt - 1 - i, 0)),
                  pl.BlockSpec((tc, CW), lambda i: (nt - 1 - i, 1)),
                  pl.BlockSpec((CK, CW), lambda i: (0, 0))],
        out_specs=[pl.BlockSpec((tc, 2 * CW), lambda i: (nt - 1 - i, 0)),
                   pl.BlockSpec((CK + 1, CW), lambda i: (0, 0))],
        out_shape=[jax.ShapeDtypeStruct((S, 2 * CW), BF), jax.ShapeDtypeStruct((CK + 1, CW), F32)],
        scratch_shapes=[pltpu.VMEM((tc + HALO, CW), F32)],
        compiler_params=_params(1, 32),
    )(du1, proj, proj, cw, *deps)


def ret_bwd(dr, proj, states, tables, l):
    S = proj.shape[0]
    tr = min(S, SEQ_TILE)
    cpb = tr // CHUNK
    nt = S // tr
    scale = HD ** -0.5

    def body(q_ref, k_ref, v_ref, cos_ref, sin_ref, dm_ref, qd_ref, kd_ref, cd_ref, dr_ref, st_ref,
             dq_ref, dk_ref, dv_ref, gst):
        @pl.when(pl.program_id(1) == 0)
        def _():
            gst[...] = jnp.zeros((HD, HD), F32)

        for c in reversed(range(cpb)):
            rows = slice(c * CHUNK, (c + 1) * CHUNK)
            cs, sn = cos_ref[rows, :], sin_ref[rows, :]
            qr = _rot(q_ref[rows, :], cs, sn)
            kr = _rot(k_ref[rows, :], cs, sn) * scale
            qb, kb = qr.astype(BF), kr.astype(BF)
            vb = v_ref[rows, :].astype(BF)
            dob = dr_ref[rows, :].astype(BF)
            sb = st_ref[c].astype(BF)
            gn1 = gst[...]
            gb = gn1.astype(BF)
            sc = (_dot_nt(qb, kb) * dm_ref[...]).astype(BF)
            dsc = (_dot_nt(dob, vb) * dm_ref[...]).astype(BF)
            dqr = _dot(dsc, kb) + _dot_nt(dob, sb) * qd_ref[...]
            dkr = _dot_tn(dsc, qb) + _dot_nt(vb, gb) * kd_ref[...]
            dvv = _dot_tn(sc, dob) + _dot((kr * kd_ref[...]).astype(BF), gb)
            gst[...] = cd_ref[...] * gn1 + _dot_tn((qr * qd_ref[...]).astype(BF), dob)
            dq_ref[rows, :] = _rot_t(dqr, cs, sn).astype(BF)
            dk_ref[rows, :] = _rot_t(dkr * scale, cs, sn).astype(BF)
            dv_ref[rows, :] = dvv.astype(BF)

    rev = lambda t: nt - 1 - t
    hblk = pl.BlockSpec((tr, HD), lambda h, t: (rev(t), h))
    return pl.pallas_call(
        body, name=f"ret_bwd_{l}", grid=(NH, nt),
        in_specs=_ret_specs(tr, rev) + [hblk, pl.BlockSpec((None, cpb, HD, HD), lambda h, t: (h, rev(t), 0, 0))],
        out_specs=[hblk, hblk, hblk],
        out_shape=[jax.ShapeDtypeStruct((S, RW), BF)] * 3,
        scratch_shapes=[pltpu.VMEM((HD, HD), F32)],
        compiler_params=_params(2, 32),
    )(proj, proj, proj, *tables, dr, states)


def in_proj_bwd(parts, win, x, g, dx2, l):
    S = x.shape[0]
    tm = min(S, SEQ_TILE)
    n = len(parts)

    def body(*refs):
        srcs = refs[:n]
        w_ref, x_ref, g_ref, dx2_ref, dx_ref, h_ref, dgain_ref = refs[n:]

        @pl.when(pl.program_id(0) == 0)
        def _():
            dgain_ref[...] = jnp.zeros((1, D), F32)

        dh, col = None, 0
        for r in srcs:
            width = r.shape[1]
            term = _dot_nt(r[...], w_ref[:, col:col + width])
            dh = term if dh is None else dh + term
            col += width
        dx, dgain, hb = _rms_bwd(x_ref[...], g_ref[...], dh, dx2_ref[...])
        dx_ref[...] = dx
        dgain_ref[...] += dgain
        h_ref[...] = hb

    row = pl.BlockSpec((tm, D), lambda i: (i, 0))
    vec = pl.BlockSpec((1, D), lambda i: (0, 0))
    return pl.pallas_call(
        body, name=f"in_proj_bwd_{l}", grid=(S // tm,),
        in_specs=[pl.BlockSpec((tm, p.shape[1]), lambda i: (i, 0)) for p in parts]
        + [pl.BlockSpec((D, INW), lambda i: (0, 0)), row, vec, row],
        out_specs=[row, row, vec],
        out_shape=[jax.ShapeDtypeStruct((S, D), F32), jax.ShapeDtypeStruct((S, D), BF),
                   jax.ShapeDtypeStruct((1, D), F32)],
        compiler_params=_params(1, 48),
    )(*parts, win, x, g, dx2)


def wgrad_in(h, parts, l):
    S = h.shape[0]
    ts = min(S, SEQ_TILE)
    ns = S // ts
    tn = 512
    tiles = [(p, c) for p, a in enumerate(parts) for c in range(a.shape[1] // tn)]
    first = [min(j for j, (q, _) in enumerate(tiles) if q == p) for p in range(len(parts))]
    count = [a.shape[1] // tn for a in parts]

    def body(*refs):
        h_ref, srcs = refs[0], refs[1:1 + len(parts)]
        o_ref, acc = refs[-2], refs[-1]
        j, s = pl.program_id(0), pl.program_id(1)

        @pl.when(s == 0)
        def _():
            acc[...] = jnp.zeros((D, tn), F32)

        for jj, (p, _) in enumerate(tiles):
            @pl.when(j == jj)
            def _(p=p):
                acc[...] += _dot_tn(h_ref[...], srcs[p][...])

        @pl.when(s == ns - 1)
        def _():
            o_ref[...] = acc[...].astype(BF)

    def part_spec(p):
        def index(j, s):
            mine = (j >= first[p]) & (j < first[p] + count[p])
            return jnp.where(mine, s, 0), jnp.clip(j - first[p], 0, count[p] - 1)
        return pl.BlockSpec((ts, tn), index)

    return pl.pallas_call(
        body, name=f"wgrad_in_{l}", grid=(len(tiles), ns),
        in_specs=[pl.BlockSpec((ts, D), lambda j, s: (s, 0))] + [part_spec(p) for p in range(len(parts))],
        out_specs=pl.BlockSpec((D, tn), lambda j, s: (0, j)),
        out_shape=jax.ShapeDtypeStruct((D, INW), BF),
        scratch_shapes=[pltpu.VMEM((D, tn), F32)],
        compiler_params=_params(2, 40),
    )(h, *parts)


def sum_slots(recv, name):
    _, R, C = recv.shape
    tr = 256 if R % 256 == 0 else R

    def body(r_ref, o_ref):
        acc = r_ref[0].astype(F32)
        for k in range(1, NCHIP):
            acc = acc + r_ref[k].astype(F32)
        o_ref[...] = acc

    return pl.pallas_call(
        body, name=name, grid=(R // tr,),
        in_specs=[pl.BlockSpec((NCHIP, tr, C), lambda i: (0, i, 0))],
        out_specs=pl.BlockSpec((tr, C), lambda i: (i, 0)),
        out_shape=jax.ShapeDtypeStruct((R, C), F32),
        compiler_params=_params(1, 32),
    )(recv)


def adamw(w, ga, gb, m, v, name):
    R, C = w.shape
    tr = 256 if R % 256 == 0 else R
    c1 = 1.0 - ADAM_B1 ** ADAM_STEP
    c2 = 1.0 - ADAM_B2 ** ADAM_STEP

    def body(w_ref, ga_ref, gb_ref, m_ref, v_ref, g_out, d_out, m_out, v_out):
        g = ga_ref[...] + gb_ref[...]
        mn = ADAM_B1 * m_ref[...] + (1.0 - ADAM_B1) * g
        vn = ADAM_B2 * v_ref[...] + (1.0 - ADAM_B2) * (g * g)
        g_out[...] = g
        m_out[...] = mn
        v_out[...] = vn
        d_out[...] = -ADAM_LR * ((mn / c1) / (jnp.sqrt(vn / c2) + ADAM_EPS) + ADAM_WD * w_ref[...])

    blk = pl.BlockSpec((tr, C), lambda i: (i, 0))
    return pl.pallas_call(
        body, name=name, grid=(R // tr,),
        in_specs=[blk] * 5, out_specs=[blk] * 4,
        out_shape=[jax.ShapeDtypeStruct((R, C), F32)] * 4,
        compiler_params=_params(1, 40),
    )(w, ga, gb, m, v)


def _place():
    x, y, c = lax.axis_index("x"), lax.axis_index("y"), lax.axis_index("c")
    chips = [(1 - x, y), (x, 1 - y), (1 - x, 1 - y)]
    return x, y, c, chips


def _window(ref, axis, j, size):
    idx = [slice(None)] * len(ref.shape)
    idx[axis] = pl.ds(pl.multiple_of(j * size, 128 if axis == len(ref.shape) - 1 else 16), size)
    return ref.at[tuple(idx)]


def _hbm(a):
    return pltpu.with_memory_space_constraint(a, pltpu.HBM)


def _hbm_like(arrs):
    return [pltpu.HBM(a.shape, a.dtype) for a in arrs]


def gather_start(shards, axes, after, tag):
    n = len(shards)
    na = len(after)
    lands = []
    for s, ax in zip(shards, axes):
        shp = list(s.shape)
        shp[ax] *= NCHIP
        lands.append(lax.empty(tuple(shp), s.dtype))

    def body(*refs):
        ins, land = refs[:n], refs[n:2 * n]
        send, recv = refs[2 * n + na], refs[2 * n + na + 1]
        token = refs[-1]
        x, y, c, chips = _place()
        for a in range(n):
            for k, chip in enumerate(chips):
                pltpu.make_async_remote_copy(
                    src_ref=ins[a], dst_ref=_window(land[a], axes[a], 2 * x + y, ins[a].shape[axes[a]]),
                    send_sem=send.at[3 * a + k], recv_sem=recv.at[3 * a + k],
                    device_id=(chip[0], chip[1], c), device_id_type=MESH).start()
        token[...] = jnp.zeros_like(token)

    outs = pl.pallas_call(
        body, name=f"gather_start_{tag}",
        in_specs=[HBM_SPEC] * (2 * n) + [ANY] * na,
        out_specs=(SEM_SPEC, SEM_SPEC, *[HBM_SPEC] * (2 * n), VMEM_SPEC),
        out_shape=(pltpu.SemaphoreType.DMA((3 * n,)), pltpu.SemaphoreType.DMA((3 * n,)),
                   *_hbm_like(shards), *_hbm_like(lands), jax.ShapeDtypeStruct((8, 128), F32)),
        input_output_aliases={a: 2 + a for a in range(2 * n)},
        compiler_params=pltpu.CompilerParams(has_side_effects=DATAFLOW),
    )(*[_hbm(s) for s in shards], *[_hbm(b) for b in lands], *after)
    return (outs[0], outs[1], list(outs[2:2 + n]), list(outs[2 + n:2 + 2 * n]), list(axes)), outs[-1]


def gather_wait(groups, after, tag):
    sizes = [len(g[2]) for g in groups]
    total = sum(sizes)

    def body(*refs):
        x, y, c, chips = _place()
        stage, loc = refs[-1 - total:-1], refs[-1]
        pos = 2 * total
        off = 0
        mine = []
        for g, n in zip(groups, sizes):
            ins, land = refs[off:off + n], refs[total + off:total + off + n]
            send_ref, recv_ref = refs[pos], refs[pos + 1]
            axes = g[4]
            for a in range(n):
                fetch = pltpu.make_async_copy(ins[a], stage[off + a], loc.at[2 * (off + a)])
                fetch.start()
                put = pltpu.make_async_copy(
                    stage[off + a], _window(land[a], axes[a], 2 * x + y, ins[a].shape[axes[a]]),
                    loc.at[2 * (off + a) + 1])
                mine.append((fetch, put))
                for k, chip in enumerate(chips):
                    cp = pltpu.make_async_remote_copy(
                        src_ref=ins[a],
                        dst_ref=_window(land[a], axes[a], 2 * chip[0] + chip[1], ins[a].shape[axes[a]]),
                        send_sem=send_ref.at[3 * a + k], recv_sem=recv_ref.at[3 * a + k],
                        device_id=(chip[0], chip[1], c), device_id_type=MESH)
                    cp.wait_send()
                    cp.wait_recv()
            pos += 2
            off += n
        for fetch, put in mine:
            fetch.wait()
            put.start()
        for fetch, put in mine:
            put.wait()

    shards = [s for g in groups for s in g[2]]
    lands = [b for g in groups for b in g[3]]
    sems = [s for g in groups for s in (g[0], g[1])]
    outs = pl.pallas_call(
        body, name=f"gather_wait_{tag}",
        in_specs=[HBM_SPEC] * (2 * total) + [SEM_SPEC] * len(sems) + [ANY],
        out_specs=[HBM_SPEC] * (2 * total),
        out_shape=(*_hbm_like(shards), *_hbm_like(lands)),
        input_output_aliases={a: a for a in range(2 * total)},
        scratch_shapes=[pltpu.VMEM(s.shape, s.dtype) for s in shards] + [pltpu.SemaphoreType.DMA((2 * total,))],
        compiler_params=pltpu.CompilerParams(has_side_effects=DATAFLOW, vmem_limit_bytes=32 << 20),
    )(*shards, *lands, *sems, after)
    return list(outs[total:])


def scatter_start(grads, axes, sizes, lands, l, tag):
    n = len(grads)

    def body(*refs):
        ins, land = refs[:n], refs[n:2 * n]
        send, recv = refs[2 * n], refs[2 * n + 1]
        token = refs[2 * n + 2 + 2 * n]
        stage, loc = refs[-1 - n:-1], refs[-1]
        x, y, c, chips = _place()
        me = 2 * x + y
        fetches = [pltpu.make_async_copy(_window(ins[a], axes[a], me, sizes[a]), stage[a], loc.at[2 * a])
                   for a in range(n)]
        for cp in fetches:
            cp.start()
        for a in range(n):
            for k, chip in enumerate(chips):
                pltpu.make_async_remote_copy(
                    src_ref=_window(ins[a], axes[a], 2 * chip[0] + chip[1], sizes[a]), dst_ref=land[a].at[me, l],
                    send_sem=send.at[3 * a + k], recv_sem=recv.at[3 * a + k],
                    device_id=(chip[0], chip[1], c), device_id_type=MESH).start()
        puts = [pltpu.make_async_copy(stage[a], land[a].at[me, l], loc.at[2 * a + 1]) for a in range(n)]
        for fetch, put in zip(fetches, puts):
            fetch.wait()
            put.start()
        for put in puts:
            put.wait()
        token[...] = jnp.zeros_like(token)

    outs = pl.pallas_call(
        body, name=f"scatter_start_{tag}",
        in_specs=[HBM_SPEC] * (2 * n),
        out_specs=(SEM_SPEC, SEM_SPEC, *[HBM_SPEC] * (2 * n), VMEM_SPEC),
        out_shape=(pltpu.SemaphoreType.DMA((3 * n,)), pltpu.SemaphoreType.DMA((3 * n,)),
                   *_hbm_like(grads), *_hbm_like(lands), jax.ShapeDtypeStruct((8, 128), F32)),
        input_output_aliases={a: 2 + a for a in range(2 * n)},
        scratch_shapes=[pltpu.VMEM(b.shape[2:], b.dtype) for b in lands] + [pltpu.SemaphoreType.DMA((2 * n,))],
        compiler_params=pltpu.CompilerParams(has_side_effects=DATAFLOW, vmem_limit_bytes=32 << 20),
    )(*[_hbm(g) for g in grads], *[_hbm(b) for b in lands])
    group = (outs[0], outs[1], list(outs[2:2 + n]), list(axes), list(sizes), l)
    return group, list(outs[2 + n:2 + 2 * n]), outs[-1]


def scatter_wait(groups, lands, which):
    nl = len(lands)

    def body(*refs):
        land = refs[:nl]
        x, y, c, chips = _place()
        pos = nl
        for g, wh in zip(groups, which):
            n = len(g[2])
            ins = refs[pos:pos + n]
            send_ref, recv_ref = refs[pos + n], refs[pos + n + 1]
            axes, sizes, l = g[3], g[4], g[5]
            for a in range(n):
                for k, chip in enumerate(chips):
                    jp = 2 * chip[0] + chip[1]
                    cp = pltpu.make_async_remote_copy(
                        src_ref=_window(ins[a], axes[a], jp, sizes[a]), dst_ref=land[wh[a]].at[jp, l],
                        send_sem=send_ref.at[3 * a + k], recv_sem=recv_ref.at[3 * a + k],
                        device_id=(chip[0], chip[1], c), device_id_type=MESH)
                    cp.wait_send()
                    cp.wait_recv()
            pos += n + 2

    operands = list(lands)
    specs = [HBM_SPEC] * nl
    for g in groups:
        operands += list(g[2]) + [g[0], g[1]]
        specs += [HBM_SPEC] * len(g[2]) + [SEM_SPEC, SEM_SPEC]
    outs = pl.pallas_call(
        body, name="scatter_wait", in_specs=specs, out_specs=[HBM_SPEC] * nl, out_shape=tuple(_hbm_like(lands)),
        input_output_aliases={a: a for a in range(nl)},
        compiler_params=pltpu.CompilerParams(has_side_effects=DATAFLOW),
    )(*operands)
    return list(outs)


def sibling_swap(parts):
    n = len(parts)

    def body(*refs):
        ins, outs = refs[:n], refs[n:2 * n]
        send, recv = refs[2 * n:]
        x, y, c, _ = _place()
        cps = [pltpu.make_async_remote_copy(src_ref=ins[a], dst_ref=outs[a], send_sem=send.at[a],
                                            recv_sem=recv.at[a], device_id=(x, y, 1 - c), device_id_type=MESH)
               for a in range(n)]
        for cp in cps:
            cp.start()
        for cp in cps:
            cp.wait_recv()
        for cp in cps:
            cp.wait_send()

    return pl.pallas_call(
        body, name="sibling_swap", in_specs=[ANY] * n, out_specs=[ANY] * n,
        out_shape=[jax.ShapeDtypeStruct(p.shape, p.dtype) for p in parts],
        scratch_shapes=[pltpu.SemaphoreType.DMA((n,)), pltpu.SemaphoreType.DMA((n,))],
    )(*parts)


def small_allreduce(p):
    R, C = p.shape
    ndev = 8

    def body(p_ref, o_ref, buf, send, recv):
        x, y, c, _ = _place()
        me = 4 * x + 2 * y + c
        buf[me] = p_ref[...]

        def peer(d):
            px = 1 - x if d & 4 else x
            py = 1 - y if d & 2 else y
            pc = 1 - c if d & 1 else c
            return px, py, pc

        def copy(d, slot):
            return pltpu.make_async_remote_copy(src_ref=p_ref, dst_ref=buf.at[slot], send_sem=send.at[d - 1],
                                                recv_sem=recv.at[d - 1], device_id=peer(d), device_id_type=MESH)

        sends = [copy(d, me) for d in range(1, ndev)]
        for cp in sends:
            cp.start()
        for d in range(1, ndev):
            px, py, pc = peer(d)
            copy(d, 4 * px + 2 * py + pc).wait_recv()
        for cp in sends:
            cp.wait_send()
        acc = buf[0]
        for k in range(1, ndev):
            acc = acc + buf[k]
        o_ref[...] = acc

    return pl.pallas_call(
        body, name="small_allreduce", in_specs=[VMEM_SPEC], out_specs=VMEM_SPEC,
        out_shape=jax.ShapeDtypeStruct((R, C), F32),
        scratch_shapes=[pltpu.VMEM((ndev, R, C), F32), pltpu.SemaphoreType.DMA((ndev - 1,)),
                        pltpu.SemaphoreType.DMA((ndev - 1,))],
        compiler_params=pltpu.CompilerParams(vmem_limit_bytes=32 << 20),
    )(p)


def kernel(x, norm1_g, w_in, conv_w, conv_b, conv_ln_g, conv_ln_b, ret_gn_g, w_out, norm2_g, w_gate, w_up, w_down, final_g, loss_target, m_norm1_g, m_w_in, m_conv_w, m_conv_b, m_conv_ln_g, m_conv_ln_b, m_ret_gn_g, m_w_out, m_norm2_g, m_w_gate, m_w_up, m_w_down, m_final_g, v_norm1_g, v_w_in, v_conv_w, v_conv_b, v_conv_ln_g, v_conv_ln_b, v_ret_gn_g, v_w_out, v_norm2_g, v_w_gate, v_w_up, v_w_down, v_final_g):
    S = x.shape[1]
    xs = x.reshape(S, D)
    tgt = loss_target.reshape(S, D)
    fsh = FF // NCHIP

    def shards_of(l):
        return [w_in[l].astype(BF), w_out[l].astype(BF), w_gate[l].T.astype(BF), w_up[l].T.astype(BF),
                w_down[l].astype(BF), conv_w[l]]

    gather_axes = [1, 0, 0, 0, 0, 1]
    shard_cache = [shards_of(l) for l in range(L)]
    tables = _ret_tables(S)
    row = lambda a, l: a[l].reshape(1, -1)
    groups = {}
    weights = [dict() for _ in range(L)]

    def begin(l, which, after):
        group, token = gather_start([shard_cache[l][i] for i in which], [gather_axes[i] for i in which],
                                    after, f"{l}_{which[0]}")
        groups[(l, which[0])] = (group, which)
        return token

    def finish(l, firsts, after, tag):
        gs = [groups[(l, f)] for f in firsts]
        outs = gather_wait([g for g, _ in gs], after, f"{l}_{tag}")
        k = 0
        for _, which in gs:
            for i in which:
                weights[l][i] = outs[k]
                k += 1

    def host(stage, after):
        tokens = []
        for gl, which in stage:
            tokens.append(begin(gl, which, [after] + tokens))
        return tokens

    def hosted(l):
        if l == 0:
            return {"in": [(0, [2])], "conv": [(0, [3])], "ret": [(0, [4])], "out": [(1, [0])],
                    "mlp": [(1, [1, 5]), (1, [2])]}
        nxt = {"ret": [(l + 1, [0])], "out": [(l + 1, [1, 5])], "mlp": [(l + 1, [2])]} if l + 1 < L else {}
        return {"in": [(l, [3])], "conv": [(l, [4])], **nxt}

    first = begin(0, [0], [])
    after = begin(0, [1, 5], [first])
    saved = []
    xc = xs
    for l in range(L):
        sched = hosted(l)
        finish(l, [0, 1], after, "a")
        win, wout, cw = weights[l][0], weights[l][1], weights[l][5]
        proj = in_proj(xc, row(norm1_g, l), win, l, host(sched.get("in", []), win))
        u1, u = conv_fwd(proj, cw, row(conv_b, l), row(conv_ln_g, l), row(conv_ln_b, l), l,
                         host(sched.get("conv", []), proj))
        r_raw, states = ret_fwd(proj, tables, l, host(sched.get("ret", []), u))
        x2, mixed = out_proj(u, r_raw, proj, row(ret_gn_g, l), wout, xc, l, host(sched.get("out", []), r_raw))
        finish(l, [2, 3, 4], x2, "b")
        wgt, wut, wd = weights[l][2], weights[l][3], weights[l][4]
        x3, gs, us = mlp_fwd(x2, row(norm2_g, l), wgt, wut, wd, l, host(sched.get("mlp", []), wd))
        saved.append((xc, proj, u1, r_raw, states, mixed, x2, gs, us))
        xc = x3
        after = x3

    dx, loss_acc, d_final = final_loss(xc, final_g.reshape(1, D), tgt)
    loss = lax.psum(loss_acc[0, 0] * (0.5 / D), ("x", "y", "c"))

    scatter_axes = [1, 0, 0, 0, 0]
    scatter_sizes = [INW // NCHIP, D // NCHIP, fsh, fsh, fsh]
    lands = [lax.empty((NCHIP, L, D, INW // NCHIP), BF), lax.empty((NCHIP, L, D // NCHIP, D), BF),
             lax.empty((NCHIP, L, fsh, D), BF), lax.empty((NCHIP, L, fsh, D), BF), lax.empty((NCHIP, L, fsh, D), BF)]
    sent, sent_which = [], []

    def send_grad(g, a, l):
        group, new_land, token = scatter_start([g], [scatter_axes[a]], [scatter_sizes[a]], [lands[a]], l, f"{l}_{a}")
        lands[a] = new_land[0]
        sent.append(group)
        sent_which.append([a])
        return [token]

    small = [None] * L
    dep = []
    for l in reversed(range(L)):
        xin, proj, u1, r_raw, states, mixed, x2, gs, us = saved[l]
        win, wout, wgt, wut, wd, cw = (weights[l][i] for i in range(6))
        dx2, dgs, dus, act, h2, d_n2 = mlp_bwd(dx, x2, row(norm2_g, l), gs, us, wgt, wut, wd, l, dep)
        g_wd = wgrad(act, dx, l, "wgrad_down")
        g_wgt = wgrad(dgs, h2, l, "wgrad_gate", send_grad(g_wd, 4, l))
        g_wut = wgrad(dus, h2, l, "wgrad_up", send_grad(g_wgt, 2, l))
        dgate, dr, du1, sums = out_proj_bwd(dx2, wout, r_raw, proj, row(ret_gn_g, l), u1,
                                            row(conv_ln_g, l), row(conv_ln_b, l), l, send_grad(g_wut, 3, l))
        g_wout = wgrad(mixed, dx2, l, "wgrad_out")
        dab, dwb = conv_bwd(du1, proj, cw, l, send_grad(g_wout, 1, l))
        dq, dk, dv = ret_bwd(dr, proj, states, tables, l)
        dproj = [dab, dq, dk, dv, dgate]
        dx, h1, d_n1 = in_proj_bwd(dproj, win, xin, row(norm1_g, l), dx2, l)
        g_win = wgrad_in(h1, dproj, l)
        dep = send_grad(g_win, 0, l)
        small[l] = jnp.concatenate([dwb, sums, d_n1.reshape(2, CW), d_n2.reshape(2, CW)], axis=0)
    grad_x = dx.reshape(1, S, D)

    per = CK + 1 + 8 + 4
    packed = jnp.concatenate(small + [d_final.reshape(2, CW), jnp.zeros((6, CW), F32)], axis=0)
    tot = small_allreduce(packed)
    lay = tot[:L * per].reshape(L, per, CW)
    g_conv_w_full = lay[:, 0:CK, :]
    j = 2 * lax.axis_index("x") + lax.axis_index("y")
    g_conv_w = lax.dynamic_slice_in_dim(g_conv_w_full, j * (CW // NCHIP), CW // NCHIP, axis=2)
    g_small = {
        "conv_b": lay[:, CK, :], "ret_gn_g": lay[:, CK + 1, :], "conv_ln_g": lay[:, CK + 2, :],
        "conv_ln_b": lay[:, CK + 3, :], "norm1_g": lay[:, CK + 9:CK + 11, :].reshape(L, D),
        "norm2_g": lay[:, CK + 11:CK + 13, :].reshape(L, D), "final_g": tot[L * per:L * per + 2].reshape(D),
    }

    recv = scatter_wait(sent, lands, sent_which)
    shard_shapes = [(L * D, INW // NCHIP), (L * D // NCHIP, D), (L * fsh, D), (L * fsh, D), (L * fsh, D)]
    names = ["w_in", "w_out", "w_gate", "w_up", "w_down"]
    parts = [sum_slots(r.reshape((NCHIP,) + shp), f"sum_{nm}") for r, shp, nm in zip(recv, shard_shapes, names)]
    theirs = sibling_swap(parts)

    def unT(a):
        return jnp.swapaxes(a.reshape(L, fsh, D), 1, 2).reshape(L * D, fsh)

    big = {}
    wmv = {"w_in": (w_in, m_w_in, v_w_in), "w_out": (w_out, m_w_out, v_w_out),
           "w_gate": (w_gate, m_w_gate, v_w_gate), "w_up": (w_up, m_w_up, v_w_up),
           "w_down": (w_down, m_w_down, v_w_down)}
    for nm, mine, other in zip(names, parts, theirs):
        w, m, v = wmv[nm]
        if nm in ("w_gate", "w_up"):
            mine, other = unT(mine), unT(other)
        shp2 = (w.shape[0] * w.shape[1], w.shape[2])
        outs = adamw(w.reshape(shp2), mine, other, m.reshape(shp2), v.reshape(shp2), f"adamw_{nm}")
        big[nm] = [o.reshape(w.shape) for o in outs]

    cshape = (L * CK, CW // NCHIP)
    zc = jnp.zeros(cshape, F32)
    big["conv_w"] = [o.reshape(conv_w.shape) for o in adamw(
        conv_w.reshape(cshape), g_conv_w.reshape(cshape), zc, m_conv_w.reshape(cshape),
        v_conv_w.reshape(cshape), "adamw_conv_w")]
    vec_names = ["norm1_g", "conv_b", "conv_ln_g", "conv_ln_b", "ret_gn_g", "norm2_g", "final_g"]
    vec_w = {"norm1_g": (norm1_g, m_norm1_g, v_norm1_g), "conv_b": (conv_b, m_conv_b, v_conv_b),
             "conv_ln_g": (conv_ln_g, m_conv_ln_g, v_conv_ln_g), "conv_ln_b": (conv_ln_b, m_conv_ln_b, v_conv_ln_b),
             "ret_gn_g": (ret_gn_g, m_ret_gn_g, v_ret_gn_g), "norm2_g": (norm2_g, m_norm2_g, v_norm2_g),
             "final_g": (final_g, m_final_g, v_final_g)}
    cat = lambda arrs: jnp.concatenate([a.reshape(-1, CW) for a in arrs], axis=0)
    vw = cat([vec_w[nm][0] for nm in vec_names])
    vm = cat([vec_w[nm][1] for nm in vec_names])
    vv = cat([vec_w[nm][2] for nm in vec_names])
    vg = cat([g_small[nm] for nm in vec_names])
    vouts = adamw(vw, vg, jnp.zeros_like(vg), vm, vv, "adamw_vectors")
    off = 0
    for nm in vec_names:
        w = vec_w[nm][0]
        nrow = w.size // CW
        big[nm] = [o[off:off + nrow].reshape(w.shape) for o in vouts]
        off += nrow

    order = ["norm1_g", "w_in", "conv_w", "conv_b", "conv_ln_g", "conv_ln_b", "ret_gn_g", "w_out", "norm2_g",
             "w_gate", "w_up", "w_down", "final_g"]
    return (loss, grad_x, *[big[nm][0] for nm in order], *[big[nm][1] for nm in order],
            *[big[nm][2] for nm in order], *[big[nm][3] for nm in order])
```

```python
import math

import jax
import jax.numpy as jnp
from jax import lax
from jax.experimental import pallas as pl
from jax.experimental.pallas import tpu as pltpu

D = 1024
L = 4
CW = 512
RW = 512
NH = 4
HD = 128
CK = 31
CHUNK = 64
INW = 3072
FF = 2816
NCHIP = 4
EPS = 1e-6
ROPE_BASE = 10000.0
SEQ_TILE = 512
MLP_ROWS = 256
MLP_COLS = 1408
HALO = 32
CONV_ROWS = 32
CONV_COLS = 256

ADAM_LR = 0.001
ADAM_B1 = 0.9
ADAM_B2 = 0.999
ADAM_EPS = 1e-08
ADAM_WD = 0.01
ADAM_STEP = 10

BF = jnp.bfloat16
F32 = jnp.float32
MESH = pl.DeviceIdType.MESH
ANY = pl.BlockSpec(memory_space=pl.ANY)
VMEM_SPEC = pl.BlockSpec(memory_space=pltpu.VMEM)
HBM_SPEC = pl.BlockSpec(memory_space=pltpu.HBM)
SEM_SPEC = pl.BlockSpec(memory_space=pltpu.SEMAPHORE)
DATAFLOW = pltpu.SideEffectType.DATAFLOW_SIDE_EFFECTING


def _params(n_grid, vmem_mb):
    return pltpu.CompilerParams(dimension_semantics=("arbitrary",) * n_grid,
                                vmem_limit_bytes=vmem_mb << 20)


def _ordered_call(body, deps, *, in_specs, **kw):
    n, nd = len(in_specs), len(deps)

    def with_deps(*refs):
        body(*refs[:n], *refs[n + nd:])

    return pl.pallas_call(with_deps, in_specs=list(in_specs) + [ANY] * nd, **kw)


def _dot(a, b):
    return jnp.dot(a, b, preferred_element_type=F32)


def _dot_nt(a, b):
    return lax.dot_general(a, b, (((1,), (1,)), ((), ())), preferred_element_type=F32)


def _dot_tn(a, b):
    return lax.dot_general(a, b, (((0,), (0,)), ((), ())), preferred_element_type=F32)


def _sigmoid(x):
    return 0.5 * jnp.tanh(0.5 * x) + 0.5


def _mean(x):
    return jnp.mean(x, axis=-1, keepdims=True)


def _tap_groups(first, count):
    groups = []
    for phase in range(8):
        taps = [(t, first + t - phase) for t in range(count) if (first + t) % 8 == phase]
        if taps:
            lo, hi = min(q for _, q in taps), max(q for _, q in taps)
            groups.append((lo + phase, hi - lo + CONV_ROWS, [(t, q - lo) for t, q in taps]))
    return groups


def _rot(t, cs, sn):
    return t * cs + pltpu.roll(t, HD // 2, 1) * sn


def _rot_t(dy, cs, sn):
    return dy * cs + pltpu.roll(dy * sn, HD // 2, 1)


def _rms_bwd(x, g, dh, dx_in):
    r = lax.rsqrt(_mean(x * x) + EPS)
    xh = x * r
    dxh = dh * g
    dx = dx_in + r * (dxh - xh * _mean(dxh * xh))
    return dx, jnp.sum(dh * xh, axis=0, keepdims=True), (xh * g).astype(BF)


def in_proj(x, g, win, l, deps=()):
    S = x.shape[0]
    tm = min(S, SEQ_TILE)

    def body(x_ref, g_ref, w_ref, o_ref):
        xv = x_ref[...]
        h = (xv * lax.rsqrt(_mean(xv * xv) + EPS) * g_ref[...]).astype(BF)
        o_ref[...] = _dot(h, w_ref[...])

    return _ordered_call(
        body, deps, name=f"in_proj_{l}", grid=(S // tm,),
        in_specs=[pl.BlockSpec((tm, D), lambda i: (i, 0)),
                  pl.BlockSpec((1, D), lambda i: (0, 0)),
                  pl.BlockSpec((D, INW), lambda i: (0, 0))],
        out_specs=pl.BlockSpec((tm, INW), lambda i: (i, 0)),
        out_shape=jax.ShapeDtypeStruct((S, INW), F32),
        compiler_params=_params(1, 48),
    )(x, g, win, *deps)


def conv_fwd(proj, cw, cb, lg, lb, l, deps=()):
    S = proj.shape[0]
    tc = min(S, SEQ_TILE)

    def body(a_ref, b_ref, w_ref, cb_ref, lg_ref, lb_ref, u1_ref, u_ref, buf, win):
        i = pl.program_id(0)

        @pl.when(i == 0)
        def _():
            buf[0:HALO, :] = jnp.zeros((HALO, CW), F32)

        @pl.when(i > 0)
        def _():
            buf[0:HALO, :] = buf[tc:tc + HALO, :]

        buf[HALO:HALO + tc, :] = a_ref[...] * _sigmoid(b_ref[...])
        groups = _tap_groups(HALO - (CK - 1), CK)
        for r0 in range(0, tc, CONV_ROWS):
            for c0 in range(0, CW, CONV_COLS):
                cols = slice(c0, c0 + CONV_COLS)
                acc = jnp.broadcast_to(cb_ref[:, cols], (CONV_ROWS, CONV_COLS))
                for start, length, taps in groups:
                    win[0:length, :] = buf[r0 + start:r0 + start + length, cols]
                    for k, at in taps:
                        acc = acc + w_ref[k:k + 1, cols] * win[at:at + CONV_ROWS, :]
                u1_ref[r0:r0 + CONV_ROWS, cols] = acc
            acc = u1_ref[r0:r0 + CONV_ROWS, :]
            d = acc - _mean(acc)
            u2 = d * lax.rsqrt(_mean(d * d) + EPS) * lg_ref[...] + lb_ref[...]
            u_ref[r0:r0 + CONV_ROWS, :] = (u2 * _sigmoid(u2)).astype(BF)

    vec = pl.BlockSpec((1, CW), lambda i: (0, 0))
    return _ordered_call(
        body, deps, name=f"conv_fwd_{l}", grid=(S // tc,),
        in_specs=[pl.BlockSpec((tc, CW), lambda i: (i, 0)),
                  pl.BlockSpec((tc, CW), lambda i: (i, 1)),
                  pl.BlockSpec((CK, CW), lambda i: (0, 0)), vec, vec, vec],
        out_specs=[pl.BlockSpec((tc, CW), lambda i: (i, 0)),
                   pl.BlockSpec((tc, CW), lambda i: (i, 0))],
        out_shape=[jax.ShapeDtypeStruct((S, CW), F32), jax.ShapeDtypeStruct((S, CW), BF)],
        scratch_shapes=[pltpu.VMEM((tc + HALO, CW), F32), pltpu.VMEM((HALO + CONV_ROWS, CONV_COLS), F32)],
        compiler_params=_params(1, 32),
    )(proj, proj, cw, cb, lg, lb, *deps)


def _ret_tables(S):
    half = HD // 2
    pos = jnp.arange(S, dtype=F32)
    freqs = ROPE_BASE ** (-jnp.arange(half, dtype=F32) / half)
    ang = pos[:, None] * freqs[None, :]
    cos, sin = jnp.cos(ang), jnp.sin(ang)
    cosf = jnp.concatenate([cos, cos], axis=-1)
    sinf = jnp.concatenate([-sin, sin], axis=-1)
    log_g = jnp.log(1.0 - 2.0 ** (-5.0 - jnp.arange(NH, dtype=F32)))
    idx = jnp.arange(CHUNK, dtype=F32)
    dmat = jnp.exp(log_g[:, None, None] * jnp.abs(idx[:, None] - idx[None, :]))
    qdec = jnp.broadcast_to(jnp.exp(log_g[:, None] * (idx + 1.0))[:, :, None], (NH, CHUNK, HD))
    kdec = jnp.broadcast_to(jnp.exp(log_g[:, None] * (CHUNK - 1 - idx))[:, :, None], (NH, CHUNK, HD))
    cdec = jnp.broadcast_to(jnp.exp(log_g * CHUNK)[:, None, None], (NH, HD, HD))
    return cosf, sinf, dmat, qdec, kdec, cdec


def _ret_specs(tr, tmap):
    q0, k0, v0 = (2 * CW) // HD, (2 * CW + RW) // HD, (2 * CW + 2 * RW) // HD
    return [pl.BlockSpec((tr, HD), lambda h, t: (tmap(t), q0 + h)),
            pl.BlockSpec((tr, HD), lambda h, t: (tmap(t), k0 + h)),
            pl.BlockSpec((tr, HD), lambda h, t: (tmap(t), v0 + h)),
            pl.BlockSpec((tr, HD), lambda h, t: (tmap(t), 0)),
            pl.BlockSpec((tr, HD), lambda h, t: (tmap(t), 0)),
            pl.BlockSpec((None, CHUNK, CHUNK), lambda h, t: (h, 0, 0)),
            pl.BlockSpec((None, CHUNK, HD), lambda h, t: (h, 0, 0)),
            pl.BlockSpec((None, CHUNK, HD), lambda h, t: (h, 0, 0)),
            pl.BlockSpec((None, HD, HD), lambda h, t: (h, 0, 0))]


def ret_fwd(proj, tables, l, deps=()):
    S = proj.shape[0]
    tr = min(S, SEQ_TILE)
    cpb = tr // CHUNK
    scale = HD ** -0.5

    def body(q_ref, k_ref, v_ref, cos_ref, sin_ref, dm_ref, qd_ref, kd_ref, cd_ref,
             r_ref, st_ref, st):
        @pl.when(pl.program_id(1) == 0)
        def _():
            st[...] = jnp.zeros((HD, HD), F32)

        for c in range(cpb):
            rows = slice(c * CHUNK, (c + 1) * CHUNK)
            cs, sn = cos_ref[rows, :], sin_ref[rows, :]
            qr = _rot(q_ref[rows, :], cs, sn)
            kr = _rot(k_ref[rows, :], cs, sn) * scale
            vb = v_ref[rows, :].astype(BF)
            s = st[...]
            st_ref[c] = s
            sc = _dot_nt(qr.astype(BF), kr.astype(BF)) * dm_ref[...]
            out = _dot(sc.astype(BF), vb) + _dot((qr * qd_ref[...]).astype(BF), s.astype(BF))
            r_ref[rows, :] = out
            st[...] = cd_ref[...] * s + _dot_tn((kr * kd_ref[...]).astype(BF), vb)

    return _ordered_call(
        body, deps, name=f"ret_fwd_{l}", grid=(NH, S // tr),
        in_specs=_ret_specs(tr, lambda t: t),
        out_specs=[pl.BlockSpec((tr, HD), lambda h, t: (t, h)),
                   pl.BlockSpec((None, cpb, HD, HD), lambda h, t: (h, t, 0, 0))],
        out_shape=[jax.ShapeDtypeStruct((S, RW), F32),
                   jax.ShapeDtypeStruct((NH, S // CHUNK, HD, HD), F32)],
        scratch_shapes=[pltpu.VMEM((HD, HD), F32)],
        compiler_params=_params(2, 32),
    )(proj, proj, proj, *tables, *deps)


def out_proj(u, r_raw, proj, gn, wout, x, l, deps=()):
    S = x.shape[0]
    tm = min(S, SEQ_TILE)
    gate_blk = (2 * CW + 3 * RW) // RW

    def body(u_ref, r_ref, gate_ref, gn_ref, w_ref, x_ref, x2_ref, mix_ref):
        mix_ref[:, 0:CW] = u_ref[...]
        gt = gate_ref[...]
        sil = gt * _sigmoid(gt) * gn_ref[...]
        for h in range(NH):
            cols = slice(h * HD, (h + 1) * HD)
            rh = r_ref[:, cols]
            d = rh - _mean(rh)
            rn = d * lax.rsqrt(_mean(d * d) + EPS)
            mix_ref[:, CW + h * HD:CW + (h + 1) * HD] = (rn * sil[:, cols]).astype(BF)
        x2_ref[...] = x_ref[...] + _dot(mix_ref[...], w_ref[...])

    return _ordered_call(
        body, deps, name=f"out_proj_{l}", grid=(S // tm,),
        in_specs=[pl.BlockSpec((tm, CW), lambda i: (i, 0)),
                  pl.BlockSpec((tm, RW), lambda i: (i, 0)),
                  pl.BlockSpec((tm, RW), lambda i: (i, gate_blk)),
                  pl.BlockSpec((1, RW), lambda i: (0, 0)),
                  pl.BlockSpec((D, D), lambda i: (0, 0)),
                  pl.BlockSpec((tm, D), lambda i: (i, 0))],
        out_specs=[pl.BlockSpec((tm, D), lambda i: (i, 0)),
                   pl.BlockSpec((tm, D), lambda i: (i, 0))],
        out_shape=[jax.ShapeDtypeStruct((S, D), F32), jax.ShapeDtypeStruct((S, D), BF)],
        compiler_params=_params(1, 40),
    )(u, r_raw, proj, gn, wout, x, *deps)


def mlp_fwd(x2, g2, wgt, wut, wd, l, deps=()):
    S = x2.shape[0]
    tm, tf = min(S, MLP_ROWS), MLP_COLS

    def body(x_ref, g_ref, wg_ref, wu_ref, wd_ref, o_ref, gs_ref, us_ref, a_scr):
        xv = x_ref[...]
        h = (xv * lax.rsqrt(_mean(xv * xv) + EPS) * g_ref[...]).astype(BF)
        for c0 in range(0, FF, tf):
            gv = _dot_nt(h, wg_ref[c0:c0 + tf, :])
            uv = _dot_nt(h, wu_ref[c0:c0 + tf, :])
            gs_ref[:, c0:c0 + tf] = gv.astype(BF)
            us_ref[:, c0:c0 + tf] = uv.astype(BF)
            a_scr[:, c0:c0 + tf] = (gv * _sigmoid(gv) * uv).astype(BF)
        o_ref[...] = xv + _dot(a_scr[...], wd_ref[...])

    wspec = pl.BlockSpec((FF, D), lambda i: (0, 0), pipeline_mode=pl.Buffered(1))
    row = pl.BlockSpec((tm, D), lambda i: (i, 0))
    wide = pl.BlockSpec((tm, FF), lambda i: (i, 0))
    return _ordered_call(
        body, deps, name=f"mlp_fwd_{l}", grid=(S // tm,),
        in_specs=[row, pl.BlockSpec((1, D), lambda i: (0, 0)), wspec, wspec, wspec],
        out_specs=[row, wide, wide],
        out_shape=[jax.ShapeDtypeStruct((S, D), F32), jax.ShapeDtypeStruct((S, FF), BF),
                   jax.ShapeDtypeStruct((S, FF), BF)],
        scratch_shapes=[pltpu.VMEM((tm, FF), BF)],
        compiler_params=_params(1, 56),
    )(x2, g2, wgt, wut, wd, *deps)


def final_loss(x, gf, tgt):
    S = x.shape[0]
    tm = min(S, SEQ_TILE)

    def body(x_ref, g_ref, t_ref, dx_ref, loss_ref, dg_ref):
        @pl.when(pl.program_id(0) == 0)
        def _():
            loss_ref[...] = jnp.zeros((8, 128), F32)
            dg_ref[...] = jnp.zeros((1, D), F32)

        xv = x_ref[...]
        r = lax.rsqrt(_mean(xv * xv) + EPS)
        xh = xv * r
        diff = xh * g_ref[...] - t_ref[...]
        loss_ref[...] += jnp.sum(jnp.sum(diff * diff, axis=-1, keepdims=True), axis=0, keepdims=True)
        dy = diff * (1.0 / D)
        dg_ref[...] += jnp.sum(dy * xh, axis=0, keepdims=True)
        dxh = dy * g_ref[...]
        dx_ref[...] = r * (dxh - xh * _mean(dxh * xh))

    return pl.pallas_call(
        body, name="final_loss", grid=(S // tm,),
        in_specs=[pl.BlockSpec((tm, D), lambda i: (i, 0)),
                  pl.BlockSpec((1, D), lambda i: (0, 0)),
                  pl.BlockSpec((tm, D), lambda i: (i, 0))],
        out_specs=[pl.BlockSpec((tm, D), lambda i: (i, 0)),
                   pl.BlockSpec((8, 128), lambda i: (0, 0)),
                   pl.BlockSpec((1, D), lambda i: (0, 0))],
        out_shape=[jax.ShapeDtypeStruct((S, D), F32), jax.ShapeDtypeStruct((8, 128), F32),
                   jax.ShapeDtypeStruct((1, D), F32)],
        compiler_params=_params(1, 40),
    )(x, gf, tgt)


def mlp_bwd(dx3, x2, g2, gs, us, wgt, wut, wd, l, deps=()):
    S = x2.shape[0]
    tm, tf = min(S, MLP_ROWS), MLP_COLS

    def body(dx_ref, x_ref, g_ref, gs_ref, us_ref, wg_ref, wu_ref, wd_ref,
             dx2_ref, dg_ref, du_ref, h_ref, dgain_ref):
        @pl.when(pl.program_id(0) == 0)
        def _():
            dgain_ref[...] = jnp.zeros((1, D), F32)

        dxv = dx_ref[...]
        dxb = dxv.astype(BF)
        for c0 in range(0, FF, tf):
            da = _dot_nt(dxb, wd_ref[c0:c0 + tf, :])
            gv = gs_ref[:, c0:c0 + tf].astype(F32)
            uv = us_ref[:, c0:c0 + tf].astype(F32)
            sg = _sigmoid(gv)
            dg_ref[:, c0:c0 + tf] = (da * uv * (sg * (1.0 + gv * (1.0 - sg)))).astype(BF)
            du_ref[:, c0:c0 + tf] = (da * (gv * sg)).astype(BF)
        dh = _dot(dg_ref[...], wg_ref[...]) + _dot(du_ref[...], wu_ref[...])
        dx2, dgain, hb = _rms_bwd(x_ref[...], g_ref[...], dh, dxv)
        dx2_ref[...] = dx2
        dgain_ref[...] += dgain
        h_ref[...] = hb

    wspec = pl.BlockSpec((FF, D), lambda i: (0, 0), pipeline_mode=pl.Buffered(1))
    row = pl.BlockSpec((tm, D), lambda i: (i, 0))
    wide = pl.BlockSpec((tm, FF), lambda i: (i, 0))
    vec = pl.BlockSpec((1, D), lambda i: (0, 0))
    return _ordered_call(
        body, deps, name=f"mlp_bwd_{l}", grid=(S // tm,),
        in_specs=[row, row, vec, wide, wide, wspec, wspec, wspec],
        out_specs=[row, wide, wide, row, vec],
        out_shape=[jax.ShapeDtypeStruct((S, D), F32), jax.ShapeDtypeStruct((S, FF), BF),
                   jax.ShapeDtypeStruct((S, FF), BF), jax.ShapeDtypeStruct((S, D), BF),
                   jax.ShapeDtypeStruct((1, D), F32)],
        compiler_params=_params(1, 56),
    )(dx3, x2, g2, gs, us, wgt, wut, wd, *deps)


def wgrad(lhs, b, l, name, deps=()):
    pair = isinstance(lhs, (tuple, list))
    arrs = list(lhs) if pair else [lhs]
    na = len(arrs)
    S, K = arrs[0].shape
    N = b.shape[1]
    tk = 1408 if K == FF else min(K, 1024)
    tn = min(N, 1024)
    ts = min(S, SEQ_TILE)
    ns = S // ts

    def body(*refs):
        a_refs, (b_ref, o_ref, acc) = refs[:na], refs[na:]
        s = pl.program_id(2)

        @pl.when(s == 0)
        def _():
            acc[...] = jnp.zeros((tk, tn), F32)

        if pair:
            gv = a_refs[0][...].astype(F32)
            a = (gv * _sigmoid(gv) * a_refs[1][...].astype(F32)).astype(BF)
        else:
            a = a_refs[0][...]
        acc[...] += _dot_tn(a, b_ref[...].astype(BF))

        @pl.when(s == ns - 1)
        def _():
            o_ref[...] = acc[...].astype(BF)

    return _ordered_call(
        body, deps, name=f"{name}_{l}", grid=(K // tk, N // tn, ns),
        in_specs=[pl.BlockSpec((ts, tk), lambda i, j, s: (s, i))] * na
        + [pl.BlockSpec((ts, tn), lambda i, j, s: (s, j))],
        out_specs=pl.BlockSpec((tk, tn), lambda i, j, s: (i, j)),
        out_shape=jax.ShapeDtypeStruct((K, N), BF),
        scratch_shapes=[pltpu.VMEM((tk, tn), F32)],
        compiler_params=_params(3, 48),
    )(*arrs, b, *deps)


def out_proj_bwd(dx2, wout, r_raw, proj, gn, u1, lg, lb, l, deps=()):
    S = dx2.shape[0]
    tm = min(S, SEQ_TILE)
    gate_blk = (2 * CW + 3 * RW) // RW

    def body(dx_ref, w_ref, r_ref, gate_ref, gn_ref, u1_ref, lg_ref, lb_ref,
             dgate_ref, dr_ref, du1_ref, sums_ref):
        @pl.when(pl.program_id(0) == 0)
        def _():
            sums_ref[...] = jnp.zeros((8, CW), F32)

        dmix = _dot_nt(dx_ref[...].astype(BF), w_ref[...])
        gt = gate_ref[...]
        sg = _sigmoid(gt)
        sil = gt * sg
        dsil = sg * (1.0 + gt * (1.0 - sg))
        for h in range(NH):
            cols = slice(h * HD, (h + 1) * HD)
            rh = r_ref[:, cols]
            d = rh - _mean(rh)
            rs = lax.rsqrt(_mean(d * d) + EPS)
            rn = d * rs
            drr = dmix[:, CW + h * HD:CW + (h + 1) * HD]
            gnh = gn_ref[:, cols]
            sums_ref[0:1, cols] += jnp.sum(drr * rn * sil[:, cols], axis=0, keepdims=True)
            dgate_ref[:, cols] = (drr * rn * gnh * dsil[:, cols]).astype(BF)
            drn = drr * gnh * sil[:, cols]
            dr_ref[:, cols] = rs * (drn - _mean(drn) - rn * _mean(drn * rn))
        du = dmix[:, 0:CW]
        u1 = u1_ref[...]
        d = u1 - _mean(u1)
        rs = lax.rsqrt(_mean(d * d) + EPS)
        xh = d * rs
        u2 = xh * lg_ref[...] + lb_ref[...]
        sg2 = _sigmoid(u2)
        du2 = du * (sg2 * (1.0 + u2 * (1.0 - sg2)))
        sums_ref[1:2, :] += jnp.sum(du2 * xh, axis=0, keepdims=True)
        sums_ref[2:3, :] += jnp.sum(du2, axis=0, keepdims=True)
        dxh = du2 * lg_ref[...]
        du1_ref[...] = rs * (dxh - _mean(dxh) - xh * _mean(dxh * xh))

    vec = pl.BlockSpec((1, CW), lambda i: (0, 0))
    half = pl.BlockSpec((tm, CW), lambda i: (i, 0))
    return _ordered_call(
        body, deps, name=f"out_proj_bwd_{l}", grid=(S // tm,),
        in_specs=[pl.BlockSpec((tm, D), lambda i: (i, 0)),
                  pl.BlockSpec((D, D), lambda i: (0, 0)),
                  half, pl.BlockSpec((tm, RW), lambda i: (i, gate_blk)), vec, half, vec, vec],
        out_specs=[half, half, half, pl.BlockSpec((8, CW), lambda i: (0, 0))],
        out_shape=[jax.ShapeDtypeStruct((S, RW), BF), jax.ShapeDtypeStruct((S, RW), F32),
                   jax.ShapeDtypeStruct((S, CW), F32), jax.ShapeDtypeStruct((8, CW), F32)],
        compiler_params=_params(1, 40),
    )(dx2, wout, r_raw, proj, gn, u1, lg, lb, *deps)


def conv_bwd(du1, proj, cw, l, deps=()):
    S = proj.shape[0]
    tc = min(S, SEQ_TILE)
    nt = S // tc

    def body(du1_ref, a_ref, b_ref, w_ref, dab_ref, dwb_ref, buf, win):
        i = pl.program_id(0)

        @pl.when(i == 0)
        def _():
            buf[tc:tc + HALO, :] = jnp.zeros((HALO, CW), F32)
            dwb_ref[...] = jnp.zeros((CK + 1, CW), F32)

        @pl.when(i > 0)
        def _():
            buf[tc:tc + HALO, :] = buf[0:HALO, :]

        buf[0:tc, :] = du1_ref[...]
        groups = _tap_groups(0, CK)
        for c0 in range(0, CW, CONV_COLS):
            cols = slice(c0, c0 + CONV_COLS)
            parts = [jnp.zeros((8, CONV_COLS), F32) for _ in range(CK)]
            for r0 in range(0, tc, CONV_ROWS):
                av = a_ref[r0:r0 + CONV_ROWS, cols]
                sgb = _sigmoid(b_ref[r0:r0 + CONV_ROWS, cols])
                u0 = av * sgb
                acc = jnp.zeros((CONV_ROWS, CONV_COLS), F32)
                for start, length, taps in groups:
                    win[0:length, :] = buf[r0 + start:r0 + start + length, cols]
                    for j, at in taps:
                        sl = win[at:at + CONV_ROWS, :]
                        acc = acc + w_ref[CK - 1 - j:CK - j, cols] * sl
                        pr = u0 * sl
                        red = pr[0:8, :]
                        for q in range(1, CONV_ROWS // 8):
                            red = red + pr[8 * q:8 * q + 8, :]
                        parts[CK - 1 - j] = parts[CK - 1 - j] + red
                dab_ref[r0:r0 + CONV_ROWS, c0:c0 + CONV_COLS] = (acc * sgb).astype(BF)
                dab_ref[r0:r0 + CONV_ROWS, CW + c0:CW + c0 + CONV_COLS] = (acc * av * sgb * (1.0 - sgb)).astype(BF)
            for k in range(CK):
                dwb_ref[k:k + 1, cols] += jnp.sum(parts[k], axis=0, keepdims=True)
        dwb_ref[CK:CK + 1, :] += jnp.sum(du1_ref[...], axis=0, keepdims=True)

    return _ordered_call(
        body, deps, name=f"conv_bwd_{l}", grid=(nt,),
        in_specs=[pl.BlockSpec((tc, CW), lambda i: (nt - 1 - i, 0)),
                  pl.BlockSpec((tc, CW), lambda i: (nt - 1 - i, 0)),
                  pl.BlockSpec((tc, CW), lambda i: (nt - 1 - i, 1)),
                  pl.BlockSpec((CK, CW), lambda i: (0, 0))],
        out_specs=[pl.BlockSpec((tc, 2 * CW), lambda i: (nt - 1 - i, 0)),
                   pl.BlockSpec((CK + 1, CW), lambda i: (0, 0))],
        out_shape=[jax.ShapeDtypeStruct((S, 2 * CW), BF), jax.ShapeDtypeStruct((CK + 1, CW), F32)],
        scratch_shapes=[pltpu.VMEM((tc + HALO, CW), F32), pltpu.VMEM((HALO + CONV_ROWS, CONV_COLS), F32)],
        compiler_params=_params(1, 32),
    )(du1, proj, proj, cw, *deps)


def ret_bwd(dr, proj, states, tables, l):
    S = proj.shape[0]
    tr = min(S, SEQ_TILE)
    cpb = tr // CHUNK
    nt = S // tr
    scale = HD ** -0.5

    def body(q_ref, k_ref, v_ref, cos_ref, sin_ref, dm_ref, qd_ref, kd_ref, cd_ref, dr_ref, st_ref,
             dq_ref, dk_ref, dv_ref, gst):
        @pl.when(pl.program_id(1) == 0)
        def _():
            gst[...] = jnp.zeros((HD, HD), F32)

        for c in reversed(range(cpb)):
            rows = slice(c * CHUNK, (c + 1) * CHUNK)
            cs, sn = cos_ref[rows, :], sin_ref[rows, :]
            qr = _rot(q_ref[rows, :], cs, sn)
            kr = _rot(k_ref[rows, :], cs, sn) * scale
            qb, kb = qr.astype(BF), kr.astype(BF)
            vb = v_ref[rows, :].astype(BF)
            dob = dr_ref[rows, :].astype(BF)
            sb = st_ref[c].astype(BF)
            gn1 = gst[...]
            gb = gn1.astype(BF)
            sc = (_dot_nt(qb, kb) * dm_ref[...]).astype(BF)
            dsc = (_dot_nt(dob, vb) * dm_ref[...]).astype(BF)
            dqr = _dot(dsc, kb) + _dot_nt(dob, sb) * qd_ref[...]
            dkr = _dot_tn(dsc, qb) + _dot_nt(vb, gb) * kd_ref[...]
            dvv = _dot_tn(sc, dob) + _dot((kr * kd_ref[...]).astype(BF), gb)
            gst[...] = cd_ref[...] * gn1 + _dot_tn((qr * qd_ref[...]).astype(BF), dob)
            dq_ref[rows, :] = _rot_t(dqr, cs, sn).astype(BF)
            dk_ref[rows, :] = _rot_t(dkr * scale, cs, sn).astype(BF)
            dv_ref[rows, :] = dvv.astype(BF)

    rev = lambda t: nt - 1 - t
    hblk = pl.BlockSpec((tr, HD), lambda h, t: (rev(t), h))
    return pl.pallas_call(
        body, name=f"ret_bwd_{l}", grid=(NH, nt),
        in_specs=_ret_specs(tr, rev) + [hblk, pl.BlockSpec((None, cpb, HD, HD), lambda h, t: (h, rev(t), 0, 0))],
        out_specs=[hblk, hblk, hblk],
        out_shape=[jax.ShapeDtypeStruct((S, RW), BF)] * 3,
        scratch_shapes=[pltpu.VMEM((HD, HD), F32)],
        compiler_params=_params(2, 32),
    )(proj, proj, proj, *tables, dr, states)


def in_proj_bwd(parts, win, x, g, dx2, l):
    S = x.shape[0]
    tm = min(S, SEQ_TILE)
    n = len(parts)

    def body(*refs):
        srcs = refs[:n]
        w_ref, x_ref, g_ref, dx2_ref, dx_ref, h_ref, dgain_ref = refs[n:]

        @pl.when(pl.program_id(0) == 0)
        def _():
            dgain_ref[...] = jnp.zeros((1, D), F32)

        dh, col = None, 0
        for r in srcs:
            width = r.shape[1]
            term = _dot_nt(r[...], w_ref[:, col:col + width])
            dh = term if dh is None else dh + term
            col += width
        dx, dgain, hb = _rms_bwd(x_ref[...], g_ref[...], dh, dx2_ref[...])
        dx_ref[...] = dx
        dgain_ref[...] += dgain
        h_ref[...] = hb

    row = pl.BlockSpec((tm, D), lambda i: (i, 0))
    vec = pl.BlockSpec((1, D), lambda i: (0, 0))
    return pl.pallas_call(
        body, name=f"in_proj_bwd_{l}", grid=(S // tm,),
        in_specs=[pl.BlockSpec((tm, p.shape[1]), lambda i: (i, 0)) for p in parts]
        + [pl.BlockSpec((D, INW), lambda i: (0, 0)), row, vec, row],
        out_specs=[row, row, vec],
        out_shape=[jax.ShapeDtypeStruct((S, D), F32), jax.ShapeDtypeStruct((S, D), BF),
                   jax.ShapeDtypeStruct((1, D), F32)],
        compiler_params=_params(1, 48),
    )(*parts, win, x, g, dx2)


def wgrad_in(h, parts, l):
    S = h.shape[0]
    ts = min(S, SEQ_TILE)
    ns = S // ts

    def body(*refs):
        h_ref, srcs = refs[0], refs[1:1 + len(parts)]
        o_ref, acc = refs[-2], refs[-1]
        s = pl.program_id(0)

        @pl.when(s == 0)
        def _():
            acc[...] = jnp.zeros((D, INW), F32)

        hv = h_ref[...]
        col = 0
        for r in srcs:
            width = r.shape[1]
            acc[:, col:col + width] += _dot_tn(hv, r[...])
            col += width

        @pl.when(s == ns - 1)
        def _():
            o_ref[...] = acc[...].astype(BF)

    return pl.pallas_call(
        body, name=f"wgrad_in_{l}", grid=(ns,),
        in_specs=[pl.BlockSpec((ts, D), lambda s: (s, 0))]
        + [pl.BlockSpec((ts, p.shape[1]), lambda s: (s, 0)) for p in parts],
        out_specs=pl.BlockSpec((D, INW), lambda s: (0, 0)),
        out_shape=jax.ShapeDtypeStruct((D, INW), BF),
        scratch_shapes=[pltpu.VMEM((D, INW), F32)],
        compiler_params=_params(1, 48),
    )(h, *parts)


def sum_slots(recv, name):
    _, R, C = recv.shape
    tr = 256 if R % 256 == 0 else R

    def body(r_ref, o_ref):
        acc = r_ref[0].astype(F32)
        for k in range(1, NCHIP):
            acc = acc + r_ref[k].astype(F32)
        o_ref[...] = acc

    return pl.pallas_call(
        body, name=name, grid=(R // tr,),
        in_specs=[pl.BlockSpec((NCHIP, tr, C), lambda i: (0, i, 0))],
        out_specs=pl.BlockSpec((tr, C), lambda i: (i, 0)),
        out_shape=jax.ShapeDtypeStruct((R, C), F32),
        compiler_params=_params(1, 32),
    )(recv)


def adamw(w, ga, gb, m, v, name):
    R, C = w.shape
    tr = 256 if R % 256 == 0 else R
    c1 = 1.0 - ADAM_B1 ** ADAM_STEP
    c2 = 1.0 - ADAM_B2 ** ADAM_STEP

    def body(w_ref, ga_ref, gb_ref, m_ref, v_ref, g_out, d_out, m_out, v_out):
        g = ga_ref[...] + gb_ref[...]
        mn = ADAM_B1 * m_ref[...] + (1.0 - ADAM_B1) * g
        vn = ADAM_B2 * v_ref[...] + (1.0 - ADAM_B2) * (g * g)
        g_out[...] = g
        m_out[...] = mn
        v_out[...] = vn
        d_out[...] = -ADAM_LR * ((mn / c1) / (jnp.sqrt(vn / c2) + ADAM_EPS) + ADAM_WD * w_ref[...])

    blk = pl.BlockSpec((tr, C), lambda i: (i, 0))
    return pl.pallas_call(
        body, name=name, grid=(R // tr,),
        in_specs=[blk] * 5, out_specs=[blk] * 4,
        out_shape=[jax.ShapeDtypeStruct((R, C), F32)] * 4,
        compiler_params=_params(1, 40),
    )(w, ga, gb, m, v)


def _place():
    x, y, c = lax.axis_index("x"), lax.axis_index("y"), lax.axis_index("c")
    chips = [(1 - x, y), (x, 1 - y), (1 - x, 1 - y)]
    return x, y, c, chips


def _window(ref, axis, j, size):
    idx = [slice(None)] * len(ref.shape)
    idx[axis] = pl.ds(pl.multiple_of(j * size, 128 if axis == len(ref.shape) - 1 else 16), size)
    return ref.at[tuple(idx)]


def _hbm(a):
    return pltpu.with_memory_space_constraint(a, pltpu.HBM)


def _hbm_like(arrs):
    return [pltpu.HBM(a.shape, a.dtype) for a in arrs]


def gather_start(shards, axes, after, tag):
    n = len(shards)
    na = len(after)
    lands = []
    for s, ax in zip(shards, axes):
        shp = list(s.shape)
        shp[ax] *= NCHIP
        lands.append(lax.empty(tuple(shp), s.dtype))

    def body(*refs):
        ins, land = refs[:n], refs[n:2 * n]
        send, recv = refs[2 * n + na], refs[2 * n + na + 1]
        token = refs[-1]
        x, y, c, chips = _place()
        for a in range(n):
            for k, chip in enumerate(chips):
                pltpu.make_async_remote_copy(
                    src_ref=ins[a], dst_ref=_window(land[a], axes[a], 2 * x + y, ins[a].shape[axes[a]]),
                    send_sem=send.at[3 * a + k], recv_sem=recv.at[3 * a + k],
                    device_id=(chip[0], chip[1], c), device_id_type=MESH).start()
        token[...] = jnp.zeros_like(token)

    outs = pl.pallas_call(
        body, name=f"gather_start_{tag}",
        in_specs=[HBM_SPEC] * (2 * n) + [ANY] * na,
        out_specs=(SEM_SPEC, SEM_SPEC, *[HBM_SPEC] * (2 * n), VMEM_SPEC),
        out_shape=(pltpu.SemaphoreType.DMA((3 * n,)), pltpu.SemaphoreType.DMA((3 * n,)),
                   *_hbm_like(shards), *_hbm_like(lands), jax.ShapeDtypeStruct((8, 128), F32)),
        input_output_aliases={a: 2 + a for a in range(2 * n)},
        compiler_params=pltpu.CompilerParams(has_side_effects=DATAFLOW),
    )(*[_hbm(s) for s in shards], *[_hbm(b) for b in lands], *after)
    return (outs[0], outs[1], list(outs[2:2 + n]), list(outs[2 + n:2 + 2 * n]), list(axes)), outs[-1]


def gather_wait(groups, after, tag):
    sizes = [len(g[2]) for g in groups]
    total = sum(sizes)

    def body(*refs):
        x, y, c, chips = _place()
        stage, loc = refs[-1 - total:-1], refs[-1]
        pos = 2 * total
        off = 0
        mine = []
        for g, n in zip(groups, sizes):
            ins, land = refs[off:off + n], refs[total + off:total + off + n]
            send_ref, recv_ref = refs[pos], refs[pos + 1]
            axes = g[4]
            for a in range(n):
                fetch = pltpu.make_async_copy(ins[a], stage[off + a], loc.at[2 * (off + a)])
                fetch.start()
                put = pltpu.make_async_copy(
                    stage[off + a], _window(land[a], axes[a], 2 * x + y, ins[a].shape[axes[a]]),
                    loc.at[2 * (off + a) + 1])
                mine.append((fetch, put))
                for k, chip in enumerate(chips):
                    cp = pltpu.make_async_remote_copy(
                        src_ref=ins[a],
                        dst_ref=_window(land[a], axes[a], 2 * chip[0] + chip[1], ins[a].shape[axes[a]]),
                        send_sem=send_ref.at[3 * a + k], recv_sem=recv_ref.at[3 * a + k],
                        device_id=(chip[0], chip[1], c), device_id_type=MESH)
                    cp.wait_send()
                    cp.wait_recv()
            pos += 2
            off += n
        for fetch, put in mine:
            fetch.wait()
            put.start()
        for fetch, put in mine:
            put.wait()

    shards = [s for g in groups for s in g[2]]
    lands = [b for g in groups for b in g[3]]
    sems = [s for g in groups for s in (g[0], g[1])]
    outs = pl.pallas_call(
        body, name=f"gather_wait_{tag}",
        in_specs=[HBM_SPEC] * (2 * total) + [SEM_SPEC] * len(sems) + [ANY],
        out_specs=[HBM_SPEC] * (2 * total),
        out_shape=(*_hbm_like(shards), *_hbm_like(lands)),
        input_output_aliases={a: a for a in range(2 * total)},
        scratch_shapes=[pltpu.VMEM(s.shape, s.dtype) for s in shards] + [pltpu.SemaphoreType.DMA((2 * total,))],
        compiler_params=pltpu.CompilerParams(has_side_effects=DATAFLOW, vmem_limit_bytes=32 << 20),
    )(*shards, *lands, *sems, after)
    return list(outs[total:])


def scatter_start(grads, axes, sizes, lands, l, tag):
    n = len(grads)

    def body(*refs):
        ins, land = refs[:n], refs[n:2 * n]
        send, recv = refs[2 * n], refs[2 * n + 1]
        token = refs[2 * n + 2 + 2 * n]
        stage, loc = refs[-1 - n:-1], refs[-1]
        x, y, c, chips = _place()
        me = 2 * x + y
        fetches = [pltpu.make_async_copy(_window(ins[a], axes[a], me, sizes[a]), stage[a], loc.at[2 * a])
                   for a in range(n)]
        for cp in fetches:
            cp.start()
        for a in range(n):
            for k, chip in enumerate(chips):
                pltpu.make_async_remote_copy(
                    src_ref=_window(ins[a], axes[a], 2 * chip[0] + chip[1], sizes[a]), dst_ref=land[a].at[me, l],
                    send_sem=send.at[3 * a + k], recv_sem=recv.at[3 * a + k],
                    device_id=(chip[0], chip[1], c), device_id_type=MESH).start()
        puts = [pltpu.make_async_copy(stage[a], land[a].at[me, l], loc.at[2 * a + 1]) for a in range(n)]
        for fetch, put in zip(fetches, puts):
            fetch.wait()
            put.start()
        for put in puts:
            put.wait()
        token[...] = jnp.zeros_like(token)

    outs = pl.pallas_call(
        body, name=f"scatter_start_{tag}",
        in_specs=[HBM_SPEC] * (2 * n),
        out_specs=(SEM_SPEC, SEM_SPEC, *[HBM_SPEC] * (2 * n), VMEM_SPEC),
        out_shape=(pltpu.SemaphoreType.DMA((3 * n,)), pltpu.SemaphoreType.DMA((3 * n,)),
                   *_hbm_like(grads), *_hbm_like(lands), jax.ShapeDtypeStruct((8, 128), F32)),
        input_output_aliases={a: 2 + a for a in range(2 * n)},
        scratch_shapes=[pltpu.VMEM(b.shape[2:], b.dtype) for b in lands] + [pltpu.SemaphoreType.DMA((2 * n,))],
        compiler_params=pltpu.CompilerParams(has_side_effects=DATAFLOW, vmem_limit_bytes=32 << 20),
    )(*[_hbm(g) for g in grads], *[_hbm(b) for b in lands])
    group = (outs[0], outs[1], list(outs[2:2 + n]), list(axes), list(sizes), l)
    return group, list(outs[2 + n:2 + 2 * n]), outs[-1]


def scatter_wait(groups, lands, which):
    nl = len(lands)

    def body(*refs):
        land = refs[:nl]
        x, y, c, chips = _place()
        pos = nl
        for g, wh in zip(groups, which):
            n = len(g[2])
            ins = refs[pos:pos + n]
            send_ref, recv_ref = refs[pos + n], refs[pos + n + 1]
            axes, sizes, l = g[3], g[4], g[5]
            for a in range(n):
                for k, chip in enumerate(chips):
                    jp = 2 * chip[0] + chip[1]
                    cp = pltpu.make_async_remote_copy(
                        src_ref=_window(ins[a], axes[a], jp, sizes[a]), dst_ref=land[wh[a]].at[jp, l],
                        send_sem=send_ref.at[3 * a + k], recv_sem=recv_ref.at[3 * a + k],
                        device_id=(chip[0], chip[1], c), device_id_type=MESH)
                    cp.wait_send()
                    cp.wait_recv()
            pos += n + 2

    operands = list(lands)
    specs = [HBM_SPEC] * nl
    for g in groups:
        operands += list(g[2]) + [g[0], g[1]]
        specs += [HBM_SPEC] * len(g[2]) + [SEM_SPEC, SEM_SPEC]
    outs = pl.pallas_call(
        body, name="scatter_wait", in_specs=specs, out_specs=[HBM_SPEC] * nl, out_shape=tuple(_hbm_like(lands)),
        input_output_aliases={a: a for a in range(nl)},
        compiler_params=pltpu.CompilerParams(has_side_effects=DATAFLOW),
    )(*operands)
    return list(outs)


def sibling_swap(parts):
    n = len(parts)

    def body(*refs):
        ins, outs = refs[:n], refs[n:2 * n]
        send, recv = refs[2 * n:]
        x, y, c, _ = _place()
        cps = [pltpu.make_async_remote_copy(src_ref=ins[a], dst_ref=outs[a], send_sem=send.at[a],
                                            recv_sem=recv.at[a], device_id=(x, y, 1 - c), device_id_type=MESH)
               for a in range(n)]
        for cp in cps:
            cp.start()
        for cp in cps:
            cp.wait_recv()
        for cp in cps:
            cp.wait_send()

    return pl.pallas_call(
        body, name="sibling_swap", in_specs=[ANY] * n, out_specs=[ANY] * n,
        out_shape=[jax.ShapeDtypeStruct(p.shape, p.dtype) for p in parts],
        scratch_shapes=[pltpu.SemaphoreType.DMA((n,)), pltpu.SemaphoreType.DMA((n,))],
    )(*parts)


def small_allreduce(p):
    R, C = p.shape
    ndev = 8

    def body(p_ref, o_ref, buf, send, recv):
        x, y, c, _ = _place()
        me = 4 * x + 2 * y + c
        buf[me] = p_ref[...]

        def peer(d):
            px = 1 - x if d & 4 else x
            py = 1 - y if d & 2 else y
            pc = 1 - c if d & 1 else c
            return px, py, pc

        def copy(d, slot):
            return pltpu.make_async_remote_copy(src_ref=p_ref, dst_ref=buf.at[slot], send_sem=send.at[d - 1],
                                                recv_sem=recv.at[d - 1], device_id=peer(d), device_id_type=MESH)

        sends = [copy(d, me) for d in range(1, ndev)]
        for cp in sends:
            cp.start()
        for d in range(1, ndev):
            px, py, pc = peer(d)
            copy(d, 4 * px + 2 * py + pc).wait_recv()
        for cp in sends:
            cp.wait_send()
        acc = buf[0]
        for k in range(1, ndev):
            acc = acc + buf[k]
        o_ref[...] = acc

    return pl.pallas_call(
        body, name="small_allreduce", in_specs=[VMEM_SPEC], out_specs=VMEM_SPEC,
        out_shape=jax.ShapeDtypeStruct((R, C), F32),
        scratch_shapes=[pltpu.VMEM((ndev, R, C), F32), pltpu.SemaphoreType.DMA((ndev - 1,)),
                        pltpu.SemaphoreType.DMA((ndev - 1,))],
        compiler_params=pltpu.CompilerParams(vmem_limit_bytes=32 << 20),
    )(p)


def kernel(x, norm1_g, w_in, conv_w, conv_b, conv_ln_g, conv_ln_b, ret_gn_g, w_out, norm2_g, w_gate, w_up, w_down, final_g, loss_target, m_norm1_g, m_w_in, m_conv_w, m_conv_b, m_conv_ln_g, m_conv_ln_b, m_ret_gn_g, m_w_out, m_norm2_g, m_w_gate, m_w_up, m_w_down, m_final_g, v_norm1_g, v_w_in, v_conv_w, v_conv_b, v_conv_ln_g, v_conv_ln_b, v_ret_gn_g, v_w_out, v_norm2_g, v_w_gate, v_w_up, v_w_down, v_final_g):
    S = x.shape[1]
    xs = x.reshape(S, D)
    tgt = loss_target.reshape(S, D)
    fsh = FF // NCHIP

    def shards_of(l):
        return [w_in[l].astype(BF), w_out[l].astype(BF), w_gate[l].T.astype(BF), w_up[l].T.astype(BF),
                w_down[l].astype(BF), conv_w[l]]

    gather_axes = [1, 0, 0, 0, 0, 1]
    shard_cache = [shards_of(l) for l in range(L)]
    tables = _ret_tables(S)
    row = lambda a, l: a[l].reshape(1, -1)
    groups = {}
    weights = [dict() for _ in range(L)]

    def begin(l, which, after):
        group, token = gather_start([shard_cache[l][i] for i in which], [gather_axes[i] for i in which],
                                    after, f"{l}_{which[0]}")
        groups[(l, which[0])] = (group, which)
        return token

    def finish(l, firsts, after, tag):
        gs = [groups[(l, f)] for f in firsts]
        outs = gather_wait([g for g, _ in gs], after, f"{l}_{tag}")
        k = 0
        for _, which in gs:
            for i in which:
                weights[l][i] = outs[k]
                k += 1

    def host(stage, after):
        tokens = []
        for gl, which in stage:
            tokens.append(begin(gl, which, [after] + tokens))
        return tokens

    def hosted(l):
        if l == 0:
            return {"in": [(0, [2])], "conv": [(0, [3])], "ret": [(0, [4])], "out": [(1, [0])],
                    "mlp": [(1, [1, 5]), (1, [2])]}
        nxt = {"ret": [(l + 1, [0])], "out": [(l + 1, [1, 5])], "mlp": [(l + 1, [2])]} if l + 1 < L else {}
        return {"in": [(l, [3])], "conv": [(l, [4])], **nxt}

    first = begin(0, [0], [])
    after = begin(0, [1, 5], [first])
    saved = []
    xc = xs
    for l in range(L):
        sched = hosted(l)
        finish(l, [0, 1], after, "a")
        win, wout, cw = weights[l][0], weights[l][1], weights[l][5]
        proj = in_proj(xc, row(norm1_g, l), win, l, host(sched.get("in", []), win))
        u1, u = conv_fwd(proj, cw, row(conv_b, l), row(conv_ln_g, l), row(conv_ln_b, l), l,
                         host(sched.get("conv", []), proj))
        r_raw, states = ret_fwd(proj, tables, l, host(sched.get("ret", []), u))
        x2, mixed = out_proj(u, r_raw, proj, row(ret_gn_g, l), wout, xc, l, host(sched.get("out", []), r_raw))
        finish(l, [2, 3, 4], x2, "b")
        wgt, wut, wd = weights[l][2], weights[l][3], weights[l][4]
        x3, gs, us = mlp_fwd(x2, row(norm2_g, l), wgt, wut, wd, l, host(sched.get("mlp", []), wd))
        saved.append((xc, proj, u1, r_raw, states, mixed, x2, gs, us))
        xc = x3
        after = x3

    dx, loss_acc, d_final = final_loss(xc, final_g.reshape(1, D), tgt)
    loss = lax.psum(loss_acc[0, 0] * (0.5 / D), ("x", "y", "c"))

    scatter_axes = [1, 0, 0, 0, 0]
    scatter_sizes = [INW // NCHIP, D // NCHIP, fsh, fsh, fsh]
    lands = [lax.empty((NCHIP, L, D, INW // NCHIP), BF), lax.empty((NCHIP, L, D // NCHIP, D), BF),
             lax.empty((NCHIP, L, fsh, D), BF), lax.empty((NCHIP, L, fsh, D), BF), lax.empty((NCHIP, L, fsh, D), BF)]
    sent, sent_which = [], []

    def send_grad(g, a, l):
        group, new_land, token = scatter_start([g], [scatter_axes[a]], [scatter_sizes[a]], [lands[a]], l, f"{l}_{a}")
        lands[a] = new_land[0]
        sent.append(group)
        sent_which.append([a])
        return [token]

    small = [None] * L
    dep = []
    for l in reversed(range(L)):
        xin, proj, u1, r_raw, states, mixed, x2, gs, us = saved[l]
        win, wout, wgt, wut, wd, cw = (weights[l][i] for i in range(6))
        dx2, dgs, dus, h2, d_n2 = mlp_bwd(dx, x2, row(norm2_g, l), gs, us, wgt, wut, wd, l, dep)
        g_wd = wgrad((gs, us), dx, l, "wgrad_down")
        g_wgt = wgrad(dgs, h2, l, "wgrad_gate", send_grad(g_wd, 4, l))
        g_wut = wgrad(dus, h2, l, "wgrad_up", send_grad(g_wgt, 2, l))
        dgate, dr, du1, sums = out_proj_bwd(dx2, wout, r_raw, proj, row(ret_gn_g, l), u1,
                                            row(conv_ln_g, l), row(conv_ln_b, l), l, send_grad(g_wut, 3, l))
        g_wout = wgrad(mixed, dx2, l, "wgrad_out")
        dab, dwb = conv_bwd(du1, proj, cw, l, send_grad(g_wout, 1, l))
        dq, dk, dv = ret_bwd(dr, proj, states, tables, l)
        dproj = [dab, dq, dk, dv, dgate]
        dx, h1, d_n1 = in_proj_bwd(dproj, win, xin, row(norm1_g, l), dx2, l)
        g_win = wgrad_in(h1, dproj, l)
        dep = send_grad(g_win, 0, l)
        small[l] = jnp.concatenate([dwb, sums, d_n1.reshape(2, CW), d_n2.reshape(2, CW)], axis=0)
    grad_x = dx.reshape(1, S, D)

    per = CK + 1 + 8 + 4
    packed = jnp.concatenate(small + [d_final.reshape(2, CW), jnp.zeros((6, CW), F32)], axis=0)
    tot = small_allreduce(packed)
    lay = tot[:L * per].reshape(L, per, CW)
    g_conv_w_full = lay[:, 0:CK, :]
    j = 2 * lax.axis_index("x") + lax.axis_index("y")
    g_conv_w = lax.dynamic_slice_in_dim(g_conv_w_full, j * (CW // NCHIP), CW // NCHIP, axis=2)
    g_small = {
        "conv_b": lay[:, CK, :], "ret_gn_g": lay[:, CK + 1, :], "conv_ln_g": lay[:, CK + 2, :],
        "conv_ln_b": lay[:, CK + 3, :], "norm1_g": lay[:, CK + 9:CK + 11, :].reshape(L, D),
        "norm2_g": lay[:, CK + 11:CK + 13, :].reshape(L, D), "final_g": tot[L * per:L * per + 2].reshape(D),
    }

    recv = scatter_wait(sent, lands, sent_which)
    shard_shapes = [(L * D, INW // NCHIP), (L * D // NCHIP, D), (L * fsh, D), (L * fsh, D), (L * fsh, D)]
    names = ["w_in", "w_out", "w_gate", "w_up", "w_down"]
    parts = [sum_slots(r.reshape((NCHIP,) + shp), f"sum_{nm}") for r, shp, nm in zip(recv, shard_shapes, names)]
    theirs = sibling_swap(parts)

    def unT(a):
        return jnp.swapaxes(a.reshape(L, fsh, D), 1, 2).reshape(L * D, fsh)

    big = {}
    wmv = {"w_in": (w_in, m_w_in, v_w_in), "w_out": (w_out, m_w_out, v_w_out),
           "w_gate": (w_gate, m_w_gate, v_w_gate), "w_up": (w_up, m_w_up, v_w_up),
           "w_down": (w_down, m_w_down, v_w_down)}
    for nm, mine, other in zip(names, parts, theirs):
        w, m, v = wmv[nm]
        if nm in ("w_gate", "w_up"):
            mine, other = unT(mine), unT(other)
        shp2 = (w.shape[0] * w.shape[1], w.shape[2])
        outs = adamw(w.reshape(shp2), mine, other, m.reshape(shp2), v.reshape(shp2), f"adamw_{nm}")
        big[nm] = [o.reshape(w.shape) for o in outs]

    cshape = (L * CK, CW // NCHIP)
    zc = jnp.zeros(cshape, F32)
    big["conv_w"] = [o.reshape(conv_w.shape) for o in adamw(
        conv_w.reshape(cshape), g_conv_w.reshape(cshape), zc, m_conv_w.reshape(cshape),
        v_conv_w.reshape(cshape), "adamw_conv_w")]
    vec_names = ["norm1_g", "conv_b", "conv_ln_g", "conv_ln_b", "ret_gn_g", "norm2_g", "final_g"]
    vec_w = {"norm1_g": (norm1_g, m_norm1_g, v_norm1_g), "conv_b": (conv_b, m_conv_b, v_conv_b),
             "conv_ln_g": (conv_ln_g, m_conv_ln_g, v_conv_ln_g), "conv_ln_b": (conv_ln_b, m_conv_ln_b, v_conv_ln_b),
             "ret_gn_g": (ret_gn_g, m_ret_gn_g, v_ret_gn_g), "norm2_g": (norm2_g, m_norm2_g, v_norm2_g),
             "final_g": (final_g, m_final_g, v_final_g)}
    cat = lambda arrs: jnp.concatenate([a.reshape(-1, CW) for a in arrs], axis=0)
    vw = cat([vec_w[nm][0] for nm in vec_names])
    vm = cat([vec_w[nm][1] for nm in vec_names])
    vv = cat([vec_w[nm][2] for nm in vec_names])
    vg = cat([g_small[nm] for nm in vec_names])
    vouts = adamw(vw, vg, jnp.zeros_like(vg), vm, vv, "adamw_vectors")
    off = 0
    for nm in vec_names:
        w = vec_w[nm][0]
        nrow = w.size // CW
        big[nm] = [o[off:off + nrow].reshape(w.shape) for o in vouts]
        off += nrow

    order = ["norm1_g", "w_in", "conv_w", "conv_b", "conv_ln_g", "conv_ln_b", "ret_gn_g", "w_out", "norm2_g",
             "w_gate", "w_up", "w_down", "final_g"]
    return (loss, grad_x, *[big[nm][0] for nm in order], *[big[nm][1] for nm in order],
            *[big[nm][2] for nm in order], *[big[nm][3] for nm in order])
```

```python
import math

import jax
import jax.numpy as jnp
from jax import lax
from jax.experimental import pallas as pl
from jax.experimental.pallas import tpu as pltpu

D = 1024
L = 4
CW = 512
RW = 512
NH = 4
HD = 128
CK = 31
CHUNK = 64
INW = 3072
FF = 2816
NCHIP = 4
EPS = 1e-6
ROPE_BASE = 10000.0
SEQ_TILE = 512
MLP_ROWS = 256
MLP_COLS = 1408
HALO = 32
CONV_ROWS = 32
CONV_COLS = 256

ADAM_LR = 0.001
ADAM_B1 = 0.9
ADAM_B2 = 0.999
ADAM_EPS = 1e-08
ADAM_WD = 0.01
ADAM_STEP = 10

BF = jnp.bfloat16
F32 = jnp.float32
MESH = pl.DeviceIdType.MESH
ANY = pl.BlockSpec(memory_space=pl.ANY)
VMEM_SPEC = pl.BlockSpec(memory_space=pltpu.VMEM)
HBM_SPEC = pl.BlockSpec(memory_space=pltpu.HBM)
SEM_SPEC = pl.BlockSpec(memory_space=pltpu.SEMAPHORE)
DATAFLOW = pltpu.SideEffectType.DATAFLOW_SIDE_EFFECTING


def _params(n_grid, vmem_mb):
    return pltpu.CompilerParams(dimension_semantics=("arbitrary",) * n_grid,
                                vmem_limit_bytes=vmem_mb << 20)


def _ordered_call(body, deps, *, in_specs, **kw):
    n, nd = len(in_specs), len(deps)

    def with_deps(*refs):
        body(*refs[:n], *refs[n + nd:])

    return pl.pallas_call(with_deps, in_specs=list(in_specs) + [ANY] * nd, **kw)


def _dot(a, b):
    return jnp.dot(a, b, preferred_element_type=F32)


def _dot_nt(a, b):
    return lax.dot_general(a, b, (((1,), (1,)), ((), ())), preferred_element_type=F32)


def _dot_tn(a, b):
    return lax.dot_general(a, b, (((0,), (0,)), ((), ())), preferred_element_type=F32)


def _sigmoid(x):
    return 0.5 * jnp.tanh(0.5 * x) + 0.5


def _mean(x):
    return jnp.mean(x, axis=-1, keepdims=True)


def _tap_groups(first, count):
    groups = []
    for phase in range(8):
        taps = [(t, first + t - phase) for t in range(count) if (first + t) % 8 == phase]
        if taps:
            lo, hi = min(q for _, q in taps), max(q for _, q in taps)
            groups.append((lo + phase, hi - lo + CONV_ROWS, [(t, q - lo) for t, q in taps]))
    return groups


def _rot(t, cs, sn):
    return t * cs + pltpu.roll(t, HD // 2, 1) * sn


def _rot_t(dy, cs, sn):
    return dy * cs + pltpu.roll(dy * sn, HD // 2, 1)


def _rms_bwd(x, g, dh, dx_in):
    r = lax.rsqrt(_mean(x * x) + EPS)
    xh = x * r
    dxh = dh * g
    dx = dx_in + r * (dxh - xh * _mean(dxh * xh))
    return dx, jnp.sum(dh * xh, axis=0, keepdims=True), (xh * g).astype(BF)


def in_proj(x, g, win, l, deps=()):
    S = x.shape[0]
    tm = min(S, SEQ_TILE)

    def body(x_ref, g_ref, w_ref, o_ref):
        xv = x_ref[...]
        h = (xv * lax.rsqrt(_mean(xv * xv) + EPS) * g_ref[...]).astype(BF)
        o_ref[...] = _dot(h, w_ref[...])

    return _ordered_call(
        body, deps, name=f"in_proj_{l}", grid=(S // tm,),
        in_specs=[pl.BlockSpec((tm, D), lambda i: (i, 0)),
                  pl.BlockSpec((1, D), lambda i: (0, 0)),
                  pl.BlockSpec((D, INW), lambda i: (0, 0))],
        out_specs=pl.BlockSpec((tm, INW), lambda i: (i, 0)),
        out_shape=jax.ShapeDtypeStruct((S, INW), F32),
        compiler_params=_params(1, 48),
    )(x, g, win, *deps)


def conv_fwd(proj, cw, cb, lg, lb, l, deps=()):
    S = proj.shape[0]
    tc = min(S, SEQ_TILE)

    def body(a_ref, b_ref, w_ref, cb_ref, lg_ref, lb_ref, u1_ref, u_ref, buf, win):
        i = pl.program_id(0)

        @pl.when(i == 0)
        def _():
            buf[0:HALO, :] = jnp.zeros((HALO, CW), F32)

        @pl.when(i > 0)
        def _():
            buf[0:HALO, :] = buf[tc:tc + HALO, :]

        buf[HALO:HALO + tc, :] = a_ref[...] * _sigmoid(b_ref[...])
        groups = _tap_groups(HALO - (CK - 1), CK)
        for r0 in range(0, tc, CONV_ROWS):
            for c0 in range(0, CW, CONV_COLS):
                cols = slice(c0, c0 + CONV_COLS)
                acc = jnp.broadcast_to(cb_ref[:, cols], (CONV_ROWS, CONV_COLS))
                for start, length, taps in groups:
                    win[0:length, :] = buf[r0 + start:r0 + start + length, cols]
                    for k, at in taps:
                        acc = acc + w_ref[k:k + 1, cols] * win[at:at + CONV_ROWS, :]
                u1_ref[r0:r0 + CONV_ROWS, cols] = acc
            acc = u1_ref[r0:r0 + CONV_ROWS, :]
            d = acc - _mean(acc)
            u2 = d * lax.rsqrt(_mean(d * d) + EPS) * lg_ref[...] + lb_ref[...]
            u_ref[r0:r0 + CONV_ROWS, :] = (u2 * _sigmoid(u2)).astype(BF)

    vec = pl.BlockSpec((1, CW), lambda i: (0, 0))
    return _ordered_call(
        body, deps, name=f"conv_fwd_{l}", grid=(S // tc,),
        in_specs=[pl.BlockSpec((tc, CW), lambda i: (i, 0)),
                  pl.BlockSpec((tc, CW), lambda i: (i, 1)),
                  pl.BlockSpec((CK, CW), lambda i: (0, 0)), vec, vec, vec],
        out_specs=[pl.BlockSpec((tc, CW), lambda i: (i, 0)),
                   pl.BlockSpec((tc, CW), lambda i: (i, 0))],
        out_shape=[jax.ShapeDtypeStruct((S, CW), F32), jax.ShapeDtypeStruct((S, CW), BF)],
        scratch_shapes=[pltpu.VMEM((tc + HALO, CW), F32), pltpu.VMEM((HALO + CONV_ROWS, CONV_COLS), F32)],
        compiler_params=_params(1, 32),
    )(proj, proj, cw, cb, lg, lb, *deps)


def _ret_tables(S):
    half = HD // 2
    pos = jnp.arange(S, dtype=F32)
    freqs = ROPE_BASE ** (-jnp.arange(half, dtype=F32) / half)
    ang = pos[:, None] * freqs[None, :]
    cos, sin = jnp.cos(ang), jnp.sin(ang)
    cosf = jnp.concatenate([cos, cos], axis=-1)
    sinf = jnp.concatenate([-sin, sin], axis=-1)
    log_g = jnp.log(1.0 - 2.0 ** (-5.0 - jnp.arange(NH, dtype=F32)))
    idx = jnp.arange(CHUNK, dtype=F32)
    dmat = jnp.exp(log_g[:, None, None] * jnp.abs(idx[:, None] - idx[None, :]))
    qdec = jnp.broadcast_to(jnp.exp(log_g[:, None] * (idx + 1.0))[:, :, None], (NH, CHUNK, HD))
    kdec = jnp.broadcast_to(jnp.exp(log_g[:, None] * (CHUNK - 1 - idx))[:, :, None], (NH, CHUNK, HD))
    cdec = jnp.broadcast_to(jnp.exp(log_g * CHUNK)[:, None, None], (NH, HD, HD))
    return cosf, sinf, dmat, qdec, kdec, cdec


def _ret_specs(tr, tmap):
    q0 = (2 * CW) // RW
    return [pl.BlockSpec((tr, RW), lambda t: (tmap(t), q0)),
            pl.BlockSpec((tr, RW), lambda t: (tmap(t), q0 + 1)),
            pl.BlockSpec((tr, RW), lambda t: (tmap(t), q0 + 2)),
            pl.BlockSpec((tr, HD), lambda t: (tmap(t), 0)),
            pl.BlockSpec((tr, HD), lambda t: (tmap(t), 0)),
            pl.BlockSpec((NH, CHUNK, CHUNK), lambda t: (0, 0, 0)),
            pl.BlockSpec((NH, CHUNK, HD), lambda t: (0, 0, 0)),
            pl.BlockSpec((NH, CHUNK, HD), lambda t: (0, 0, 0)),
            pl.BlockSpec((NH, HD, HD), lambda t: (0, 0, 0))]


def ret_fwd(proj, tables, l, deps=()):
    S = proj.shape[0]
    tr = min(S, SEQ_TILE)
    cpb = tr // CHUNK
    scale = HD ** -0.5

    def body(q_ref, k_ref, v_ref, cos_ref, sin_ref, dm_ref, qd_ref, kd_ref, cd_ref,
             r_ref, st_ref, st):
        @pl.when(pl.program_id(0) == 0)
        def _():
            st[...] = jnp.zeros((NH, HD, HD), F32)

        for c in range(cpb):
            rows = slice(c * CHUNK, (c + 1) * CHUNK)
            cs, sn = cos_ref[rows, :], sin_ref[rows, :]
            for h in range(NH):
                cols = slice(h * HD, (h + 1) * HD)
                qr = _rot(q_ref[rows, cols], cs, sn)
                kr = _rot(k_ref[rows, cols], cs, sn) * scale
                vb = v_ref[rows, cols].astype(BF)
                s = st[h]
                sb = s.astype(BF)
                st_ref[h, c] = sb
                sc = _dot_nt(qr.astype(BF), kr.astype(BF)) * dm_ref[h]
                r_ref[rows, cols] = _dot(sc.astype(BF), vb) + _dot((qr * qd_ref[h]).astype(BF), sb)
                st[h] = cd_ref[h] * s + _dot_tn((kr * kd_ref[h]).astype(BF), vb)

    return _ordered_call(
        body, deps, name=f"ret_fwd_{l}", grid=(S // tr,),
        in_specs=_ret_specs(tr, lambda t: t),
        out_specs=[pl.BlockSpec((tr, RW), lambda t: (t, 0)),
                   pl.BlockSpec((NH, cpb, HD, HD), lambda t: (0, t, 0, 0))],
        out_shape=[jax.ShapeDtypeStruct((S, RW), F32),
                   jax.ShapeDtypeStruct((NH, S // CHUNK, HD, HD), BF)],
        scratch_shapes=[pltpu.VMEM((NH, HD, HD), F32)],
        compiler_params=_params(1, 32),
    )(proj, proj, proj, *tables, *deps)


def out_proj(u, r_raw, proj, gn, wout, x, l, deps=()):
    S = x.shape[0]
    tm = min(S, SEQ_TILE)
    gate_blk = (2 * CW + 3 * RW) // RW

    def body(u_ref, r_ref, gate_ref, gn_ref, w_ref, x_ref, x2_ref, mix_ref):
        mix_ref[:, 0:CW] = u_ref[...]
        gt = gate_ref[...]
        sil = gt * _sigmoid(gt) * gn_ref[...]
        for h in range(NH):
            cols = slice(h * HD, (h + 1) * HD)
            rh = r_ref[:, cols]
            d = rh - _mean(rh)
            rn = d * lax.rsqrt(_mean(d * d) + EPS)
            mix_ref[:, CW + h * HD:CW + (h + 1) * HD] = (rn * sil[:, cols]).astype(BF)
        x2_ref[...] = x_ref[...] + _dot(mix_ref[...], w_ref[...])

    return _ordered_call(
        body, deps, name=f"out_proj_{l}", grid=(S // tm,),
        in_specs=[pl.BlockSpec((tm, CW), lambda i: (i, 0)),
                  pl.BlockSpec((tm, RW), lambda i: (i, 0)),
                  pl.BlockSpec((tm, RW), lambda i: (i, gate_blk)),
                  pl.BlockSpec((1, RW), lambda i: (0, 0)),
                  pl.BlockSpec((D, D), lambda i: (0, 0)),
                  pl.BlockSpec((tm, D), lambda i: (i, 0))],
        out_specs=[pl.BlockSpec((tm, D), lambda i: (i, 0)),
                   pl.BlockSpec((tm, D), lambda i: (i, 0))],
        out_shape=[jax.ShapeDtypeStruct((S, D), F32), jax.ShapeDtypeStruct((S, D), BF)],
        compiler_params=_params(1, 40),
    )(u, r_raw, proj, gn, wout, x, *deps)


def mlp_fwd(x2, g2, wgt, wut, wd, l, deps=()):
    S = x2.shape[0]
    tm, tf = min(S, MLP_ROWS), MLP_COLS

    def body(x_ref, g_ref, wg_ref, wu_ref, wd_ref, o_ref, gs_ref, us_ref, a_ref):
        xv = x_ref[...]
        h = (xv * lax.rsqrt(_mean(xv * xv) + EPS) * g_ref[...]).astype(BF)
        for c0 in range(0, FF, tf):
            gv = _dot_nt(h, wg_ref[c0:c0 + tf, :])
            uv = _dot_nt(h, wu_ref[c0:c0 + tf, :])
            gs_ref[:, c0:c0 + tf] = gv.astype(BF)
            us_ref[:, c0:c0 + tf] = uv.astype(BF)
            a_ref[:, c0:c0 + tf] = (gv * _sigmoid(gv) * uv).astype(BF)
        o_ref[...] = xv + _dot(a_ref[...], wd_ref[...])

    wspec = pl.BlockSpec((FF, D), lambda i: (0, 0), pipeline_mode=pl.Buffered(1))
    row = pl.BlockSpec((tm, D), lambda i: (i, 0))
    wide = pl.BlockSpec((tm, FF), lambda i: (i, 0))
    return _ordered_call(
        body, deps, name=f"mlp_fwd_{l}", grid=(S // tm,),
        in_specs=[row, pl.BlockSpec((1, D), lambda i: (0, 0)), wspec, wspec, wspec],
        out_specs=[row, wide, wide, wide],
        out_shape=[jax.ShapeDtypeStruct((S, D), F32)] + [jax.ShapeDtypeStruct((S, FF), BF)] * 3,
        compiler_params=_params(1, 56),
    )(x2, g2, wgt, wut, wd, *deps)


def final_loss(x, gf, tgt):
    S = x.shape[0]
    tm = min(S, SEQ_TILE)

    def body(x_ref, g_ref, t_ref, dx_ref, loss_ref, dg_ref):
        @pl.when(pl.program_id(0) == 0)
        def _():
            loss_ref[...] = jnp.zeros((8, 128), F32)
            dg_ref[...] = jnp.zeros((1, D), F32)

        xv = x_ref[...]
        r = lax.rsqrt(_mean(xv * xv) + EPS)
        xh = xv * r
        diff = xh * g_ref[...] - t_ref[...]
        loss_ref[...] += jnp.sum(jnp.sum(diff * diff, axis=-1, keepdims=True), axis=0, keepdims=True)
        dy = diff * (1.0 / D)
        dg_ref[...] += jnp.sum(dy * xh, axis=0, keepdims=True)
        dxh = dy * g_ref[...]
        dx_ref[...] = r * (dxh - xh * _mean(dxh * xh))

    return pl.pallas_call(
        body, name="final_loss", grid=(S // tm,),
        in_specs=[pl.BlockSpec((tm, D), lambda i: (i, 0)),
                  pl.BlockSpec((1, D), lambda i: (0, 0)),
                  pl.BlockSpec((tm, D), lambda i: (i, 0))],
        out_specs=[pl.BlockSpec((tm, D), lambda i: (i, 0)),
                   pl.BlockSpec((8, 128), lambda i: (0, 0)),
                   pl.BlockSpec((1, D), lambda i: (0, 0))],
        out_shape=[jax.ShapeDtypeStruct((S, D), F32), jax.ShapeDtypeStruct((8, 128), F32),
                   jax.ShapeDtypeStruct((1, D), F32)],
        compiler_params=_params(1, 40),
    )(x, gf, tgt)


def mlp_bwd(dx3, x2, g2, gs, us, wgt, wut, wd, l, deps=()):
    S = x2.shape[0]
    tm, tf = min(S, MLP_ROWS), MLP_COLS

    def body(dx_ref, x_ref, g_ref, gs_ref, us_ref, wg_ref, wu_ref, wd_ref,
             dx2_ref, dg_ref, du_ref, h_ref, dgain_ref):
        @pl.when(pl.program_id(0) == 0)
        def _():
            dgain_ref[...] = jnp.zeros((1, D), F32)

        dxv = dx_ref[...]
        dxb = dxv.astype(BF)
        for c0 in range(0, FF, tf):
            da = _dot_nt(dxb, wd_ref[c0:c0 + tf, :])
            gv = gs_ref[:, c0:c0 + tf].astype(F32)
            uv = us_ref[:, c0:c0 + tf].astype(F32)
            sg = _sigmoid(gv)
            dg_ref[:, c0:c0 + tf] = (da * uv * (sg * (1.0 + gv * (1.0 - sg)))).astype(BF)
            du_ref[:, c0:c0 + tf] = (da * (gv * sg)).astype(BF)
        dh = _dot(dg_ref[...], wg_ref[...]) + _dot(du_ref[...], wu_ref[...])
        dx2, dgain, hb = _rms_bwd(x_ref[...], g_ref[...], dh, dxv)
        dx2_ref[...] = dx2
        dgain_ref[...] += dgain
        h_ref[...] = hb

    wspec = pl.BlockSpec((FF, D), lambda i: (0, 0), pipeline_mode=pl.Buffered(1))
    row = pl.BlockSpec((tm, D), lambda i: (i, 0))
    wide = pl.BlockSpec((tm, FF), lambda i: (i, 0))
    vec = pl.BlockSpec((1, D), lambda i: (0, 0))
    return _ordered_call(
        body, deps, name=f"mlp_bwd_{l}", grid=(S // tm,),
        in_specs=[row, row, vec, wide, wide, wspec, wspec, wspec],
        out_specs=[row, wide, wide, row, vec],
        out_shape=[jax.ShapeDtypeStruct((S, D), F32), jax.ShapeDtypeStruct((S, FF), BF),
                   jax.ShapeDtypeStruct((S, FF), BF), jax.ShapeDtypeStruct((S, D), BF),
                   jax.ShapeDtypeStruct((1, D), F32)],
        compiler_params=_params(1, 56),
    )(dx3, x2, g2, gs, us, wgt, wut, wd, *deps)


def wgrad(a, b, l, name, deps=()):
    S, K = a.shape
    N = b.shape[1]
    tk = 1408 if K == FF else min(K, 1024)
    tn = min(N, 1024)
    ts = min(S, SEQ_TILE)
    ns = S // ts

    def body(a_ref, b_ref, o_ref, acc):
        s = pl.program_id(2)

        @pl.when(s == 0)
        def _():
            acc[...] = jnp.zeros((tk, tn), F32)

        acc[...] += _dot_tn(a_ref[...], b_ref[...].astype(BF))

        @pl.when(s == ns - 1)
        def _():
            o_ref[...] = acc[...].astype(BF)

    return _ordered_call(
        body, deps, name=f"{name}_{l}", grid=(K // tk, N // tn, ns),
        in_specs=[pl.BlockSpec((ts, tk), lambda i, j, s: (s, i)),
                  pl.BlockSpec((ts, tn), lambda i, j, s: (s, j))],
        out_specs=pl.BlockSpec((tk, tn), lambda i, j, s: (i, j)),
        out_shape=jax.ShapeDtypeStruct((K, N), BF),
        scratch_shapes=[pltpu.VMEM((tk, tn), F32)],
        compiler_params=_params(3, 48),
    )(a, b, *deps)


def out_proj_bwd(dx2, wout, r_raw, proj, gn, u1, lg, lb, l, deps=()):
    S = dx2.shape[0]
    tm = min(S, SEQ_TILE)
    gate_blk = (2 * CW + 3 * RW) // RW

    def body(dx_ref, w_ref, r_ref, gate_ref, gn_ref, u1_ref, lg_ref, lb_ref,
             dgate_ref, dr_ref, du1_ref, sums_ref):
        @pl.when(pl.program_id(0) == 0)
        def _():
            sums_ref[...] = jnp.zeros((8, CW), F32)

        dmix = _dot_nt(dx_ref[...].astype(BF), w_ref[...])
        gt = gate_ref[...]
        sg = _sigmoid(gt)
        sil = gt * sg
        dsil = sg * (1.0 + gt * (1.0 - sg))
        for h in range(NH):
            cols = slice(h * HD, (h + 1) * HD)
            rh = r_ref[:, cols]
            d = rh - _mean(rh)
            rs = lax.rsqrt(_mean(d * d) + EPS)
            rn = d * rs
            drr = dmix[:, CW + h * HD:CW + (h + 1) * HD]
            gnh = gn_ref[:, cols]
            sums_ref[0:1, cols] += jnp.sum(drr * rn * sil[:, cols], axis=0, keepdims=True)
            dgate_ref[:, cols] = (drr * rn * gnh * dsil[:, cols]).astype(BF)
            drn = drr * gnh * sil[:, cols]
            dr_ref[:, cols] = (rs * (drn - _mean(drn) - rn * _mean(drn * rn))).astype(BF)
        du = dmix[:, 0:CW]
        u1 = u1_ref[...]
        d = u1 - _mean(u1)
        rs = lax.rsqrt(_mean(d * d) + EPS)
        xh = d * rs
        u2 = xh * lg_ref[...] + lb_ref[...]
        sg2 = _sigmoid(u2)
        du2 = du * (sg2 * (1.0 + u2 * (1.0 - sg2)))
        sums_ref[1:2, :] += jnp.sum(du2 * xh, axis=0, keepdims=True)
        sums_ref[2:3, :] += jnp.sum(du2, axis=0, keepdims=True)
        dxh = du2 * lg_ref[...]
        du1_ref[...] = rs * (dxh - _mean(dxh) - xh * _mean(dxh * xh))

    vec = pl.BlockSpec((1, CW), lambda i: (0, 0))
    half = pl.BlockSpec((tm, CW), lambda i: (i, 0))
    return _ordered_call(
        body, deps, name=f"out_proj_bwd_{l}", grid=(S // tm,),
        in_specs=[pl.BlockSpec((tm, D), lambda i: (i, 0)),
                  pl.BlockSpec((D, D), lambda i: (0, 0)),
                  half, pl.BlockSpec((tm, RW), lambda i: (i, gate_blk)), vec, half, vec, vec],
        out_specs=[half, half, half, pl.BlockSpec((8, CW), lambda i: (0, 0))],
        out_shape=[jax.ShapeDtypeStruct((S, RW), BF), jax.ShapeDtypeStruct((S, RW), BF),
                   jax.ShapeDtypeStruct((S, CW), F32), jax.ShapeDtypeStruct((8, CW), F32)],
        compiler_params=_params(1, 40),
    )(dx2, wout, r_raw, proj, gn, u1, lg, lb, *deps)


def conv_bwd(du1, proj, cw, l, deps=()):
    S = proj.shape[0]
    tc = min(S, SEQ_TILE)
    nt = S // tc

    def body(du1_ref, a_ref, b_ref, w_ref, dab_ref, dwb_ref, buf, win):
        i = pl.program_id(0)

        @pl.when(i == 0)
        def _():
            buf[tc:tc + HALO, :] = jnp.zeros((HALO, CW), F32)
            dwb_ref[...] = jnp.zeros((CK + 1, CW), F32)

        @pl.when(i > 0)
        def _():
            buf[tc:tc + HALO, :] = buf[0:HALO, :]

        buf[0:tc, :] = du1_ref[...]
        groups = _tap_groups(0, CK)
        for c0 in range(0, CW, CONV_COLS):
            cols = slice(c0, c0 + CONV_COLS)
            parts = [jnp.zeros((8, CONV_COLS), F32) for _ in range(CK)]
            for r0 in range(0, tc, CONV_ROWS):
                av = a_ref[r0:r0 + CONV_ROWS, cols]
                sgb = _sigmoid(b_ref[r0:r0 + CONV_ROWS, cols])
                u0 = av * sgb
                acc = jnp.zeros((CONV_ROWS, CONV_COLS), F32)
                for start, length, taps in groups:
                    win[0:length, :] = buf[r0 + start:r0 + start + length, cols]
                    for j, at in taps:
                        sl = win[at:at + CONV_ROWS, :]
                        acc = acc + w_ref[CK - 1 - j:CK - j, cols] * sl
                        pr = u0 * sl
                        red = pr[0:8, :]
                        for q in range(1, CONV_ROWS // 8):
                            red = red + pr[8 * q:8 * q + 8, :]
                        parts[CK - 1 - j] = parts[CK - 1 - j] + red
                dab_ref[r0:r0 + CONV_ROWS, c0:c0 + CONV_COLS] = (acc * sgb).astype(BF)
                dab_ref[r0:r0 + CONV_ROWS, CW + c0:CW + c0 + CONV_COLS] = (acc * av * sgb * (1.0 - sgb)).astype(BF)
            for k in range(CK):
                dwb_ref[k:k + 1, cols] += jnp.sum(parts[k], axis=0, keepdims=True)
        dwb_ref[CK:CK + 1, :] += jnp.sum(du1_ref[...], axis=0, keepdims=True)

    return _ordered_call(
        body, deps, name=f"conv_bwd_{l}", grid=(nt,),
        in_specs=[pl.BlockSpec((tc, CW), lambda i: (nt - 1 - i, 0)),
                  pl.BlockSpec((tc, CW), lambda i: (nt - 1 - i, 0)),
                  pl.BlockSpec((tc, CW), lambda i: (nt - 1 - i, 1)),
                  pl.BlockSpec((CK, CW), lambda i: (0, 0))],
        out_specs=[pl.BlockSpec((tc, 2 * CW), lambda i: (nt - 1 - i, 0)),
                   pl.BlockSpec((CK + 1, CW), lambda i: (0, 0))],
        out_shape=[jax.ShapeDtypeStruct((S, 2 * CW), BF), jax.ShapeDtypeStruct((CK + 1, CW), F32)],
        scratch_shapes=[pltpu.VMEM((tc + HALO, CW), F32), pltpu.VMEM((HALO + CONV_ROWS, CONV_COLS), F32)],
        compiler_params=_params(1, 32),
    )(du1, proj, proj, cw, *deps)


def ret_bwd(dr, proj, states, tables, l):
    S = proj.shape[0]
    tr = min(S, SEQ_TILE)
    cpb = tr // CHUNK
    nt = S // tr
    scale = HD ** -0.5

    def body(q_ref, k_ref, v_ref, cos_ref, sin_ref, dm_ref, qd_ref, kd_ref, cd_ref, dr_ref, st_ref,
             dq_ref, dk_ref, dv_ref, gst):
        @pl.when(pl.program_id(0) == 0)
        def _():
            gst[...] = jnp.zeros((NH, HD, HD), F32)

        for c in reversed(range(cpb)):
            rows = slice(c * CHUNK, (c + 1) * CHUNK)
            cs, sn = cos_ref[rows, :], sin_ref[rows, :]
            for h in range(NH):
                cols = slice(h * HD, (h + 1) * HD)
                qr = _rot(q_ref[rows, cols], cs, sn)
                kr = _rot(k_ref[rows, cols], cs, sn) * scale
                qb, kb = qr.astype(BF), kr.astype(BF)
                vb = v_ref[rows, cols].astype(BF)
                dob = dr_ref[rows, cols]
                sb = st_ref[h, c]
                gn1 = gst[h]
                gb = gn1.astype(BF)
                sc = (_dot_nt(qb, kb) * dm_ref[h]).astype(BF)
                dsc = (_dot_nt(dob, vb) * dm_ref[h]).astype(BF)
                dqr = _dot(dsc, kb) + _dot_nt(dob, sb) * qd_ref[h]
                dkr = _dot_tn(dsc, qb) + _dot_nt(vb, gb) * kd_ref[h]
                dvv = _dot_tn(sc, dob) + _dot((kr * kd_ref[h]).astype(BF), gb)
                gst[h] = cd_ref[h] * gn1 + _dot_tn((qr * qd_ref[h]).astype(BF), dob)
                dq_ref[rows, cols] = _rot_t(dqr, cs, sn).astype(BF)
                dk_ref[rows, cols] = _rot_t(dkr * scale, cs, sn).astype(BF)
                dv_ref[rows, cols] = dvv.astype(BF)

    rev = lambda t: nt - 1 - t
    blk = pl.BlockSpec((tr, RW), lambda t: (rev(t), 0))
    return pl.pallas_call(
        body, name=f"ret_bwd_{l}", grid=(nt,),
        in_specs=_ret_specs(tr, rev) + [blk, pl.BlockSpec((NH, cpb, HD, HD), lambda t: (0, rev(t), 0, 0))],
        out_specs=[blk, blk, blk],
        out_shape=[jax.ShapeDtypeStruct((S, RW), BF)] * 3,
        scratch_shapes=[pltpu.VMEM((NH, HD, HD), F32)],
        compiler_params=_params(1, 32),
    )(proj, proj, proj, *tables, dr, states)


def in_proj_bwd(parts, win, x, g, dx2, l):
    S = x.shape[0]
    tm = min(S, SEQ_TILE)
    n = len(parts)

    def body(*refs):
        srcs = refs[:n]
        w_ref, x_ref, g_ref, dx2_ref, dx_ref, h_ref, dgain_ref = refs[n:]

        @pl.when(pl.program_id(0) == 0)
        def _():
            dgain_ref[...] = jnp.zeros((1, D), F32)

        dh, col = None, 0
        for r in srcs:
            width = r.shape[1]
            term = _dot_nt(r[...], w_ref[:, col:col + width])
            dh = term if dh is None else dh + term
            col += width
        dx, dgain, hb = _rms_bwd(x_ref[...], g_ref[...], dh, dx2_ref[...])
        dx_ref[...] = dx
        dgain_ref[...] += dgain
        h_ref[...] = hb

    row = pl.BlockSpec((tm, D), lambda i: (i, 0))
    vec = pl.BlockSpec((1, D), lambda i: (0, 0))
    return pl.pallas_call(
        body, name=f"in_proj_bwd_{l}", grid=(S // tm,),
        in_specs=[pl.BlockSpec((tm, p.shape[1]), lambda i: (i, 0)) for p in parts]
        + [pl.BlockSpec((D, INW), lambda i: (0, 0)), row, vec, row],
        out_specs=[row, row, vec],
        out_shape=[jax.ShapeDtypeStruct((S, D), F32), jax.ShapeDtypeStruct((S, D), BF),
                   jax.ShapeDtypeStruct((1, D), F32)],
        compiler_params=_params(1, 48),
    )(*parts, win, x, g, dx2)


def wgrad_in(h, parts, l):
    S = h.shape[0]
    ts = min(S, SEQ_TILE)
    ns = S // ts

    def body(*refs):
        h_ref, srcs = refs[0], refs[1:1 + len(parts)]
        o_ref, acc = refs[-2], refs[-1]
        s = pl.program_id(0)

        @pl.when(s == 0)
        def _():
            acc[...] = jnp.zeros((D, INW), F32)

        hv = h_ref[...]
        col = 0
        for r in srcs:
            width = r.shape[1]
            acc[:, col:col + width] += _dot_tn(hv, r[...])
            col += width

        @pl.when(s == ns - 1)
        def _():
            o_ref[...] = acc[...].astype(BF)

    return pl.pallas_call(
        body, name=f"wgrad_in_{l}", grid=(ns,),
        in_specs=[pl.BlockSpec((ts, D), lambda s: (s, 0))]
        + [pl.BlockSpec((ts, p.shape[1]), lambda s: (s, 0)) for p in parts],
        out_specs=pl.BlockSpec((D, INW), lambda s: (0, 0)),
        out_shape=jax.ShapeDtypeStruct((D, INW), BF),
        scratch_shapes=[pltpu.VMEM((D, INW), F32)],
        compiler_params=_params(1, 48),
    )(h, *parts)


def sum_slots(recv, name):
    _, R, C = recv.shape
    tr = 256 if R % 256 == 0 else R

    def body(r_ref, o_ref):
        acc = r_ref[0].astype(F32)
        for k in range(1, NCHIP):
            acc = acc + r_ref[k].astype(F32)
        o_ref[...] = acc

    return pl.pallas_call(
        body, name=name, grid=(R // tr,),
        in_specs=[pl.BlockSpec((NCHIP, tr, C), lambda i: (0, i, 0))],
        out_specs=pl.BlockSpec((tr, C), lambda i: (i, 0)),
        out_shape=jax.ShapeDtypeStruct((R, C), F32),
        compiler_params=_params(1, 32),
    )(recv)


def adamw(w, ga, gb, m, v, name):
    R, C = w.shape
    tr = 256 if R % 256 == 0 else R
    c1 = 1.0 - ADAM_B1 ** ADAM_STEP
    c2 = 1.0 - ADAM_B2 ** ADAM_STEP

    def body(w_ref, ga_ref, gb_ref, m_ref, v_ref, g_out, d_out, m_out, v_out):
        g = ga_ref[...] + gb_ref[...]
        mn = ADAM_B1 * m_ref[...] + (1.0 - ADAM_B1) * g
        vn = ADAM_B2 * v_ref[...] + (1.0 - ADAM_B2) * (g * g)
        g_out[...] = g
        m_out[...] = mn
        v_out[...] = vn
        d_out[...] = -ADAM_LR * ((mn / c1) / (jnp.sqrt(vn / c2) + ADAM_EPS) + ADAM_WD * w_ref[...])

    blk = pl.BlockSpec((tr, C), lambda i: (i, 0))
    return pl.pallas_call(
        body, name=name, grid=(R // tr,),
        in_specs=[blk] * 5, out_specs=[blk] * 4,
        out_shape=[jax.ShapeDtypeStruct((R, C), F32)] * 4,
        compiler_params=_params(1, 40),
    )(w, ga, gb, m, v)


def _place():
    x, y, c = lax.axis_index("x"), lax.axis_index("y"), lax.axis_index("c")
    chips = [(1 - x, y), (x, 1 - y), (1 - x, 1 - y)]
    return x, y, c, chips


def _window(ref, axis, j, size):
    idx = [slice(None)] * len(ref.shape)
    idx[axis] = pl.ds(pl.multiple_of(j * size, 128 if axis == len(ref.shape) - 1 else 16), size)
    return ref.at[tuple(idx)]


def _hbm(a):
    return pltpu.with_memory_space_constraint(a, pltpu.HBM)


def _hbm_like(arrs):
    return [pltpu.HBM(a.shape, a.dtype) for a in arrs]


def gather_start(shards, axes, after, tag):
    n = len(shards)
    na = len(after)
    lands = []
    for s, ax in zip(shards, axes):
        shp = list(s.shape)
        shp[ax] *= NCHIP
        lands.append(lax.empty(tuple(shp), s.dtype))

    def body(*refs):
        ins, land = refs[:n], refs[n:2 * n]
        send, recv = refs[2 * n + na], refs[2 * n + na + 1]
        token = refs[-1]
        x, y, c, chips = _place()
        for a in range(n):
            for k, chip in enumerate(chips):
                pltpu.make_async_remote_copy(
                    src_ref=ins[a], dst_ref=_window(land[a], axes[a], 2 * x + y, ins[a].shape[axes[a]]),
                    send_sem=send.at[3 * a + k], recv_sem=recv.at[3 * a + k],
                    device_id=(chip[0], chip[1], c), device_id_type=MESH).start()
        token[...] = jnp.zeros_like(token)

    outs = pl.pallas_call(
        body, name=f"gather_start_{tag}",
        in_specs=[HBM_SPEC] * (2 * n) + [ANY] * na,
        out_specs=(SEM_SPEC, SEM_SPEC, *[HBM_SPEC] * (2 * n), VMEM_SPEC),
        out_shape=(pltpu.SemaphoreType.DMA((3 * n,)), pltpu.SemaphoreType.DMA((3 * n,)),
                   *_hbm_like(shards), *_hbm_like(lands), jax.ShapeDtypeStruct((8, 128), F32)),
        input_output_aliases={a: 2 + a for a in range(2 * n)},
        compiler_params=pltpu.CompilerParams(has_side_effects=DATAFLOW),
    )(*[_hbm(s) for s in shards], *[_hbm(b) for b in lands], *after)
    return (outs[0], outs[1], list(outs[2:2 + n]), list(outs[2 + n:2 + 2 * n]), list(axes)), outs[-1]


def gather_wait(groups, after, tag):
    sizes = [len(g[2]) for g in groups]
    total = sum(sizes)

    def body(*refs):
        x, y, c, chips = _place()
        stage, loc = refs[-1 - total:-1], refs[-1]
        pos = 2 * total
        off = 0
        mine = []
        for g, n in zip(groups, sizes):
            ins, land = refs[off:off + n], refs[total + off:total + off + n]
            send_ref, recv_ref = refs[pos], refs[pos + 1]
            axes = g[4]
            for a in range(n):
                fetch = pltpu.make_async_copy(ins[a], stage[off + a], loc.at[2 * (off + a)])
                fetch.start()
                put = pltpu.make_async_copy(
                    stage[off + a], _window(land[a], axes[a], 2 * x + y, ins[a].shape[axes[a]]),
                    loc.at[2 * (off + a) + 1])
                mine.append((fetch, put))
                for k, chip in enumerate(chips):
                    cp = pltpu.make_async_remote_copy(
                        src_ref=ins[a],
                        dst_ref=_window(land[a], axes[a], 2 * chip[0] + chip[1], ins[a].shape[axes[a]]),
                        send_sem=send_ref.at[3 * a + k], recv_sem=recv_ref.at[3 * a + k],
                        device_id=(chip[0], chip[1], c), device_id_type=MESH)
                    cp.wait_send()
                    cp.wait_recv()
            pos += 2
            off += n
        for fetch, put in mine:
            fetch.wait()
            put.start()
        for fetch, put in mine:
            put.wait()

    shards = [s for g in groups for s in g[2]]
    lands = [b for g in groups for b in g[3]]
    sems = [s for g in groups for s in (g[0], g[1])]
    outs = pl.pallas_call(
        body, name=f"gather_wait_{tag}",
        in_specs=[HBM_SPEC] * (2 * total) + [SEM_SPEC] * len(sems) + [ANY],
        out_specs=[HBM_SPEC] * (2 * total),
        out_shape=(*_hbm_like(shards), *_hbm_like(lands)),
        input_output_aliases={a: a for a in range(2 * total)},
        scratch_shapes=[pltpu.VMEM(s.shape, s.dtype) for s in shards] + [pltpu.SemaphoreType.DMA((2 * total,))],
        compiler_params=pltpu.CompilerParams(has_side_effects=DATAFLOW, vmem_limit_bytes=32 << 20),
    )(*shards, *lands, *sems, after)
    return list(outs[total:])


def scatter_start(grads, axes, sizes, lands, l, tag):
    n = len(grads)

    def body(*refs):
        ins, land = refs[:n], refs[n:2 * n]
        send, recv = refs[2 * n], refs[2 * n + 1]
        token = refs[2 * n + 2 + 2 * n]
        stage, loc = refs[-1 - n:-1], refs[-1]
        x, y, c, chips = _place()
        me = 2 * x + y
        fetches = [pltpu.make_async_copy(_window(ins[a], axes[a], me, sizes[a]), stage[a], loc.at[2 * a])
                   for a in range(n)]
        for cp in fetches:
            cp.start()
        for a in range(n):
            for k, chip in enumerate(chips):
                pltpu.make_async_remote_copy(
                    src_ref=_window(ins[a], axes[a], 2 * chip[0] + chip[1], sizes[a]), dst_ref=land[a].at[me, l],
                    send_sem=send.at[3 * a + k], recv_sem=recv.at[3 * a + k],
                    device_id=(chip[0], chip[1], c), device_id_type=MESH).start()
        puts = [pltpu.make_async_copy(stage[a], land[a].at[me, l], loc.at[2 * a + 1]) for a in range(n)]
        for fetch, put in zip(fetches, puts):
            fetch.wait()
            put.start()
        for put in puts:
            put.wait()
        token[...] = jnp.zeros_like(token)

    outs = pl.pallas_call(
        body, name=f"scatter_start_{tag}",
        in_specs=[HBM_SPEC] * (2 * n),
        out_specs=(SEM_SPEC, SEM_SPEC, *[HBM_SPEC] * (2 * n), VMEM_SPEC),
        out_shape=(pltpu.SemaphoreType.DMA((3 * n,)), pltpu.SemaphoreType.DMA((3 * n,)),
                   *_hbm_like(grads), *_hbm_like(lands), jax.ShapeDtypeStruct((8, 128), F32)),
        input_output_aliases={a: 2 + a for a in range(2 * n)},
        scratch_shapes=[pltpu.VMEM(b.shape[2:], b.dtype) for b in lands] + [pltpu.SemaphoreType.DMA((2 * n,))],
        compiler_params=pltpu.CompilerParams(has_side_effects=DATAFLOW, vmem_limit_bytes=32 << 20),
    )(*[_hbm(g) for g in grads], *[_hbm(b) for b in lands])
    group = (outs[0], outs[1], list(outs[2:2 + n]), list(axes), list(sizes), l)
    return group, list(outs[2 + n:2 + 2 * n]), outs[-1]


def scatter_wait(groups, lands, which):
    nl = len(lands)

    def body(*refs):
        land = refs[:nl]
        x, y, c, chips = _place()
        pos = nl
        for g, wh in zip(groups, which):
            n = len(g[2])
            ins = refs[pos:pos + n]
            send_ref, recv_ref = refs[pos + n], refs[pos + n + 1]
            axes, sizes, l = g[3], g[4], g[5]
            for a in range(n):
                for k, chip in enumerate(chips):
                    jp = 2 * chip[0] + chip[1]
                    cp = pltpu.make_async_remote_copy(
                        src_ref=_window(ins[a], axes[a], jp, sizes[a]), dst_ref=land[wh[a]].at[jp, l],
                        send_sem=send_ref.at[3 * a + k], recv_sem=recv_ref.at[3 * a + k],
                        device_id=(chip[0], chip[1], c), device_id_type=MESH)
                    cp.wait_send()
                    cp.wait_recv()
            pos += n + 2

    operands = list(lands)
    specs = [HBM_SPEC] * nl
    for g in groups:
        operands += list(g[2]) + [g[0], g[1]]
        specs += [HBM_SPEC] * len(g[2]) + [SEM_SPEC, SEM_SPEC]
    outs = pl.pallas_call(
        body, name="scatter_wait", in_specs=specs, out_specs=[HBM_SPEC] * nl, out_shape=tuple(_hbm_like(lands)),
        input_output_aliases={a: a for a in range(nl)},
        compiler_params=pltpu.CompilerParams(has_side_effects=DATAFLOW),
    )(*operands)
    return list(outs)


def sibling_swap(parts):
    n = len(parts)

    def body(*refs):
        ins, outs = refs[:n], refs[n:2 * n]
        send, recv = refs[2 * n:]
        x, y, c, _ = _place()
        cps = [pltpu.make_async_remote_copy(src_ref=ins[a], dst_ref=outs[a], send_sem=send.at[a],
                                            recv_sem=recv.at[a], device_id=(x, y, 1 - c), device_id_type=MESH)
               for a in range(n)]
        for cp in cps:
            cp.start()
        for cp in cps:
            cp.wait_recv()
        for cp in cps:
            cp.wait_send()

    return pl.pallas_call(
        body, name="sibling_swap", in_specs=[ANY] * n, out_specs=[ANY] * n,
        out_shape=[jax.ShapeDtypeStruct(p.shape, p.dtype) for p in parts],
        scratch_shapes=[pltpu.SemaphoreType.DMA((n,)), pltpu.SemaphoreType.DMA((n,))],
    )(*parts)


def small_allreduce(p):
    R, C = p.shape
    ndev = 8

    def body(p_ref, o_ref, buf, send, recv):
        x, y, c, _ = _place()
        me = 4 * x + 2 * y + c
        buf[me] = p_ref[...]

        def peer(d):
            px = 1 - x if d & 4 else x
            py = 1 - y if d & 2 else y
            pc = 1 - c if d & 1 else c
            return px, py, pc

        def copy(d, slot):
            return pltpu.make_async_remote_copy(src_ref=p_ref, dst_ref=buf.at[slot], send_sem=send.at[d - 1],
                                                recv_sem=recv.at[d - 1], device_id=peer(d), device_id_type=MESH)

        sends = [copy(d, me) for d in range(1, ndev)]
        for cp in sends:
            cp.start()
        for d in range(1, ndev):
            px, py, pc = peer(d)
            copy(d, 4 * px + 2 * py + pc).wait_recv()
        for cp in sends:
            cp.wait_send()
        acc = buf[0]
        for k in range(1, ndev):
            acc = acc + buf[k]
        o_ref[...] = acc

    return pl.pallas_call(
        body, name="small_allreduce", in_specs=[VMEM_SPEC], out_specs=VMEM_SPEC,
        out_shape=jax.ShapeDtypeStruct((R, C), F32),
        scratch_shapes=[pltpu.VMEM((ndev, R, C), F32), pltpu.SemaphoreType.DMA((ndev - 1,)),
                        pltpu.SemaphoreType.DMA((ndev - 1,))],
        compiler_params=pltpu.CompilerParams(vmem_limit_bytes=32 << 20),
    )(p)


def kernel(x, norm1_g, w_in, conv_w, conv_b, conv_ln_g, conv_ln_b, ret_gn_g, w_out, norm2_g, w_gate, w_up, w_down, final_g, loss_target, m_norm1_g, m_w_in, m_conv_w, m_conv_b, m_conv_ln_g, m_conv_ln_b, m_ret_gn_g, m_w_out, m_norm2_g, m_w_gate, m_w_up, m_w_down, m_final_g, v_norm1_g, v_w_in, v_conv_w, v_conv_b, v_conv_ln_g, v_conv_ln_b, v_ret_gn_g, v_w_out, v_norm2_g, v_w_gate, v_w_up, v_w_down, v_final_g):
    S = x.shape[1]
    xs = x.reshape(S, D)
    tgt = loss_target.reshape(S, D)
    fsh = FF // NCHIP

    def shards_of(l):
        return [w_in[l].astype(BF), w_out[l].astype(BF), w_gate[l].T.astype(BF), w_up[l].T.astype(BF),
                w_down[l].astype(BF), conv_w[l]]

    gather_axes = [1, 0, 0, 0, 0, 1]
    shard_cache = [shards_of(l) for l in range(L)]
    tables = _ret_tables(S)
    row = lambda a, l: a[l].reshape(1, -1)
    groups = {}
    weights = [dict() for _ in range(L)]

    def begin(l, which, after):
        group, token = gather_start([shard_cache[l][i] for i in which], [gather_axes[i] for i in which],
                                    after, f"{l}_{which[0]}")
        groups[(l, which[0])] = (group, which)
        return token

    def finish(l, firsts, after, tag):
        gs = [groups[(l, f)] for f in firsts]
        outs = gather_wait([g for g, _ in gs], after, f"{l}_{tag}")
        k = 0
        for _, which in gs:
            for i in which:
                weights[l][i] = outs[k]
                k += 1

    def host(stage, after):
        tokens = []
        for gl, which in stage:
            tokens.append(begin(gl, which, [after] + tokens))
        return tokens

    def hosted(l):
        if l == 0:
            return {"in": [(0, [2])], "conv": [(0, [3])], "ret": [(0, [4])], "out": [(1, [0])],
                    "mlp": [(1, [1, 5]), (1, [2])]}
        nxt = {"ret": [(l + 1, [0])], "out": [(l + 1, [1, 5])], "mlp": [(l + 1, [2])]} if l + 1 < L else {}
        return {"in": [(l, [3])], "conv": [(l, [4])], **nxt}

    first = begin(0, [0], [])
    after = begin(0, [1, 5], [first])
    saved = []
    xc = xs
    for l in range(L):
        sched = hosted(l)
        finish(l, [0, 1], after, "a")
        win, wout, cw = weights[l][0], weights[l][1], weights[l][5]
        proj = in_proj(xc, row(norm1_g, l), win, l, host(sched.get("in", []), win))
        u1, u = conv_fwd(proj, cw, row(conv_b, l), row(conv_ln_g, l), row(conv_ln_b, l), l,
                         host(sched.get("conv", []), proj))
        r_raw, states = ret_fwd(proj, tables, l, host(sched.get("ret", []), u))
        x2, mixed = out_proj(u, r_raw, proj, row(ret_gn_g, l), wout, xc, l, host(sched.get("out", []), r_raw))
        finish(l, [2, 3, 4], x2, "b")
        wgt, wut, wd = weights[l][2], weights[l][3], weights[l][4]
        x3, gs, us, act = mlp_fwd(x2, row(norm2_g, l), wgt, wut, wd, l, host(sched.get("mlp", []), wd))
        saved.append((xc, proj, u1, r_raw, states, mixed, x2, gs, us, act))
        xc = x3
        after = x3

    dx, loss_acc, d_final = final_loss(xc, final_g.reshape(1, D), tgt)
    loss = lax.psum(loss_acc[0, 0] * (0.5 / D), ("x", "y", "c"))

    scatter_axes = [1, 0, 0, 0, 0]
    scatter_sizes = [INW // NCHIP, D // NCHIP, fsh, fsh, fsh]
    lands = [lax.empty((NCHIP, L, D, INW // NCHIP), BF), lax.empty((NCHIP, L, D // NCHIP, D), BF),
             lax.empty((NCHIP, L, fsh, D), BF), lax.empty((NCHIP, L, fsh, D), BF), lax.empty((NCHIP, L, fsh, D), BF)]
    sent, sent_which = [], []

    def send_grad(g, a, l):
        group, new_land, token = scatter_start([g], [scatter_axes[a]], [scatter_sizes[a]], [lands[a]], l, f"{l}_{a}")
        lands[a] = new_land[0]
        sent.append(group)
        sent_which.append([a])
        return [token]

    small = [None] * L
    dep = []
    for l in reversed(range(L)):
        xin, proj, u1, r_raw, states, mixed, x2, gs, us, act = saved[l]
        win, wout, wgt, wut, wd, cw = (weights[l][i] for i in range(6))
        dx2, dgs, dus, h2, d_n2 = mlp_bwd(dx, x2, row(norm2_g, l), gs, us, wgt, wut, wd, l, dep)
        g_wd = wgrad(act, dx, l, "wgrad_down")
        g_wgt = wgrad(dgs, h2, l, "wgrad_gate", send_grad(g_wd, 4, l))
        g_wut = wgrad(dus, h2, l, "wgrad_up", send_grad(g_wgt, 2, l))
        dgate, dr, du1, sums = out_proj_bwd(dx2, wout, r_raw, proj, row(ret_gn_g, l), u1,
                                            row(conv_ln_g, l), row(conv_ln_b, l), l, send_grad(g_wut, 3, l))
        g_wout = wgrad(mixed, dx2, l, "wgrad_out")
        dab, dwb = conv_bwd(du1, proj, cw, l, send_grad(g_wout, 1, l))
        dq, dk, dv = ret_bwd(dr, proj, states, tables, l)
        dproj = [dab, dq, dk, dv, dgate]
        dx, h1, d_n1 = in_proj_bwd(dproj, win, xin, row(norm1_g, l), dx2, l)
        g_win = wgrad_in(h1, dproj, l)
        dep = send_grad(g_win, 0, l)
        small[l] = jnp.concatenate([dwb, sums, d_n1.reshape(2, CW), d_n2.reshape(2, CW)], axis=0)
    grad_x = dx.reshape(1, S, D)

    per = CK + 1 + 8 + 4
    packed = jnp.concatenate(small + [d_final.reshape(2, CW), jnp.zeros((6, CW), F32)], axis=0)
    tot = small_allreduce(packed)
    lay = tot[:L * per].reshape(L, per, CW)
    g_conv_w_full = lay[:, 0:CK, :]
    j = 2 * lax.axis_index("x") + lax.axis_index("y")
    g_conv_w = lax.dynamic_slice_in_dim(g_conv_w_full, j * (CW // NCHIP), CW // NCHIP, axis=2)
    g_small = {
        "conv_b": lay[:, CK, :], "ret_gn_g": lay[:, CK + 1, :], "conv_ln_g": lay[:, CK + 2, :],
        "conv_ln_b": lay[:, CK + 3, :], "norm1_g": lay[:, CK + 9:CK + 11, :].reshape(L, D),
        "norm2_g": lay[:, CK + 11:CK + 13, :].reshape(L, D), "final_g": tot[L * per:L * per + 2].reshape(D),
    }

    recv = scatter_wait(sent, lands, sent_which)
    shard_shapes = [(L * D, INW // NCHIP), (L * D // NCHIP, D), (L * fsh, D), (L * fsh, D), (L * fsh, D)]
    names = ["w_in", "w_out", "w_gate", "w_up", "w_down"]
    parts = [sum_slots(r.reshape((NCHIP,) + shp), f"sum_{nm}") for r, shp, nm in zip(recv, shard_shapes, names)]
    theirs = sibling_swap(parts)

    def unT(a):
        return jnp.swapaxes(a.reshape(L, fsh, D), 1, 2).reshape(L * D, fsh)

    big = {}
    wmv = {"w_in": (w_in, m_w_in, v_w_in), "w_out": (w_out, m_w_out, v_w_out),
           "w_gate": (w_gate, m_w_gate, v_w_gate), "w_up": (w_up, m_w_up, v_w_up),
           "w_down": (w_down, m_w_down, v_w_down)}
    for nm, mine, other in zip(names, parts, theirs):
        w, m, v = wmv[nm]
        if nm in ("w_gate", "w_up"):
            mine, other = unT(mine), unT(other)
        shp2 = (w.shape[0] * w.shape[1], w.shape[2])
        outs = adamw(w.reshape(shp2), mine, other, m.reshape(shp2), v.reshape(shp2), f"adamw_{nm}")
        big[nm] = [o.reshape(w.shape) for o in outs]

    cshape = (L * CK, CW // NCHIP)
    zc = jnp.zeros(cshape, F32)
    big["conv_w"] = [o.reshape(conv_w.shape) for o in adamw(
        conv_w.reshape(cshape), g_conv_w.reshape(cshape), zc, m_conv_w.reshape(cshape),
        v_conv_w.reshape(cshape), "adamw_conv_w")]
    vec_names = ["norm1_g", "conv_b", "conv_ln_g", "conv_ln_b", "ret_gn_g", "norm2_g", "final_g"]
    vec_w = {"norm1_g": (norm1_g, m_norm1_g, v_norm1_g), "conv_b": (conv_b, m_conv_b, v_conv_b),
             "conv_ln_g": (conv_ln_g, m_conv_ln_g, v_conv_ln_g), "conv_ln_b": (conv_ln_b, m_conv_ln_b, v_conv_ln_b),
             "ret_gn_g": (ret_gn_g, m_ret_gn_g, v_ret_gn_g), "norm2_g": (norm2_g, m_norm2_g, v_norm2_g),
             "final_g": (final_g, m_final_g, v_final_g)}
    cat = lambda arrs: jnp.concatenate([a.reshape(-1, CW) for a in arrs], axis=0)
    vw = cat([vec_w[nm][0] for nm in vec_names])
    vm = cat([vec_w[nm][1] for nm in vec_names])
    vv = cat([vec_w[nm][2] for nm in vec_names])
    vg = cat([g_small[nm] for nm in vec_names])
    vouts = adamw(vw, vg, jnp.zeros_like(vg), vm, vv, "adamw_vectors")
    off = 0
    for nm in vec_names:
        w = vec_w[nm][0]
        nrow = w.size // CW
        big[nm] = [o[off:off + nrow].reshape(w.shape) for o in vouts]
        off += nrow

    order = ["norm1_g", "w_in", "conv_w", "conv_b", "conv_ln_g", "conv_ln_b", "ret_gn_g", "w_out", "norm2_g",
             "w_gate", "w_up", "w_down", "final_g"]
    return (loss, grad_x, *[big[nm][0] for nm in order], *[big[nm][1] for nm in order],
            *[big[nm][2] for nm in order], *[big[nm][3] for nm in order])
```

```python
import math

import jax
import jax.numpy as jnp
from jax import lax
from jax.experimental import pallas as pl
from jax.experimental.pallas import tpu as pltpu

D = 1024
L = 4
CW = 512
RW = 512
NH = 4
HD = 128
CK = 31
CHUNK = 64
INW = 3072
FF = 2816
NCHIP = 4
EPS = 1e-6
ROPE_BASE = 10000.0
SEQ_TILE = 512
WGRAD_ROWS = 1024
MLP_ROWS = 256
MLP_COLS = 1408
HALO = 32
CONV_ROWS = 32
CONV_COLS = 256

ADAM_LR = 0.001
ADAM_B1 = 0.9
ADAM_B2 = 0.999
ADAM_EPS = 1e-08
ADAM_WD = 0.01
ADAM_STEP = 10

BF = jnp.bfloat16
F32 = jnp.float32
MESH = pl.DeviceIdType.MESH
ANY = pl.BlockSpec(memory_space=pl.ANY)
VMEM_SPEC = pl.BlockSpec(memory_space=pltpu.VMEM)
HBM_SPEC = pl.BlockSpec(memory_space=pltpu.HBM)
SEM_SPEC = pl.BlockSpec(memory_space=pltpu.SEMAPHORE)
DATAFLOW = pltpu.SideEffectType.DATAFLOW_SIDE_EFFECTING


def _params(n_grid, vmem_mb):
    return pltpu.CompilerParams(dimension_semantics=("arbitrary",) * n_grid,
                                vmem_limit_bytes=vmem_mb << 20)


def _ordered_call(body, deps, *, in_specs, **kw):
    n, nd = len(in_specs), len(deps)

    def with_deps(*refs):
        body(*refs[:n], *refs[n + nd:])

    return pl.pallas_call(with_deps, in_specs=list(in_specs) + [ANY] * nd, **kw)


def _dot(a, b):
    return jnp.dot(a, b, preferred_element_type=F32)


def _dot_nt(a, b):
    return lax.dot_general(a, b, (((1,), (1,)), ((), ())), preferred_element_type=F32)


def _dot_tn(a, b):
    return lax.dot_general(a, b, (((0,), (0,)), ((), ())), preferred_element_type=F32)


def _sigmoid(x):
    return 0.5 * jnp.tanh(0.5 * x) + 0.5


def _mean(x):
    return jnp.mean(x, axis=-1, keepdims=True)


def _tap_groups(first, count):
    groups = []
    for phase in range(8):
        taps = [(t, first + t - phase) for t in range(count) if (first + t) % 8 == phase]
        if taps:
            lo, hi = min(q for _, q in taps), max(q for _, q in taps)
            groups.append((lo + phase, hi - lo + CONV_ROWS, [(t, q - lo) for t, q in taps]))
    return groups


def _rot(t, cs, sn):
    return t * cs + pltpu.roll(t, HD // 2, 1) * sn


def _rot_t(dy, cs, sn):
    return dy * cs + pltpu.roll(dy * sn, HD // 2, 1)


def _rms_bwd(x, g, dh, dx_in):
    r = lax.rsqrt(_mean(x * x) + EPS)
    xh = x * r
    dxh = dh * g
    dx = dx_in + r * (dxh - xh * _mean(dxh * xh))
    return dx, jnp.sum(dh * xh, axis=0, keepdims=True), (xh * g).astype(BF)


def in_proj(x, g, win, l, deps=()):
    S = x.shape[0]
    tm = min(S, SEQ_TILE)

    def body(x_ref, g_ref, w_ref, o_ref):
        xv = x_ref[...]
        h = (xv * lax.rsqrt(_mean(xv * xv) + EPS) * g_ref[...]).astype(BF)
        o_ref[...] = _dot(h, w_ref[...])

    return _ordered_call(
        body, deps, name=f"in_proj_{l}", grid=(S // tm,),
        in_specs=[pl.BlockSpec((tm, D), lambda i: (i, 0)),
                  pl.BlockSpec((1, D), lambda i: (0, 0)),
                  pl.BlockSpec((D, INW), lambda i: (0, 0))],
        out_specs=pl.BlockSpec((tm, INW), lambda i: (i, 0)),
        out_shape=jax.ShapeDtypeStruct((S, INW), F32),
        compiler_params=_params(1, 48),
    )(x, g, win, *deps)


def _conv_fwd_fill(tc, a_ref, b_ref, buf):
    i = pl.program_id(0)

    @pl.when(i == 0)
    def _():
        buf[0:HALO, :] = jnp.zeros((HALO, CW), F32)

    @pl.when(i > 0)
    def _():
        buf[0:HALO, :] = buf[tc:tc + HALO, :]

    buf[HALO:HALO + tc, :] = a_ref[...] * _sigmoid(b_ref[...])


def _conv_fwd_rows(r0, groups, w_ref, cb_ref, lg_ref, lb_ref, u1_ref, u_ref, buf, win):
    for c0 in range(0, CW, CONV_COLS):
        cols = slice(c0, c0 + CONV_COLS)
        acc = jnp.broadcast_to(cb_ref[:, cols], (CONV_ROWS, CONV_COLS))
        for start, length, taps in groups:
            win[0:length, :] = buf[r0 + start:r0 + start + length, cols]
            for k, at in taps:
                acc = acc + w_ref[k:k + 1, cols] * win[at:at + CONV_ROWS, :]
        u1_ref[r0:r0 + CONV_ROWS, cols] = acc
    acc = u1_ref[r0:r0 + CONV_ROWS, :]
    d = acc - _mean(acc)
    u2 = d * lax.rsqrt(_mean(d * d) + EPS) * lg_ref[...] + lb_ref[...]
    u_ref[r0:r0 + CONV_ROWS, :] = (u2 * _sigmoid(u2)).astype(BF)


def _ret_tables(S):
    half = HD // 2
    pos = jnp.arange(S, dtype=F32)
    freqs = ROPE_BASE ** (-jnp.arange(half, dtype=F32) / half)
    ang = pos[:, None] * freqs[None, :]
    cos, sin = jnp.cos(ang), jnp.sin(ang)
    cosf = jnp.concatenate([cos, cos], axis=-1)
    sinf = jnp.concatenate([-sin, sin], axis=-1)
    log_g = jnp.log(1.0 - 2.0 ** (-5.0 - jnp.arange(NH, dtype=F32)))
    idx = jnp.arange(CHUNK, dtype=F32)
    dmat = jnp.exp(log_g[:, None, None] * jnp.abs(idx[:, None] - idx[None, :]))
    qdec = jnp.broadcast_to(jnp.exp(log_g[:, None] * (idx + 1.0))[:, :, None], (NH, CHUNK, HD))
    kdec = jnp.broadcast_to(jnp.exp(log_g[:, None] * (CHUNK - 1 - idx))[:, :, None], (NH, CHUNK, HD))
    cdec = jnp.broadcast_to(jnp.exp(log_g * CHUNK)[:, None, None], (NH, HD, HD))
    return cosf, sinf, dmat, qdec, kdec, cdec


def _ret_specs(tr, tmap):
    q0 = (2 * CW) // RW
    return [pl.BlockSpec((tr, RW), lambda t: (tmap(t), q0)),
            pl.BlockSpec((tr, RW), lambda t: (tmap(t), q0 + 1)),
            pl.BlockSpec((tr, RW), lambda t: (tmap(t), q0 + 2)),
            pl.BlockSpec((tr, HD), lambda t: (tmap(t), 0)),
            pl.BlockSpec((tr, HD), lambda t: (tmap(t), 0)),
            pl.BlockSpec((NH, CHUNK, CHUNK), lambda t: (0, 0, 0)),
            pl.BlockSpec((NH, CHUNK, HD), lambda t: (0, 0, 0)),
            pl.BlockSpec((NH, CHUNK, HD), lambda t: (0, 0, 0)),
            pl.BlockSpec((NH, HD, HD), lambda t: (0, 0, 0))]


def _ret_fwd_chunk(c, q_ref, k_ref, v_ref, cos_ref, sin_ref, dm_ref, qd_ref, kd_ref, cd_ref, r_ref, st_ref, st):
    scale = HD ** -0.5
    rows = slice(c * CHUNK, (c + 1) * CHUNK)
    cs, sn = cos_ref[rows, :], sin_ref[rows, :]
    for h in range(NH):
        cols = slice(h * HD, (h + 1) * HD)
        qr = _rot(q_ref[rows, cols], cs, sn)
        kr = _rot(k_ref[rows, cols], cs, sn) * scale
        vb = v_ref[rows, cols].astype(BF)
        s = st[h]
        sb = s.astype(BF)
        st_ref[h, c] = sb
        sc = _dot_nt(qr.astype(BF), kr.astype(BF)) * dm_ref[h]
        r_ref[rows, cols] = _dot(sc.astype(BF), vb) + _dot((qr * qd_ref[h]).astype(BF), sb)
        st[h] = cd_ref[h] * s + _dot_tn((kr * kd_ref[h]).astype(BF), vb)


def mix_fwd(proj, cw, cb, lg, lb, tables, l, deps=()):
    S = proj.shape[0]
    tt = min(S, SEQ_TILE)
    cpb = tt // CHUNK
    groups = _tap_groups(HALO - (CK - 1), CK)

    def body(a_ref, b_ref, w_ref, cb_ref, lg_ref, lb_ref, q_ref, k_ref, v_ref, cos_ref, sin_ref,
             dm_ref, qd_ref, kd_ref, cd_ref, u1_ref, u_ref, r_ref, st_ref, buf, win, st):
        @pl.when(pl.program_id(0) == 0)
        def _():
            st[...] = jnp.zeros((NH, HD, HD), F32)

        _conv_fwd_fill(tt, a_ref, b_ref, buf)
        for c in range(cpb):
            _ret_fwd_chunk(c, q_ref, k_ref, v_ref, cos_ref, sin_ref, dm_ref, qd_ref, kd_ref, cd_ref,
                           r_ref, st_ref, st)
            for r0 in range(c * CHUNK, (c + 1) * CHUNK, CONV_ROWS):
                _conv_fwd_rows(r0, groups, w_ref, cb_ref, lg_ref, lb_ref, u1_ref, u_ref, buf, win)

    vec = pl.BlockSpec((1, CW), lambda t: (0, 0))
    half = pl.BlockSpec((tt, CW), lambda t: (t, 0))
    return _ordered_call(
        body, deps, name=f"mix_fwd_{l}", grid=(S // tt,),
        in_specs=[half, pl.BlockSpec((tt, CW), lambda t: (t, 1)),
                  pl.BlockSpec((CK, CW), lambda t: (0, 0)), vec, vec, vec] + _ret_specs(tt, lambda t: t),
        out_specs=[half, half, half, pl.BlockSpec((NH, cpb, HD, HD), lambda t: (0, t, 0, 0))],
        out_shape=[jax.ShapeDtypeStruct((S, CW), F32), jax.ShapeDtypeStruct((S, CW), BF),
                   jax.ShapeDtypeStruct((S, RW), F32), jax.ShapeDtypeStruct((NH, S // CHUNK, HD, HD), BF)],
        scratch_shapes=[pltpu.VMEM((tt + HALO, CW), F32), pltpu.VMEM((HALO + CONV_ROWS, CONV_COLS), F32),
                        pltpu.VMEM((NH, HD, HD), F32)],
        compiler_params=_params(1, 40),
    )(proj, proj, cw, cb, lg, lb, proj, proj, proj, *tables, *deps)


def out_proj(u, r_raw, proj, gn, wout, x, l, deps=()):
    S = x.shape[0]
    tm = min(S, SEQ_TILE)
    gate_blk = (2 * CW + 3 * RW) // RW

    def body(u_ref, r_ref, gate_ref, gn_ref, w_ref, x_ref, x2_ref, mix_ref):
        mix_ref[:, 0:CW] = u_ref[...]
        gt = gate_ref[...]
        sil = gt * _sigmoid(gt) * gn_ref[...]
        for h in range(NH):
            cols = slice(h * HD, (h + 1) * HD)
            rh = r_ref[:, cols]
            d = rh - _mean(rh)
            rn = d * lax.rsqrt(_mean(d * d) + EPS)
            mix_ref[:, CW + h * HD:CW + (h + 1) * HD] = (rn * sil[:, cols]).astype(BF)
        x2_ref[...] = x_ref[...] + _dot(mix_ref[...], w_ref[...])

    return _ordered_call(
        body, deps, name=f"out_proj_{l}", grid=(S // tm,),
        in_specs=[pl.BlockSpec((tm, CW), lambda i: (i, 0)),
                  pl.BlockSpec((tm, RW), lambda i: (i, 0)),
                  pl.BlockSpec((tm, RW), lambda i: (i, gate_blk)),
                  pl.BlockSpec((1, RW), lambda i: (0, 0)),
                  pl.BlockSpec((D, D), lambda i: (0, 0)),
                  pl.BlockSpec((tm, D), lambda i: (i, 0))],
        out_specs=[pl.BlockSpec((tm, D), lambda i: (i, 0)),
                   pl.BlockSpec((tm, D), lambda i: (i, 0))],
        out_shape=[jax.ShapeDtypeStruct((S, D), F32), jax.ShapeDtypeStruct((S, D), BF)],
        compiler_params=_params(1, 40),
    )(u, r_raw, proj, gn, wout, x, *deps)


def mlp_fwd(x2, g2, wgt, wut, wd, l, deps=()):
    S = x2.shape[0]
    tm, tf = min(S, MLP_ROWS), MLP_COLS

    def body(x_ref, g_ref, wg_ref, wu_ref, wd_ref, o_ref, gs_ref, us_ref, a_ref):
        xv = x_ref[...]
        h = (xv * lax.rsqrt(_mean(xv * xv) + EPS) * g_ref[...]).astype(BF)
        for c0 in range(0, FF, tf):
            gv = _dot_nt(h, wg_ref[c0:c0 + tf, :])
            uv = _dot_nt(h, wu_ref[c0:c0 + tf, :])
            gs_ref[:, c0:c0 + tf] = gv.astype(BF)
            us_ref[:, c0:c0 + tf] = uv.astype(BF)
            a_ref[:, c0:c0 + tf] = (gv * _sigmoid(gv) * uv).astype(BF)
        o_ref[...] = xv + _dot(a_ref[...], wd_ref[...])

    wspec = pl.BlockSpec((FF, D), lambda i: (0, 0), pipeline_mode=pl.Buffered(1))
    row = pl.BlockSpec((tm, D), lambda i: (i, 0))
    wide = pl.BlockSpec((tm, FF), lambda i: (i, 0))
    return _ordered_call(
        body, deps, name=f"mlp_fwd_{l}", grid=(S // tm,),
        in_specs=[row, pl.BlockSpec((1, D), lambda i: (0, 0)), wspec, wspec, wspec],
        out_specs=[row, wide, wide, wide],
        out_shape=[jax.ShapeDtypeStruct((S, D), F32)] + [jax.ShapeDtypeStruct((S, FF), BF)] * 3,
        compiler_params=_params(1, 56),
    )(x2, g2, wgt, wut, wd, *deps)


def final_loss(x, gf, tgt):
    S = x.shape[0]
    tm = min(S, SEQ_TILE)

    def body(x_ref, g_ref, t_ref, dx_ref, loss_ref, dg_ref):
        @pl.when(pl.program_id(0) == 0)
        def _():
            loss_ref[...] = jnp.zeros((8, 128), F32)
            dg_ref[...] = jnp.zeros((1, D), F32)

        xv = x_ref[...]
        r = lax.rsqrt(_mean(xv * xv) + EPS)
        xh = xv * r
        diff = xh * g_ref[...] - t_ref[...]
        loss_ref[...] += jnp.sum(jnp.sum(diff * diff, axis=-1, keepdims=True), axis=0, keepdims=True)
        dy = diff * (1.0 / D)
        dg_ref[...] += jnp.sum(dy * xh, axis=0, keepdims=True)
        dxh = dy * g_ref[...]
        dx_ref[...] = r * (dxh - xh * _mean(dxh * xh))

    return pl.pallas_call(
        body, name="final_loss", grid=(S // tm,),
        in_specs=[pl.BlockSpec((tm, D), lambda i: (i, 0)),
                  pl.BlockSpec((1, D), lambda i: (0, 0)),
                  pl.BlockSpec((tm, D), lambda i: (i, 0))],
        out_specs=[pl.BlockSpec((tm, D), lambda i: (i, 0)),
                   pl.BlockSpec((8, 128), lambda i: (0, 0)),
                   pl.BlockSpec((1, D), lambda i: (0, 0))],
        out_shape=[jax.ShapeDtypeStruct((S, D), F32), jax.ShapeDtypeStruct((8, 128), F32),
                   jax.ShapeDtypeStruct((1, D), F32)],
        compiler_params=_params(1, 40),
    )(x, gf, tgt)


def mlp_bwd(dx3, x2, g2, gs, us, wgt, wut, wd, l, deps=()):
    S = x2.shape[0]
    tm, tf = min(S, MLP_ROWS), MLP_COLS

    def body(dx_ref, x_ref, g_ref, gs_ref, us_ref, wg_ref, wu_ref, wd_ref,
             dx2_ref, dg_ref, du_ref, h_ref, dgain_ref):
        @pl.when(pl.program_id(0) == 0)
        def _():
            dgain_ref[...] = jnp.zeros((1, D), F32)

        dxv = dx_ref[...]
        dxb = dxv.astype(BF)
        for c0 in range(0, FF, tf):
            da = _dot_nt(dxb, wd_ref[c0:c0 + tf, :])
            gv = gs_ref[:, c0:c0 + tf].astype(F32)
            uv = us_ref[:, c0:c0 + tf].astype(F32)
            sg = _sigmoid(gv)
            dg_ref[:, c0:c0 + tf] = (da * uv * (sg * (1.0 + gv * (1.0 - sg)))).astype(BF)
            du_ref[:, c0:c0 + tf] = (da * (gv * sg)).astype(BF)
        dh = _dot(dg_ref[...], wg_ref[...]) + _dot(du_ref[...], wu_ref[...])
        dx2, dgain, hb = _rms_bwd(x_ref[...], g_ref[...], dh, dxv)
        dx2_ref[...] = dx2
        dgain_ref[...] += dgain
        h_ref[...] = hb

    wspec = pl.BlockSpec((FF, D), lambda i: (0, 0), pipeline_mode=pl.Buffered(1))
    row = pl.BlockSpec((tm, D), lambda i: (i, 0))
    wide = pl.BlockSpec((tm, FF), lambda i: (i, 0))
    vec = pl.BlockSpec((1, D), lambda i: (0, 0))
    return _ordered_call(
        body, deps, name=f"mlp_bwd_{l}", grid=(S // tm,),
        in_specs=[row, row, vec, wide, wide, wspec, wspec, wspec],
        out_specs=[row, wide, wide, row, vec],
        out_shape=[jax.ShapeDtypeStruct((S, D), F32), jax.ShapeDtypeStruct((S, FF), BF),
                   jax.ShapeDtypeStruct((S, FF), BF), jax.ShapeDtypeStruct((S, D), BF),
                   jax.ShapeDtypeStruct((1, D), F32)],
        compiler_params=_params(1, 56),
    )(dx3, x2, g2, gs, us, wgt, wut, wd, *deps)


def wgrad(a, b, l, name, deps=()):
    S, K = a.shape
    N = b.shape[1]
    tk = 1408 if K == FF else min(K, 1024)
    tn = min(N, 1024)
    ts = min(S, WGRAD_ROWS)
    ns = S // ts

    def body(a_ref, b_ref, o_ref, acc):
        s = pl.program_id(2)

        @pl.when(s == 0)
        def _():
            acc[...] = jnp.zeros((tk, tn), F32)

        acc[...] += _dot_tn(a_ref[...], b_ref[...].astype(BF))

        @pl.when(s == ns - 1)
        def _():
            o_ref[...] = acc[...].astype(BF)

    return _ordered_call(
        body, deps, name=f"{name}_{l}", grid=(K // tk, N // tn, ns),
        in_specs=[pl.BlockSpec((ts, tk), lambda i, j, s: (s, i)),
                  pl.BlockSpec((ts, tn), lambda i, j, s: (s, j))],
        out_specs=pl.BlockSpec((tk, tn), lambda i, j, s: (i, j)),
        out_shape=jax.ShapeDtypeStruct((K, N), BF),
        scratch_shapes=[pltpu.VMEM((tk, tn), F32)],
        compiler_params=_params(3, 48),
    )(a, b, *deps)


def out_proj_bwd(dx2, wout, r_raw, proj, gn, u1, lg, lb, l, deps=()):
    S = dx2.shape[0]
    tm = min(S, SEQ_TILE)
    gate_blk = (2 * CW + 3 * RW) // RW

    def body(dx_ref, w_ref, r_ref, gate_ref, gn_ref, u1_ref, lg_ref, lb_ref,
             dgate_ref, dr_ref, du1_ref, sums_ref):
        @pl.when(pl.program_id(0) == 0)
        def _():
            sums_ref[...] = jnp.zeros((8, CW), F32)

        dmix = _dot_nt(dx_ref[...].astype(BF), w_ref[...])
        gt = gate_ref[...]
        sg = _sigmoid(gt)
        sil = gt * sg
        dsil = sg * (1.0 + gt * (1.0 - sg))
        for h in range(NH):
            cols = slice(h * HD, (h + 1) * HD)
            rh = r_ref[:, cols]
            d = rh - _mean(rh)
            rs = lax.rsqrt(_mean(d * d) + EPS)
            rn = d * rs
            drr = dmix[:, CW + h * HD:CW + (h + 1) * HD]
            gnh = gn_ref[:, cols]
            sums_ref[0:1, cols] += jnp.sum(drr * rn * sil[:, cols], axis=0, keepdims=True)
            dgate_ref[:, cols] = (drr * rn * gnh * dsil[:, cols]).astype(BF)
            drn = drr * gnh * sil[:, cols]
            dr_ref[:, cols] = (rs * (drn - _mean(drn) - rn * _mean(drn * rn))).astype(BF)
        du = dmix[:, 0:CW]
        u1 = u1_ref[...]
        d = u1 - _mean(u1)
        rs = lax.rsqrt(_mean(d * d) + EPS)
        xh = d * rs
        u2 = xh * lg_ref[...] + lb_ref[...]
        sg2 = _sigmoid(u2)
        du2 = du * (sg2 * (1.0 + u2 * (1.0 - sg2)))
        sums_ref[1:2, :] += jnp.sum(du2 * xh, axis=0, keepdims=True)
        sums_ref[2:3, :] += jnp.sum(du2, axis=0, keepdims=True)
        dxh = du2 * lg_ref[...]
        du1_ref[...] = rs * (dxh - _mean(dxh) - xh * _mean(dxh * xh))

    vec = pl.BlockSpec((1, CW), lambda i: (0, 0))
    half = pl.BlockSpec((tm, CW), lambda i: (i, 0))
    return _ordered_call(
        body, deps, name=f"out_proj_bwd_{l}", grid=(S // tm,),
        in_specs=[pl.BlockSpec((tm, D), lambda i: (i, 0)),
                  pl.BlockSpec((D, D), lambda i: (0, 0)),
                  half, pl.BlockSpec((tm, RW), lambda i: (i, gate_blk)), vec, half, vec, vec],
        out_specs=[half, half, half, pl.BlockSpec((8, CW), lambda i: (0, 0))],
        out_shape=[jax.ShapeDtypeStruct((S, RW), BF), jax.ShapeDtypeStruct((S, RW), BF),
                   jax.ShapeDtypeStruct((S, CW), F32), jax.ShapeDtypeStruct((8, CW), F32)],
        compiler_params=_params(1, 40),
    )(dx2, wout, r_raw, proj, gn, u1, lg, lb, *deps)


def _conv_bwd_fill(tc, du1_ref, dwb_ref, buf, pacc):
    i = pl.program_id(0)

    @pl.when(i == 0)
    def _():
        buf[tc:tc + HALO, :] = jnp.zeros((HALO, CW), F32)
        dwb_ref[...] = jnp.zeros((CK + 1, CW), F32)
        pacc[...] = jnp.zeros((CK, 8, CW), F32)

    @pl.when(i > 0)
    def _():
        buf[tc:tc + HALO, :] = buf[0:HALO, :]

    buf[0:tc, :] = du1_ref[...]


def _conv_bwd_rows(r0, groups, pacc, a_ref, b_ref, w_ref, dab_ref, buf, win):
    for c0 in range(0, CW, CONV_COLS):
        cols = slice(c0, c0 + CONV_COLS)
        av = a_ref[r0:r0 + CONV_ROWS, cols]
        sgb = _sigmoid(b_ref[r0:r0 + CONV_ROWS, cols])
        u0 = av * sgb
        acc = jnp.zeros((CONV_ROWS, CONV_COLS), F32)
        for start, length, taps in groups:
            win[0:length, :] = buf[r0 + start:r0 + start + length, cols]
            for j, at in taps:
                sl = win[at:at + CONV_ROWS, :]
                acc = acc + w_ref[CK - 1 - j:CK - j, cols] * sl
                pr = u0 * sl
                red = pr[0:8, :]
                for q in range(1, CONV_ROWS // 8):
                    red = red + pr[8 * q:8 * q + 8, :]
                pacc[CK - 1 - j, :, cols] += red
        dab_ref[r0:r0 + CONV_ROWS, c0:c0 + CONV_COLS] = (acc * sgb).astype(BF)
        dab_ref[r0:r0 + CONV_ROWS, CW + c0:CW + c0 + CONV_COLS] = (acc * av * sgb * (1.0 - sgb)).astype(BF)


def _conv_bwd_finish(nt, pacc, du1_ref, dwb_ref):
    dwb_ref[CK:CK + 1, :] += jnp.sum(du1_ref[...], axis=0, keepdims=True)

    @pl.when(pl.program_id(0) == nt - 1)
    def _():
        for k in range(CK):
            dwb_ref[k:k + 1, :] = jnp.sum(pacc[k], axis=0, keepdims=True)


def _ret_bwd_chunk(c, q_ref, k_ref, v_ref, cos_ref, sin_ref, dm_ref, qd_ref, kd_ref, cd_ref, dr_ref, st_ref,
                   dq_ref, dk_ref, dv_ref, gst):
    scale = HD ** -0.5
    rows = slice(c * CHUNK, (c + 1) * CHUNK)
    cs, sn = cos_ref[rows, :], sin_ref[rows, :]
    for h in range(NH):
        cols = slice(h * HD, (h + 1) * HD)
        qr = _rot(q_ref[rows, cols], cs, sn)
        kr = _rot(k_ref[rows, cols], cs, sn) * scale
        qb, kb = qr.astype(BF), kr.astype(BF)
        vb = v_ref[rows, cols].astype(BF)
        dob = dr_ref[rows, cols]
        sb = st_ref[h, c]
        gn1 = gst[h]
        gb = gn1.astype(BF)
        sc = (_dot_nt(qb, kb) * dm_ref[h]).astype(BF)
        dsc = (_dot_nt(dob, vb) * dm_ref[h]).astype(BF)
        dqr = _dot(dsc, kb) + _dot_nt(dob, sb) * qd_ref[h]
        dkr = _dot_tn(dsc, qb) + _dot_nt(vb, gb) * kd_ref[h]
        dvv = _dot_tn(sc, dob) + _dot((kr * kd_ref[h]).astype(BF), gb)
        gst[h] = cd_ref[h] * gn1 + _dot_tn((qr * qd_ref[h]).astype(BF), dob)
        dq_ref[rows, cols] = _rot_t(dqr, cs, sn).astype(BF)
        dk_ref[rows, cols] = _rot_t(dkr * scale, cs, sn).astype(BF)
        dv_ref[rows, cols] = dvv.astype(BF)


def mix_bwd(du1, dr, proj, cw, states, tables, l, deps=()):
    S = proj.shape[0]
    tt = min(S, SEQ_TILE)
    cpb = tt // CHUNK
    nt = S // tt
    groups = _tap_groups(0, CK)

    def body(du1_ref, a_ref, b_ref, w_ref, q_ref, k_ref, v_ref, cos_ref, sin_ref, dm_ref, qd_ref, kd_ref, cd_ref,
             dr_ref, st_ref, dab_ref, dwb_ref, dq_ref, dk_ref, dv_ref, buf, win, gst, pacc):
        @pl.when(pl.program_id(0) == 0)
        def _():
            gst[...] = jnp.zeros((NH, HD, HD), F32)

        _conv_bwd_fill(tt, du1_ref, dwb_ref, buf, pacc)
        for c in reversed(range(cpb)):
            _ret_bwd_chunk(c, q_ref, k_ref, v_ref, cos_ref, sin_ref, dm_ref, qd_ref, kd_ref, cd_ref, dr_ref, st_ref,
                           dq_ref, dk_ref, dv_ref, gst)
            for r0 in range(c * CHUNK, (c + 1) * CHUNK, CONV_ROWS):
                _conv_bwd_rows(r0, groups, pacc, a_ref, b_ref, w_ref, dab_ref, buf, win)
        _conv_bwd_finish(nt, pacc, du1_ref, dwb_ref)

    rev = lambda t: nt - 1 - t
    half = pl.BlockSpec((tt, CW), lambda t: (rev(t), 0))
    return _ordered_call(
        body, deps, name=f"mix_bwd_{l}", grid=(nt,),
        in_specs=[half, half, pl.BlockSpec((tt, CW), lambda t: (rev(t), 1)), pl.BlockSpec((CK, CW), lambda t: (0, 0))]
        + _ret_specs(tt, rev) + [half, pl.BlockSpec((NH, cpb, HD, HD), lambda t: (0, rev(t), 0, 0))],
        out_specs=[pl.BlockSpec((tt, 2 * CW), lambda t: (rev(t), 0)), pl.BlockSpec((CK + 1, CW), lambda t: (0, 0)),
                   half, half, half],
        out_shape=[jax.ShapeDtypeStruct((S, 2 * CW), BF), jax.ShapeDtypeStruct((CK + 1, CW), F32)]
        + [jax.ShapeDtypeStruct((S, RW), BF)] * 3,
        scratch_shapes=[pltpu.VMEM((tt + HALO, CW), F32), pltpu.VMEM((HALO + CONV_ROWS, CONV_COLS), F32),
                        pltpu.VMEM((NH, HD, HD), F32), pltpu.VMEM((CK, 8, CW), F32)],
        compiler_params=_params(1, 40),
    )(du1, proj, proj, cw, proj, proj, proj, *tables, dr, states, *deps)


def in_proj_bwd(parts, win, x, g, dx2, l):
    S = x.shape[0]
    tm = min(S, SEQ_TILE)
    n = len(parts)

    def body(*refs):
        srcs = refs[:n]
        w_ref, x_ref, g_ref, dx2_ref, dx_ref, h_ref, dgain_ref = refs[n:]

        @pl.when(pl.program_id(0) == 0)
        def _():
            dgain_ref[...] = jnp.zeros((1, D), F32)

        dh, col = None, 0
        for r in srcs:
            width = r.shape[1]
            term = _dot_nt(r[...], w_ref[:, col:col + width])
            dh = term if dh is None else dh + term
            col += width
        dx, dgain, hb = _rms_bwd(x_ref[...], g_ref[...], dh, dx2_ref[...])
        dx_ref[...] = dx
        dgain_ref[...] += dgain
        h_ref[...] = hb

    row = pl.BlockSpec((tm, D), lambda i: (i, 0))
    vec = pl.BlockSpec((1, D), lambda i: (0, 0))
    return pl.pallas_call(
        body, name=f"in_proj_bwd_{l}", grid=(S // tm,),
        in_specs=[pl.BlockSpec((tm, p.shape[1]), lambda i: (i, 0)) for p in parts]
        + [pl.BlockSpec((D, INW), lambda i: (0, 0)), row, vec, row],
        out_specs=[row, row, vec],
        out_shape=[jax.ShapeDtypeStruct((S, D), F32), jax.ShapeDtypeStruct((S, D), BF),
                   jax.ShapeDtypeStruct((1, D), F32)],
        compiler_params=_params(1, 48),
    )(*parts, win, x, g, dx2)


def wgrad_in(h, parts, l):
    S = h.shape[0]
    ts = min(S, SEQ_TILE)
    ns = S // ts

    def body(*refs):
        h_ref, srcs = refs[0], refs[1:1 + len(parts)]
        o_ref, acc = refs[-2], refs[-1]
        s = pl.program_id(0)

        @pl.when(s == 0)
        def _():
            acc[...] = jnp.zeros((D, INW), F32)

        hv = h_ref[...]
        col = 0
        for r in srcs:
            width = r.shape[1]
            acc[:, col:col + width] += _dot_tn(hv, r[...])
            col += width

        @pl.when(s == ns - 1)
        def _():
            o_ref[...] = acc[...].astype(BF)

    return pl.pallas_call(
        body, name=f"wgrad_in_{l}", grid=(ns,),
        in_specs=[pl.BlockSpec((ts, D), lambda s: (s, 0))]
        + [pl.BlockSpec((ts, p.shape[1]), lambda s: (s, 0)) for p in parts],
        out_specs=pl.BlockSpec((D, INW), lambda s: (0, 0)),
        out_shape=jax.ShapeDtypeStruct((D, INW), BF),
        scratch_shapes=[pltpu.VMEM((D, INW), F32)],
        compiler_params=_params(1, 48),
    )(h, *parts)


def sum_slots(recv, name):
    _, R, C = recv.shape
    tr = 256 if R % 256 == 0 else R

    def body(r_ref, o_ref):
        acc = r_ref[0].astype(F32)
        for k in range(1, NCHIP):
            acc = acc + r_ref[k].astype(F32)
        o_ref[...] = acc

    return pl.pallas_call(
        body, name=name, grid=(R // tr,),
        in_specs=[pl.BlockSpec((NCHIP, tr, C), lambda i: (0, i, 0))],
        out_specs=pl.BlockSpec((tr, C), lambda i: (i, 0)),
        out_shape=jax.ShapeDtypeStruct((R, C), F32),
        compiler_params=_params(1, 32),
    )(recv)


def adamw(w, ga, gb, m, v, name):
    R, C = w.shape
    tr = 256 if R % 256 == 0 else R
    c1 = 1.0 - ADAM_B1 ** ADAM_STEP
    c2 = 1.0 - ADAM_B2 ** ADAM_STEP

    def body(w_ref, ga_ref, gb_ref, m_ref, v_ref, g_out, d_out, m_out, v_out):
        g = ga_ref[...] + gb_ref[...]
        mn = ADAM_B1 * m_ref[...] + (1.0 - ADAM_B1) * g
        vn = ADAM_B2 * v_ref[...] + (1.0 - ADAM_B2) * (g * g)
        g_out[...] = g
        m_out[...] = mn
        v_out[...] = vn
        d_out[...] = -ADAM_LR * ((mn / c1) / (jnp.sqrt(vn / c2) + ADAM_EPS) + ADAM_WD * w_ref[...])

    blk = pl.BlockSpec((tr, C), lambda i: (i, 0))
    return pl.pallas_call(
        body, name=name, grid=(R // tr,),
        in_specs=[blk] * 5, out_specs=[blk] * 4,
        out_shape=[jax.ShapeDtypeStruct((R, C), F32)] * 4,
        compiler_params=_params(1, 40),
    )(w, ga, gb, m, v)


def _place():
    x, y, c = lax.axis_index("x"), lax.axis_index("y"), lax.axis_index("c")
    chips = [(1 - x, y), (x, 1 - y), (1 - x, 1 - y)]
    return x, y, c, chips


def _window(ref, axis, j, size):
    idx = [slice(None)] * len(ref.shape)
    idx[axis] = pl.ds(pl.multiple_of(j * size, 128 if axis == len(ref.shape) - 1 else 16), size)
    return ref.at[tuple(idx)]


def _hbm(a):
    return pltpu.with_memory_space_constraint(a, pltpu.HBM)


def _hbm_like(arrs):
    return [pltpu.HBM(a.shape, a.dtype) for a in arrs]


def gather_start(shards, axes, after, tag):
    n = len(shards)
    na = len(after)
    lands = []
    for s, ax in zip(shards, axes):
        shp = list(s.shape)
        shp[ax] *= NCHIP
        lands.append(lax.empty(tuple(shp), s.dtype))

    def body(*refs):
        ins, land = refs[:n], refs[n:2 * n]
        send, recv = refs[2 * n + na], refs[2 * n + na + 1]
        token = refs[-1]
        x, y, c, chips = _place()
        for a in range(n):
            for k, chip in enumerate(chips):
                pltpu.make_async_remote_copy(
                    src_ref=ins[a], dst_ref=_window(land[a], axes[a], 2 * x + y, ins[a].shape[axes[a]]),
                    send_sem=send.at[3 * a + k], recv_sem=recv.at[3 * a + k],
                    device_id=(chip[0], chip[1], c), device_id_type=MESH).start()
        token[...] = jnp.zeros_like(token)

    outs = pl.pallas_call(
        body, name=f"gather_start_{tag}",
        in_specs=[HBM_SPEC] * (2 * n) + [ANY] * na,
        out_specs=(SEM_SPEC, SEM_SPEC, *[HBM_SPEC] * (2 * n), VMEM_SPEC),
        out_shape=(pltpu.SemaphoreType.DMA((3 * n,)), pltpu.SemaphoreType.DMA((3 * n,)),
                   *_hbm_like(shards), *_hbm_like(lands), jax.ShapeDtypeStruct((8, 128), F32)),
        input_output_aliases={a: 2 + a for a in range(2 * n)},
        compiler_params=pltpu.CompilerParams(has_side_effects=DATAFLOW),
    )(*[_hbm(s) for s in shards], *[_hbm(b) for b in lands], *after)
    return (outs[0], outs[1], list(outs[2:2 + n]), list(outs[2 + n:2 + 2 * n]), list(axes)), outs[-1]


def gather_wait(groups, after, tag):
    sizes = [len(g[2]) for g in groups]
    total = sum(sizes)

    def body(*refs):
        x, y, c, chips = _place()
        stage, loc = refs[-1 - total:-1], refs[-1]
        pos = 2 * total
        off = 0
        mine = []
        for g, n in zip(groups, sizes):
            ins, land = refs[off:off + n], refs[total + off:total + off + n]
            send_ref, recv_ref = refs[pos], refs[pos + 1]
            axes = g[4]
            for a in range(n):
                fetch = pltpu.make_async_copy(ins[a], stage[off + a], loc.at[2 * (off + a)])
                fetch.start()
                put = pltpu.make_async_copy(
                    stage[off + a], _window(land[a], axes[a], 2 * x + y, ins[a].shape[axes[a]]),
                    loc.at[2 * (off + a) + 1])
                mine.append((fetch, put))
                for k, chip in enumerate(chips):
                    cp = pltpu.make_async_remote_copy(
                        src_ref=ins[a],
                        dst_ref=_window(land[a], axes[a], 2 * chip[0] + chip[1], ins[a].shape[axes[a]]),
                        send_sem=send_ref.at[3 * a + k], recv_sem=recv_ref.at[3 * a + k],
                        device_id=(chip[0], chip[1], c), device_id_type=MESH)
                    cp.wait_send()
                    cp.wait_recv()
            pos += 2
            off += n
        for fetch, put in mine:
            fetch.wait()
            put.start()
        for fetch, put in mine:
            put.wait()

    shards = [s for g in groups for s in g[2]]
    lands = [b for g in groups for b in g[3]]
    sems = [s for g in groups for s in (g[0], g[1])]
    outs = pl.pallas_call(
        body, name=f"gather_wait_{tag}",
        in_specs=[HBM_SPEC] * (2 * total) + [SEM_SPEC] * len(sems) + [ANY],
        out_specs=[HBM_SPEC] * (2 * total),
        out_shape=(*_hbm_like(shards), *_hbm_like(lands)),
        input_output_aliases={a: a for a in range(2 * total)},
        scratch_shapes=[pltpu.VMEM(s.shape, s.dtype) for s in shards] + [pltpu.SemaphoreType.DMA((2 * total,))],
        compiler_params=pltpu.CompilerParams(has_side_effects=DATAFLOW, vmem_limit_bytes=32 << 20),
    )(*shards, *lands, *sems, after)
    return list(outs[total:])


def scatter_start(grads, axes, sizes, lands, l, tag):
    n = len(grads)

    def body(*refs):
        ins, land = refs[:n], refs[n:2 * n]
        send, recv = refs[2 * n], refs[2 * n + 1]
        token = refs[2 * n + 2 + 2 * n]
        stage, loc = refs[-1 - n:-1], refs[-1]
        x, y, c, chips = _place()
        me = 2 * x + y
        fetches = [pltpu.make_async_copy(_window(ins[a], axes[a], me, sizes[a]), stage[a], loc.at[2 * a])
                   for a in range(n)]
        for cp in fetches:
            cp.start()
        for a in range(n):
            for k, chip in enumerate(chips):
                pltpu.make_async_remote_copy(
                    src_ref=_window(ins[a], axes[a], 2 * chip[0] + chip[1], sizes[a]), dst_ref=land[a].at[me, l],
                    send_sem=send.at[3 * a + k], recv_sem=recv.at[3 * a + k],
                    device_id=(chip[0], chip[1], c), device_id_type=MESH).start()
        puts = [pltpu.make_async_copy(stage[a], land[a].at[me, l], loc.at[2 * a + 1]) for a in range(n)]
        for fetch, put in zip(fetches, puts):
            fetch.wait()
            put.start()
        for put in puts:
            put.wait()
        token[...] = jnp.zeros_like(token)

    outs = pl.pallas_call(
        body, name=f"scatter_start_{tag}",
        in_specs=[HBM_SPEC] * (2 * n),
        out_specs=(SEM_SPEC, SEM_SPEC, *[HBM_SPEC] * (2 * n), VMEM_SPEC),
        out_shape=(pltpu.SemaphoreType.DMA((3 * n,)), pltpu.SemaphoreType.DMA((3 * n,)),
                   *_hbm_like(grads), *_hbm_like(lands), jax.ShapeDtypeStruct((8, 128), F32)),
        input_output_aliases={a: 2 + a for a in range(2 * n)},
        scratch_shapes=[pltpu.VMEM(b.shape[2:], b.dtype) for b in lands] + [pltpu.SemaphoreType.DMA((2 * n,))],
        compiler_params=pltpu.CompilerParams(has_side_effects=DATAFLOW, vmem_limit_bytes=32 << 20),
    )(*[_hbm(g) for g in grads], *[_hbm(b) for b in lands])
    group = (outs[0], outs[1], list(outs[2:2 + n]), list(axes), list(sizes), l)
    return group, list(outs[2 + n:2 + 2 * n]), outs[-1]


def scatter_wait(groups, lands, which):
    nl = len(lands)

    def body(*refs):
        land = refs[:nl]
        x, y, c, chips = _place()
        pos = nl
        for g, wh in zip(groups, which):
            n = len(g[2])
            ins = refs[pos:pos + n]
            send_ref, recv_ref = refs[pos + n], refs[pos + n + 1]
            axes, sizes, l = g[3], g[4], g[5]
            for a in range(n):
                for k, chip in enumerate(chips):
                    jp = 2 * chip[0] + chip[1]
                    cp = pltpu.make_async_remote_copy(
                        src_ref=_window(ins[a], axes[a], jp, sizes[a]), dst_ref=land[wh[a]].at[jp, l],
                        send_sem=send_ref.at[3 * a + k], recv_sem=recv_ref.at[3 * a + k],
                        device_id=(chip[0], chip[1], c), device_id_type=MESH)
                    cp.wait_send()
                    cp.wait_recv()
            pos += n + 2

    operands = list(lands)
    specs = [HBM_SPEC] * nl
    for g in groups:
        operands += list(g[2]) + [g[0], g[1]]
        specs += [HBM_SPEC] * len(g[2]) + [SEM_SPEC, SEM_SPEC]
    outs = pl.pallas_call(
        body, name="scatter_wait", in_specs=specs, out_specs=[HBM_SPEC] * nl, out_shape=tuple(_hbm_like(lands)),
        input_output_aliases={a: a for a in range(nl)},
        compiler_params=pltpu.CompilerParams(has_side_effects=DATAFLOW),
    )(*operands)
    return list(outs)


def sibling_swap(parts):
    n = len(parts)

    def body(*refs):
        ins, outs = refs[:n], refs[n:2 * n]
        send, recv = refs[2 * n:]
        x, y, c, _ = _place()
        cps = [pltpu.make_async_remote_copy(src_ref=ins[a], dst_ref=outs[a], send_sem=send.at[a],
                                            recv_sem=recv.at[a], device_id=(x, y, 1 - c), device_id_type=MESH)
               for a in range(n)]
        for cp in cps:
            cp.start()
        for cp in cps:
            cp.wait_recv()
        for cp in cps:
            cp.wait_send()

    return pl.pallas_call(
        body, name="sibling_swap", in_specs=[ANY] * n, out_specs=[ANY] * n,
        out_shape=[jax.ShapeDtypeStruct(p.shape, p.dtype) for p in parts],
        scratch_shapes=[pltpu.SemaphoreType.DMA((n,)), pltpu.SemaphoreType.DMA((n,))],
    )(*parts)


def small_allreduce(p):
    R, C = p.shape
    ndev = 8

    def body(p_ref, o_ref, buf, send, recv):
        x, y, c, _ = _place()
        me = 4 * x + 2 * y + c
        buf[me] = p_ref[...]

        def peer(d):
            px = 1 - x if d & 4 else x
            py = 1 - y if d & 2 else y
            pc = 1 - c if d & 1 else c
            return px, py, pc

        def copy(d, slot):
            return pltpu.make_async_remote_copy(src_ref=p_ref, dst_ref=buf.at[slot], send_sem=send.at[d - 1],
                                                recv_sem=recv.at[d - 1], device_id=peer(d), device_id_type=MESH)

        sends = [copy(d, me) for d in range(1, ndev)]
        for cp in sends:
            cp.start()
        for d in range(1, ndev):
            px, py, pc = peer(d)
            copy(d, 4 * px + 2 * py + pc).wait_recv()
        for cp in sends:
            cp.wait_send()
        acc = buf[0]
        for k in range(1, ndev):
            acc = acc + buf[k]
        o_ref[...] = acc

    return pl.pallas_call(
        body, name="small_allreduce", in_specs=[VMEM_SPEC], out_specs=VMEM_SPEC,
        out_shape=jax.ShapeDtypeStruct((R, C), F32),
        scratch_shapes=[pltpu.VMEM((ndev, R, C), F32), pltpu.SemaphoreType.DMA((ndev - 1,)),
                        pltpu.SemaphoreType.DMA((ndev - 1,))],
        compiler_params=pltpu.CompilerParams(vmem_limit_bytes=32 << 20),
    )(p)


def kernel(x, norm1_g, w_in, conv_w, conv_b, conv_ln_g, conv_ln_b, ret_gn_g, w_out, norm2_g, w_gate, w_up, w_down, final_g, loss_target, m_norm1_g, m_w_in, m_conv_w, m_conv_b, m_conv_ln_g, m_conv_ln_b, m_ret_gn_g, m_w_out, m_norm2_g, m_w_gate, m_w_up, m_w_down, m_final_g, v_norm1_g, v_w_in, v_conv_w, v_conv_b, v_conv_ln_g, v_conv_ln_b, v_ret_gn_g, v_w_out, v_norm2_g, v_w_gate, v_w_up, v_w_down, v_final_g):
    S = x.shape[1]
    xs = x.reshape(S, D)
    tgt = loss_target.reshape(S, D)
    fsh = FF // NCHIP

    def shards_of(l):
        return [w_in[l].astype(BF), w_out[l].astype(BF), w_gate[l].T.astype(BF), w_up[l].T.astype(BF),
                w_down[l].astype(BF), conv_w[l]]

    gather_axes = [1, 0, 0, 0, 0, 1]
    shard_cache = [shards_of(l) for l in range(L)]
    tables = _ret_tables(S)
    row = lambda a, l: a[l].reshape(1, -1)
    groups = {}
    weights = [dict() for _ in range(L)]

    def begin(l, which, after):
        group, token = gather_start([shard_cache[l][i] for i in which], [gather_axes[i] for i in which],
                                    after, f"{l}_{which[0]}")
        groups[(l, which[0])] = (group, which)
        return token

    def finish(l, firsts, after, tag):
        gs = [groups[(l, f)] for f in firsts]
        outs = gather_wait([g for g, _ in gs], after, f"{l}_{tag}")
        k = 0
        for _, which in gs:
            for i in which:
                weights[l][i] = outs[k]
                k += 1

    def host(stage, after):
        tokens = []
        for gl, which in stage:
            tokens.append(begin(gl, which, [after] + tokens))
        return tokens

    def hosted(l):
        if l == 0:
            return {"in": [(0, [2])], "conv": [(0, [3])], "ret": [(0, [4])], "out": [(1, [0])],
                    "mlp": [(1, [1, 5]), (1, [2])]}
        nxt = {"ret": [(l + 1, [0])], "out": [(l + 1, [1, 5])], "mlp": [(l + 1, [2])]} if l + 1 < L else {}
        return {"in": [(l, [3])], "conv": [(l, [4])], **nxt}

    first = begin(0, [0], [])
    after = begin(0, [1, 5], [first])
    saved = []
    xc = xs
    for l in range(L):
        sched = hosted(l)
        finish(l, [0, 1], after, "a")
        win, wout, cw = weights[l][0], weights[l][1], weights[l][5]
        proj = in_proj(xc, row(norm1_g, l), win, l, host(sched.get("in", []), win))
        u1, u, r_raw, states = mix_fwd(proj, cw, row(conv_b, l), row(conv_ln_g, l), row(conv_ln_b, l), tables, l,
                                       host(sched.get("conv", []) + sched.get("ret", []), proj))
        x2, mixed = out_proj(u, r_raw, proj, row(ret_gn_g, l), wout, xc, l, host(sched.get("out", []), r_raw))
        finish(l, [2, 3, 4], x2, "b")
        wgt, wut, wd = weights[l][2], weights[l][3], weights[l][4]
        x3, gs, us, act = mlp_fwd(x2, row(norm2_g, l), wgt, wut, wd, l, host(sched.get("mlp", []), wd))
        saved.append((xc, proj, u1, r_raw, states, mixed, x2, gs, us, act))
        xc = x3
        after = x3

    dx, loss_acc, d_final = final_loss(xc, final_g.reshape(1, D), tgt)
    loss = lax.psum(loss_acc[0, 0] * (0.5 / D), ("x", "y", "c"))

    scatter_axes = [1, 0, 0, 0, 0]
    scatter_sizes = [INW // NCHIP, D // NCHIP, fsh, fsh, fsh]
    lands = [lax.empty((NCHIP, L, D, INW // NCHIP), BF), lax.empty((NCHIP, L, D // NCHIP, D), BF),
             lax.empty((NCHIP, L, fsh, D), BF), lax.empty((NCHIP, L, fsh, D), BF), lax.empty((NCHIP, L, fsh, D), BF)]
    sent, sent_which = [], []

    def send_grad(g, a, l):
        group, new_land, token = scatter_start([g], [scatter_axes[a]], [scatter_sizes[a]], [lands[a]], l, f"{l}_{a}")
        lands[a] = new_land[0]
        sent.append(group)
        sent_which.append([a])
        return [token]

    small = [None] * L
    dep = []
    for l in reversed(range(L)):
        xin, proj, u1, r_raw, states, mixed, x2, gs, us, act = saved[l]
        win, wout, wgt, wut, wd, cw = (weights[l][i] for i in range(6))
        dx2, dgs, dus, h2, d_n2 = mlp_bwd(dx, x2, row(norm2_g, l), gs, us, wgt, wut, wd, l, dep)
        g_wd = wgrad(act, dx, l, "wgrad_down")
        g_wgt = wgrad(dgs, h2, l, "wgrad_gate", send_grad(g_wd, 4, l))
        g_wut = wgrad(dus, h2, l, "wgrad_up", send_grad(g_wgt, 2, l))
        dgate, dr, du1, sums = out_proj_bwd(dx2, wout, r_raw, proj, row(ret_gn_g, l), u1,
                                            row(conv_ln_g, l), row(conv_ln_b, l), l, send_grad(g_wut, 3, l))
        g_wout = wgrad(mixed, dx2, l, "wgrad_out")
        dab, dwb, dq, dk, dv = mix_bwd(du1, dr, proj, cw, states, tables, l, send_grad(g_wout, 1, l))
        dproj = [dab, dq, dk, dv, dgate]
        dx, h1, d_n1 = in_proj_bwd(dproj, win, xin, row(norm1_g, l), dx2, l)
        g_win = wgrad_in(h1, dproj, l)
        dep = send_grad(g_win, 0, l)
        small[l] = jnp.concatenate([dwb, sums, d_n1.reshape(2, CW), d_n2.reshape(2, CW)], axis=0)
    grad_x = dx.reshape(1, S, D)

    per = CK + 1 + 8 + 4
    packed = jnp.concatenate(small + [d_final.reshape(2, CW), jnp.zeros((6, CW), F32)], axis=0)
    tot = small_allreduce(packed)
    lay = tot[:L * per].reshape(L, per, CW)
    g_conv_w_full = lay[:, 0:CK, :]
    j = 2 * lax.axis_index("x") + lax.axis_index("y")
    g_conv_w = lax.dynamic_slice_in_dim(g_conv_w_full, j * (CW // NCHIP), CW // NCHIP, axis=2)
    g_small = {
        "conv_b": lay[:, CK, :], "ret_gn_g": lay[:, CK + 1, :], "conv_ln_g": lay[:, CK + 2, :],
        "conv_ln_b": lay[:, CK + 3, :], "norm1_g": lay[:, CK + 9:CK + 11, :].reshape(L, D),
        "norm2_g": lay[:, CK + 11:CK + 13, :].reshape(L, D), "final_g": tot[L * per:L * per + 2].reshape(D),
    }

    recv = scatter_wait(sent, lands, sent_which)
    shard_shapes = [(L * D, INW // NCHIP), (L * D // NCHIP, D), (L * fsh, D), (L * fsh, D), (L * fsh, D)]
    names = ["w_in", "w_out", "w_gate", "w_up", "w_down"]
    parts = [sum_slots(r.reshape((NCHIP,) + shp), f"sum_{nm}") for r, shp, nm in zip(recv, shard_shapes, names)]
    theirs = sibling_swap(parts)

    def unT(a):
        return jnp.swapaxes(a.reshape(L, fsh, D), 1, 2).reshape(L * D, fsh)

    big = {}
    wmv = {"w_in": (w_in, m_w_in, v_w_in), "w_out": (w_out, m_w_out, v_w_out),
           "w_gate": (w_gate, m_w_gate, v_w_gate), "w_up": (w_up, m_w_up, v_w_up),
           "w_down": (w_down, m_w_down, v_w_down)}
    for nm, mine, other in zip(names, parts, theirs):
        w, m, v = wmv[nm]
        if nm in ("w_gate", "w_up"):
            mine, other = unT(mine), unT(other)
        shp2 = (w.shape[0] * w.shape[1], w.shape[2])
        outs = adamw(w.reshape(shp2), mine, other, m.reshape(shp2), v.reshape(shp2), f"adamw_{nm}")
        big[nm] = [o.reshape(w.shape) for o in outs]

    cshape = (L * CK, CW // NCHIP)
    zc = jnp.zeros(cshape, F32)
    big["conv_w"] = [o.reshape(conv_w.shape) for o in adamw(
        conv_w.reshape(cshape), g_conv_w.reshape(cshape), zc, m_conv_w.reshape(cshape),
        v_conv_w.reshape(cshape), "adamw_conv_w")]
    vec_names = ["norm1_g", "conv_b", "conv_ln_g", "conv_ln_b", "ret_gn_g", "norm2_g", "final_g"]
    vec_w = {"norm1_g": (norm1_g, m_norm1_g, v_norm1_g), "conv_b": (conv_b, m_conv_b, v_conv_b),
             "conv_ln_g": (conv_ln_g, m_conv_ln_g, v_conv_ln_g), "conv_ln_b": (conv_ln_b, m_conv_ln_b, v_conv_ln_b),
             "ret_gn_g": (ret_gn_g, m_ret_gn_g, v_ret_gn_g), "norm2_g": (norm2_g, m_norm2_g, v_norm2_g),
             "final_g": (final_g, m_final_g, v_final_g)}
    cat = lambda arrs: jnp.concatenate([a.reshape(-1, CW) for a in arrs], axis=0)
    vw = cat([vec_w[nm][0] for nm in vec_names])
    vm = cat([vec_w[nm][1] for nm in vec_names])
    vv = cat([vec_w[nm][2] for nm in vec_names])
    vg = cat([g_small[nm] for nm in vec_names])
    vouts = adamw(vw, vg, jnp.zeros_like(vg), vm, vv, "adamw_vectors")
    off = 0
    for nm in vec_names:
        w = vec_w[nm][0]
        nrow = w.size // CW
        big[nm] = [o[off:off + nrow].reshape(w.shape) for o in vouts]
        off += nrow

    order = ["norm1_g", "w_in", "conv_w", "conv_b", "conv_ln_g", "conv_ln_b", "ret_gn_g", "w_out", "norm2_g",
             "w_gate", "w_up", "w_down", "final_g"]
    return (loss, grad_x, *[big[nm][0] for nm in order], *[big[nm][1] for nm in order],
            *[big[nm][2] for nm in order], *[big[nm][3] for nm in order])
```

```python
import math

import jax
import jax.numpy as jnp
from jax import lax
from jax.experimental import pallas as pl
from jax.experimental.pallas import tpu as pltpu

D = 1024
L = 4
CW = 512
RW = 512
NH = 4
HD = 128
CK = 31
CHUNK = 64
INW = 3072
FF = 2816
NCHIP = 4
EPS = 1e-6
ROPE_BASE = 10000.0
SEQ_TILE = 512
WGRAD_ROWS = 1024
MLP_ROWS = 256
MLP_COLS = 1408
HALO = 32
CONV_ROWS = 32
CONV_COLS = 256

ADAM_LR = 0.001
ADAM_B1 = 0.9
ADAM_B2 = 0.999
ADAM_EPS = 1e-08
ADAM_WD = 0.01
ADAM_STEP = 10

BF = jnp.bfloat16
F32 = jnp.float32
MESH = pl.DeviceIdType.MESH
ANY = pl.BlockSpec(memory_space=pl.ANY)
VMEM_SPEC = pl.BlockSpec(memory_space=pltpu.VMEM)
HBM_SPEC = pl.BlockSpec(memory_space=pltpu.HBM)
SEM_SPEC = pl.BlockSpec(memory_space=pltpu.SEMAPHORE)
DATAFLOW = pltpu.SideEffectType.DATAFLOW_SIDE_EFFECTING


def _params(n_grid, vmem_mb):
    return pltpu.CompilerParams(dimension_semantics=("arbitrary",) * n_grid,
                                vmem_limit_bytes=vmem_mb << 20)


def _ordered_call(body, deps, *, in_specs, **kw):
    n, nd = len(in_specs), len(deps)

    def with_deps(*refs):
        body(*refs[:n], *refs[n + nd:])

    return pl.pallas_call(with_deps, in_specs=list(in_specs) + [ANY] * nd, **kw)


def _dot(a, b):
    return jnp.dot(a, b, preferred_element_type=F32)


def _dot_nt(a, b):
    return lax.dot_general(a, b, (((1,), (1,)), ((), ())), preferred_element_type=F32)


def _dot_tn(a, b):
    return lax.dot_general(a, b, (((0,), (0,)), ((), ())), preferred_element_type=F32)


def _sigmoid(x):
    return 0.5 * jnp.tanh(0.5 * x) + 0.5


def _mean(x):
    return jnp.mean(x, axis=-1, keepdims=True)


def _fold8(x):
    out = x[0:8, :]
    for q in range(1, x.shape[0] // 8):
        out = out + x[8 * q:8 * q + 8, :]
    return out


def _tap_groups(first, count):
    groups = []
    for phase in range(8):
        taps = [(t, first + t - phase) for t in range(count) if (first + t) % 8 == phase]
        if taps:
            lo, hi = min(q for _, q in taps), max(q for _, q in taps)
            groups.append((lo + phase, hi - lo + CONV_ROWS, [(t, q - lo) for t, q in taps]))
    return groups


def _rot(t, cs, sn):
    return t * cs + pltpu.roll(t, HD // 2, 1) * sn


def _rot_t(dy, cs, sn):
    return dy * cs + pltpu.roll(dy * sn, HD // 2, 1)


def _rms_bwd(x, g, dh, dx_in):
    r = lax.rsqrt(_mean(x * x) + EPS)
    xh = x * r
    dxh = dh * g
    dx = dx_in + r * (dxh - xh * _mean(dxh * xh))
    return dx, jnp.sum(dh * xh, axis=0, keepdims=True), (xh * g).astype(BF)


def in_proj(x, g, win, l, deps=()):
    S = x.shape[0]
    tm = min(S, SEQ_TILE)

    def body(x_ref, g_ref, w_ref, o_ref, h_ref):
        xv = x_ref[...]
        h = (xv * lax.rsqrt(_mean(xv * xv) + EPS) * g_ref[...]).astype(BF)
        h_ref[...] = h
        o_ref[...] = _dot(h, w_ref[...])

    return _ordered_call(
        body, deps, name=f"in_proj_{l}", grid=(S // tm,),
        in_specs=[pl.BlockSpec((tm, D), lambda i: (i, 0)),
                  pl.BlockSpec((1, D), lambda i: (0, 0)),
                  pl.BlockSpec((D, INW), lambda i: (0, 0))],
        out_specs=[pl.BlockSpec((tm, INW), lambda i: (i, 0)), pl.BlockSpec((tm, D), lambda i: (i, 0))],
        out_shape=[jax.ShapeDtypeStruct((S, INW), F32), jax.ShapeDtypeStruct((S, D), BF)],
        compiler_params=_params(1, 48),
    )(x, g, win, *deps)


def _conv_fwd_fill(tc, a_ref, b_ref, buf):
    i = pl.program_id(0)

    @pl.when(i == 0)
    def _():
        buf[0:HALO, :] = jnp.zeros((HALO, CW), F32)

    @pl.when(i > 0)
    def _():
        buf[0:HALO, :] = buf[tc:tc + HALO, :]

    buf[HALO:HALO + tc, :] = a_ref[...] * _sigmoid(b_ref[...])


def _conv_fwd_rows(r0, groups, w_ref, cb_ref, lg_ref, lb_ref, u1_ref, u_ref, buf, win):
    for c0 in range(0, CW, CONV_COLS):
        cols = slice(c0, c0 + CONV_COLS)
        acc = jnp.broadcast_to(cb_ref[:, cols], (CONV_ROWS, CONV_COLS))
        for start, length, taps in groups:
            win[0:length, :] = buf[r0 + start:r0 + start + length, cols]
            for k, at in taps:
                acc = acc + w_ref[k:k + 1, cols] * win[at:at + CONV_ROWS, :]
        u1_ref[r0:r0 + CONV_ROWS, cols] = acc
    acc = u1_ref[r0:r0 + CONV_ROWS, :]
    d = acc - _mean(acc)
    u2 = d * lax.rsqrt(_mean(d * d) + EPS) * lg_ref[...] + lb_ref[...]
    u_ref[r0:r0 + CONV_ROWS, :] = (u2 * _sigmoid(u2)).astype(BF)


def _ret_tables(S):
    half = HD // 2
    pos = jnp.arange(S, dtype=F32)
    freqs = ROPE_BASE ** (-jnp.arange(half, dtype=F32) / half)
    ang = pos[:, None] * freqs[None, :]
    cos, sin = jnp.cos(ang), jnp.sin(ang)
    cosf = jnp.concatenate([cos, cos], axis=-1)
    sinf = jnp.concatenate([-sin, sin], axis=-1)
    log_g = jnp.log(1.0 - 2.0 ** (-5.0 - jnp.arange(NH, dtype=F32)))
    idx = jnp.arange(CHUNK, dtype=F32)
    dmat = jnp.exp(log_g[:, None, None] * jnp.abs(idx[:, None] - idx[None, :]))
    qdec = jnp.broadcast_to(jnp.exp(log_g[:, None] * (idx + 1.0))[:, :, None], (NH, CHUNK, HD))
    kdec = jnp.broadcast_to(jnp.exp(log_g[:, None] * (CHUNK - 1 - idx))[:, :, None], (NH, CHUNK, HD))
    cdec = jnp.broadcast_to(jnp.exp(log_g * CHUNK)[:, None, None], (NH, HD, HD))
    return cosf, sinf, dmat, qdec, kdec, cdec


def _ret_specs(tr, tmap):
    q0 = (2 * CW) // RW
    return [pl.BlockSpec((tr, RW), lambda t: (tmap(t), q0)),
            pl.BlockSpec((tr, RW), lambda t: (tmap(t), q0 + 1)),
            pl.BlockSpec((tr, RW), lambda t: (tmap(t), q0 + 2)),
            pl.BlockSpec((tr, HD), lambda t: (tmap(t), 0)),
            pl.BlockSpec((tr, HD), lambda t: (tmap(t), 0)),
            pl.BlockSpec((NH, CHUNK, CHUNK), lambda t: (0, 0, 0)),
            pl.BlockSpec((NH, CHUNK, HD), lambda t: (0, 0, 0)),
            pl.BlockSpec((NH, CHUNK, HD), lambda t: (0, 0, 0)),
            pl.BlockSpec((NH, HD, HD), lambda t: (0, 0, 0))]


def _ret_fwd_chunk(c, q_ref, k_ref, v_ref, cos_ref, sin_ref, dm_ref, qd_ref, kd_ref, cd_ref, r_ref, st_ref, st):
    scale = HD ** -0.5
    rows = slice(c * CHUNK, (c + 1) * CHUNK)
    cs, sn = cos_ref[rows, :], sin_ref[rows, :]
    for h in range(NH):
        cols = slice(h * HD, (h + 1) * HD)
        qr = _rot(q_ref[rows, cols], cs, sn)
        kr = _rot(k_ref[rows, cols], cs, sn) * scale
        vb = v_ref[rows, cols].astype(BF)
        s = st[h]
        sb = s.astype(BF)
        st_ref[h, c] = sb
        sc = _dot_nt(qr.astype(BF), kr.astype(BF)) * dm_ref[h]
        r_ref[rows, cols] = _dot(sc.astype(BF), vb) + _dot((qr * qd_ref[h]).astype(BF), sb)
        st[h] = cd_ref[h] * s + _dot_tn((kr * kd_ref[h]).astype(BF), vb)


def mix_fwd(proj, cw, cb, lg, lb, tables, l, deps=()):
    S = proj.shape[0]
    tt = min(S, SEQ_TILE)
    cpb = tt // CHUNK
    groups = _tap_groups(HALO - (CK - 1), CK)

    def body(a_ref, b_ref, w_ref, cb_ref, lg_ref, lb_ref, q_ref, k_ref, v_ref, cos_ref, sin_ref,
             dm_ref, qd_ref, kd_ref, cd_ref, u1_ref, u_ref, r_ref, st_ref, buf, win, st):
        @pl.when(pl.program_id(0) == 0)
        def _():
            st[...] = jnp.zeros((NH, HD, HD), F32)

        _conv_fwd_fill(tt, a_ref, b_ref, buf)
        for c in range(cpb):
            _ret_fwd_chunk(c, q_ref, k_ref, v_ref, cos_ref, sin_ref, dm_ref, qd_ref, kd_ref, cd_ref,
                           r_ref, st_ref, st)
            for r0 in range(c * CHUNK, (c + 1) * CHUNK, CONV_ROWS):
                _conv_fwd_rows(r0, groups, w_ref, cb_ref, lg_ref, lb_ref, u1_ref, u_ref, buf, win)

    vec = pl.BlockSpec((1, CW), lambda t: (0, 0))
    half = pl.BlockSpec((tt, CW), lambda t: (t, 0))
    return _ordered_call(
        body, deps, name=f"mix_fwd_{l}", grid=(S // tt,),
        in_specs=[half, pl.BlockSpec((tt, CW), lambda t: (t, 1)),
                  pl.BlockSpec((CK, CW), lambda t: (0, 0)), vec, vec, vec] + _ret_specs(tt, lambda t: t),
        out_specs=[half, half, half, pl.BlockSpec((NH, cpb, HD, HD), lambda t: (0, t, 0, 0))],
        out_shape=[jax.ShapeDtypeStruct((S, CW), F32), jax.ShapeDtypeStruct((S, CW), BF),
                   jax.ShapeDtypeStruct((S, RW), F32), jax.ShapeDtypeStruct((NH, S // CHUNK, HD, HD), BF)],
        scratch_shapes=[pltpu.VMEM((tt + HALO, CW), F32), pltpu.VMEM((HALO + CONV_ROWS, CONV_COLS), F32),
                        pltpu.VMEM((NH, HD, HD), F32)],
        compiler_params=_params(1, 40),
    )(proj, proj, cw, cb, lg, lb, proj, proj, proj, *tables, *deps)


def out_proj(u, r_raw, proj, gn, wout, x, l, deps=()):
    S = x.shape[0]
    tm = min(S, SEQ_TILE)
    gate_blk = (2 * CW + 3 * RW) // RW

    def body(u_ref, r_ref, gate_ref, gn_ref, w_ref, x_ref, x2_ref, mix_ref):
        mix_ref[:, 0:CW] = u_ref[...]
        gt = gate_ref[...]
        sil = gt * _sigmoid(gt) * gn_ref[...]
        for h in range(NH):
            cols = slice(h * HD, (h + 1) * HD)
            rh = r_ref[:, cols]
            d = rh - _mean(rh)
            rn = d * lax.rsqrt(_mean(d * d) + EPS)
            mix_ref[:, CW + h * HD:CW + (h + 1) * HD] = (rn * sil[:, cols]).astype(BF)
        x2_ref[...] = x_ref[...] + _dot(mix_ref[...], w_ref[...])

    return _ordered_call(
        body, deps, name=f"out_proj_{l}", grid=(S // tm,),
        in_specs=[pl.BlockSpec((tm, CW), lambda i: (i, 0)),
                  pl.BlockSpec((tm, RW), lambda i: (i, 0)),
                  pl.BlockSpec((tm, RW), lambda i: (i, gate_blk)),
                  pl.BlockSpec((1, RW), lambda i: (0, 0)),
                  pl.BlockSpec((D, D), lambda i: (0, 0)),
                  pl.BlockSpec((tm, D), lambda i: (i, 0))],
        out_specs=[pl.BlockSpec((tm, D), lambda i: (i, 0)),
                   pl.BlockSpec((tm, D), lambda i: (i, 0))],
        out_shape=[jax.ShapeDtypeStruct((S, D), F32), jax.ShapeDtypeStruct((S, D), BF)],
        compiler_params=_params(1, 40),
    )(u, r_raw, proj, gn, wout, x, *deps)


def mlp_fwd(x2, g2, wgt, wut, wd, l, deps=()):
    S = x2.shape[0]
    tm, tf = min(S, MLP_ROWS), MLP_COLS

    def body(x_ref, g_ref, wg_ref, wu_ref, wd_ref, o_ref, gs_ref, us_ref, a_ref):
        xv = x_ref[...]
        h = (xv * lax.rsqrt(_mean(xv * xv) + EPS) * g_ref[...]).astype(BF)
        for c0 in range(0, FF, tf):
            gv = _dot_nt(h, wg_ref[c0:c0 + tf, :])
            uv = _dot_nt(h, wu_ref[c0:c0 + tf, :])
            gs_ref[:, c0:c0 + tf] = gv.astype(BF)
            us_ref[:, c0:c0 + tf] = uv.astype(BF)
            a_ref[:, c0:c0 + tf] = (gv * _sigmoid(gv) * uv).astype(BF)
        o_ref[...] = xv + _dot(a_ref[...], wd_ref[...])

    wspec = pl.BlockSpec((FF, D), lambda i: (0, 0), pipeline_mode=pl.Buffered(1))
    row = pl.BlockSpec((tm, D), lambda i: (i, 0))
    wide = pl.BlockSpec((tm, FF), lambda i: (i, 0))
    return _ordered_call(
        body, deps, name=f"mlp_fwd_{l}", grid=(S // tm,),
        in_specs=[row, pl.BlockSpec((1, D), lambda i: (0, 0)), wspec, wspec, wspec],
        out_specs=[row, wide, wide, wide],
        out_shape=[jax.ShapeDtypeStruct((S, D), F32)] + [jax.ShapeDtypeStruct((S, FF), BF)] * 3,
        compiler_params=_params(1, 56),
    )(x2, g2, wgt, wut, wd, *deps)


def final_loss(x, gf, tgt):
    S = x.shape[0]
    tm = min(S, SEQ_TILE)

    def body(x_ref, g_ref, t_ref, dx_ref, loss_ref, dg_ref):
        @pl.when(pl.program_id(0) == 0)
        def _():
            loss_ref[...] = jnp.zeros((8, 128), F32)
            dg_ref[...] = jnp.zeros((1, D), F32)

        xv = x_ref[...]
        r = lax.rsqrt(_mean(xv * xv) + EPS)
        xh = xv * r
        diff = xh * g_ref[...] - t_ref[...]
        loss_ref[...] += jnp.sum(jnp.sum(diff * diff, axis=-1, keepdims=True), axis=0, keepdims=True)
        dy = diff * (1.0 / D)
        dg_ref[...] += jnp.sum(dy * xh, axis=0, keepdims=True)
        dxh = dy * g_ref[...]
        dx_ref[...] = r * (dxh - xh * _mean(dxh * xh))

    return pl.pallas_call(
        body, name="final_loss", grid=(S // tm,),
        in_specs=[pl.BlockSpec((tm, D), lambda i: (i, 0)),
                  pl.BlockSpec((1, D), lambda i: (0, 0)),
                  pl.BlockSpec((tm, D), lambda i: (i, 0))],
        out_specs=[pl.BlockSpec((tm, D), lambda i: (i, 0)),
                   pl.BlockSpec((8, 128), lambda i: (0, 0)),
                   pl.BlockSpec((1, D), lambda i: (0, 0))],
        out_shape=[jax.ShapeDtypeStruct((S, D), F32), jax.ShapeDtypeStruct((8, 128), F32),
                   jax.ShapeDtypeStruct((1, D), F32)],
        compiler_params=_params(1, 40),
    )(x, gf, tgt)


def mlp_bwd(dx3, x2, g2, gs, us, wgt, wut, wd, l, deps=()):
    S = x2.shape[0]
    tm, tf = min(S, MLP_ROWS), MLP_COLS

    def body(dx_ref, x_ref, g_ref, gs_ref, us_ref, wg_ref, wu_ref, wd_ref,
             dx2_ref, dg_ref, du_ref, h_ref, dgain_ref):
        @pl.when(pl.program_id(0) == 0)
        def _():
            dgain_ref[...] = jnp.zeros((1, D), F32)

        dxv = dx_ref[...]
        dxb = dxv.astype(BF)
        for c0 in range(0, FF, tf):
            da = _dot_nt(dxb, wd_ref[c0:c0 + tf, :])
            gv = gs_ref[:, c0:c0 + tf].astype(F32)
            uv = us_ref[:, c0:c0 + tf].astype(F32)
            sg = _sigmoid(gv)
            dg_ref[:, c0:c0 + tf] = (da * uv * (sg * (1.0 + gv * (1.0 - sg)))).astype(BF)
            du_ref[:, c0:c0 + tf] = (da * (gv * sg)).astype(BF)
        dh = _dot(dg_ref[...], wg_ref[...]) + _dot(du_ref[...], wu_ref[...])
        dx2, dgain, hb = _rms_bwd(x_ref[...], g_ref[...], dh, dxv)
        dx2_ref[...] = dx2
        dgain_ref[...] += dgain
        h_ref[...] = hb

    wspec = pl.BlockSpec((FF, D), lambda i: (0, 0), pipeline_mode=pl.Buffered(1))
    row = pl.BlockSpec((tm, D), lambda i: (i, 0))
    wide = pl.BlockSpec((tm, FF), lambda i: (i, 0))
    vec = pl.BlockSpec((1, D), lambda i: (0, 0))
    return _ordered_call(
        body, deps, name=f"mlp_bwd_{l}", grid=(S // tm,),
        in_specs=[row, row, vec, wide, wide, wspec, wspec, wspec],
        out_specs=[row, wide, wide, row, vec],
        out_shape=[jax.ShapeDtypeStruct((S, D), F32), jax.ShapeDtypeStruct((S, FF), BF),
                   jax.ShapeDtypeStruct((S, FF), BF), jax.ShapeDtypeStruct((S, D), BF),
                   jax.ShapeDtypeStruct((1, D), F32)],
        compiler_params=_params(1, 56),
    )(dx3, x2, g2, gs, us, wgt, wut, wd, *deps)


def wgrad(a, b, l, name, deps=()):
    S, K = a.shape
    N = b.shape[1]
    tk = 1408 if K == FF else min(K, 1024)
    tn = min(N, 1024)
    ts = min(S, WGRAD_ROWS)
    ns = S // ts

    def body(a_ref, b_ref, o_ref, acc):
        s = pl.program_id(2)

        @pl.when(s == 0)
        def _():
            acc[...] = jnp.zeros((tk, tn), F32)

        acc[...] += _dot_tn(a_ref[...], b_ref[...].astype(BF))

        @pl.when(s == ns - 1)
        def _():
            o_ref[...] = acc[...].astype(BF)

    return _ordered_call(
        body, deps, name=f"{name}_{l}", grid=(K // tk, N // tn, ns),
        in_specs=[pl.BlockSpec((ts, tk), lambda i, j, s: (s, i)),
                  pl.BlockSpec((ts, tn), lambda i, j, s: (s, j))],
        out_specs=pl.BlockSpec((tk, tn), lambda i, j, s: (i, j)),
        out_shape=jax.ShapeDtypeStruct((K, N), BF),
        scratch_shapes=[pltpu.VMEM((tk, tn), F32)],
        compiler_params=_params(3, 48),
    )(a, b, *deps)


def out_proj_bwd(dx2, wout, r_raw, proj, gn, u1, lg, lb, l, deps=()):
    S = dx2.shape[0]
    tm = min(S, SEQ_TILE)
    gate_blk = (2 * CW + 3 * RW) // RW

    def body(dx_ref, w_ref, r_ref, gate_ref, gn_ref, u1_ref, lg_ref, lb_ref,
             dgate_ref, dr_ref, du1_ref, sums_ref):
        @pl.when(pl.program_id(0) == 0)
        def _():
            sums_ref[...] = jnp.zeros((8, CW), F32)

        dmix = _dot_nt(dx_ref[...].astype(BF), w_ref[...])
        gt = gate_ref[...]
        sg = _sigmoid(gt)
        sil = gt * sg
        dsil = sg * (1.0 + gt * (1.0 - sg))
        for h in range(NH):
            cols = slice(h * HD, (h + 1) * HD)
            rh = r_ref[:, cols]
            d = rh - _mean(rh)
            rs = lax.rsqrt(_mean(d * d) + EPS)
            rn = d * rs
            drr = dmix[:, CW + h * HD:CW + (h + 1) * HD]
            gnh = gn_ref[:, cols]
            sums_ref[0:1, cols] += jnp.sum(drr * rn * sil[:, cols], axis=0, keepdims=True)
            dgate_ref[:, cols] = (drr * rn * gnh * dsil[:, cols]).astype(BF)
            drn = drr * gnh * sil[:, cols]
            dr_ref[:, cols] = (rs * (drn - _mean(drn) - rn * _mean(drn * rn))).astype(BF)
        du = dmix[:, 0:CW]
        u1 = u1_ref[...]
        d = u1 - _mean(u1)
        rs = lax.rsqrt(_mean(d * d) + EPS)
        xh = d * rs
        u2 = xh * lg_ref[...] + lb_ref[...]
        sg2 = _sigmoid(u2)
        du2 = du * (sg2 * (1.0 + u2 * (1.0 - sg2)))
        sums_ref[1:2, :] += jnp.sum(du2 * xh, axis=0, keepdims=True)
        sums_ref[2:3, :] += jnp.sum(du2, axis=0, keepdims=True)
        dxh = du2 * lg_ref[...]
        du1_ref[...] = rs * (dxh - _mean(dxh) - xh * _mean(dxh * xh))

    vec = pl.BlockSpec((1, CW), lambda i: (0, 0))
    half = pl.BlockSpec((tm, CW), lambda i: (i, 0))
    return _ordered_call(
        body, deps, name=f"out_proj_bwd_{l}", grid=(S // tm,),
        in_specs=[pl.BlockSpec((tm, D), lambda i: (i, 0)),
                  pl.BlockSpec((D, D), lambda i: (0, 0)),
                  half, pl.BlockSpec((tm, RW), lambda i: (i, gate_blk)), vec, half, vec, vec],
        out_specs=[half, half, half, pl.BlockSpec((8, CW), lambda i: (0, 0))],
        out_shape=[jax.ShapeDtypeStruct((S, RW), BF), jax.ShapeDtypeStruct((S, RW), BF),
                   jax.ShapeDtypeStruct((S, CW), F32), jax.ShapeDtypeStruct((8, CW), F32)],
        compiler_params=_params(1, 40),
    )(dx2, wout, r_raw, proj, gn, u1, lg, lb, *deps)


def _conv_bwd_fill(tc, du1_ref, dwb_ref, buf, pacc):
    i = pl.program_id(0)

    @pl.when(i == 0)
    def _():
        buf[tc:tc + HALO, :] = jnp.zeros((HALO, CW), F32)
        dwb_ref[...] = jnp.zeros((CK + 1, CW), F32)
        pacc[...] = jnp.zeros((CK, 8, CW), F32)

    @pl.when(i > 0)
    def _():
        buf[tc:tc + HALO, :] = buf[0:HALO, :]

    buf[0:tc, :] = du1_ref[...]


def _conv_bwd_rows(r0, groups, pacc, a_ref, b_ref, w_ref, dab_ref, buf, win):
    for c0 in range(0, CW, CONV_COLS):
        cols = slice(c0, c0 + CONV_COLS)
        av = a_ref[r0:r0 + CONV_ROWS, cols]
        sgb = _sigmoid(b_ref[r0:r0 + CONV_ROWS, cols])
        u0 = av * sgb
        acc = jnp.zeros((CONV_ROWS, CONV_COLS), F32)
        for start, length, taps in groups:
            win[0:length, :] = buf[r0 + start:r0 + start + length, cols]
            for j, at in taps:
                sl = win[at:at + CONV_ROWS, :]
                acc = acc + w_ref[CK - 1 - j:CK - j, cols] * sl
                pacc[CK - 1 - j, :, cols] += _fold8(u0 * sl)
        dab_ref[r0:r0 + CONV_ROWS, c0:c0 + CONV_COLS] = (acc * sgb).astype(BF)
        dab_ref[r0:r0 + CONV_ROWS, CW + c0:CW + c0 + CONV_COLS] = (acc * av * sgb * (1.0 - sgb)).astype(BF)


def _conv_bwd_finish(nt, pacc, du1_ref, dwb_ref):
    dwb_ref[CK:CK + 1, :] += jnp.sum(du1_ref[...], axis=0, keepdims=True)

    @pl.when(pl.program_id(0) == nt - 1)
    def _():
        for k in range(CK):
            dwb_ref[k:k + 1, :] = jnp.sum(pacc[k], axis=0, keepdims=True)


def _ret_bwd_chunk(c, q_ref, k_ref, v_ref, cos_ref, sin_ref, dm_ref, qd_ref, kd_ref, cd_ref, dr_ref, st_ref,
                   dq_ref, dk_ref, dv_ref, gst):
    scale = HD ** -0.5
    rows = slice(c * CHUNK, (c + 1) * CHUNK)
    cs, sn = cos_ref[rows, :], sin_ref[rows, :]
    for h in range(NH):
        cols = slice(h * HD, (h + 1) * HD)
        qr = _rot(q_ref[rows, cols], cs, sn)
        kr = _rot(k_ref[rows, cols], cs, sn) * scale
        qb, kb = qr.astype(BF), kr.astype(BF)
        vb = v_ref[rows, cols].astype(BF)
        dob = dr_ref[rows, cols]
        sb = st_ref[h, c]
        gn1 = gst[h]
        gb = gn1.astype(BF)
        sc = (_dot_nt(qb, kb) * dm_ref[h]).astype(BF)
        dsc = (_dot_nt(dob, vb) * dm_ref[h]).astype(BF)
        dqr = _dot(dsc, kb) + _dot_nt(dob, sb) * qd_ref[h]
        dkr = _dot_tn(dsc, qb) + _dot_nt(vb, gb) * kd_ref[h]
        dvv = _dot_tn(sc, dob) + _dot((kr * kd_ref[h]).astype(BF), gb)
        gst[h] = cd_ref[h] * gn1 + _dot_tn((qr * qd_ref[h]).astype(BF), dob)
        dq_ref[rows, cols] = _rot_t(dqr, cs, sn).astype(BF)
        dk_ref[rows, cols] = _rot_t(dkr * scale, cs, sn).astype(BF)
        dv_ref[rows, cols] = dvv.astype(BF)


def mix_bwd(du1, dr, proj, cw, states, tables, l, deps=()):
    S = proj.shape[0]
    tt = min(S, SEQ_TILE)
    cpb = tt // CHUNK
    nt = S // tt
    groups = _tap_groups(0, CK)

    def body(du1_ref, a_ref, b_ref, w_ref, q_ref, k_ref, v_ref, cos_ref, sin_ref, dm_ref, qd_ref, kd_ref, cd_ref,
             dr_ref, st_ref, dab_ref, dwb_ref, dq_ref, dk_ref, dv_ref, buf, win, gst, pacc):
        @pl.when(pl.program_id(0) == 0)
        def _():
            gst[...] = jnp.zeros((NH, HD, HD), F32)

        _conv_bwd_fill(tt, du1_ref, dwb_ref, buf, pacc)
        for c in reversed(range(cpb)):
            _ret_bwd_chunk(c, q_ref, k_ref, v_ref, cos_ref, sin_ref, dm_ref, qd_ref, kd_ref, cd_ref, dr_ref, st_ref,
                           dq_ref, dk_ref, dv_ref, gst)
            for r0 in range(c * CHUNK, (c + 1) * CHUNK, CONV_ROWS):
                _conv_bwd_rows(r0, groups, pacc, a_ref, b_ref, w_ref, dab_ref, buf, win)
        _conv_bwd_finish(nt, pacc, du1_ref, dwb_ref)

    rev = lambda t: nt - 1 - t
    half = pl.BlockSpec((tt, CW), lambda t: (rev(t), 0))
    return _ordered_call(
        body, deps, name=f"mix_bwd_{l}", grid=(nt,),
        in_specs=[half, half, pl.BlockSpec((tt, CW), lambda t: (rev(t), 1)), pl.BlockSpec((CK, CW), lambda t: (0, 0))]
        + _ret_specs(tt, rev) + [half, pl.BlockSpec((NH, cpb, HD, HD), lambda t: (0, rev(t), 0, 0))],
        out_specs=[pl.BlockSpec((tt, 2 * CW), lambda t: (rev(t), 0)), pl.BlockSpec((CK + 1, CW), lambda t: (0, 0)),
                   half, half, half],
        out_shape=[jax.ShapeDtypeStruct((S, 2 * CW), BF), jax.ShapeDtypeStruct((CK + 1, CW), F32)]
        + [jax.ShapeDtypeStruct((S, RW), BF)] * 3,
        scratch_shapes=[pltpu.VMEM((tt + HALO, CW), F32), pltpu.VMEM((HALO + CONV_ROWS, CONV_COLS), F32),
                        pltpu.VMEM((NH, HD, HD), F32), pltpu.VMEM((CK, 8, CW), F32)],
        compiler_params=_params(1, 40),
    )(du1, proj, proj, cw, proj, proj, proj, *tables, dr, states, *deps)


def in_proj_bwd(parts, win, x, g, dx2, l, deps=()):
    S = x.shape[0]
    tm = min(S, SEQ_TILE)
    n = len(parts)

    def body(*refs):
        srcs = refs[:n]
        w_ref, x_ref, g_ref, dx2_ref, dx_ref, dgain_ref = refs[n:]

        @pl.when(pl.program_id(0) == 0)
        def _():
            dgain_ref[...] = jnp.zeros((1, D), F32)

        dh, col = None, 0
        for r in srcs:
            width = r.shape[1]
            term = _dot_nt(r[...], w_ref[:, col:col + width])
            dh = term if dh is None else dh + term
            col += width
        dx, dgain, _ = _rms_bwd(x_ref[...], g_ref[...], dh, dx2_ref[...])
        dx_ref[...] = dx
        dgain_ref[...] += dgain

    row = pl.BlockSpec((tm, D), lambda i: (i, 0))
    vec = pl.BlockSpec((1, D), lambda i: (0, 0))
    return _ordered_call(
        body, deps, name=f"in_proj_bwd_{l}", grid=(S // tm,),
        in_specs=[pl.BlockSpec((tm, p.shape[1]), lambda i: (i, 0)) for p in parts]
        + [pl.BlockSpec((D, INW), lambda i: (0, 0)), row, vec, row],
        out_specs=[row, vec],
        out_shape=[jax.ShapeDtypeStruct((S, D), F32), jax.ShapeDtypeStruct((1, D), F32)],
        compiler_params=_params(1, 48),
    )(*parts, win, x, g, dx2, *deps)


def wgrad_in(h, parts, l):
    S = h.shape[0]
    ts = min(S, SEQ_TILE)
    ns = S // ts

    def body(*refs):
        h_ref, srcs = refs[0], refs[1:1 + len(parts)]
        o_ref, acc = refs[-2], refs[-1]
        s = pl.program_id(0)

        @pl.when(s == 0)
        def _():
            acc[...] = jnp.zeros((D, INW), F32)

        hv = h_ref[...]
        col = 0
        for r in srcs:
            width = r.shape[1]
            acc[:, col:col + width] += _dot_tn(hv, r[...])
            col += width

        @pl.when(s == ns - 1)
        def _():
            o_ref[...] = acc[...].astype(BF)

    return pl.pallas_call(
        body, name=f"wgrad_in_{l}", grid=(ns,),
        in_specs=[pl.BlockSpec((ts, D), lambda s: (s, 0))]
        + [pl.BlockSpec((ts, p.shape[1]), lambda s: (s, 0)) for p in parts],
        out_specs=pl.BlockSpec((D, INW), lambda s: (0, 0)),
        out_shape=jax.ShapeDtypeStruct((D, INW), BF),
        scratch_shapes=[pltpu.VMEM((D, INW), F32)],
        compiler_params=_params(1, 48),
    )(h, *parts)


def sum_slots(recv, name):
    _, R, C = recv.shape
    tr = 256 if R % 256 == 0 else R

    def body(r_ref, o_ref):
        acc = r_ref[0].astype(F32)
        for k in range(1, NCHIP):
            acc = acc + r_ref[k].astype(F32)
        o_ref[...] = acc

    return pl.pallas_call(
        body, name=name, grid=(R // tr,),
        in_specs=[pl.BlockSpec((NCHIP, tr, C), lambda i: (0, i, 0))],
        out_specs=pl.BlockSpec((tr, C), lambda i: (i, 0)),
        out_shape=jax.ShapeDtypeStruct((R, C), F32),
        compiler_params=_params(1, 32),
    )(recv)


def adamw(w, ga, gb, m, v, name):
    R, C = w.shape
    tr = 256 if R % 256 == 0 else R
    c1 = 1.0 - ADAM_B1 ** ADAM_STEP
    c2 = 1.0 - ADAM_B2 ** ADAM_STEP

    def body(w_ref, ga_ref, gb_ref, m_ref, v_ref, g_out, d_out, m_out, v_out):
        g = ga_ref[...] + gb_ref[...]
        mn = ADAM_B1 * m_ref[...] + (1.0 - ADAM_B1) * g
        vn = ADAM_B2 * v_ref[...] + (1.0 - ADAM_B2) * (g * g)
        g_out[...] = g
        m_out[...] = mn
        v_out[...] = vn
        d_out[...] = -ADAM_LR * ((mn / c1) / (jnp.sqrt(vn / c2) + ADAM_EPS) + ADAM_WD * w_ref[...])

    blk = pl.BlockSpec((tr, C), lambda i: (i, 0))
    return pl.pallas_call(
        body, name=name, grid=(R // tr,),
        in_specs=[blk] * 5, out_specs=[blk] * 4,
        out_shape=[jax.ShapeDtypeStruct((R, C), F32)] * 4,
        compiler_params=_params(1, 40),
    )(w, ga, gb, m, v)


def _place():
    x, y, c = lax.axis_index("x"), lax.axis_index("y"), lax.axis_index("c")
    chips = [(1 - x, y), (x, 1 - y), (1 - x, 1 - y)]
    return x, y, c, chips


def _window(ref, axis, j, size):
    idx = [slice(None)] * len(ref.shape)
    idx[axis] = pl.ds(pl.multiple_of(j * size, 128 if axis == len(ref.shape) - 1 else 16), size)
    return ref.at[tuple(idx)]


def _hbm(a):
    return pltpu.with_memory_space_constraint(a, pltpu.HBM)


def _hbm_like(arrs):
    return [pltpu.HBM(a.shape, a.dtype) for a in arrs]


def gather_start(shards, axes, after, tag):
    n = len(shards)
    na = len(after)
    lands = []
    for s, ax in zip(shards, axes):
        shp = list(s.shape)
        shp[ax] *= NCHIP
        lands.append(lax.empty(tuple(shp), s.dtype))

    def body(*refs):
        ins, land = refs[:n], refs[n:2 * n]
        send, recv = refs[2 * n + na], refs[2 * n + na + 1]
        token = refs[-1]
        x, y, c, chips = _place()
        for a in range(n):
            for k, chip in enumerate(chips):
                pltpu.make_async_remote_copy(
                    src_ref=ins[a], dst_ref=_window(land[a], axes[a], 2 * x + y, ins[a].shape[axes[a]]),
                    send_sem=send.at[3 * a + k], recv_sem=recv.at[3 * a + k],
                    device_id=(chip[0], chip[1], c), device_id_type=MESH).start()
        token[...] = jnp.zeros_like(token)

    outs = pl.pallas_call(
        body, name=f"gather_start_{tag}",
        in_specs=[HBM_SPEC] * (2 * n) + [ANY] * na,
        out_specs=(SEM_SPEC, SEM_SPEC, *[HBM_SPEC] * (2 * n), VMEM_SPEC),
        out_shape=(pltpu.SemaphoreType.DMA((3 * n,)), pltpu.SemaphoreType.DMA((3 * n,)),
                   *_hbm_like(shards), *_hbm_like(lands), jax.ShapeDtypeStruct((8, 128), F32)),
        input_output_aliases={a: 2 + a for a in range(2 * n)},
        compiler_params=pltpu.CompilerParams(has_side_effects=DATAFLOW),
    )(*[_hbm(s) for s in shards], *[_hbm(b) for b in lands], *after)
    return (outs[0], outs[1], list(outs[2:2 + n]), list(outs[2 + n:2 + 2 * n]), list(axes)), outs[-1]


def gather_wait(groups, after, tag):
    sizes = [len(g[2]) for g in groups]
    total = sum(sizes)

    def body(*refs):
        x, y, c, chips = _place()
        stage, loc = refs[-1 - total:-1], refs[-1]
        pos = 2 * total
        off = 0
        mine = []
        for g, n in zip(groups, sizes):
            ins, land = refs[off:off + n], refs[total + off:total + off + n]
            send_ref, recv_ref = refs[pos], refs[pos + 1]
            axes = g[4]
            for a in range(n):
                fetch = pltpu.make_async_copy(ins[a], stage[off + a], loc.at[2 * (off + a)])
                fetch.start()
                put = pltpu.make_async_copy(
                    stage[off + a], _window(land[a], axes[a], 2 * x + y, ins[a].shape[axes[a]]),
                    loc.at[2 * (off + a) + 1])
                mine.append((fetch, put))
                for k, chip in enumerate(chips):
                    cp = pltpu.make_async_remote_copy(
                        src_ref=ins[a],
                        dst_ref=_window(land[a], axes[a], 2 * chip[0] + chip[1], ins[a].shape[axes[a]]),
                        send_sem=send_ref.at[3 * a + k], recv_sem=recv_ref.at[3 * a + k],
                        device_id=(chip[0], chip[1], c), device_id_type=MESH)
                    cp.wait_send()
                    cp.wait_recv()
            pos += 2
            off += n
        for fetch, put in mine:
            fetch.wait()
            put.start()
        for fetch, put in mine:
            put.wait()

    shards = [s for g in groups for s in g[2]]
    lands = [b for g in groups for b in g[3]]
    sems = [s for g in groups for s in (g[0], g[1])]
    outs = pl.pallas_call(
        body, name=f"gather_wait_{tag}",
        in_specs=[HBM_SPEC] * (2 * total) + [SEM_SPEC] * len(sems) + [ANY],
        out_specs=[HBM_SPEC] * (2 * total),
        out_shape=(*_hbm_like(shards), *_hbm_like(lands)),
        input_output_aliases={a: a for a in range(2 * total)},
        scratch_shapes=[pltpu.VMEM(s.shape, s.dtype) for s in shards] + [pltpu.SemaphoreType.DMA((2 * total,))],
        compiler_params=pltpu.CompilerParams(has_side_effects=DATAFLOW, vmem_limit_bytes=32 << 20),
    )(*shards, *lands, *sems, after)
    return list(outs[total:])


def scatter_start(grads, axes, sizes, lands, l, tag):
    n = len(grads)

    def body(*refs):
        ins, land = refs[:n], refs[n:2 * n]
        send, recv = refs[2 * n], refs[2 * n + 1]
        token = refs[2 * n + 2 + 2 * n]
        stage, loc = refs[-1 - n:-1], refs[-1]
        x, y, c, chips = _place()
        me = 2 * x + y
        fetches = [pltpu.make_async_copy(_window(ins[a], axes[a], me, sizes[a]), stage[a], loc.at[2 * a])
                   for a in range(n)]
        for cp in fetches:
            cp.start()
        for a in range(n):
            for k, chip in enumerate(chips):
                pltpu.make_async_remote_copy(
                    src_ref=_window(ins[a], axes[a], 2 * chip[0] + chip[1], sizes[a]), dst_ref=land[a].at[me, l],
                    send_sem=send.at[3 * a + k], recv_sem=recv.at[3 * a + k],
                    device_id=(chip[0], chip[1], c), device_id_type=MESH).start()
        puts = [pltpu.make_async_copy(stage[a], land[a].at[me, l], loc.at[2 * a + 1]) for a in range(n)]
        for fetch, put in zip(fetches, puts):
            fetch.wait()
            put.start()
        for put in puts:
            put.wait()
        token[...] = jnp.zeros_like(token)

    outs = pl.pallas_call(
        body, name=f"scatter_start_{tag}",
        in_specs=[HBM_SPEC] * (2 * n),
        out_specs=(SEM_SPEC, SEM_SPEC, *[HBM_SPEC] * (2 * n), VMEM_SPEC),
        out_shape=(pltpu.SemaphoreType.DMA((3 * n,)), pltpu.SemaphoreType.DMA((3 * n,)),
                   *_hbm_like(grads), *_hbm_like(lands), jax.ShapeDtypeStruct((8, 128), F32)),
        input_output_aliases={a: 2 + a for a in range(2 * n)},
        scratch_shapes=[pltpu.VMEM(b.shape[2:], b.dtype) for b in lands] + [pltpu.SemaphoreType.DMA((2 * n,))],
        compiler_params=pltpu.CompilerParams(has_side_effects=DATAFLOW, vmem_limit_bytes=32 << 20),
    )(*[_hbm(g) for g in grads], *[_hbm(b) for b in lands])
    group = (outs[0], outs[1], list(outs[2:2 + n]), list(axes), list(sizes), l)
    return group, list(outs[2 + n:2 + 2 * n]), outs[-1]


def scatter_wait(groups, lands, which):
    nl = len(lands)

    def body(*refs):
        land = refs[:nl]
        x, y, c, chips = _place()
        pos = nl
        for g, wh in zip(groups, which):
            n = len(g[2])
            ins = refs[pos:pos + n]
            send_ref, recv_ref = refs[pos + n], refs[pos + n + 1]
            axes, sizes, l = g[3], g[4], g[5]
            for a in range(n):
                for k, chip in enumerate(chips):
                    jp = 2 * chip[0] + chip[1]
                    cp = pltpu.make_async_remote_copy(
                        src_ref=_window(ins[a], axes[a], jp, sizes[a]), dst_ref=land[wh[a]].at[jp, l],
                        send_sem=send_ref.at[3 * a + k], recv_sem=recv_ref.at[3 * a + k],
                        device_id=(chip[0], chip[1], c), device_id_type=MESH)
                    cp.wait_send()
                    cp.wait_recv()
            pos += n + 2

    operands = list(lands)
    specs = [HBM_SPEC] * nl
    for g in groups:
        operands += list(g[2]) + [g[0], g[1]]
        specs += [HBM_SPEC] * len(g[2]) + [SEM_SPEC, SEM_SPEC]
    outs = pl.pallas_call(
        body, name="scatter_wait", in_specs=specs, out_specs=[HBM_SPEC] * nl, out_shape=tuple(_hbm_like(lands)),
        input_output_aliases={a: a for a in range(nl)},
        compiler_params=pltpu.CompilerParams(has_side_effects=DATAFLOW),
    )(*operands)
    return list(outs)


def sibling_swap(parts):
    n = len(parts)

    def body(*refs):
        ins, outs = refs[:n], refs[n:2 * n]
        send, recv = refs[2 * n:]
        x, y, c, _ = _place()
        cps = [pltpu.make_async_remote_copy(src_ref=ins[a], dst_ref=outs[a], send_sem=send.at[a],
                                            recv_sem=recv.at[a], device_id=(x, y, 1 - c), device_id_type=MESH)
               for a in range(n)]
        for cp in cps:
            cp.start()
        for cp in cps:
            cp.wait_recv()
        for cp in cps:
            cp.wait_send()

    return pl.pallas_call(
        body, name="sibling_swap", in_specs=[ANY] * n, out_specs=[ANY] * n,
        out_shape=[jax.ShapeDtypeStruct(p.shape, p.dtype) for p in parts],
        scratch_shapes=[pltpu.SemaphoreType.DMA((n,)), pltpu.SemaphoreType.DMA((n,))],
    )(*parts)


def small_allreduce(p):
    R, C = p.shape
    ndev = 8

    def body(p_ref, o_ref, buf, send, recv):
        x, y, c, _ = _place()
        me = 4 * x + 2 * y + c
        buf[me] = p_ref[...]

        def peer(d):
            px = 1 - x if d & 4 else x
            py = 1 - y if d & 2 else y
            pc = 1 - c if d & 1 else c
            return px, py, pc

        def copy(d, slot):
            return pltpu.make_async_remote_copy(src_ref=p_ref, dst_ref=buf.at[slot], send_sem=send.at[d - 1],
                                                recv_sem=recv.at[d - 1], device_id=peer(d), device_id_type=MESH)

        sends = [copy(d, me) for d in range(1, ndev)]
        for cp in sends:
            cp.start()
        for d in range(1, ndev):
            px, py, pc = peer(d)
            copy(d, 4 * px + 2 * py + pc).wait_recv()
        for cp in sends:
            cp.wait_send()
        acc = buf[0]
        for k in range(1, ndev):
            acc = acc + buf[k]
        o_ref[...] = acc

    return pl.pallas_call(
        body, name="small_allreduce", in_specs=[VMEM_SPEC], out_specs=VMEM_SPEC,
        out_shape=jax.ShapeDtypeStruct((R, C), F32),
        scratch_shapes=[pltpu.VMEM((ndev, R, C), F32), pltpu.SemaphoreType.DMA((ndev - 1,)),
                        pltpu.SemaphoreType.DMA((ndev - 1,))],
        compiler_params=pltpu.CompilerParams(vmem_limit_bytes=32 << 20),
    )(p)


def kernel(x, norm1_g, w_in, conv_w, conv_b, conv_ln_g, conv_ln_b, ret_gn_g, w_out, norm2_g, w_gate, w_up, w_down, final_g, loss_target, m_norm1_g, m_w_in, m_conv_w, m_conv_b, m_conv_ln_g, m_conv_ln_b, m_ret_gn_g, m_w_out, m_norm2_g, m_w_gate, m_w_up, m_w_down, m_final_g, v_norm1_g, v_w_in, v_conv_w, v_conv_b, v_conv_ln_g, v_conv_ln_b, v_ret_gn_g, v_w_out, v_norm2_g, v_w_gate, v_w_up, v_w_down, v_final_g):
    S = x.shape[1]
    xs = x.reshape(S, D)
    tgt = loss_target.reshape(S, D)
    fsh = FF // NCHIP

    def shards_of(l):
        return [w_in[l].astype(BF), w_out[l].astype(BF), w_gate[l].T.astype(BF), w_up[l].T.astype(BF),
                w_down[l].astype(BF), conv_w[l]]

    gather_axes = [1, 0, 0, 0, 0, 1]
    shard_cache = [shards_of(l) for l in range(L)]
    tables = _ret_tables(S)
    row = lambda a, l: a[l].reshape(1, -1)
    groups = {}
    weights = [dict() for _ in range(L)]

    def begin(l, which, after):
        group, token = gather_start([shard_cache[l][i] for i in which], [gather_axes[i] for i in which],
                                    after, f"{l}_{which[0]}")
        groups[(l, which[0])] = (group, which)
        return token

    def finish(l, firsts, after, tag):
        gs = [groups[(l, f)] for f in firsts]
        outs = gather_wait([g for g, _ in gs], after, f"{l}_{tag}")
        k = 0
        for _, which in gs:
            for i in which:
                weights[l][i] = outs[k]
                k += 1

    first = begin(0, [0], [])
    second = begin(0, [1, 5], [first])
    after = begin(0, [2, 3, 4], [second])
    saved = []
    xc = xs
    for l in range(L):
        if l == 0:
            finish(l, [0], after, "a")
        else:
            finish(l, [0], after, "all")
        win = weights[l][0]
        proj, h1 = in_proj(xc, row(norm1_g, l), win, l)
        if l == 0:
            finish(l, [1], proj, "b")
        wout, cw = weights[l][1], weights[l][5]
        ahead = [begin(l + 1, [0, 1, 2, 3, 4, 5], [proj])] if l + 1 < L else []
        u1, u, r_raw, states = mix_fwd(proj, cw, row(conv_b, l), row(conv_ln_g, l), row(conv_ln_b, l), tables, l, ahead)
        x2, mixed = out_proj(u, r_raw, proj, row(ret_gn_g, l), wout, xc, l)
        if l == 0:
            finish(l, [2], x2, "c")
        wgt, wut, wd = weights[l][2], weights[l][3], weights[l][4]
        x3, gs, us, act = mlp_fwd(x2, row(norm2_g, l), wgt, wut, wd, l)
        saved.append((xc, proj, h1, u1, r_raw, states, mixed, x2, gs, us, act))
        xc = x3
        after = x3

    dx, loss_acc, d_final = final_loss(xc, final_g.reshape(1, D), tgt)
    loss = lax.psum(loss_acc[0, 0] * (0.5 / D), ("x", "y", "c"))

    scatter_axes = [1, 0, 0, 0, 0]
    scatter_sizes = [INW // NCHIP, D // NCHIP, fsh, fsh, fsh]
    lands = [lax.empty((NCHIP, L, D, INW // NCHIP), BF), lax.empty((NCHIP, L, D // NCHIP, D), BF),
             lax.empty((NCHIP, L, fsh, D), BF), lax.empty((NCHIP, L, fsh, D), BF), lax.empty((NCHIP, L, fsh, D), BF)]
    sent, sent_which = [], []

    def send_grad(g, a, l):
        group, new_land, token = scatter_start([g], [scatter_axes[a]], [scatter_sizes[a]], [lands[a]], l, f"{l}_{a}")
        lands[a] = new_land[0]
        sent.append(group)
        sent_which.append([a])
        return [token]

    small = [None] * L
    for l in reversed(range(L)):
        xin, proj, h1, u1, r_raw, states, mixed, x2, gs, us, act = saved[l]
        win, wout, wgt, wut, wd, cw = (weights[l][i] for i in range(6))
        dx2, dgs, dus, h2, d_n2 = mlp_bwd(dx, x2, row(norm2_g, l), gs, us, wgt, wut, wd, l)
        g_wd = wgrad(act, dx, l, "wgrad_down")
        g_wgt = wgrad(dgs, h2, l, "wgrad_gate", send_grad(g_wd, 4, l))
        g_wut = wgrad(dus, h2, l, "wgrad_up", send_grad(g_wgt, 2, l))
        dgate, dr, du1, sums = out_proj_bwd(dx2, wout, r_raw, proj, row(ret_gn_g, l), u1,
                                            row(conv_ln_g, l), row(conv_ln_b, l), l, send_grad(g_wut, 3, l))
        g_wout = wgrad(mixed, dx2, l, "wgrad_out")
        dab, dwb, dq, dk, dv = mix_bwd(du1, dr, proj, cw, states, tables, l, send_grad(g_wout, 1, l))
        dproj = [dab, dq, dk, dv, dgate]
        g_win = wgrad_in(h1, dproj, l)
        dx, d_n1 = in_proj_bwd(dproj, win, xin, row(norm1_g, l), dx2, l, send_grad(g_win, 0, l))
        small[l] = jnp.concatenate([dwb, sums, d_n1.reshape(2, CW), d_n2.reshape(2, CW)], axis=0)
    grad_x = dx.reshape(1, S, D)

    per = CK + 1 + 8 + 4
    packed = jnp.concatenate(small + [d_final.reshape(2, CW), jnp.zeros((6, CW), F32)], axis=0)
    tot = small_allreduce(packed)
    lay = tot[:L * per].reshape(L, per, CW)
    g_conv_w_full = lay[:, 0:CK, :]
    j = 2 * lax.axis_index("x") + lax.axis_index("y")
    g_conv_w = lax.dynamic_slice_in_dim(g_conv_w_full, j * (CW // NCHIP), CW // NCHIP, axis=2)
    g_small = {
        "conv_b": lay[:, CK, :], "ret_gn_g": lay[:, CK + 1, :], "conv_ln_g": lay[:, CK + 2, :],
        "conv_ln_b": lay[:, CK + 3, :], "norm1_g": lay[:, CK + 9:CK + 11, :].reshape(L, D),
        "norm2_g": lay[:, CK + 11:CK + 13, :].reshape(L, D), "final_g": tot[L * per:L * per + 2].reshape(D),
    }

    recv = scatter_wait(sent, lands, sent_which)
    shard_shapes = [(L * D, INW // NCHIP), (L * D // NCHIP, D), (L * fsh, D), (L * fsh, D), (L * fsh, D)]
    names = ["w_in", "w_out", "w_gate", "w_up", "w_down"]
    parts = [sum_slots(r.reshape((NCHIP,) + shp), f"sum_{nm}") for r, shp, nm in zip(recv, shard_shapes, names)]
    theirs = sibling_swap(parts)

    def unT(a):
        return jnp.swapaxes(a.reshape(L, fsh, D), 1, 2).reshape(L * D, fsh)

    big = {}
    wmv = {"w_in": (w_in, m_w_in, v_w_in), "w_out": (w_out, m_w_out, v_w_out),
           "w_gate": (w_gate, m_w_gate, v_w_gate), "w_up": (w_up, m_w_up, v_w_up),
           "w_down": (w_down, m_w_down, v_w_down)}
    for nm, mine, other in zip(names, parts, theirs):
        w, m, v = wmv[nm]
        if nm in ("w_gate", "w_up"):
            mine, other = unT(mine), unT(other)
        shp2 = (w.shape[0] * w.shape[1], w.shape[2])
        outs = adamw(w.reshape(shp2), mine, other, m.reshape(shp2), v.reshape(shp2), f"adamw_{nm}")
        big[nm] = [o.reshape(w.shape) for o in outs]

    cshape = (L * CK, CW // NCHIP)
    zc = jnp.zeros(cshape, F32)
    big["conv_w"] = [o.reshape(conv_w.shape) for o in adamw(
        conv_w.reshape(cshape), g_conv_w.reshape(cshape), zc, m_conv_w.reshape(cshape),
        v_conv_w.reshape(cshape), "adamw_conv_w")]
    vec_names = ["norm1_g", "conv_b", "conv_ln_g", "conv_ln_b", "ret_gn_g", "norm2_g", "final_g"]
    vec_w = {"norm1_g": (norm1_g, m_norm1_g, v_norm1_g), "conv_b": (conv_b, m_conv_b, v_conv_b),
             "conv_ln_g": (conv_ln_g, m_conv_ln_g, v_conv_ln_g), "conv_ln_b": (conv_ln_b, m_conv_ln_b, v_conv_ln_b),
             "ret_gn_g": (ret_gn_g, m_ret_gn_g, v_ret_gn_g), "norm2_g": (norm2_g, m_norm2_g, v_norm2_g),
             "final_g": (final_g, m_final_g, v_final_g)}
    cat = lambda arrs: jnp.concatenate([a.reshape(-1, CW) for a in arrs], axis=0)
    vw = cat([vec_w[nm][0] for nm in vec_names])
    vm = cat([vec_w[nm][1] for nm in vec_names])
    vv = cat([vec_w[nm][2] for nm in vec_names])
    vg = cat([g_small[nm] for nm in vec_names])
    vouts = adamw(vw, vg, jnp.zeros_like(vg), vm, vv, "adamw_vectors")
    off = 0
    for nm in vec_names:
        w = vec_w[nm][0]
        nrow = w.size // CW
        big[nm] = [o[off:off + nrow].reshape(w.shape) for o in vouts]
        off += nrow

    order = ["norm1_g", "w_in", "conv_w", "conv_b", "conv_ln_g", "conv_ln_b", "ret_gn_g", "w_out", "norm2_g",
             "w_gate", "w_up", "w_down", "final_g"]
    return (loss, grad_x, *[big[nm][0] for nm in order], *[big[nm][1] for nm in order],
            *[big[nm][2] for nm in order], *[big[nm][3] for nm in order])
```

```python
import math

import jax
import jax.numpy as jnp
from jax import lax
from jax.experimental import pallas as pl
from jax.experimental.pallas import tpu as pltpu

D = 1024
L = 4
CW = 512
RW = 512
NH = 4
HD = 128
CK = 31
CHUNK = 64
INW = 3072
FF = 2816
NCHIP = 4
EPS = 1e-6
ROPE_BASE = 10000.0
SEQ_TILE = 512
WGRAD_ROWS = 1024
MLP_ROWS = 256
MLP_COLS = 1408
HALO = 32
CONV_ROWS = 32
CONV_COLS = 256

ADAM_LR = 0.001
ADAM_B1 = 0.9
ADAM_B2 = 0.999
ADAM_EPS = 1e-08
ADAM_WD = 0.01
ADAM_STEP = 10

BF = jnp.bfloat16
F32 = jnp.float32
MESH = pl.DeviceIdType.MESH
ANY = pl.BlockSpec(memory_space=pl.ANY)
VMEM_SPEC = pl.BlockSpec(memory_space=pltpu.VMEM)
HBM_SPEC = pl.BlockSpec(memory_space=pltpu.HBM)
SEM_SPEC = pl.BlockSpec(memory_space=pltpu.SEMAPHORE)
DATAFLOW = pltpu.SideEffectType.DATAFLOW_SIDE_EFFECTING


def _params(n_grid, vmem_mb):
    return pltpu.CompilerParams(dimension_semantics=("arbitrary",) * n_grid,
                                vmem_limit_bytes=vmem_mb << 20)


def _ordered_call(body, deps, *, in_specs, **kw):
    n, nd = len(in_specs), len(deps)

    def with_deps(*refs):
        body(*refs[:n], *refs[n + nd:])

    return pl.pallas_call(with_deps, in_specs=list(in_specs) + [ANY] * nd, **kw)


def _dot(a, b):
    return jnp.dot(a, b, preferred_element_type=F32)


def _dot_nt(a, b):
    return lax.dot_general(a, b, (((1,), (1,)), ((), ())), preferred_element_type=F32)


def _dot_tn(a, b):
    return lax.dot_general(a, b, (((0,), (0,)), ((), ())), preferred_element_type=F32)


def _sigmoid(x):
    return 0.5 * jnp.tanh(0.5 * x) + 0.5


def _mean(x):
    return jnp.mean(x, axis=-1, keepdims=True)


def _fold8(x):
    out = x[0:8, :]
    for q in range(1, x.shape[0] // 8):
        out = out + x[8 * q:8 * q + 8, :]
    return out


def _tap_groups(first, count):
    groups = []
    for phase in range(8):
        taps = [(t, first + t - phase) for t in range(count) if (first + t) % 8 == phase]
        if taps:
            lo, hi = min(q for _, q in taps), max(q for _, q in taps)
            groups.append((lo + phase, hi - lo + CONV_ROWS, [(t, q - lo) for t, q in taps]))
    return groups


def _rot(t, cs, sn):
    return t * cs + pltpu.roll(t, HD // 2, 1) * sn


def _rot_t(dy, cs, sn):
    return dy * cs + pltpu.roll(dy * sn, HD // 2, 1)


def _rms_bwd(x, g, dh, dx_in):
    r = lax.rsqrt(_mean(x * x) + EPS)
    xh = x * r
    dxh = dh * g
    dx = dx_in + r * (dxh - xh * _mean(dxh * xh))
    return dx, jnp.sum(dh * xh, axis=0, keepdims=True), (xh * g).astype(BF)


def in_proj(x, g, win, l, deps=()):
    S = x.shape[0]
    tm = min(S, SEQ_TILE)

    def body(x_ref, g_ref, w_ref, o_ref, h_ref):
        xv = x_ref[...]
        h = (xv * lax.rsqrt(_mean(xv * xv) + EPS) * g_ref[...]).astype(BF)
        h_ref[...] = h
        o_ref[...] = _dot(h, w_ref[...])

    return _ordered_call(
        body, deps, name=f"in_proj_{l}", grid=(S // tm,),
        in_specs=[pl.BlockSpec((tm, D), lambda i: (i, 0)),
                  pl.BlockSpec((1, D), lambda i: (0, 0)),
                  pl.BlockSpec((D, INW), lambda i: (0, 0))],
        out_specs=[pl.BlockSpec((tm, INW), lambda i: (i, 0)), pl.BlockSpec((tm, D), lambda i: (i, 0))],
        out_shape=[jax.ShapeDtypeStruct((S, INW), F32), jax.ShapeDtypeStruct((S, D), BF)],
        compiler_params=_params(1, 48),
    )(x, g, win, *deps)


def _conv_fwd_fill(tc, a_ref, b_ref, buf):
    i = pl.program_id(0)

    @pl.when(i == 0)
    def _():
        buf[0:HALO, :] = jnp.zeros((HALO, CW), F32)

    @pl.when(i > 0)
    def _():
        buf[0:HALO, :] = buf[tc:tc + HALO, :]

    buf[HALO:HALO + tc, :] = a_ref[...] * _sigmoid(b_ref[...])


def _conv_fwd_rows(r0, groups, w_ref, cb_ref, lg_ref, lb_ref, u1_ref, u_ref, buf, win):
    for c0 in range(0, CW, CONV_COLS):
        cols = slice(c0, c0 + CONV_COLS)
        acc = jnp.broadcast_to(cb_ref[:, cols], (CONV_ROWS, CONV_COLS))
        for start, length, taps in groups:
            win[0:length, :] = buf[r0 + start:r0 + start + length, cols]
            for k, at in taps:
                acc = acc + w_ref[k:k + 1, cols] * win[at:at + CONV_ROWS, :]
        u1_ref[r0:r0 + CONV_ROWS, cols] = acc
    acc = u1_ref[r0:r0 + CONV_ROWS, :]
    d = acc - _mean(acc)
    u2 = d * lax.rsqrt(_mean(d * d) + EPS) * lg_ref[...] + lb_ref[...]
    u_ref[r0:r0 + CONV_ROWS, :] = (u2 * _sigmoid(u2)).astype(BF)


def _ret_tables(S):
    half = HD // 2
    pos = jnp.arange(S, dtype=F32)
    freqs = ROPE_BASE ** (-jnp.arange(half, dtype=F32) / half)
    ang = pos[:, None] * freqs[None, :]
    cos, sin = jnp.cos(ang), jnp.sin(ang)
    cosf = jnp.concatenate([cos, cos], axis=-1)
    sinf = jnp.concatenate([-sin, sin], axis=-1)
    log_g = jnp.log(1.0 - 2.0 ** (-5.0 - jnp.arange(NH, dtype=F32)))
    idx = jnp.arange(CHUNK, dtype=F32)
    dmat = jnp.exp(log_g[:, None, None] * jnp.abs(idx[:, None] - idx[None, :]))
    qdec = jnp.broadcast_to(jnp.exp(log_g[:, None] * (idx + 1.0))[:, :, None], (NH, CHUNK, HD))
    kdec = jnp.broadcast_to(jnp.exp(log_g[:, None] * (CHUNK - 1 - idx))[:, :, None], (NH, CHUNK, HD))
    cdec = jnp.broadcast_to(jnp.exp(log_g * CHUNK)[:, None, None], (NH, HD, HD))
    return cosf, sinf, dmat, qdec, kdec, cdec


def _ret_specs(tr, tmap):
    q0 = (2 * CW) // RW
    return [pl.BlockSpec((tr, RW), lambda t: (tmap(t), q0)),
            pl.BlockSpec((tr, RW), lambda t: (tmap(t), q0 + 1)),
            pl.BlockSpec((tr, RW), lambda t: (tmap(t), q0 + 2)),
            pl.BlockSpec((tr, HD), lambda t: (tmap(t), 0)),
            pl.BlockSpec((tr, HD), lambda t: (tmap(t), 0)),
            pl.BlockSpec((NH, CHUNK, CHUNK), lambda t: (0, 0, 0)),
            pl.BlockSpec((NH, CHUNK, HD), lambda t: (0, 0, 0)),
            pl.BlockSpec((NH, CHUNK, HD), lambda t: (0, 0, 0)),
            pl.BlockSpec((NH, HD, HD), lambda t: (0, 0, 0))]


def _ret_fwd_chunk(c, q_ref, k_ref, v_ref, cos_ref, sin_ref, dm_ref, qd_ref, kd_ref, cd_ref, r_ref, st_ref, st):
    scale = HD ** -0.5
    rows = slice(c * CHUNK, (c + 1) * CHUNK)
    cs, sn = cos_ref[rows, :], sin_ref[rows, :]
    for h in range(NH):
        cols = slice(h * HD, (h + 1) * HD)
        qr = _rot(q_ref[rows, cols], cs, sn)
        kr = _rot(k_ref[rows, cols], cs, sn) * scale
        vb = v_ref[rows, cols].astype(BF)
        s = st[h]
        sb = s.astype(BF)
        st_ref[h, c] = sb
        sc = _dot_nt(qr.astype(BF), kr.astype(BF)) * dm_ref[h]
        r_ref[rows, cols] = _dot(sc.astype(BF), vb) + _dot((qr * qd_ref[h]).astype(BF), sb)
        st[h] = cd_ref[h] * s + _dot_tn((kr * kd_ref[h]).astype(BF), vb)


def mix_fwd(proj, cw, cb, lg, lb, tables, l, deps=()):
    S = proj.shape[0]
    tt = min(S, SEQ_TILE)
    cpb = tt // CHUNK
    groups = _tap_groups(HALO - (CK - 1), CK)

    def body(a_ref, b_ref, w_ref, cb_ref, lg_ref, lb_ref, q_ref, k_ref, v_ref, cos_ref, sin_ref,
             dm_ref, qd_ref, kd_ref, cd_ref, u1_ref, u_ref, r_ref, st_ref, buf, win, st):
        @pl.when(pl.program_id(0) == 0)
        def _():
            st[...] = jnp.zeros((NH, HD, HD), F32)

        _conv_fwd_fill(tt, a_ref, b_ref, buf)
        for c in range(cpb):
            _ret_fwd_chunk(c, q_ref, k_ref, v_ref, cos_ref, sin_ref, dm_ref, qd_ref, kd_ref, cd_ref,
                           r_ref, st_ref, st)
            for r0 in range(c * CHUNK, (c + 1) * CHUNK, CONV_ROWS):
                _conv_fwd_rows(r0, groups, w_ref, cb_ref, lg_ref, lb_ref, u1_ref, u_ref, buf, win)

    vec = pl.BlockSpec((1, CW), lambda t: (0, 0))
    half = pl.BlockSpec((tt, CW), lambda t: (t, 0))
    return _ordered_call(
        body, deps, name=f"mix_fwd_{l}", grid=(S // tt,),
        in_specs=[half, pl.BlockSpec((tt, CW), lambda t: (t, 1)),
                  pl.BlockSpec((CK, CW), lambda t: (0, 0)), vec, vec, vec] + _ret_specs(tt, lambda t: t),
        out_specs=[half, half, half, pl.BlockSpec((NH, cpb, HD, HD), lambda t: (0, t, 0, 0))],
        out_shape=[jax.ShapeDtypeStruct((S, CW), F32), jax.ShapeDtypeStruct((S, CW), BF),
                   jax.ShapeDtypeStruct((S, RW), F32), jax.ShapeDtypeStruct((NH, S // CHUNK, HD, HD), BF)],
        scratch_shapes=[pltpu.VMEM((tt + HALO, CW), F32), pltpu.VMEM((HALO + CONV_ROWS, CONV_COLS), F32),
                        pltpu.VMEM((NH, HD, HD), F32)],
        compiler_params=_params(1, 40),
    )(proj, proj, cw, cb, lg, lb, proj, proj, proj, *tables, *deps)


def out_proj(u, r_raw, proj, gn, wout, x, l, deps=()):
    S = x.shape[0]
    tm = min(S, SEQ_TILE)
    gate_blk = (2 * CW + 3 * RW) // RW

    def body(u_ref, r_ref, gate_ref, gn_ref, w_ref, x_ref, x2_ref, mix_ref):
        mix_ref[:, 0:CW] = u_ref[...]
        gt = gate_ref[...]
        sil = gt * _sigmoid(gt) * gn_ref[...]
        for h in range(NH):
            cols = slice(h * HD, (h + 1) * HD)
            rh = r_ref[:, cols]
            d = rh - _mean(rh)
            rn = d * lax.rsqrt(_mean(d * d) + EPS)
            mix_ref[:, CW + h * HD:CW + (h + 1) * HD] = (rn * sil[:, cols]).astype(BF)
        x2_ref[...] = x_ref[...] + _dot(mix_ref[...], w_ref[...])

    return _ordered_call(
        body, deps, name=f"out_proj_{l}", grid=(S // tm,),
        in_specs=[pl.BlockSpec((tm, CW), lambda i: (i, 0)),
                  pl.BlockSpec((tm, RW), lambda i: (i, 0)),
                  pl.BlockSpec((tm, RW), lambda i: (i, gate_blk)),
                  pl.BlockSpec((1, RW), lambda i: (0, 0)),
                  pl.BlockSpec((D, D), lambda i: (0, 0)),
                  pl.BlockSpec((tm, D), lambda i: (i, 0))],
        out_specs=[pl.BlockSpec((tm, D), lambda i: (i, 0)),
                   pl.BlockSpec((tm, D), lambda i: (i, 0))],
        out_shape=[jax.ShapeDtypeStruct((S, D), F32), jax.ShapeDtypeStruct((S, D), BF)],
        compiler_params=_params(1, 40),
    )(u, r_raw, proj, gn, wout, x, *deps)


def mlp_fwd(x2, g2, wgt, wut, wd, l, deps=()):
    S = x2.shape[0]
    tm, tf = min(S, MLP_ROWS), MLP_COLS

    def body(x_ref, g_ref, wg_ref, wu_ref, wd_ref, o_ref, gs_ref, us_ref, a_ref):
        xv = x_ref[...]
        h = (xv * lax.rsqrt(_mean(xv * xv) + EPS) * g_ref[...]).astype(BF)
        for c0 in range(0, FF, tf):
            gv = _dot_nt(h, wg_ref[c0:c0 + tf, :])
            uv = _dot_nt(h, wu_ref[c0:c0 + tf, :])
            gs_ref[:, c0:c0 + tf] = gv.astype(BF)
            us_ref[:, c0:c0 + tf] = uv.astype(BF)
            a_ref[:, c0:c0 + tf] = (gv * _sigmoid(gv) * uv).astype(BF)
        o_ref[...] = xv + _dot(a_ref[...], wd_ref[...])

    wspec = pl.BlockSpec((FF, D), lambda i: (0, 0), pipeline_mode=pl.Buffered(1))
    row = pl.BlockSpec((tm, D), lambda i: (i, 0))
    wide = pl.BlockSpec((tm, FF), lambda i: (i, 0))
    return _ordered_call(
        body, deps, name=f"mlp_fwd_{l}", grid=(S // tm,),
        in_specs=[row, pl.BlockSpec((1, D), lambda i: (0, 0)), wspec, wspec, wspec],
        out_specs=[row, wide, wide, wide],
        out_shape=[jax.ShapeDtypeStruct((S, D), F32)] + [jax.ShapeDtypeStruct((S, FF), BF)] * 3,
        compiler_params=_params(1, 56),
    )(x2, g2, wgt, wut, wd, *deps)


def final_loss(x, gf, tgt):
    S = x.shape[0]
    tm = min(S, SEQ_TILE)

    def body(x_ref, g_ref, t_ref, dx_ref, loss_ref, dg_ref):
        @pl.when(pl.program_id(0) == 0)
        def _():
            loss_ref[...] = jnp.zeros((8, 128), F32)
            dg_ref[...] = jnp.zeros((1, D), F32)

        xv = x_ref[...]
        r = lax.rsqrt(_mean(xv * xv) + EPS)
        xh = xv * r
        diff = xh * g_ref[...] - t_ref[...]
        loss_ref[...] += jnp.sum(jnp.sum(diff * diff, axis=-1, keepdims=True), axis=0, keepdims=True)
        dy = diff * (1.0 / D)
        dg_ref[...] += jnp.sum(dy * xh, axis=0, keepdims=True)
        dxh = dy * g_ref[...]
        dx_ref[...] = r * (dxh - xh * _mean(dxh * xh))

    return pl.pallas_call(
        body, name="final_loss", grid=(S // tm,),
        in_specs=[pl.BlockSpec((tm, D), lambda i: (i, 0)),
                  pl.BlockSpec((1, D), lambda i: (0, 0)),
                  pl.BlockSpec((tm, D), lambda i: (i, 0))],
        out_specs=[pl.BlockSpec((tm, D), lambda i: (i, 0)),
                   pl.BlockSpec((8, 128), lambda i: (0, 0)),
                   pl.BlockSpec((1, D), lambda i: (0, 0))],
        out_shape=[jax.ShapeDtypeStruct((S, D), F32), jax.ShapeDtypeStruct((8, 128), F32),
                   jax.ShapeDtypeStruct((1, D), F32)],
        compiler_params=_params(1, 40),
    )(x, gf, tgt)


def mlp_bwd(dx3, x2, g2, gs, us, wgt, wut, wd, l, deps=()):
    S = x2.shape[0]
    tm, tf = min(S, MLP_ROWS), MLP_COLS

    def body(dx_ref, x_ref, g_ref, gs_ref, us_ref, wg_ref, wu_ref, wd_ref,
             dx2_ref, dg_ref, du_ref, h_ref, dgain_ref):
        @pl.when(pl.program_id(0) == 0)
        def _():
            dgain_ref[...] = jnp.zeros((1, D), F32)

        dxv = dx_ref[...]
        dxb = dxv.astype(BF)
        for c0 in range(0, FF, tf):
            da = _dot_nt(dxb, wd_ref[c0:c0 + tf, :])
            gv = gs_ref[:, c0:c0 + tf].astype(F32)
            uv = us_ref[:, c0:c0 + tf].astype(F32)
            sg = _sigmoid(gv)
            dg_ref[:, c0:c0 + tf] = (da * uv * (sg * (1.0 + gv * (1.0 - sg)))).astype(BF)
            du_ref[:, c0:c0 + tf] = (da * (gv * sg)).astype(BF)
        dh = _dot(dg_ref[...], wg_ref[...]) + _dot(du_ref[...], wu_ref[...])
        dx2, dgain, hb = _rms_bwd(x_ref[...], g_ref[...], dh, dxv)
        dx2_ref[...] = dx2
        dgain_ref[...] += dgain
        h_ref[...] = hb

    wspec = pl.BlockSpec((FF, D), lambda i: (0, 0), pipeline_mode=pl.Buffered(1))
    row = pl.BlockSpec((tm, D), lambda i: (i, 0))
    wide = pl.BlockSpec((tm, FF), lambda i: (i, 0))
    vec = pl.BlockSpec((1, D), lambda i: (0, 0))
    return _ordered_call(
        body, deps, name=f"mlp_bwd_{l}", grid=(S // tm,),
        in_specs=[row, row, vec, wide, wide, wspec, wspec, wspec],
        out_specs=[row, wide, wide, row, vec],
        out_shape=[jax.ShapeDtypeStruct((S, D), F32), jax.ShapeDtypeStruct((S, FF), BF),
                   jax.ShapeDtypeStruct((S, FF), BF), jax.ShapeDtypeStruct((S, D), BF),
                   jax.ShapeDtypeStruct((1, D), F32)],
        compiler_params=_params(1, 56),
    )(dx3, x2, g2, gs, us, wgt, wut, wd, *deps)


def wgrad(a, b, l, name, deps=()):
    S, K = a.shape
    N = b.shape[1]
    tk = 1408 if K == FF else min(K, 1024)
    tn = min(N, 1024)
    ts = min(S, WGRAD_ROWS)
    ns = S // ts

    def body(a_ref, b_ref, o_ref, acc):
        s = pl.program_id(2)

        @pl.when(s == 0)
        def _():
            acc[...] = jnp.zeros((tk, tn), F32)

        acc[...] += _dot_tn(a_ref[...], b_ref[...].astype(BF))

        @pl.when(s == ns - 1)
        def _():
            o_ref[...] = acc[...].astype(BF)

    return _ordered_call(
        body, deps, name=f"{name}_{l}", grid=(K // tk, N // tn, ns),
        in_specs=[pl.BlockSpec((ts, tk), lambda i, j, s: (s, i)),
                  pl.BlockSpec((ts, tn), lambda i, j, s: (s, j))],
        out_specs=pl.BlockSpec((tk, tn), lambda i, j, s: (i, j)),
        out_shape=jax.ShapeDtypeStruct((K, N), BF),
        scratch_shapes=[pltpu.VMEM((tk, tn), F32)],
        compiler_params=_params(3, 48),
    )(a, b, *deps)


def out_proj_bwd(dx2, wout, r_raw, proj, gn, u1, lg, lb, l, deps=()):
    S = dx2.shape[0]
    tm = min(S, SEQ_TILE)
    gate_blk = (2 * CW + 3 * RW) // RW

    def body(dx_ref, w_ref, r_ref, gate_ref, gn_ref, u1_ref, lg_ref, lb_ref,
             dgate_ref, dr_ref, du1_ref, sums_ref):
        @pl.when(pl.program_id(0) == 0)
        def _():
            sums_ref[...] = jnp.zeros((8, CW), F32)

        dmix = _dot_nt(dx_ref[...].astype(BF), w_ref[...])
        gt = gate_ref[...]
        sg = _sigmoid(gt)
        sil = gt * sg
        dsil = sg * (1.0 + gt * (1.0 - sg))
        for h in range(NH):
            cols = slice(h * HD, (h + 1) * HD)
            rh = r_ref[:, cols]
            d = rh - _mean(rh)
            rs = lax.rsqrt(_mean(d * d) + EPS)
            rn = d * rs
            drr = dmix[:, CW + h * HD:CW + (h + 1) * HD]
            gnh = gn_ref[:, cols]
            sums_ref[0:1, cols] += jnp.sum(drr * rn * sil[:, cols], axis=0, keepdims=True)
            dgate_ref[:, cols] = (drr * rn * gnh * dsil[:, cols]).astype(BF)
            drn = drr * gnh * sil[:, cols]
            dr_ref[:, cols] = (rs * (drn - _mean(drn) - rn * _mean(drn * rn))).astype(BF)
        du = dmix[:, 0:CW]
        u1 = u1_ref[...]
        d = u1 - _mean(u1)
        rs = lax.rsqrt(_mean(d * d) + EPS)
        xh = d * rs
        u2 = xh * lg_ref[...] + lb_ref[...]
        sg2 = _sigmoid(u2)
        du2 = du * (sg2 * (1.0 + u2 * (1.0 - sg2)))
        sums_ref[1:2, :] += jnp.sum(du2 * xh, axis=0, keepdims=True)
        sums_ref[2:3, :] += jnp.sum(du2, axis=0, keepdims=True)
        dxh = du2 * lg_ref[...]
        du1_ref[...] = rs * (dxh - _mean(dxh) - xh * _mean(dxh * xh))

    vec = pl.BlockSpec((1, CW), lambda i: (0, 0))
    half = pl.BlockSpec((tm, CW), lambda i: (i, 0))
    return _ordered_call(
        body, deps, name=f"out_proj_bwd_{l}", grid=(S // tm,),
        in_specs=[pl.BlockSpec((tm, D), lambda i: (i, 0)),
                  pl.BlockSpec((D, D), lambda i: (0, 0)),
                  half, pl.BlockSpec((tm, RW), lambda i: (i, gate_blk)), vec, half, vec, vec],
        out_specs=[half, half, half, pl.BlockSpec((8, CW), lambda i: (0, 0))],
        out_shape=[jax.ShapeDtypeStruct((S, RW), BF), jax.ShapeDtypeStruct((S, RW), BF),
                   jax.ShapeDtypeStruct((S, CW), F32), jax.ShapeDtypeStruct((8, CW), F32)],
        compiler_params=_params(1, 40),
    )(dx2, wout, r_raw, proj, gn, u1, lg, lb, *deps)


def _conv_bwd_fill(tc, du1_ref, dwb_ref, buf, pacc):
    i = pl.program_id(0)

    @pl.when(i == 0)
    def _():
        buf[tc:tc + HALO, :] = jnp.zeros((HALO, CW), F32)
        dwb_ref[...] = jnp.zeros((CK + 1, CW), F32)
        pacc[...] = jnp.zeros((CK, 8, CW), F32)

    @pl.when(i > 0)
    def _():
        buf[tc:tc + HALO, :] = buf[0:HALO, :]

    buf[0:tc, :] = du1_ref[...]


def _conv_bwd_rows(r0, groups, pacc, a_ref, b_ref, w_ref, dab_ref, buf, win):
    for c0 in range(0, CW, CONV_COLS):
        cols = slice(c0, c0 + CONV_COLS)
        av = a_ref[r0:r0 + CONV_ROWS, cols]
        sgb = _sigmoid(b_ref[r0:r0 + CONV_ROWS, cols])
        u0 = av * sgb
        acc = jnp.zeros((CONV_ROWS, CONV_COLS), F32)
        for start, length, taps in groups:
            win[0:length, :] = buf[r0 + start:r0 + start + length, cols]
            for j, at in taps:
                sl = win[at:at + CONV_ROWS, :]
                acc = acc + w_ref[CK - 1 - j:CK - j, cols] * sl
                pacc[CK - 1 - j, :, cols] += _fold8(u0 * sl)
        dab_ref[r0:r0 + CONV_ROWS, c0:c0 + CONV_COLS] = (acc * sgb).astype(BF)
        dab_ref[r0:r0 + CONV_ROWS, CW + c0:CW + c0 + CONV_COLS] = (acc * av * sgb * (1.0 - sgb)).astype(BF)


def _conv_bwd_finish(nt, pacc, du1_ref, dwb_ref):
    dwb_ref[CK:CK + 1, :] += jnp.sum(du1_ref[...], axis=0, keepdims=True)

    @pl.when(pl.program_id(0) == nt - 1)
    def _():
        for k in range(CK):
            dwb_ref[k:k + 1, :] = jnp.sum(pacc[k], axis=0, keepdims=True)


def _ret_bwd_chunk(c, q_ref, k_ref, v_ref, cos_ref, sin_ref, dm_ref, qd_ref, kd_ref, cd_ref, dr_ref, st_ref,
                   dq_ref, dk_ref, dv_ref, gst):
    scale = HD ** -0.5
    rows = slice(c * CHUNK, (c + 1) * CHUNK)
    cs, sn = cos_ref[rows, :], sin_ref[rows, :]
    for h in range(NH):
        cols = slice(h * HD, (h + 1) * HD)
        qr = _rot(q_ref[rows, cols], cs, sn)
        kr = _rot(k_ref[rows, cols], cs, sn) * scale
        qb, kb = qr.astype(BF), kr.astype(BF)
        vb = v_ref[rows, cols].astype(BF)
        dob = dr_ref[rows, cols]
        sb = st_ref[h, c]
        gn1 = gst[h]
        gb = gn1.astype(BF)
        sc = (_dot_nt(qb, kb) * dm_ref[h]).astype(BF)
        dsc = (_dot_nt(dob, vb) * dm_ref[h]).astype(BF)
        dqr = _dot(dsc, kb) + _dot_nt(dob, sb) * qd_ref[h]
        dkr = _dot_tn(dsc, qb) + _dot_nt(vb, gb) * kd_ref[h]
        dvv = _dot_tn(sc, dob) + _dot((kr * kd_ref[h]).astype(BF), gb)
        gst[h] = cd_ref[h] * gn1 + _dot_tn((qr * qd_ref[h]).astype(BF), dob)
        dq_ref[rows, cols] = _rot_t(dqr, cs, sn).astype(BF)
        dk_ref[rows, cols] = _rot_t(dkr * scale, cs, sn).astype(BF)
        dv_ref[rows, cols] = dvv.astype(BF)


def mix_bwd(du1, dr, proj, cw, states, tables, l, deps=()):
    S = proj.shape[0]
    tt = min(S, SEQ_TILE)
    cpb = tt // CHUNK
    nt = S // tt
    groups = _tap_groups(0, CK)

    def body(du1_ref, a_ref, b_ref, w_ref, q_ref, k_ref, v_ref, cos_ref, sin_ref, dm_ref, qd_ref, kd_ref, cd_ref,
             dr_ref, st_ref, dab_ref, dwb_ref, dq_ref, dk_ref, dv_ref, buf, win, gst, pacc):
        @pl.when(pl.program_id(0) == 0)
        def _():
            gst[...] = jnp.zeros((NH, HD, HD), F32)

        _conv_bwd_fill(tt, du1_ref, dwb_ref, buf, pacc)
        for c in reversed(range(cpb)):
            _ret_bwd_chunk(c, q_ref, k_ref, v_ref, cos_ref, sin_ref, dm_ref, qd_ref, kd_ref, cd_ref, dr_ref, st_ref,
                           dq_ref, dk_ref, dv_ref, gst)
            for r0 in range(c * CHUNK, (c + 1) * CHUNK, CONV_ROWS):
                _conv_bwd_rows(r0, groups, pacc, a_ref, b_ref, w_ref, dab_ref, buf, win)
        _conv_bwd_finish(nt, pacc, du1_ref, dwb_ref)

    rev = lambda t: nt - 1 - t
    half = pl.BlockSpec((tt, CW), lambda t: (rev(t), 0))
    return _ordered_call(
        body, deps, name=f"mix_bwd_{l}", grid=(nt,),
        in_specs=[half, half, pl.BlockSpec((tt, CW), lambda t: (rev(t), 1)), pl.BlockSpec((CK, CW), lambda t: (0, 0))]
        + _ret_specs(tt, rev) + [half, pl.BlockSpec((NH, cpb, HD, HD), lambda t: (0, rev(t), 0, 0))],
        out_specs=[pl.BlockSpec((tt, 2 * CW), lambda t: (rev(t), 0)), pl.BlockSpec((CK + 1, CW), lambda t: (0, 0)),
                   half, half, half],
        out_shape=[jax.ShapeDtypeStruct((S, 2 * CW), BF), jax.ShapeDtypeStruct((CK + 1, CW), F32)]
        + [jax.ShapeDtypeStruct((S, RW), BF)] * 3,
        scratch_shapes=[pltpu.VMEM((tt + HALO, CW), F32), pltpu.VMEM((HALO + CONV_ROWS, CONV_COLS), F32),
                        pltpu.VMEM((NH, HD, HD), F32), pltpu.VMEM((CK, 8, CW), F32)],
        compiler_params=_params(1, 40),
    )(du1, proj, proj, cw, proj, proj, proj, *tables, dr, states, *deps)


def in_proj_bwd(parts, win, x, g, dx2, l, deps=()):
    S = x.shape[0]
    tm = min(S, SEQ_TILE)
    n = len(parts)

    def body(*refs):
        srcs = refs[:n]
        w_ref, x_ref, g_ref, dx2_ref, dx_ref, dgain_ref = refs[n:]

        @pl.when(pl.program_id(0) == 0)
        def _():
            dgain_ref[...] = jnp.zeros((1, D), F32)

        dh, col = None, 0
        for r in srcs:
            width = r.shape[1]
            term = _dot_nt(r[...], w_ref[:, col:col + width])
            dh = term if dh is None else dh + term
            col += width
        dx, dgain, _ = _rms_bwd(x_ref[...], g_ref[...], dh, dx2_ref[...])
        dx_ref[...] = dx
        dgain_ref[...] += dgain

    row = pl.BlockSpec((tm, D), lambda i: (i, 0))
    vec = pl.BlockSpec((1, D), lambda i: (0, 0))
    return _ordered_call(
        body, deps, name=f"in_proj_bwd_{l}", grid=(S // tm,),
        in_specs=[pl.BlockSpec((tm, p.shape[1]), lambda i: (i, 0)) for p in parts]
        + [pl.BlockSpec((D, INW), lambda i: (0, 0)), row, vec, row],
        out_specs=[row, vec],
        out_shape=[jax.ShapeDtypeStruct((S, D), F32), jax.ShapeDtypeStruct((1, D), F32)],
        compiler_params=_params(1, 48),
    )(*parts, win, x, g, dx2, *deps)


def wgrad_in(h, parts, l):
    S = h.shape[0]
    ts = min(S, SEQ_TILE)
    ns = S // ts

    def body(*refs):
        h_ref, srcs = refs[0], refs[1:1 + len(parts)]
        o_ref, acc = refs[-2], refs[-1]
        s = pl.program_id(0)

        @pl.when(s == 0)
        def _():
            acc[...] = jnp.zeros((D, INW), F32)

        hv = h_ref[...]
        col = 0
        for r in srcs:
            width = r.shape[1]
            acc[:, col:col + width] += _dot_tn(hv, r[...])
            col += width

        @pl.when(s == ns - 1)
        def _():
            o_ref[...] = acc[...].astype(BF)

    return pl.pallas_call(
        body, name=f"wgrad_in_{l}", grid=(ns,),
        in_specs=[pl.BlockSpec((ts, D), lambda s: (s, 0))]
        + [pl.BlockSpec((ts, p.shape[1]), lambda s: (s, 0)) for p in parts],
        out_specs=pl.BlockSpec((D, INW), lambda s: (0, 0)),
        out_shape=jax.ShapeDtypeStruct((D, INW), BF),
        scratch_shapes=[pltpu.VMEM((D, INW), F32)],
        compiler_params=_params(1, 48),
    )(h, *parts)


def sum_slots(recv, name, deps=()):
    _, R, C = recv.shape
    tr = 256 if R % 256 == 0 else R

    def body(r_ref, o_ref):
        acc = r_ref[0].astype(F32)
        for k in range(1, NCHIP):
            acc = acc + r_ref[k].astype(F32)
        o_ref[...] = acc

    return _ordered_call(
        body, deps, name=name, grid=(R // tr,),
        in_specs=[pl.BlockSpec((NCHIP, tr, C), lambda i: (0, i, 0))],
        out_specs=pl.BlockSpec((tr, C), lambda i: (i, 0)),
        out_shape=jax.ShapeDtypeStruct((R, C), F32),
        compiler_params=_params(1, 32),
    )(recv, *deps)


def adamw(w, ga, gb, m, v, name):
    R, C = w.shape
    tr = 256 if R % 256 == 0 else R
    c1 = 1.0 - ADAM_B1 ** ADAM_STEP
    c2 = 1.0 - ADAM_B2 ** ADAM_STEP

    def body(w_ref, ga_ref, gb_ref, m_ref, v_ref, g_out, d_out, m_out, v_out):
        g = ga_ref[...] + gb_ref[...]
        mn = ADAM_B1 * m_ref[...] + (1.0 - ADAM_B1) * g
        vn = ADAM_B2 * v_ref[...] + (1.0 - ADAM_B2) * (g * g)
        g_out[...] = g
        m_out[...] = mn
        v_out[...] = vn
        d_out[...] = -ADAM_LR * ((mn / c1) / (jnp.sqrt(vn / c2) + ADAM_EPS) + ADAM_WD * w_ref[...])

    blk = pl.BlockSpec((tr, C), lambda i: (i, 0))
    return pl.pallas_call(
        body, name=name, grid=(R // tr,),
        in_specs=[blk] * 5, out_specs=[blk] * 4,
        out_shape=[jax.ShapeDtypeStruct((R, C), F32)] * 4,
        compiler_params=_params(1, 40),
    )(w, ga, gb, m, v)


def _place():
    x, y, c = lax.axis_index("x"), lax.axis_index("y"), lax.axis_index("c")
    chips = [(1 - x, y), (x, 1 - y), (1 - x, 1 - y)]
    return x, y, c, chips


def _window(ref, axis, j, size):
    idx = [slice(None)] * len(ref.shape)
    idx[axis] = pl.ds(pl.multiple_of(j * size, 128 if axis == len(ref.shape) - 1 else 16), size)
    return ref.at[tuple(idx)]


def _hbm(a):
    return pltpu.with_memory_space_constraint(a, pltpu.HBM)


def _hbm_like(arrs):
    return [pltpu.HBM(a.shape, a.dtype) for a in arrs]


def gather_start(shards, axes, after, tag):
    n = len(shards)
    na = len(after)
    lands = []
    for s, ax in zip(shards, axes):
        shp = list(s.shape)
        shp[ax] *= NCHIP
        lands.append(lax.empty(tuple(shp), s.dtype))

    def body(*refs):
        ins, land = refs[:n], refs[n:2 * n]
        send, recv = refs[2 * n + na], refs[2 * n + na + 1]
        token = refs[-1]
        x, y, c, chips = _place()
        for a in range(n):
            for k, chip in enumerate(chips):
                pltpu.make_async_remote_copy(
                    src_ref=ins[a], dst_ref=_window(land[a], axes[a], 2 * x + y, ins[a].shape[axes[a]]),
                    send_sem=send.at[3 * a + k], recv_sem=recv.at[3 * a + k],
                    device_id=(chip[0], chip[1], c), device_id_type=MESH).start()
        token[...] = jnp.zeros_like(token)

    outs = pl.pallas_call(
        body, name=f"gather_start_{tag}",
        in_specs=[HBM_SPEC] * (2 * n) + [ANY] * na,
        out_specs=(SEM_SPEC, SEM_SPEC, *[HBM_SPEC] * (2 * n), VMEM_SPEC),
        out_shape=(pltpu.SemaphoreType.DMA((3 * n,)), pltpu.SemaphoreType.DMA((3 * n,)),
                   *_hbm_like(shards), *_hbm_like(lands), jax.ShapeDtypeStruct((8, 128), F32)),
        input_output_aliases={a: 2 + a for a in range(2 * n)},
        compiler_params=pltpu.CompilerParams(has_side_effects=DATAFLOW),
    )(*[_hbm(s) for s in shards], *[_hbm(b) for b in lands], *after)
    return (outs[0], outs[1], list(outs[2:2 + n]), list(outs[2 + n:2 + 2 * n]), list(axes)), outs[-1]


def gather_wait(groups, after, tag):
    sizes = [len(g[2]) for g in groups]
    total = sum(sizes)

    def body(*refs):
        x, y, c, chips = _place()
        stage, loc = refs[-1 - total:-1], refs[-1]
        pos = 2 * total
        off = 0
        mine = []
        for g, n in zip(groups, sizes):
            ins, land = refs[off:off + n], refs[total + off:total + off + n]
            send_ref, recv_ref = refs[pos], refs[pos + 1]
            axes = g[4]
            for a in range(n):
                fetch = pltpu.make_async_copy(ins[a], stage[off + a], loc.at[2 * (off + a)])
                fetch.start()
                put = pltpu.make_async_copy(
                    stage[off + a], _window(land[a], axes[a], 2 * x + y, ins[a].shape[axes[a]]),
                    loc.at[2 * (off + a) + 1])
                mine.append((fetch, put))
                for k, chip in enumerate(chips):
                    cp = pltpu.make_async_remote_copy(
                        src_ref=ins[a],
                        dst_ref=_window(land[a], axes[a], 2 * chip[0] + chip[1], ins[a].shape[axes[a]]),
                        send_sem=send_ref.at[3 * a + k], recv_sem=recv_ref.at[3 * a + k],
                        device_id=(chip[0], chip[1], c), device_id_type=MESH)
                    cp.wait_send()
                    cp.wait_recv()
            pos += 2
            off += n
        for fetch, put in mine:
            fetch.wait()
            put.start()
        for fetch, put in mine:
            put.wait()

    shards = [s for g in groups for s in g[2]]
    lands = [b for g in groups for b in g[3]]
    sems = [s for g in groups for s in (g[0], g[1])]
    outs = pl.pallas_call(
        body, name=f"gather_wait_{tag}",
        in_specs=[HBM_SPEC] * (2 * total) + [SEM_SPEC] * len(sems) + [ANY],
        out_specs=[HBM_SPEC] * (2 * total),
        out_shape=(*_hbm_like(shards), *_hbm_like(lands)),
        input_output_aliases={a: a for a in range(2 * total)},
        scratch_shapes=[pltpu.VMEM(s.shape, s.dtype) for s in shards] + [pltpu.SemaphoreType.DMA((2 * total,))],
        compiler_params=pltpu.CompilerParams(has_side_effects=DATAFLOW, vmem_limit_bytes=32 << 20),
    )(*shards, *lands, *sems, after)
    return list(outs[total:])


def scatter_start(grads, axes, sizes, lands, l, tag):
    n = len(grads)

    def body(*refs):
        ins, land = refs[:n], refs[n:2 * n]
        send, recv = refs[2 * n], refs[2 * n + 1]
        token = refs[2 * n + 2 + 2 * n]
        stage, loc = refs[-1 - n:-1], refs[-1]
        x, y, c, chips = _place()
        me = 2 * x + y
        fetches = [pltpu.make_async_copy(_window(ins[a], axes[a], me, sizes[a]), stage[a], loc.at[2 * a])
                   for a in range(n)]
        for cp in fetches:
            cp.start()
        for a in range(n):
            for k, chip in enumerate(chips):
                pltpu.make_async_remote_copy(
                    src_ref=_window(ins[a], axes[a], 2 * chip[0] + chip[1], sizes[a]), dst_ref=land[a].at[me, l],
                    send_sem=send.at[3 * a + k], recv_sem=recv.at[3 * a + k],
                    device_id=(chip[0], chip[1], c), device_id_type=MESH).start()
        puts = [pltpu.make_async_copy(stage[a], land[a].at[me, l], loc.at[2 * a + 1]) for a in range(n)]
        for fetch, put in zip(fetches, puts):
            fetch.wait()
            put.start()
        for put in puts:
            put.wait()
        token[...] = jnp.zeros_like(token)

    outs = pl.pallas_call(
        body, name=f"scatter_start_{tag}",
        in_specs=[HBM_SPEC] * (2 * n),
        out_specs=(SEM_SPEC, SEM_SPEC, *[HBM_SPEC] * (2 * n), VMEM_SPEC),
        out_shape=(pltpu.SemaphoreType.DMA((3 * n,)), pltpu.SemaphoreType.DMA((3 * n,)),
                   *_hbm_like(grads), *_hbm_like(lands), jax.ShapeDtypeStruct((8, 128), F32)),
        input_output_aliases={a: 2 + a for a in range(2 * n)},
        scratch_shapes=[pltpu.VMEM(b.shape[2:], b.dtype) for b in lands] + [pltpu.SemaphoreType.DMA((2 * n,))],
        compiler_params=pltpu.CompilerParams(has_side_effects=DATAFLOW, vmem_limit_bytes=32 << 20),
    )(*[_hbm(g) for g in grads], *[_hbm(b) for b in lands])
    group = (outs[0], outs[1], list(outs[2:2 + n]), list(axes), list(sizes), l)
    return group, list(outs[2 + n:2 + 2 * n]), outs[-1]


def scatter_wait(groups, lands, which, after):
    nl = len(lands)

    def body(*refs):
        land = refs[:nl]
        x, y, c, chips = _place()
        pos = nl
        for g, wh in zip(groups, which):
            n = len(g[2])
            ins = refs[pos:pos + n]
            send_ref, recv_ref = refs[pos + n], refs[pos + n + 1]
            axes, sizes, l = g[3], g[4], g[5]
            for a in range(n):
                for k, chip in enumerate(chips):
                    jp = 2 * chip[0] + chip[1]
                    cp = pltpu.make_async_remote_copy(
                        src_ref=_window(ins[a], axes[a], jp, sizes[a]), dst_ref=land[wh[a]].at[jp, l],
                        send_sem=send_ref.at[3 * a + k], recv_sem=recv_ref.at[3 * a + k],
                        device_id=(chip[0], chip[1], c), device_id_type=MESH)
                    cp.wait_send()
                    cp.wait_recv()
            pos += n + 2

    operands = list(lands)
    specs = [HBM_SPEC] * nl
    for g in groups:
        operands += list(g[2]) + [g[0], g[1]]
        specs += [HBM_SPEC] * len(g[2]) + [SEM_SPEC, SEM_SPEC]
    outs = pl.pallas_call(
        body, name="scatter_wait", in_specs=specs + [ANY], out_specs=[HBM_SPEC] * nl,
        out_shape=tuple(_hbm_like(lands)),
        input_output_aliases={a: a for a in range(nl)},
        compiler_params=pltpu.CompilerParams(has_side_effects=DATAFLOW),
    )(*operands, after)
    return list(outs)


def swap_start(part, tag):
    def body(p_ref, land_ref, send, recv, p_thru, land_thru, token):
        x, y, c, _ = _place()
        pltpu.make_async_remote_copy(src_ref=p_ref, dst_ref=land_ref, send_sem=send, recv_sem=recv,
                                     device_id=(x, y, 1 - c), device_id_type=MESH).start()
        token[...] = jnp.zeros_like(token)

    outs = pl.pallas_call(
        body, name=f"swap_start_{tag}", in_specs=[HBM_SPEC, HBM_SPEC],
        out_specs=(SEM_SPEC, SEM_SPEC, HBM_SPEC, HBM_SPEC, VMEM_SPEC),
        out_shape=(pltpu.SemaphoreType.DMA(()), pltpu.SemaphoreType.DMA(()), *_hbm_like([part, part]),
                   jax.ShapeDtypeStruct((8, 128), F32)),
        input_output_aliases={0: 2, 1: 3},
        compiler_params=pltpu.CompilerParams(has_side_effects=DATAFLOW),
    )(_hbm(part), _hbm(lax.empty(part.shape, part.dtype)))
    return tuple(outs[:4]), outs[4]


def swap_wait(group, after, tag):
    send, recv, part, land = group

    def body(p_ref, land_ref, send_ref, recv_ref, after_ref, p_out, land_out):
        x, y, c, _ = _place()
        cp = pltpu.make_async_remote_copy(src_ref=p_ref, dst_ref=land_ref, send_sem=send_ref, recv_sem=recv_ref,
                                          device_id=(x, y, 1 - c), device_id_type=MESH)
        cp.wait_send()
        cp.wait_recv()

    outs = pl.pallas_call(
        body, name=f"swap_wait_{tag}", in_specs=[HBM_SPEC, HBM_SPEC, SEM_SPEC, SEM_SPEC, ANY],
        out_specs=[HBM_SPEC, HBM_SPEC], out_shape=tuple(_hbm_like([part, land])),
        input_output_aliases={0: 0, 1: 1},
        compiler_params=pltpu.CompilerParams(has_side_effects=DATAFLOW),
    )(part, land, send, recv, after)
    return outs[0], outs[1]


def small_allreduce(p):
    R, C = p.shape
    ndev = 8

    def body(p_ref, o_ref, buf, send, recv):
        x, y, c, _ = _place()
        me = 4 * x + 2 * y + c
        buf[me] = p_ref[...]

        def peer(d):
            px = 1 - x if d & 4 else x
            py = 1 - y if d & 2 else y
            pc = 1 - c if d & 1 else c
            return px, py, pc

        def copy(d, slot):
            return pltpu.make_async_remote_copy(src_ref=p_ref, dst_ref=buf.at[slot], send_sem=send.at[d - 1],
                                                recv_sem=recv.at[d - 1], device_id=peer(d), device_id_type=MESH)

        sends = [copy(d, me) for d in range(1, ndev)]
        for cp in sends:
            cp.start()
        for d in range(1, ndev):
            px, py, pc = peer(d)
            copy(d, 4 * px + 2 * py + pc).wait_recv()
        for cp in sends:
            cp.wait_send()
        acc = buf[0]
        for k in range(1, ndev):
            acc = acc + buf[k]
        o_ref[...] = acc

    return pl.pallas_call(
        body, name="small_allreduce", in_specs=[VMEM_SPEC], out_specs=VMEM_SPEC,
        out_shape=jax.ShapeDtypeStruct((R, C), F32),
        scratch_shapes=[pltpu.VMEM((ndev, R, C), F32), pltpu.SemaphoreType.DMA((ndev - 1,)),
                        pltpu.SemaphoreType.DMA((ndev - 1,))],
        compiler_params=pltpu.CompilerParams(vmem_limit_bytes=32 << 20),
    )(p)


def kernel(x, norm1_g, w_in, conv_w, conv_b, conv_ln_g, conv_ln_b, ret_gn_g, w_out, norm2_g, w_gate, w_up, w_down, final_g, loss_target, m_norm1_g, m_w_in, m_conv_w, m_conv_b, m_conv_ln_g, m_conv_ln_b, m_ret_gn_g, m_w_out, m_norm2_g, m_w_gate, m_w_up, m_w_down, m_final_g, v_norm1_g, v_w_in, v_conv_w, v_conv_b, v_conv_ln_g, v_conv_ln_b, v_ret_gn_g, v_w_out, v_norm2_g, v_w_gate, v_w_up, v_w_down, v_final_g):
    S = x.shape[1]
    xs = x.reshape(S, D)
    tgt = loss_target.reshape(S, D)
    fsh = FF // NCHIP

    def shards_of(l):
        return [w_in[l].astype(BF), w_out[l].astype(BF), w_gate[l].T.astype(BF), w_up[l].T.astype(BF),
                w_down[l].astype(BF), conv_w[l]]

    gather_axes = [1, 0, 0, 0, 0, 1]
    shard_cache = [shards_of(l) for l in range(L)]
    tables = _ret_tables(S)
    row = lambda a, l: a[l].reshape(1, -1)
    groups = {}
    weights = [dict() for _ in range(L)]

    def begin(l, which, after):
        group, token = gather_start([shard_cache[l][i] for i in which], [gather_axes[i] for i in which],
                                    after, f"{l}_{which[0]}")
        groups[(l, which[0])] = (group, which)
        return token

    def finish(l, firsts, after, tag):
        gs = [groups[(l, f)] for f in firsts]
        outs = gather_wait([g for g, _ in gs], after, f"{l}_{tag}")
        k = 0
        for _, which in gs:
            for i in which:
                weights[l][i] = outs[k]
                k += 1

    first = begin(0, [0], [])
    second = begin(0, [1, 5], [first])
    after = begin(0, [2, 3, 4], [second])
    saved = []
    xc = xs
    for l in range(L):
        if l == 0:
            finish(l, [0], after, "a")
        else:
            finish(l, [0], after, "all")
        win = weights[l][0]
        proj, h1 = in_proj(xc, row(norm1_g, l), win, l)
        if l == 0:
            finish(l, [1], proj, "b")
        wout, cw = weights[l][1], weights[l][5]
        ahead = [begin(l + 1, [0, 1, 2, 3, 4, 5], [proj])] if l + 1 < L else []
        u1, u, r_raw, states = mix_fwd(proj, cw, row(conv_b, l), row(conv_ln_g, l), row(conv_ln_b, l), tables, l, ahead)
        x2, mixed = out_proj(u, r_raw, proj, row(ret_gn_g, l), wout, xc, l)
        if l == 0:
            finish(l, [2], x2, "c")
        wgt, wut, wd = weights[l][2], weights[l][3], weights[l][4]
        x3, gs, us, act = mlp_fwd(x2, row(norm2_g, l), wgt, wut, wd, l)
        saved.append((xc, proj, h1, u1, r_raw, states, mixed, x2, gs, us, act))
        xc = x3
        after = x3

    dx, loss_acc, d_final = final_loss(xc, final_g.reshape(1, D), tgt)
    loss = lax.psum(loss_acc[0, 0] * (0.5 / D), ("x", "y", "c"))

    scatter_axes = [1, 0, 0, 0, 0]
    scatter_sizes = [INW // NCHIP, D // NCHIP, fsh, fsh, fsh]
    lands = [lax.empty((NCHIP, L, D, INW // NCHIP), BF), lax.empty((NCHIP, L, D // NCHIP, D), BF),
             lax.empty((NCHIP, L, fsh, D), BF), lax.empty((NCHIP, L, fsh, D), BF), lax.empty((NCHIP, L, fsh, D), BF)]
    sent, sent_which = [], []

    def send_grad(g, a, l):
        group, new_land, token = scatter_start([g], [scatter_axes[a]], [scatter_sizes[a]], [lands[a]], l, f"{l}_{a}")
        lands[a] = new_land[0]
        sent.append(group)
        sent_which.append([a])
        return [token]

    small = [None] * L
    for l in reversed(range(L)):
        xin, proj, h1, u1, r_raw, states, mixed, x2, gs, us, act = saved[l]
        win, wout, wgt, wut, wd, cw = (weights[l][i] for i in range(6))
        dx2, dgs, dus, h2, d_n2 = mlp_bwd(dx, x2, row(norm2_g, l), gs, us, wgt, wut, wd, l)
        g_wd = wgrad(act, dx, l, "wgrad_down")
        g_wgt = wgrad(dgs, h2, l, "wgrad_gate", send_grad(g_wd, 4, l))
        g_wut = wgrad(dus, h2, l, "wgrad_up", send_grad(g_wgt, 2, l))
        dgate, dr, du1, sums = out_proj_bwd(dx2, wout, r_raw, proj, row(ret_gn_g, l), u1,
                                            row(conv_ln_g, l), row(conv_ln_b, l), l, send_grad(g_wut, 3, l))
        g_wout = wgrad(mixed, dx2, l, "wgrad_out")
        dab, dwb, dq, dk, dv = mix_bwd(du1, dr, proj, cw, states, tables, l, send_grad(g_wout, 1, l))
        dproj = [dab, dq, dk, dv, dgate]
        g_win = wgrad_in(h1, dproj, l)
        dx, d_n1 = in_proj_bwd(dproj, win, xin, row(norm1_g, l), dx2, l, send_grad(g_win, 0, l))
        small[l] = jnp.concatenate([dwb, sums, d_n1.reshape(2, CW), d_n2.reshape(2, CW)], axis=0)
    grad_x = dx.reshape(1, S, D)

    per = CK + 1 + 8 + 4
    packed = jnp.concatenate(small + [d_final.reshape(2, CW), jnp.zeros((6, CW), F32)], axis=0)
    tot = small_allreduce(packed)
    lay = tot[:L * per].reshape(L, per, CW)
    g_conv_w_full = lay[:, 0:CK, :]
    j = 2 * lax.axis_index("x") + lax.axis_index("y")
    g_conv_w = lax.dynamic_slice_in_dim(g_conv_w_full, j * (CW // NCHIP), CW // NCHIP, axis=2)
    g_small = {
        "conv_b": lay[:, CK, :], "ret_gn_g": lay[:, CK + 1, :], "conv_ln_g": lay[:, CK + 2, :],
        "conv_ln_b": lay[:, CK + 3, :], "norm1_g": lay[:, CK + 9:CK + 11, :].reshape(L, D),
        "norm2_g": lay[:, CK + 11:CK + 13, :].reshape(L, D), "final_g": tot[L * per:L * per + 2].reshape(D),
    }

    recv = scatter_wait(sent, lands, sent_which, tot)
    shard_shapes = [(L * D, INW // NCHIP), (L * D // NCHIP, D), (L * fsh, D), (L * fsh, D), (L * fsh, D)]
    names = ["w_in", "w_out", "w_gate", "w_up", "w_down"]
    swaps, token = [], []
    for r, shp, nm in zip(recv, shard_shapes, names):
        group, tk = swap_start(sum_slots(r.reshape((NCHIP,) + shp), f"sum_{nm}", token), nm)
        swaps.append(group)
        token = [tk]
    parts, theirs = zip(*[swap_wait(group, token[0], nm) for group, nm in zip(swaps, names)])

    def unT(a):
        return jnp.swapaxes(a.reshape(L, fsh, D), 1, 2).reshape(L * D, fsh)

    big = {}
    wmv = {"w_in": (w_in, m_w_in, v_w_in), "w_out": (w_out, m_w_out, v_w_out),
           "w_gate": (w_gate, m_w_gate, v_w_gate), "w_up": (w_up, m_w_up, v_w_up),
           "w_down": (w_down, m_w_down, v_w_down)}
    for nm, mine, other in zip(names, parts, theirs):
        w, m, v = wmv[nm]
        if nm in ("w_gate", "w_up"):
            mine, other = unT(mine), unT(other)
        shp2 = (w.shape[0] * w.shape[1], w.shape[2])
        outs = adamw(w.reshape(shp2), mine, other, m.reshape(shp2), v.reshape(shp2), f"adamw_{nm}")
        big[nm] = [o.reshape(w.shape) for o in outs]

    cshape = (L * CK, CW // NCHIP)
    zc = jnp.zeros(cshape, F32)
    big["conv_w"] = [o.reshape(conv_w.shape) for o in adamw(
        conv_w.reshape(cshape), g_conv_w.reshape(cshape), zc, m_conv_w.reshape(cshape),
        v_conv_w.reshape(cshape), "adamw_conv_w")]
    vec_names = ["norm1_g", "conv_b", "conv_ln_g", "conv_ln_b", "ret_gn_g", "norm2_g", "final_g"]
    vec_w = {"norm1_g": (norm1_g, m_norm1_g, v_norm1_g), "conv_b": (conv_b, m_conv_b, v_conv_b),
             "conv_ln_g": (conv_ln_g, m_conv_ln_g, v_conv_ln_g), "conv_ln_b": (conv_ln_b, m_conv_ln_b, v_conv_ln_b),
             "ret_gn_g": (ret_gn_g, m_ret_gn_g, v_ret_gn_g), "norm2_g": (norm2_g, m_norm2_g, v_norm2_g),
             "final_g": (final_g, m_final_g, v_final_g)}
    cat = lambda arrs: jnp.concatenate([a.reshape(-1, CW) for a in arrs], axis=0)
    vw = cat([vec_w[nm][0] for nm in vec_names])
    vm = cat([vec_w[nm][1] for nm in vec_names])
    vv = cat([vec_w[nm][2] for nm in vec_names])
    vg = cat([g_small[nm] for nm in vec_names])
    vouts = adamw(vw, vg, jnp.zeros_like(vg), vm, vv, "adamw_vectors")
    off = 0
    for nm in vec_names:
        w = vec_w[nm][0]
        nrow = w.size // CW
        big[nm] = [o[off:off + nrow].reshape(w.shape) for o in vouts]
        off += nrow

    order = ["norm1_g", "w_in", "conv_w", "conv_b", "conv_ln_g", "conv_ln_b", "ret_gn_g", "w_out", "norm2_g",
             "w_gate", "w_up", "w_down", "final_g"]
    return (loss, grad_x, *[big[nm][0] for nm in order], *[big[nm][1] for nm in order],
            *[big[nm][2] for nm in order], *[big[nm][3] for nm in order])
```

```python
import math

import jax
import jax.numpy as jnp
from jax import lax
from jax.experimental import pallas as pl
from jax.experimental.pallas import tpu as pltpu

D = 1024
L = 4
CW = 512
RW = 512
NH = 4
HD = 128
CK = 31
CHUNK = 64
INW = 3072
FF = 2816
NCHIP = 4
EPS = 1e-6
ROPE_BASE = 10000.0
SEQ_TILE = 512
WGRAD_ROWS = 1024
MLP_ROWS = 256
MLP_COLS = 1408
HALO = 32
CONV_ROWS = 32
CONV_COLS = 256

ADAM_LR = 0.001
ADAM_B1 = 0.9
ADAM_B2 = 0.999
ADAM_EPS = 1e-08
ADAM_WD = 0.01
ADAM_STEP = 10

BF = jnp.bfloat16
F32 = jnp.float32
MESH = pl.DeviceIdType.MESH
ANY = pl.BlockSpec(memory_space=pl.ANY)
VMEM_SPEC = pl.BlockSpec(memory_space=pltpu.VMEM)
HBM_SPEC = pl.BlockSpec(memory_space=pltpu.HBM)
SEM_SPEC = pl.BlockSpec(memory_space=pltpu.SEMAPHORE)
DATAFLOW = pltpu.SideEffectType.DATAFLOW_SIDE_EFFECTING


def _params(n_grid, vmem_mb):
    return pltpu.CompilerParams(dimension_semantics=("arbitrary",) * n_grid,
                                vmem_limit_bytes=vmem_mb << 20)


def _ordered_call(body, deps, *, in_specs, **kw):
    n, nd = len(in_specs), len(deps)

    def with_deps(*refs):
        body(*refs[:n], *refs[n + nd:])

    return pl.pallas_call(with_deps, in_specs=list(in_specs) + [ANY] * nd, **kw)


def _dot(a, b):
    return jnp.dot(a, b, preferred_element_type=F32)


def _dot_nt(a, b):
    return lax.dot_general(a, b, (((1,), (1,)), ((), ())), preferred_element_type=F32)


def _dot_tn(a, b):
    return lax.dot_general(a, b, (((0,), (0,)), ((), ())), preferred_element_type=F32)


def _sigmoid(x):
    return 0.5 * jnp.tanh(0.5 * x) + 0.5


def _mean(x):
    return jnp.mean(x, axis=-1, keepdims=True)


def _fold8(x):
    out = x[0:8, :]
    for q in range(1, x.shape[0] // 8):
        out = out + x[8 * q:8 * q + 8, :]
    return out


def _tap_groups(first, count):
    groups = []
    for phase in range(8):
        taps = [(t, first + t - phase) for t in range(count) if (first + t) % 8 == phase]
        if taps:
            lo, hi = min(q for _, q in taps), max(q for _, q in taps)
            groups.append((lo + phase, hi - lo + CONV_ROWS, [(t, q - lo) for t, q in taps]))
    return groups


def _rot(t, cs, sn):
    return t * cs + pltpu.roll(t, HD // 2, 1) * sn


def _rot_t(dy, cs, sn):
    return dy * cs + pltpu.roll(dy * sn, HD // 2, 1)


def _rms_bwd(x, g, dh, dx_in):
    r = lax.rsqrt(_mean(x * x) + EPS)
    xh = x * r
    dxh = dh * g
    dx = dx_in + r * (dxh - xh * _mean(dxh * xh))
    return dx, jnp.sum(dh * xh, axis=0, keepdims=True), (xh * g).astype(BF)


def in_proj(x, g, win, l, deps=()):
    S = x.shape[0]
    tm = min(S, SEQ_TILE)

    def body(x_ref, g_ref, w_ref, o_ref, h_ref):
        xv = x_ref[...]
        h = (xv * lax.rsqrt(_mean(xv * xv) + EPS) * g_ref[...]).astype(BF)
        h_ref[...] = h
        o_ref[...] = _dot(h, w_ref[...]).astype(BF)

    return _ordered_call(
        body, deps, name=f"in_proj_{l}", grid=(S // tm,),
        in_specs=[pl.BlockSpec((tm, D), lambda i: (i, 0)),
                  pl.BlockSpec((1, D), lambda i: (0, 0)),
                  pl.BlockSpec((D, INW), lambda i: (0, 0))],
        out_specs=[pl.BlockSpec((tm, INW), lambda i: (i, 0)), pl.BlockSpec((tm, D), lambda i: (i, 0))],
        out_shape=[jax.ShapeDtypeStruct((S, INW), BF), jax.ShapeDtypeStruct((S, D), BF)],
        compiler_params=_params(1, 48),
    )(x, g, win, *deps)


def _conv_fwd_fill(tc, a_ref, b_ref, buf):
    i = pl.program_id(0)

    @pl.when(i == 0)
    def _():
        buf[0:HALO, :] = jnp.zeros((HALO, CW), F32)

    @pl.when(i > 0)
    def _():
        buf[0:HALO, :] = buf[tc:tc + HALO, :]

    buf[HALO:HALO + tc, :] = a_ref[...].astype(F32) * _sigmoid(b_ref[...].astype(F32))


def _conv_fwd_rows(r0, groups, w_ref, cb_ref, lg_ref, lb_ref, u1_ref, u_ref, buf, win):
    for c0 in range(0, CW, CONV_COLS):
        cols = slice(c0, c0 + CONV_COLS)
        acc = jnp.broadcast_to(cb_ref[:, cols], (CONV_ROWS, CONV_COLS))
        for start, length, taps in groups:
            win[0:length, :] = buf[r0 + start:r0 + start + length, cols]
            for k, at in taps:
                acc = acc + w_ref[k:k + 1, cols] * win[at:at + CONV_ROWS, :]
        u1_ref[r0:r0 + CONV_ROWS, cols] = acc
    acc = u1_ref[r0:r0 + CONV_ROWS, :]
    d = acc - _mean(acc)
    u2 = d * lax.rsqrt(_mean(d * d) + EPS) * lg_ref[...] + lb_ref[...]
    u_ref[r0:r0 + CONV_ROWS, :] = (u2 * _sigmoid(u2)).astype(BF)


def _ret_tables(S):
    half = HD // 2
    pos = jnp.arange(S, dtype=F32)
    freqs = ROPE_BASE ** (-jnp.arange(half, dtype=F32) / half)
    ang = pos[:, None] * freqs[None, :]
    cos, sin = jnp.cos(ang), jnp.sin(ang)
    cosf = jnp.concatenate([cos, cos], axis=-1)
    sinf = jnp.concatenate([-sin, sin], axis=-1)
    log_g = jnp.log(1.0 - 2.0 ** (-5.0 - jnp.arange(NH, dtype=F32)))
    idx = jnp.arange(CHUNK, dtype=F32)
    dmat = jnp.exp(log_g[:, None, None] * jnp.abs(idx[:, None] - idx[None, :]))
    qdec = jnp.broadcast_to(jnp.exp(log_g[:, None] * (idx + 1.0))[:, :, None], (NH, CHUNK, HD))
    kdec = jnp.broadcast_to(jnp.exp(log_g[:, None] * (CHUNK - 1 - idx))[:, :, None], (NH, CHUNK, HD))
    cdec = jnp.broadcast_to(jnp.exp(log_g * CHUNK)[:, None, None], (NH, HD, HD))
    return cosf, sinf, dmat, qdec, kdec, cdec


def _ret_specs(tr, tmap):
    q0 = (2 * CW) // RW
    return [pl.BlockSpec((tr, RW), lambda t: (tmap(t), q0)),
            pl.BlockSpec((tr, RW), lambda t: (tmap(t), q0 + 1)),
            pl.BlockSpec((tr, RW), lambda t: (tmap(t), q0 + 2)),
            pl.BlockSpec((tr, HD), lambda t: (tmap(t), 0)),
            pl.BlockSpec((tr, HD), lambda t: (tmap(t), 0)),
            pl.BlockSpec((NH, CHUNK, CHUNK), lambda t: (0, 0, 0)),
            pl.BlockSpec((NH, CHUNK, HD), lambda t: (0, 0, 0)),
            pl.BlockSpec((NH, CHUNK, HD), lambda t: (0, 0, 0)),
            pl.BlockSpec((NH, HD, HD), lambda t: (0, 0, 0))]


def _ret_fwd_chunk(c, q_ref, k_ref, v_ref, cos_ref, sin_ref, dm_ref, qd_ref, kd_ref, cd_ref, r_ref, st_ref, st):
    scale = HD ** -0.5
    rows = slice(c * CHUNK, (c + 1) * CHUNK)
    cs, sn = cos_ref[rows, :], sin_ref[rows, :]
    for h in range(NH):
        cols = slice(h * HD, (h + 1) * HD)
        qr = _rot(q_ref[rows, cols].astype(F32), cs, sn)
        kr = _rot(k_ref[rows, cols].astype(F32), cs, sn) * scale
        vb = v_ref[rows, cols].astype(BF)
        s = st[h]
        sb = s.astype(BF)
        st_ref[h, c] = sb
        sc = _dot_nt(qr.astype(BF), kr.astype(BF)) * dm_ref[h]
        r_ref[rows, cols] = _dot(sc.astype(BF), vb) + _dot((qr * qd_ref[h]).astype(BF), sb)
        st[h] = cd_ref[h] * s + _dot_tn((kr * kd_ref[h]).astype(BF), vb)


def mix_fwd(proj, cw, cb, lg, lb, tables, l, deps=()):
    S = proj.shape[0]
    tt = min(S, SEQ_TILE)
    cpb = tt // CHUNK
    groups = _tap_groups(HALO - (CK - 1), CK)

    def body(a_ref, b_ref, w_ref, cb_ref, lg_ref, lb_ref, q_ref, k_ref, v_ref, cos_ref, sin_ref,
             dm_ref, qd_ref, kd_ref, cd_ref, u1_ref, u_ref, r_ref, st_ref, buf, win, st):
        @pl.when(pl.program_id(0) == 0)
        def _():
            st[...] = jnp.zeros((NH, HD, HD), F32)

        _conv_fwd_fill(tt, a_ref, b_ref, buf)
        for c in range(cpb):
            _ret_fwd_chunk(c, q_ref, k_ref, v_ref, cos_ref, sin_ref, dm_ref, qd_ref, kd_ref, cd_ref,
                           r_ref, st_ref, st)
            for r0 in range(c * CHUNK, (c + 1) * CHUNK, CONV_ROWS):
                _conv_fwd_rows(r0, groups, w_ref, cb_ref, lg_ref, lb_ref, u1_ref, u_ref, buf, win)

    vec = pl.BlockSpec((1, CW), lambda t: (0, 0))
    half = pl.BlockSpec((tt, CW), lambda t: (t, 0))
    return _ordered_call(
        body, deps, name=f"mix_fwd_{l}", grid=(S // tt,),
        in_specs=[half, pl.BlockSpec((tt, CW), lambda t: (t, 1)),
                  pl.BlockSpec((CK, CW), lambda t: (0, 0)), vec, vec, vec] + _ret_specs(tt, lambda t: t),
        out_specs=[half, half, half, pl.BlockSpec((NH, cpb, HD, HD), lambda t: (0, t, 0, 0))],
        out_shape=[jax.ShapeDtypeStruct((S, CW), F32), jax.ShapeDtypeStruct((S, CW), BF),
                   jax.ShapeDtypeStruct((S, RW), F32), jax.ShapeDtypeStruct((NH, S // CHUNK, HD, HD), BF)],
        scratch_shapes=[pltpu.VMEM((tt + HALO, CW), F32), pltpu.VMEM((HALO + CONV_ROWS, CONV_COLS), F32),
                        pltpu.VMEM((NH, HD, HD), F32)],
        compiler_params=_params(1, 40),
    )(proj, proj, cw, cb, lg, lb, proj, proj, proj, *tables, *deps)


def out_proj(u, r_raw, proj, gn, wout, x, l, deps=()):
    S = x.shape[0]
    tm = min(S, SEQ_TILE)
    gate_blk = (2 * CW + 3 * RW) // RW

    def body(u_ref, r_ref, gate_ref, gn_ref, w_ref, x_ref, x2_ref, mix_ref):
        mix_ref[:, 0:CW] = u_ref[...]
        gt = gate_ref[...].astype(F32)
        sil = gt * _sigmoid(gt) * gn_ref[...]
        for h in range(NH):
            cols = slice(h * HD, (h + 1) * HD)
            rh = r_ref[:, cols]
            d = rh - _mean(rh)
            rn = d * lax.rsqrt(_mean(d * d) + EPS)
            mix_ref[:, CW + h * HD:CW + (h + 1) * HD] = (rn * sil[:, cols]).astype(BF)
        x2_ref[...] = x_ref[...] + _dot(mix_ref[...], w_ref[...])

    return _ordered_call(
        body, deps, name=f"out_proj_{l}", grid=(S // tm,),
        in_specs=[pl.BlockSpec((tm, CW), lambda i: (i, 0)),
                  pl.BlockSpec((tm, RW), lambda i: (i, 0)),
                  pl.BlockSpec((tm, RW), lambda i: (i, gate_blk)),
                  pl.BlockSpec((1, RW), lambda i: (0, 0)),
                  pl.BlockSpec((D, D), lambda i: (0, 0)),
                  pl.BlockSpec((tm, D), lambda i: (i, 0))],
        out_specs=[pl.BlockSpec((tm, D), lambda i: (i, 0)),
                   pl.BlockSpec((tm, D), lambda i: (i, 0))],
        out_shape=[jax.ShapeDtypeStruct((S, D), F32), jax.ShapeDtypeStruct((S, D), BF)],
        compiler_params=_params(1, 40),
    )(u, r_raw, proj, gn, wout, x, *deps)


def mlp_fwd(x2, g2, wgt, wut, wd, l, deps=()):
    S = x2.shape[0]
    tm, tf = min(S, MLP_ROWS), MLP_COLS

    def body(x_ref, g_ref, wg_ref, wu_ref, wd_ref, o_ref, gs_ref, us_ref, a_ref):
        xv = x_ref[...]
        h = (xv * lax.rsqrt(_mean(xv * xv) + EPS) * g_ref[...]).astype(BF)
        for c0 in range(0, FF, tf):
            gv = _dot_nt(h, wg_ref[c0:c0 + tf, :])
            uv = _dot_nt(h, wu_ref[c0:c0 + tf, :])
            gs_ref[:, c0:c0 + tf] = gv.astype(BF)
            us_ref[:, c0:c0 + tf] = uv.astype(BF)
            a_ref[:, c0:c0 + tf] = (gv * _sigmoid(gv) * uv).astype(BF)
        o_ref[...] = xv + _dot(a_ref[...], wd_ref[...])

    wspec = pl.BlockSpec((FF, D), lambda i: (0, 0), pipeline_mode=pl.Buffered(1))
    row = pl.BlockSpec((tm, D), lambda i: (i, 0))
    wide = pl.BlockSpec((tm, FF), lambda i: (i, 0))
    return _ordered_call(
        body, deps, name=f"mlp_fwd_{l}", grid=(S // tm,),
        in_specs=[row, pl.BlockSpec((1, D), lambda i: (0, 0)), wspec, wspec, wspec],
        out_specs=[row, wide, wide, wide],
        out_shape=[jax.ShapeDtypeStruct((S, D), F32)] + [jax.ShapeDtypeStruct((S, FF), BF)] * 3,
        compiler_params=_params(1, 56),
    )(x2, g2, wgt, wut, wd, *deps)


def final_loss(x, gf, tgt):
    S = x.shape[0]
    tm = min(S, SEQ_TILE)

    def body(x_ref, g_ref, t_ref, dx_ref, loss_ref, dg_ref):
        @pl.when(pl.program_id(0) == 0)
        def _():
            loss_ref[...] = jnp.zeros((8, 128), F32)
            dg_ref[...] = jnp.zeros((1, D), F32)

        xv = x_ref[...]
        r = lax.rsqrt(_mean(xv * xv) + EPS)
        xh = xv * r
        diff = xh * g_ref[...] - t_ref[...]
        loss_ref[...] += jnp.sum(jnp.sum(diff * diff, axis=-1, keepdims=True), axis=0, keepdims=True)
        dy = diff * (1.0 / D)
        dg_ref[...] += jnp.sum(dy * xh, axis=0, keepdims=True)
        dxh = dy * g_ref[...]
        dx_ref[...] = r * (dxh - xh * _mean(dxh * xh))

    return pl.pallas_call(
        body, name="final_loss", grid=(S // tm,),
        in_specs=[pl.BlockSpec((tm, D), lambda i: (i, 0)),
                  pl.BlockSpec((1, D), lambda i: (0, 0)),
                  pl.BlockSpec((tm, D), lambda i: (i, 0))],
        out_specs=[pl.BlockSpec((tm, D), lambda i: (i, 0)),
                   pl.BlockSpec((8, 128), lambda i: (0, 0)),
                   pl.BlockSpec((1, D), lambda i: (0, 0))],
        out_shape=[jax.ShapeDtypeStruct((S, D), F32), jax.ShapeDtypeStruct((8, 128), F32),
                   jax.ShapeDtypeStruct((1, D), F32)],
        compiler_params=_params(1, 40),
    )(x, gf, tgt)


def mlp_bwd(dx3, x2, g2, gs, us, wgt, wut, wd, l, deps=()):
    S = x2.shape[0]
    tm, tf = min(S, MLP_ROWS), MLP_COLS

    def body(dx_ref, x_ref, g_ref, gs_ref, us_ref, wg_ref, wu_ref, wd_ref,
             dx2_ref, dg_ref, du_ref, h_ref, dgain_ref):
        @pl.when(pl.program_id(0) == 0)
        def _():
            dgain_ref[...] = jnp.zeros((1, D), F32)

        dxv = dx_ref[...]
        dxb = dxv.astype(BF)
        for c0 in range(0, FF, tf):
            da = _dot_nt(dxb, wd_ref[c0:c0 + tf, :])
            gv = gs_ref[:, c0:c0 + tf].astype(F32)
            uv = us_ref[:, c0:c0 + tf].astype(F32)
            sg = _sigmoid(gv)
            dg_ref[:, c0:c0 + tf] = (da * uv * (sg * (1.0 + gv * (1.0 - sg)))).astype(BF)
            du_ref[:, c0:c0 + tf] = (da * (gv * sg)).astype(BF)
        dh = _dot(dg_ref[...], wg_ref[...]) + _dot(du_ref[...], wu_ref[...])
        dx2, dgain, hb = _rms_bwd(x_ref[...], g_ref[...], dh, dxv)
        dx2_ref[...] = dx2
        dgain_ref[...] += dgain
        h_ref[...] = hb

    wspec = pl.BlockSpec((FF, D), lambda i: (0, 0), pipeline_mode=pl.Buffered(1))
    row = pl.BlockSpec((tm, D), lambda i: (i, 0))
    wide = pl.BlockSpec((tm, FF), lambda i: (i, 0))
    vec = pl.BlockSpec((1, D), lambda i: (0, 0))
    return _ordered_call(
        body, deps, name=f"mlp_bwd_{l}", grid=(S // tm,),
        in_specs=[row, row, vec, wide, wide, wspec, wspec, wspec],
        out_specs=[row, wide, wide, row, vec],
        out_shape=[jax.ShapeDtypeStruct((S, D), F32), jax.ShapeDtypeStruct((S, FF), BF),
                   jax.ShapeDtypeStruct((S, FF), BF), jax.ShapeDtypeStruct((S, D), BF),
                   jax.ShapeDtypeStruct((1, D), F32)],
        compiler_params=_params(1, 56),
    )(dx3, x2, g2, gs, us, wgt, wut, wd, *deps)


def wgrad(a, b, l, name, deps=()):
    S, K = a.shape
    N = b.shape[1]
    tk = 1408 if K == FF else min(K, 1024)
    tn = min(N, 1024)
    ts = min(S, WGRAD_ROWS)
    ns = S // ts

    def body(a_ref, b_ref, o_ref, acc):
        s = pl.program_id(2)

        @pl.when(s == 0)
        def _():
            acc[...] = jnp.zeros((tk, tn), F32)

        acc[...] += _dot_tn(a_ref[...], b_ref[...].astype(BF))

        @pl.when(s == ns - 1)
        def _():
            o_ref[...] = acc[...].astype(BF)

    return _ordered_call(
        body, deps, name=f"{name}_{l}", grid=(K // tk, N // tn, ns),
        in_specs=[pl.BlockSpec((ts, tk), lambda i, j, s: (s, i)),
                  pl.BlockSpec((ts, tn), lambda i, j, s: (s, j))],
        out_specs=pl.BlockSpec((tk, tn), lambda i, j, s: (i, j)),
        out_shape=jax.ShapeDtypeStruct((K, N), BF),
        scratch_shapes=[pltpu.VMEM((tk, tn), F32)],
        compiler_params=_params(3, 48),
    )(a, b, *deps)


def out_proj_bwd(dx2, wout, r_raw, proj, gn, u1, lg, lb, l, deps=()):
    S = dx2.shape[0]
    tm = min(S, SEQ_TILE)
    gate_blk = (2 * CW + 3 * RW) // RW

    def body(dx_ref, w_ref, r_ref, gate_ref, gn_ref, u1_ref, lg_ref, lb_ref,
             dgate_ref, dr_ref, du1_ref, sums_ref):
        @pl.when(pl.program_id(0) == 0)
        def _():
            sums_ref[...] = jnp.zeros((8, CW), F32)

        dmix = _dot_nt(dx_ref[...].astype(BF), w_ref[...])
        gt = gate_ref[...].astype(F32)
        sg = _sigmoid(gt)
        sil = gt * sg
        dsil = sg * (1.0 + gt * (1.0 - sg))
        for h in range(NH):
            cols = slice(h * HD, (h + 1) * HD)
            rh = r_ref[:, cols]
            d = rh - _mean(rh)
            rs = lax.rsqrt(_mean(d * d) + EPS)
            rn = d * rs
            drr = dmix[:, CW + h * HD:CW + (h + 1) * HD]
            gnh = gn_ref[:, cols]
            sums_ref[0:1, cols] += jnp.sum(drr * rn * sil[:, cols], axis=0, keepdims=True)
            dgate_ref[:, cols] = (drr * rn * gnh * dsil[:, cols]).astype(BF)
            drn = drr * gnh * sil[:, cols]
            dr_ref[:, cols] = (rs * (drn - _mean(drn) - rn * _mean(drn * rn))).astype(BF)
        du = dmix[:, 0:CW]
        u1 = u1_ref[...]
        d = u1 - _mean(u1)
        rs = lax.rsqrt(_mean(d * d) + EPS)
        xh = d * rs
        u2 = xh * lg_ref[...] + lb_ref[...]
        sg2 = _sigmoid(u2)
        du2 = du * (sg2 * (1.0 + u2 * (1.0 - sg2)))
        sums_ref[1:2, :] += jnp.sum(du2 * xh, axis=0, keepdims=True)
        sums_ref[2:3, :] += jnp.sum(du2, axis=0, keepdims=True)
        dxh = du2 * lg_ref[...]
        du1_ref[...] = rs * (dxh - _mean(dxh) - xh * _mean(dxh * xh))

    vec = pl.BlockSpec((1, CW), lambda i: (0, 0))
    half = pl.BlockSpec((tm, CW), lambda i: (i, 0))
    return _ordered_call(
        body, deps, name=f"out_proj_bwd_{l}", grid=(S // tm,),
        in_specs=[pl.BlockSpec((tm, D), lambda i: (i, 0)),
                  pl.BlockSpec((D, D), lambda i: (0, 0)),
                  half, pl.BlockSpec((tm, RW), lambda i: (i, gate_blk)), vec, half, vec, vec],
        out_specs=[half, half, half, pl.BlockSpec((8, CW), lambda i: (0, 0))],
        out_shape=[jax.ShapeDtypeStruct((S, RW), BF), jax.ShapeDtypeStruct((S, RW), BF),
                   jax.ShapeDtypeStruct((S, CW), F32), jax.ShapeDtypeStruct((8, CW), F32)],
        compiler_params=_params(1, 40),
    )(dx2, wout, r_raw, proj, gn, u1, lg, lb, *deps)


def _conv_bwd_fill(tc, du1_ref, dwb_ref, buf, pacc):
    i = pl.program_id(0)

    @pl.when(i == 0)
    def _():
        buf[tc:tc + HALO, :] = jnp.zeros((HALO, CW), F32)
        dwb_ref[...] = jnp.zeros((CK + 1, CW), F32)
        pacc[...] = jnp.zeros((CK, 8, CW), F32)

    @pl.when(i > 0)
    def _():
        buf[tc:tc + HALO, :] = buf[0:HALO, :]

    buf[0:tc, :] = du1_ref[...]


def _conv_bwd_rows(r0, groups, pacc, a_ref, b_ref, w_ref, dab_ref, buf, win):
    for c0 in range(0, CW, CONV_COLS):
        cols = slice(c0, c0 + CONV_COLS)
        av = a_ref[r0:r0 + CONV_ROWS, cols].astype(F32)
        sgb = _sigmoid(b_ref[r0:r0 + CONV_ROWS, cols].astype(F32))
        u0 = av * sgb
        acc = jnp.zeros((CONV_ROWS, CONV_COLS), F32)
        for start, length, taps in groups:
            win[0:length, :] = buf[r0 + start:r0 + start + length, cols]
            for j, at in taps:
                sl = win[at:at + CONV_ROWS, :]
                acc = acc + w_ref[CK - 1 - j:CK - j, cols] * sl
                pacc[CK - 1 - j, :, cols] += _fold8(u0 * sl)
        dab_ref[r0:r0 + CONV_ROWS, c0:c0 + CONV_COLS] = (acc * sgb).astype(BF)
        dab_ref[r0:r0 + CONV_ROWS, CW + c0:CW + c0 + CONV_COLS] = (acc * av * sgb * (1.0 - sgb)).astype(BF)


def _conv_bwd_finish(nt, pacc, du1_ref, dwb_ref):
    dwb_ref[CK:CK + 1, :] += jnp.sum(du1_ref[...], axis=0, keepdims=True)

    @pl.when(pl.program_id(0) == nt - 1)
    def _():
        for k in range(CK):
            dwb_ref[k:k + 1, :] = jnp.sum(pacc[k], axis=0, keepdims=True)


def _ret_bwd_chunk(c, q_ref, k_ref, v_ref, cos_ref, sin_ref, dm_ref, qd_ref, kd_ref, cd_ref, dr_ref, st_ref,
                   dq_ref, dk_ref, dv_ref, gst):
    scale = HD ** -0.5
    rows = slice(c * CHUNK, (c + 1) * CHUNK)
    cs, sn = cos_ref[rows, :], sin_ref[rows, :]
    for h in range(NH):
        cols = slice(h * HD, (h + 1) * HD)
        qr = _rot(q_ref[rows, cols].astype(F32), cs, sn)
        kr = _rot(k_ref[rows, cols].astype(F32), cs, sn) * scale
        qb, kb = qr.astype(BF), kr.astype(BF)
        vb = v_ref[rows, cols].astype(BF)
        dob = dr_ref[rows, cols]
        sb = st_ref[h, c]
        gn1 = gst[h]
        gb = gn1.astype(BF)
        sc = (_dot_nt(qb, kb) * dm_ref[h]).astype(BF)
        dsc = (_dot_nt(dob, vb) * dm_ref[h]).astype(BF)
        dqr = _dot(dsc, kb) + _dot_nt(dob, sb) * qd_ref[h]
        dkr = _dot_tn(dsc, qb) + _dot_nt(vb, gb) * kd_ref[h]
        dvv = _dot_tn(sc, dob) + _dot((kr * kd_ref[h]).astype(BF), gb)
        gst[h] = cd_ref[h] * gn1 + _dot_tn((qr * qd_ref[h]).astype(BF), dob)
        dq_ref[rows, cols] = _rot_t(dqr, cs, sn).astype(BF)
        dk_ref[rows, cols] = _rot_t(dkr * scale, cs, sn).astype(BF)
        dv_ref[rows, cols] = dvv.astype(BF)


def mix_bwd(du1, dr, proj, cw, states, tables, l, deps=()):
    S = proj.shape[0]
    tt = min(S, SEQ_TILE)
    cpb = tt // CHUNK
    nt = S // tt
    groups = _tap_groups(0, CK)

    def body(du1_ref, a_ref, b_ref, w_ref, q_ref, k_ref, v_ref, cos_ref, sin_ref, dm_ref, qd_ref, kd_ref, cd_ref,
             dr_ref, st_ref, dab_ref, dwb_ref, dq_ref, dk_ref, dv_ref, buf, win, gst, pacc):
        @pl.when(pl.program_id(0) == 0)
        def _():
            gst[...] = jnp.zeros((NH, HD, HD), F32)

        _conv_bwd_fill(tt, du1_ref, dwb_ref, buf, pacc)
        for c in reversed(range(cpb)):
            _ret_bwd_chunk(c, q_ref, k_ref, v_ref, cos_ref, sin_ref, dm_ref, qd_ref, kd_ref, cd_ref, dr_ref, st_ref,
                           dq_ref, dk_ref, dv_ref, gst)
            for r0 in range(c * CHUNK, (c + 1) * CHUNK, CONV_ROWS):
                _conv_bwd_rows(r0, groups, pacc, a_ref, b_ref, w_ref, dab_ref, buf, win)
        _conv_bwd_finish(nt, pacc, du1_ref, dwb_ref)

    rev = lambda t: nt - 1 - t
    half = pl.BlockSpec((tt, CW), lambda t: (rev(t), 0))
    return _ordered_call(
        body, deps, name=f"mix_bwd_{l}", grid=(nt,),
        in_specs=[half, half, pl.BlockSpec((tt, CW), lambda t: (rev(t), 1)), pl.BlockSpec((CK, CW), lambda t: (0, 0))]
        + _ret_specs(tt, rev) + [half, pl.BlockSpec((NH, cpb, HD, HD), lambda t: (0, rev(t), 0, 0))],
        out_specs=[pl.BlockSpec((tt, 2 * CW), lambda t: (rev(t), 0)), pl.BlockSpec((CK + 1, CW), lambda t: (0, 0)),
                   half, half, half],
        out_shape=[jax.ShapeDtypeStruct((S, 2 * CW), BF), jax.ShapeDtypeStruct((CK + 1, CW), F32)]
        + [jax.ShapeDtypeStruct((S, RW), BF)] * 3,
        scratch_shapes=[pltpu.VMEM((tt + HALO, CW), F32), pltpu.VMEM((HALO + CONV_ROWS, CONV_COLS), F32),
                        pltpu.VMEM((NH, HD, HD), F32), pltpu.VMEM((CK, 8, CW), F32)],
        compiler_params=_params(1, 40),
    )(du1, proj, proj, cw, proj, proj, proj, *tables, dr, states, *deps)


def in_proj_bwd(parts, win, x, g, dx2, l, deps=()):
    S = x.shape[0]
    tm = min(S, SEQ_TILE)
    n = len(parts)

    def body(*refs):
        srcs = refs[:n]
        w_ref, x_ref, g_ref, dx2_ref, dx_ref, dgain_ref = refs[n:]

        @pl.when(pl.program_id(0) == 0)
        def _():
            dgain_ref[...] = jnp.zeros((1, D), F32)

        dh, col = None, 0
        for r in srcs:
            width = r.shape[1]
            term = _dot_nt(r[...], w_ref[:, col:col + width])
            dh = term if dh is None else dh + term
            col += width
        dx, dgain, _ = _rms_bwd(x_ref[...], g_ref[...], dh, dx2_ref[...])
        dx_ref[...] = dx
        dgain_ref[...] += dgain

    row = pl.BlockSpec((tm, D), lambda i: (i, 0))
    vec = pl.BlockSpec((1, D), lambda i: (0, 0))
    return _ordered_call(
        body, deps, name=f"in_proj_bwd_{l}", grid=(S // tm,),
        in_specs=[pl.BlockSpec((tm, p.shape[1]), lambda i: (i, 0)) for p in parts]
        + [pl.BlockSpec((D, INW), lambda i: (0, 0)), row, vec, row],
        out_specs=[row, vec],
        out_shape=[jax.ShapeDtypeStruct((S, D), F32), jax.ShapeDtypeStruct((1, D), F32)],
        compiler_params=_params(1, 48),
    )(*parts, win, x, g, dx2, *deps)


def wgrad_in(h, parts, l):
    S = h.shape[0]
    ts = min(S, SEQ_TILE)
    ns = S // ts

    def body(*refs):
        h_ref, srcs = refs[0], refs[1:1 + len(parts)]
        o_ref, acc = refs[-2], refs[-1]
        s = pl.program_id(0)

        @pl.when(s == 0)
        def _():
            acc[...] = jnp.zeros((D, INW), F32)

        hv = h_ref[...]
        col = 0
        for r in srcs:
            width = r.shape[1]
            acc[:, col:col + width] += _dot_tn(hv, r[...])
            col += width

        @pl.when(s == ns - 1)
        def _():
            o_ref[...] = acc[...].astype(BF)

    return pl.pallas_call(
        body, name=f"wgrad_in_{l}", grid=(ns,),
        in_specs=[pl.BlockSpec((ts, D), lambda s: (s, 0))]
        + [pl.BlockSpec((ts, p.shape[1]), lambda s: (s, 0)) for p in parts],
        out_specs=pl.BlockSpec((D, INW), lambda s: (0, 0)),
        out_shape=jax.ShapeDtypeStruct((D, INW), BF),
        scratch_shapes=[pltpu.VMEM((D, INW), F32)],
        compiler_params=_params(1, 48),
    )(h, *parts)


def sum_slots(recv, name, deps=()):
    _, R, C = recv.shape
    tr = 256 if R % 256 == 0 else R

    def body(r_ref, o_ref):
        acc = r_ref[0].astype(F32)
        for k in range(1, NCHIP):
            acc = acc + r_ref[k].astype(F32)
        o_ref[...] = acc

    return _ordered_call(
        body, deps, name=name, grid=(R // tr,),
        in_specs=[pl.BlockSpec((NCHIP, tr, C), lambda i: (0, i, 0))],
        out_specs=pl.BlockSpec((tr, C), lambda i: (i, 0)),
        out_shape=jax.ShapeDtypeStruct((R, C), F32),
        compiler_params=_params(1, 32),
    )(recv, *deps)


def adamw(w, ga, gb, m, v, name):
    R, C = w.shape
    tr = 256 if R % 256 == 0 else R
    c1 = 1.0 - ADAM_B1 ** ADAM_STEP
    c2 = 1.0 - ADAM_B2 ** ADAM_STEP

    def body(w_ref, ga_ref, gb_ref, m_ref, v_ref, g_out, d_out, m_out, v_out):
        g = ga_ref[...] + gb_ref[...]
        mn = ADAM_B1 * m_ref[...] + (1.0 - ADAM_B1) * g
        vn = ADAM_B2 * v_ref[...] + (1.0 - ADAM_B2) * (g * g)
        g_out[...] = g
        m_out[...] = mn
        v_out[...] = vn
        d_out[...] = -ADAM_LR * ((mn / c1) / (jnp.sqrt(vn / c2) + ADAM_EPS) + ADAM_WD * w_ref[...])

    blk = pl.BlockSpec((tr, C), lambda i: (i, 0))
    return pl.pallas_call(
        body, name=name, grid=(R // tr,),
        in_specs=[blk] * 5, out_specs=[blk] * 4,
        out_shape=[jax.ShapeDtypeStruct((R, C), F32)] * 4,
        compiler_params=_params(1, 40),
    )(w, ga, gb, m, v)


def _place():
    x, y, c = lax.axis_index("x"), lax.axis_index("y"), lax.axis_index("c")
    chips = [(1 - x, y), (x, 1 - y), (1 - x, 1 - y)]
    return x, y, c, chips


def _window(ref, axis, j, size):
    idx = [slice(None)] * len(ref.shape)
    idx[axis] = pl.ds(pl.multiple_of(j * size, 128 if axis == len(ref.shape) - 1 else 16), size)
    return ref.at[tuple(idx)]


def _hbm(a):
    return pltpu.with_memory_space_constraint(a, pltpu.HBM)


def _hbm_like(arrs):
    return [pltpu.HBM(a.shape, a.dtype) for a in arrs]


def gather_start(shards, axes, after, tag):
    n = len(shards)
    na = len(after)
    lands = []
    for s, ax in zip(shards, axes):
        shp = list(s.shape)
        shp[ax] *= NCHIP
        lands.append(lax.empty(tuple(shp), s.dtype))

    def body(*refs):
        ins, land = refs[:n], refs[n:2 * n]
        send, recv = refs[2 * n + na], refs[2 * n + na + 1]
        token = refs[-1]
        x, y, c, chips = _place()
        for a in range(n):
            for k, chip in enumerate(chips):
                pltpu.make_async_remote_copy(
                    src_ref=ins[a], dst_ref=_window(land[a], axes[a], 2 * x + y, ins[a].shape[axes[a]]),
                    send_sem=send.at[3 * a + k], recv_sem=recv.at[3 * a + k],
                    device_id=(chip[0], chip[1], c), device_id_type=MESH).start()
        token[...] = jnp.zeros_like(token)

    outs = pl.pallas_call(
        body, name=f"gather_start_{tag}",
        in_specs=[HBM_SPEC] * (2 * n) + [ANY] * na,
        out_specs=(SEM_SPEC, SEM_SPEC, *[HBM_SPEC] * (2 * n), VMEM_SPEC),
        out_shape=(pltpu.SemaphoreType.DMA((3 * n,)), pltpu.SemaphoreType.DMA((3 * n,)),
                   *_hbm_like(shards), *_hbm_like(lands), jax.ShapeDtypeStruct((8, 128), F32)),
        input_output_aliases={a: 2 + a for a in range(2 * n)},
        compiler_params=pltpu.CompilerParams(has_side_effects=DATAFLOW),
    )(*[_hbm(s) for s in shards], *[_hbm(b) for b in lands], *after)
    return (outs[0], outs[1], list(outs[2:2 + n]), list(outs[2 + n:2 + 2 * n]), list(axes)), outs[-1]


def gather_wait(groups, after, tag):
    sizes = [len(g[2]) for g in groups]
    total = sum(sizes)

    def body(*refs):
        x, y, c, chips = _place()
        stage, loc = refs[-1 - total:-1], refs[-1]
        pos = 2 * total
        off = 0
        mine = []
        for g, n in zip(groups, sizes):
            ins, land = refs[off:off + n], refs[total + off:total + off + n]
            send_ref, recv_ref = refs[pos], refs[pos + 1]
            axes = g[4]
            for a in range(n):
                fetch = pltpu.make_async_copy(ins[a], stage[off + a], loc.at[2 * (off + a)])
                fetch.start()
                put = pltpu.make_async_copy(
                    stage[off + a], _window(land[a], axes[a], 2 * x + y, ins[a].shape[axes[a]]),
                    loc.at[2 * (off + a) + 1])
                mine.append((fetch, put))
                for k, chip in enumerate(chips):
                    cp = pltpu.make_async_remote_copy(
                        src_ref=ins[a],
                        dst_ref=_window(land[a], axes[a], 2 * chip[0] + chip[1], ins[a].shape[axes[a]]),
                        send_sem=send_ref.at[3 * a + k], recv_sem=recv_ref.at[3 * a + k],
                        device_id=(chip[0], chip[1], c), device_id_type=MESH)
                    cp.wait_send()
                    cp.wait_recv()
            pos += 2
            off += n
        for fetch, put in mine:
            fetch.wait()
            put.start()
        for fetch, put in mine:
            put.wait()

    shards = [s for g in groups for s in g[2]]
    lands = [b for g in groups for b in g[3]]
    sems = [s for g in groups for s in (g[0], g[1])]
    outs = pl.pallas_call(
        body, name=f"gather_wait_{tag}",
        in_specs=[HBM_SPEC] * (2 * total) + [SEM_SPEC] * len(sems) + [ANY],
        out_specs=[HBM_SPEC] * (2 * total),
        out_shape=(*_hbm_like(shards), *_hbm_like(lands)),
        input_output_aliases={a: a for a in range(2 * total)},
        scratch_shapes=[pltpu.VMEM(s.shape, s.dtype) for s in shards] + [pltpu.SemaphoreType.DMA((2 * total,))],
        compiler_params=pltpu.CompilerParams(has_side_effects=DATAFLOW, vmem_limit_bytes=32 << 20),
    )(*shards, *lands, *sems, after)
    return list(outs[total:])


def scatter_start(grads, axes, sizes, lands, l, tag):
    n = len(grads)

    def body(*refs):
        ins, land = refs[:n], refs[n:2 * n]
        send, recv = refs[2 * n], refs[2 * n + 1]
        token = refs[2 * n + 2 + 2 * n]
        stage, loc = refs[-1 - n:-1], refs[-1]
        x, y, c, chips = _place()
        me = 2 * x + y
        fetches = [pltpu.make_async_copy(_window(ins[a], axes[a], me, sizes[a]), stage[a], loc.at[2 * a])
                   for a in range(n)]
        for cp in fetches:
            cp.start()
        for a in range(n):
            for k, chip in enumerate(chips):
                pltpu.make_async_remote_copy(
                    src_ref=_window(ins[a], axes[a], 2 * chip[0] + chip[1], sizes[a]), dst_ref=land[a].at[me, l],
                    send_sem=send.at[3 * a + k], recv_sem=recv.at[3 * a + k],
                    device_id=(chip[0], chip[1], c), device_id_type=MESH).start()
        puts = [pltpu.make_async_copy(stage[a], land[a].at[me, l], loc.at[2 * a + 1]) for a in range(n)]
        for fetch, put in zip(fetches, puts):
            fetch.wait()
            put.start()
        for put in puts:
            put.wait()
        token[...] = jnp.zeros_like(token)

    outs = pl.pallas_call(
        body, name=f"scatter_start_{tag}",
        in_specs=[HBM_SPEC] * (2 * n),
        out_specs=(SEM_SPEC, SEM_SPEC, *[HBM_SPEC] * (2 * n), VMEM_SPEC),
        out_shape=(pltpu.SemaphoreType.DMA((3 * n,)), pltpu.SemaphoreType.DMA((3 * n,)),
                   *_hbm_like(grads), *_hbm_like(lands), jax.ShapeDtypeStruct((8, 128), F32)),
        input_output_aliases={a: 2 + a for a in range(2 * n)},
        scratch_shapes=[pltpu.VMEM(b.shape[2:], b.dtype) for b in lands] + [pltpu.SemaphoreType.DMA((2 * n,))],
        compiler_params=pltpu.CompilerParams(has_side_effects=DATAFLOW, vmem_limit_bytes=32 << 20),
    )(*[_hbm(g) for g in grads], *[_hbm(b) for b in lands])
    group = (outs[0], outs[1], list(outs[2:2 + n]), list(axes), list(sizes), l)
    return group, list(outs[2 + n:2 + 2 * n]), outs[-1]


def scatter_wait(groups, lands, which, after):
    nl = len(lands)

    def body(*refs):
        land = refs[:nl]
        x, y, c, chips = _place()
        pos = nl
        for g, wh in zip(groups, which):
            n = len(g[2])
            ins = refs[pos:pos + n]
            send_ref, recv_ref = refs[pos + n], refs[pos + n + 1]
            axes, sizes, l = g[3], g[4], g[5]
            for a in range(n):
                for k, chip in enumerate(chips):
                    jp = 2 * chip[0] + chip[1]
                    cp = pltpu.make_async_remote_copy(
                        src_ref=_window(ins[a], axes[a], jp, sizes[a]), dst_ref=land[wh[a]].at[jp, l],
                        send_sem=send_ref.at[3 * a + k], recv_sem=recv_ref.at[3 * a + k],
                        device_id=(chip[0], chip[1], c), device_id_type=MESH)
                    cp.wait_send()
                    cp.wait_recv()
            pos += n + 2

    operands = list(lands)
    specs = [HBM_SPEC] * nl
    for g in groups:
        operands += list(g[2]) + [g[0], g[1]]
        specs += [HBM_SPEC] * len(g[2]) + [SEM_SPEC, SEM_SPEC]
    outs = pl.pallas_call(
        body, name="scatter_wait", in_specs=specs + [ANY], out_specs=[HBM_SPEC] * nl,
        out_shape=tuple(_hbm_like(lands)),
        input_output_aliases={a: a for a in range(nl)},
        compiler_params=pltpu.CompilerParams(has_side_effects=DATAFLOW),
    )(*operands, after)
    return list(outs)


def swap_start(part, tag):
    def body(p_ref, land_ref, send, recv, p_thru, land_thru, token):
        x, y, c, _ = _place()
        pltpu.make_async_remote_copy(src_ref=p_ref, dst_ref=land_ref, send_sem=send, recv_sem=recv,
                                     device_id=(x, y, 1 - c), device_id_type=MESH).start()
        token[...] = jnp.zeros_like(token)

    outs = pl.pallas_call(
        body, name=f"swap_start_{tag}", in_specs=[HBM_SPEC, HBM_SPEC],
        out_specs=(SEM_SPEC, SEM_SPEC, HBM_SPEC, HBM_SPEC, VMEM_SPEC),
        out_shape=(pltpu.SemaphoreType.DMA(()), pltpu.SemaphoreType.DMA(()), *_hbm_like([part, part]),
                   jax.ShapeDtypeStruct((8, 128), F32)),
        input_output_aliases={0: 2, 1: 3},
        compiler_params=pltpu.CompilerParams(has_side_effects=DATAFLOW),
    )(_hbm(part), _hbm(lax.empty(part.shape, part.dtype)))
    return tuple(outs[:4]), outs[4]


def swap_wait(group, after, tag):
    send, recv, part, land = group

    def body(p_ref, land_ref, send_ref, recv_ref, after_ref, p_out, land_out):
        x, y, c, _ = _place()
        cp = pltpu.make_async_remote_copy(src_ref=p_ref, dst_ref=land_ref, send_sem=send_ref, recv_sem=recv_ref,
                                          device_id=(x, y, 1 - c), device_id_type=MESH)
        cp.wait_send()
        cp.wait_recv()

    outs = pl.pallas_call(
        body, name=f"swap_wait_{tag}", in_specs=[HBM_SPEC, HBM_SPEC, SEM_SPEC, SEM_SPEC, ANY],
        out_specs=[HBM_SPEC, HBM_SPEC], out_shape=tuple(_hbm_like([part, land])),
        input_output_aliases={0: 0, 1: 1},
        compiler_params=pltpu.CompilerParams(has_side_effects=DATAFLOW),
    )(part, land, send, recv, after)
    return outs[0], outs[1]


def small_allreduce(p):
    R, C = p.shape
    ndev = 8

    def body(p_ref, o_ref, buf, send, recv):
        x, y, c, _ = _place()
        me = 4 * x + 2 * y + c
        buf[me] = p_ref[...]

        def peer(d):
            px = 1 - x if d & 4 else x
            py = 1 - y if d & 2 else y
            pc = 1 - c if d & 1 else c
            return px, py, pc

        def copy(d, slot):
            return pltpu.make_async_remote_copy(src_ref=p_ref, dst_ref=buf.at[slot], send_sem=send.at[d - 1],
                                                recv_sem=recv.at[d - 1], device_id=peer(d), device_id_type=MESH)

        sends = [copy(d, me) for d in range(1, ndev)]
        for cp in sends:
            cp.start()
        for d in range(1, ndev):
            px, py, pc = peer(d)
            copy(d, 4 * px + 2 * py + pc).wait_recv()
        for cp in sends:
            cp.wait_send()
        acc = buf[0]
        for k in range(1, ndev):
            acc = acc + buf[k]
        o_ref[...] = acc

    return pl.pallas_call(
        body, name="small_allreduce", in_specs=[VMEM_SPEC], out_specs=VMEM_SPEC,
        out_shape=jax.ShapeDtypeStruct((R, C), F32),
        scratch_shapes=[pltpu.VMEM((ndev, R, C), F32), pltpu.SemaphoreType.DMA((ndev - 1,)),
                        pltpu.SemaphoreType.DMA((ndev - 1,))],
        compiler_params=pltpu.CompilerParams(vmem_limit_bytes=32 << 20),
    )(p)


def kernel(x, norm1_g, w_in, conv_w, conv_b, conv_ln_g, conv_ln_b, ret_gn_g, w_out, norm2_g, w_gate, w_up, w_down, final_g, loss_target, m_norm1_g, m_w_in, m_conv_w, m_conv_b, m_conv_ln_g, m_conv_ln_b, m_ret_gn_g, m_w_out, m_norm2_g, m_w_gate, m_w_up, m_w_down, m_final_g, v_norm1_g, v_w_in, v_conv_w, v_conv_b, v_conv_ln_g, v_conv_ln_b, v_ret_gn_g, v_w_out, v_norm2_g, v_w_gate, v_w_up, v_w_down, v_final_g):
    S = x.shape[1]
    xs = x.reshape(S, D)
    tgt = loss_target.reshape(S, D)
    fsh = FF // NCHIP

    def shards_of(l):
        return [w_in[l].astype(BF), w_out[l].astype(BF), w_gate[l].T.astype(BF), w_up[l].T.astype(BF),
                w_down[l].astype(BF), conv_w[l]]

    gather_axes = [1, 0, 0, 0, 0, 1]
    shard_cache = [shards_of(l) for l in range(L)]
    tables = _ret_tables(S)
    row = lambda a, l: a[l].reshape(1, -1)
    groups = {}
    weights = [dict() for _ in range(L)]

    def begin(l, which, after):
        group, token = gather_start([shard_cache[l][i] for i in which], [gather_axes[i] for i in which],
                                    after, f"{l}_{which[0]}")
        groups[(l, which[0])] = (group, which)
        return token

    def finish(l, firsts, after, tag):
        gs = [groups[(l, f)] for f in firsts]
        outs = gather_wait([g for g, _ in gs], after, f"{l}_{tag}")
        k = 0
        for _, which in gs:
            for i in which:
                weights[l][i] = outs[k]
                k += 1

    first = begin(0, [0], [])
    second = begin(0, [1, 5], [first])
    after = begin(0, [2, 3, 4], [second])
    saved = []
    xc = xs
    for l in range(L):
        if l == 0:
            finish(l, [0], after, "a")
        else:
            finish(l, [0], after, "all")
        win = weights[l][0]
        proj, h1 = in_proj(xc, row(norm1_g, l), win, l)
        if l == 0:
            finish(l, [1], proj, "b")
        wout, cw = weights[l][1], weights[l][5]
        ahead = [begin(l + 1, [0, 1, 2, 3, 4, 5], [proj])] if l + 1 < L else []
        u1, u, r_raw, states = mix_fwd(proj, cw, row(conv_b, l), row(conv_ln_g, l), row(conv_ln_b, l), tables, l, ahead)
        x2, mixed = out_proj(u, r_raw, proj, row(ret_gn_g, l), wout, xc, l)
        if l == 0:
            finish(l, [2], x2, "c")
        wgt, wut, wd = weights[l][2], weights[l][3], weights[l][4]
        x3, gs, us, act = mlp_fwd(x2, row(norm2_g, l), wgt, wut, wd, l)
        saved.append((xc, proj, h1, u1, r_raw, states, mixed, x2, gs, us, act))
        xc = x3
        after = x3

    dx, loss_acc, d_final = final_loss(xc, final_g.reshape(1, D), tgt)
    loss = lax.psum(loss_acc[0, 0] * (0.5 / D), ("x", "y", "c"))

    scatter_axes = [1, 0, 0, 0, 0]
    scatter_sizes = [INW // NCHIP, D // NCHIP, fsh, fsh, fsh]
    lands = [lax.empty((NCHIP, L, D, INW // NCHIP), BF), lax.empty((NCHIP, L, D // NCHIP, D), BF),
             lax.empty((NCHIP, L, fsh, D), BF), lax.empty((NCHIP, L, fsh, D), BF), lax.empty((NCHIP, L, fsh, D), BF)]
    sent, sent_which = [], []

    def send_grad(g, a, l):
        group, new_land, token = scatter_start([g], [scatter_axes[a]], [scatter_sizes[a]], [lands[a]], l, f"{l}_{a}")
        lands[a] = new_land[0]
        sent.append(group)
        sent_which.append([a])
        return [token]

    small = [None] * L
    for l in reversed(range(L)):
        xin, proj, h1, u1, r_raw, states, mixed, x2, gs, us, act = saved[l]
        win, wout, wgt, wut, wd, cw = (weights[l][i] for i in range(6))
        dx2, dgs, dus, h2, d_n2 = mlp_bwd(dx, x2, row(norm2_g, l), gs, us, wgt, wut, wd, l)
        g_wd = wgrad(act, dx, l, "wgrad_down")
        g_wgt = wgrad(dgs, h2, l, "wgrad_gate", send_grad(g_wd, 4, l))
        g_wut = wgrad(dus, h2, l, "wgrad_up", send_grad(g_wgt, 2, l))
        dgate, dr, du1, sums = out_proj_bwd(dx2, wout, r_raw, proj, row(ret_gn_g, l), u1,
                                            row(conv_ln_g, l), row(conv_ln_b, l), l, send_grad(g_wut, 3, l))
        g_wout = wgrad(mixed, dx2, l, "wgrad_out")
        dab, dwb, dq, dk, dv = mix_bwd(du1, dr, proj, cw, states, tables, l, send_grad(g_wout, 1, l))
        dproj = [dab, dq, dk, dv, dgate]
        g_win = wgrad_in(h1, dproj, l)
        dx, d_n1 = in_proj_bwd(dproj, win, xin, row(norm1_g, l), dx2, l, send_grad(g_win, 0, l))
        small[l] = jnp.concatenate([dwb, sums, d_n1.reshape(2, CW), d_n2.reshape(2, CW)], axis=0)
    grad_x = dx.reshape(1, S, D)

    per = CK + 1 + 8 + 4
    packed = jnp.concatenate(small + [d_final.reshape(2, CW), jnp.zeros((6, CW), F32)], axis=0)
    tot = small_allreduce(packed)
    lay = tot[:L * per].reshape(L, per, CW)
    g_conv_w_full = lay[:, 0:CK, :]
    j = 2 * lax.axis_index("x") + lax.axis_index("y")
    g_conv_w = lax.dynamic_slice_in_dim(g_conv_w_full, j * (CW // NCHIP), CW // NCHIP, axis=2)
    g_small = {
        "conv_b": lay[:, CK, :], "ret_gn_g": lay[:, CK + 1, :], "conv_ln_g": lay[:, CK + 2, :],
        "conv_ln_b": lay[:, CK + 3, :], "norm1_g": lay[:, CK + 9:CK + 11, :].reshape(L, D),
        "norm2_g": lay[:, CK + 11:CK + 13, :].reshape(L, D), "final_g": tot[L * per:L * per + 2].reshape(D),
    }

    recv = scatter_wait(sent, lands, sent_which, tot)
    shard_shapes = [(L * D, INW // NCHIP), (L * D // NCHIP, D), (L * fsh, D), (L * fsh, D), (L * fsh, D)]
    names = ["w_in", "w_out", "w_gate", "w_up", "w_down"]
    swaps, token = [], []
    for r, shp, nm in zip(recv, shard_shapes, names):
        group, tk = swap_start(sum_slots(r.reshape((NCHIP,) + shp), f"sum_{nm}", token), nm)
        swaps.append(group)
        token = [tk]
    parts, theirs = zip(*[swap_wait(group, token[0], nm) for group, nm in zip(swaps, names)])

    def unT(a):
        return jnp.swapaxes(a.reshape(L, fsh, D), 1, 2).reshape(L * D, fsh)

    big = {}
    wmv = {"w_in": (w_in, m_w_in, v_w_in), "w_out": (w_out, m_w_out, v_w_out),
           "w_gate": (w_gate, m_w_gate, v_w_gate), "w_up": (w_up, m_w_up, v_w_up),
           "w_down": (w_down, m_w_down, v_w_down)}
    for nm, mine, other in zip(names, parts, theirs):
        w, m, v = wmv[nm]
        if nm in ("w_gate", "w_up"):
            mine, other = unT(mine), unT(other)
        shp2 = (w.shape[0] * w.shape[1], w.shape[2])
        outs = adamw(w.reshape(shp2), mine, other, m.reshape(shp2), v.reshape(shp2), f"adamw_{nm}")
        big[nm] = [o.reshape(w.shape) for o in outs]

    cshape = (L * CK, CW // NCHIP)
    zc = jnp.zeros(cshape, F32)
    big["conv_w"] = [o.reshape(conv_w.shape) for o in adamw(
        conv_w.reshape(cshape), g_conv_w.reshape(cshape), zc, m_conv_w.reshape(cshape),
        v_conv_w.reshape(cshape), "adamw_conv_w")]
    vec_names = ["norm1_g", "conv_b", "conv_ln_g", "conv_ln_b", "ret_gn_g", "norm2_g", "final_g"]
    vec_w = {"norm1_g": (norm1_g, m_norm1_g, v_norm1_g), "conv_b": (conv_b, m_conv_b, v_conv_b),
             "conv_ln_g": (conv_ln_g, m_conv_ln_g, v_conv_ln_g), "conv_ln_b": (conv_ln_b, m_conv_ln_b, v_conv_ln_b),
             "ret_gn_g": (ret_gn_g, m_ret_gn_g, v_ret_gn_g), "norm2_g": (norm2_g, m_norm2_g, v_norm2_g),
             "final_g": (final_g, m_final_g, v_final_g)}
    cat = lambda arrs: jnp.concatenate([a.reshape(-1, CW) for a in arrs], axis=0)
    vw = cat([vec_w[nm][0] for nm in vec_names])
    vm = cat([vec_w[nm][1] for nm in vec_names])
    vv = cat([vec_w[nm][2] for nm in vec_names])
    vg = cat([g_small[nm] for nm in vec_names])
    vouts = adamw(vw, vg, jnp.zeros_like(vg), vm, vv, "adamw_vectors")
    off = 0
    for nm in vec_names:
        w = vec_w[nm][0]
        nrow = w.size // CW
        big[nm] = [o[off:off + nrow].reshape(w.shape) for o in vouts]
        off += nrow

    order = ["norm1_g", "w_in", "conv_w", "conv_b", "conv_ln_g", "conv_ln_b", "ret_gn_g", "w_out", "norm2_g",
             "w_gate", "w_up", "w_down", "final_g"]
    return (loss, grad_x, *[big[nm][0] for nm in order], *[big[nm][1] for nm in order],
            *[big[nm][2] for nm in order], *[big[nm][3] for nm in order])
```

```python
import math

import jax
import jax.numpy as jnp
from jax import lax
from jax.experimental import pallas as pl
from jax.experimental.pallas import tpu as pltpu

D = 1024
L = 4
CW = 512
RW = 512
NH = 4
HD = 128
CK = 31
CHUNK = 64
INW = 3072
FF = 2816
NCHIP = 4
EPS = 1e-6
ROPE_BASE = 10000.0
SEQ_TILE = 512
WGRAD_ROWS = 1024
MLP_ROWS = 256
MLP_COLS = 1408
HALO = 32
CONV_ROWS = 32
CONV_COLS = 256

ADAM_LR = 0.001
ADAM_B1 = 0.9
ADAM_B2 = 0.999
ADAM_EPS = 1e-08
ADAM_WD = 0.01
ADAM_STEP = 10

BF = jnp.bfloat16
F32 = jnp.float32
MESH = pl.DeviceIdType.MESH
ANY = pl.BlockSpec(memory_space=pl.ANY)
VMEM_SPEC = pl.BlockSpec(memory_space=pltpu.VMEM)
HBM_SPEC = pl.BlockSpec(memory_space=pltpu.HBM)
SEM_SPEC = pl.BlockSpec(memory_space=pltpu.SEMAPHORE)
DATAFLOW = pltpu.SideEffectType.DATAFLOW_SIDE_EFFECTING


def _params(n_grid, vmem_mb):
    return pltpu.CompilerParams(dimension_semantics=("arbitrary",) * n_grid,
                                vmem_limit_bytes=vmem_mb << 20)


def _ordered_call(body, deps, *, in_specs, **kw):
    n, nd = len(in_specs), len(deps)

    def with_deps(*refs):
        body(*refs[:n], *refs[n + nd:])

    return pl.pallas_call(with_deps, in_specs=list(in_specs) + [ANY] * nd, **kw)


def _dot(a, b):
    return jnp.dot(a, b, preferred_element_type=F32)


def _dot_nt(a, b):
    return lax.dot_general(a, b, (((1,), (1,)), ((), ())), preferred_element_type=F32)


def _dot_tn(a, b):
    return lax.dot_general(a, b, (((0,), (0,)), ((), ())), preferred_element_type=F32)


def _sigmoid(x):
    return 0.5 * jnp.tanh(0.5 * x) + 0.5


def _mean(x):
    return jnp.mean(x, axis=-1, keepdims=True)


def _fold8(x):
    out = x[0:8, :]
    for q in range(1, x.shape[0] // 8):
        out = out + x[8 * q:8 * q + 8, :]
    return out


def _tap_groups(first, count):
    groups = []
    for phase in range(8):
        taps = [(t, first + t - phase) for t in range(count) if (first + t) % 8 == phase]
        if taps:
            lo, hi = min(q for _, q in taps), max(q for _, q in taps)
            groups.append((lo + phase, hi - lo + CONV_ROWS, [(t, q - lo) for t, q in taps]))
    return groups


def _rot(t, cs, sn):
    return t * cs + pltpu.roll(t, HD // 2, 1) * sn


def _rot_t(dy, cs, sn):
    return dy * cs + pltpu.roll(dy * sn, HD // 2, 1)


def _rms_bwd(x, g, dh, dx_in):
    r = lax.rsqrt(_mean(x * x) + EPS)
    xh = x * r
    dxh = dh * g
    dx = dx_in + r * (dxh - xh * _mean(dxh * xh))
    return dx, jnp.sum(dh * xh, axis=0, keepdims=True), (xh * g).astype(BF)


def in_proj(x, g, win, l, deps=()):
    S = x.shape[0]
    tm = min(S, SEQ_TILE)

    def body(x_ref, g_ref, w_ref, o_ref, h_ref):
        xv = x_ref[...]
        h = (xv * lax.rsqrt(_mean(xv * xv) + EPS) * g_ref[...]).astype(BF)
        h_ref[...] = h
        o_ref[...] = _dot(h, w_ref[...]).astype(BF)

    return _ordered_call(
        body, deps, name=f"in_proj_{l}", grid=(S // tm,),
        in_specs=[pl.BlockSpec((tm, D), lambda i: (i, 0)),
                  pl.BlockSpec((1, D), lambda i: (0, 0)),
                  pl.BlockSpec((D, INW), lambda i: (0, 0))],
        out_specs=[pl.BlockSpec((tm, INW), lambda i: (i, 0)), pl.BlockSpec((tm, D), lambda i: (i, 0))],
        out_shape=[jax.ShapeDtypeStruct((S, INW), BF), jax.ShapeDtypeStruct((S, D), BF)],
        compiler_params=_params(1, 48),
    )(x, g, win, *deps)


def _conv_fwd_fill(tc, a_ref, b_ref, buf):
    i = pl.program_id(0)

    @pl.when(i == 0)
    def _():
        buf[0:HALO, :] = jnp.zeros((HALO, CW), F32)

    @pl.when(i > 0)
    def _():
        buf[0:HALO, :] = buf[tc:tc + HALO, :]

    buf[HALO:HALO + tc, :] = a_ref[...].astype(F32) * _sigmoid(b_ref[...].astype(F32))


def _conv_fwd_rows(r0, groups, w_ref, cb_ref, lg_ref, lb_ref, u1_ref, u_ref, buf, win):
    for c0 in range(0, CW, CONV_COLS):
        cols = slice(c0, c0 + CONV_COLS)
        acc = jnp.broadcast_to(cb_ref[:, cols], (CONV_ROWS, CONV_COLS))
        for start, length, taps in groups:
            win[0:length, :] = buf[r0 + start:r0 + start + length, cols]
            for k, at in taps:
                acc = acc + w_ref[k:k + 1, cols] * win[at:at + CONV_ROWS, :]
        u1_ref[r0:r0 + CONV_ROWS, cols] = acc
    acc = u1_ref[r0:r0 + CONV_ROWS, :]
    d = acc - _mean(acc)
    u2 = d * lax.rsqrt(_mean(d * d) + EPS) * lg_ref[...] + lb_ref[...]
    u_ref[r0:r0 + CONV_ROWS, :] = (u2 * _sigmoid(u2)).astype(BF)


def _ret_tables(S):
    half = HD // 2
    pos = jnp.arange(S, dtype=F32)
    freqs = ROPE_BASE ** (-jnp.arange(half, dtype=F32) / half)
    ang = pos[:, None] * freqs[None, :]
    cos, sin = jnp.cos(ang), jnp.sin(ang)
    cosf = jnp.concatenate([cos, cos], axis=-1)
    sinf = jnp.concatenate([-sin, sin], axis=-1)
    log_g = jnp.log(1.0 - 2.0 ** (-5.0 - jnp.arange(NH, dtype=F32)))
    idx = jnp.arange(CHUNK, dtype=F32)
    dmat = jnp.exp(log_g[:, None, None] * jnp.abs(idx[:, None] - idx[None, :]))
    qdec = jnp.broadcast_to(jnp.exp(log_g[:, None] * (idx + 1.0))[:, :, None], (NH, CHUNK, HD))
    kdec = jnp.broadcast_to(jnp.exp(log_g[:, None] * (CHUNK - 1 - idx))[:, :, None], (NH, CHUNK, HD))
    cdec = jnp.broadcast_to(jnp.exp(log_g * CHUNK)[:, None, None], (NH, HD, HD))
    return cosf, sinf, dmat, qdec, kdec, cdec


def _ret_specs(tr, tmap):
    q0 = (2 * CW) // RW
    return [pl.BlockSpec((tr, RW), lambda t: (tmap(t), q0)),
            pl.BlockSpec((tr, RW), lambda t: (tmap(t), q0 + 1)),
            pl.BlockSpec((tr, RW), lambda t: (tmap(t), q0 + 2)),
            pl.BlockSpec((tr, HD), lambda t: (tmap(t), 0)),
            pl.BlockSpec((tr, HD), lambda t: (tmap(t), 0)),
            pl.BlockSpec((NH, CHUNK, CHUNK), lambda t: (0, 0, 0)),
            pl.BlockSpec((NH, CHUNK, HD), lambda t: (0, 0, 0)),
            pl.BlockSpec((NH, CHUNK, HD), lambda t: (0, 0, 0)),
            pl.BlockSpec((NH, HD, HD), lambda t: (0, 0, 0))]


def _ret_fwd_chunk(c, q_ref, k_ref, v_ref, cos_ref, sin_ref, dm_ref, qd_ref, kd_ref, cd_ref, r_ref, st_ref, st):
    scale = HD ** -0.5
    rows = slice(c * CHUNK, (c + 1) * CHUNK)
    cs, sn = cos_ref[rows, :], sin_ref[rows, :]
    for h in range(NH):
        cols = slice(h * HD, (h + 1) * HD)
        qr = _rot(q_ref[rows, cols].astype(F32), cs, sn)
        kr = _rot(k_ref[rows, cols].astype(F32), cs, sn) * scale
        vb = v_ref[rows, cols].astype(BF)
        s = st[h]
        sb = s.astype(BF)
        st_ref[h, c] = sb
        sc = _dot_nt(qr.astype(BF), kr.astype(BF)) * dm_ref[h]
        r_ref[rows, cols] = _dot(sc.astype(BF), vb) + _dot((qr * qd_ref[h]).astype(BF), sb)
        st[h] = cd_ref[h] * s + _dot_tn((kr * kd_ref[h]).astype(BF), vb)


def mix_fwd(proj, cw, cb, lg, lb, tables, l, deps=()):
    S = proj.shape[0]
    tt = min(S, SEQ_TILE)
    cpb = tt // CHUNK
    groups = _tap_groups(HALO - (CK - 1), CK)

    def body(a_ref, b_ref, w_ref, cb_ref, lg_ref, lb_ref, q_ref, k_ref, v_ref, cos_ref, sin_ref,
             dm_ref, qd_ref, kd_ref, cd_ref, u1_ref, u_ref, r_ref, st_ref, buf, win, st):
        @pl.when(pl.program_id(0) == 0)
        def _():
            st[...] = jnp.zeros((NH, HD, HD), F32)

        _conv_fwd_fill(tt, a_ref, b_ref, buf)
        for c in range(cpb):
            _ret_fwd_chunk(c, q_ref, k_ref, v_ref, cos_ref, sin_ref, dm_ref, qd_ref, kd_ref, cd_ref,
                           r_ref, st_ref, st)
            for r0 in range(c * CHUNK, (c + 1) * CHUNK, CONV_ROWS):
                _conv_fwd_rows(r0, groups, w_ref, cb_ref, lg_ref, lb_ref, u1_ref, u_ref, buf, win)

    vec = pl.BlockSpec((1, CW), lambda t: (0, 0))
    half = pl.BlockSpec((tt, CW), lambda t: (t, 0))
    return _ordered_call(
        body, deps, name=f"mix_fwd_{l}", grid=(S // tt,),
        in_specs=[half, pl.BlockSpec((tt, CW), lambda t: (t, 1)),
                  pl.BlockSpec((CK, CW), lambda t: (0, 0)), vec, vec, vec] + _ret_specs(tt, lambda t: t),
        out_specs=[half, half, half, pl.BlockSpec((NH, cpb, HD, HD), lambda t: (0, t, 0, 0))],
        out_shape=[jax.ShapeDtypeStruct((S, CW), F32), jax.ShapeDtypeStruct((S, CW), BF),
                   jax.ShapeDtypeStruct((S, RW), F32), jax.ShapeDtypeStruct((NH, S // CHUNK, HD, HD), BF)],
        scratch_shapes=[pltpu.VMEM((tt + HALO, CW), F32), pltpu.VMEM((HALO + CONV_ROWS, CONV_COLS), F32),
                        pltpu.VMEM((NH, HD, HD), F32)],
        compiler_params=_params(1, 40),
    )(proj, proj, cw, cb, lg, lb, proj, proj, proj, *tables, *deps)


def out_proj(u, r_raw, proj, gn, wout, x, l, deps=()):
    S = x.shape[0]
    tm = min(S, SEQ_TILE)
    gate_blk = (2 * CW + 3 * RW) // RW

    def body(u_ref, r_ref, gate_ref, gn_ref, w_ref, x_ref, x2_ref, mix_ref):
        mix_ref[:, 0:CW] = u_ref[...]
        gt = gate_ref[...].astype(F32)
        sil = gt * _sigmoid(gt) * gn_ref[...]
        for h in range(NH):
            cols = slice(h * HD, (h + 1) * HD)
            rh = r_ref[:, cols]
            d = rh - _mean(rh)
            rn = d * lax.rsqrt(_mean(d * d) + EPS)
            mix_ref[:, CW + h * HD:CW + (h + 1) * HD] = (rn * sil[:, cols]).astype(BF)
        x2_ref[...] = x_ref[...] + _dot(mix_ref[...], w_ref[...])

    return _ordered_call(
        body, deps, name=f"out_proj_{l}", grid=(S // tm,),
        in_specs=[pl.BlockSpec((tm, CW), lambda i: (i, 0)),
                  pl.BlockSpec((tm, RW), lambda i: (i, 0)),
                  pl.BlockSpec((tm, RW), lambda i: (i, gate_blk)),
                  pl.BlockSpec((1, RW), lambda i: (0, 0)),
                  pl.BlockSpec((D, D), lambda i: (0, 0)),
                  pl.BlockSpec((tm, D), lambda i: (i, 0))],
        out_specs=[pl.BlockSpec((tm, D), lambda i: (i, 0)),
                   pl.BlockSpec((tm, D), lambda i: (i, 0))],
        out_shape=[jax.ShapeDtypeStruct((S, D), F32), jax.ShapeDtypeStruct((S, D), BF)],
        compiler_params=_params(1, 40),
    )(u, r_raw, proj, gn, wout, x, *deps)


def mlp_fwd(x2, g2, wgt, wut, wd, l, deps=()):
    S = x2.shape[0]
    tm, tf = min(S, MLP_ROWS), MLP_COLS

    def body(x_ref, g_ref, wg_ref, wu_ref, wd_ref, o_ref, gs_ref, us_ref, a_ref):
        xv = x_ref[...]
        h = (xv * lax.rsqrt(_mean(xv * xv) + EPS) * g_ref[...]).astype(BF)
        for c0 in range(0, FF, tf):
            gv = _dot_nt(h, wg_ref[c0:c0 + tf, :])
            uv = _dot_nt(h, wu_ref[c0:c0 + tf, :])
            gs_ref[:, c0:c0 + tf] = gv.astype(BF)
            us_ref[:, c0:c0 + tf] = uv.astype(BF)
            a_ref[:, c0:c0 + tf] = (gv * _sigmoid(gv) * uv).astype(BF)
        o_ref[...] = xv + _dot(a_ref[...], wd_ref[...])

    wspec = pl.BlockSpec((FF, D), lambda i: (0, 0), pipeline_mode=pl.Buffered(1))
    row = pl.BlockSpec((tm, D), lambda i: (i, 0))
    wide = pl.BlockSpec((tm, FF), lambda i: (i, 0))
    return _ordered_call(
        body, deps, name=f"mlp_fwd_{l}", grid=(S // tm,),
        in_specs=[row, pl.BlockSpec((1, D), lambda i: (0, 0)), wspec, wspec, wspec],
        out_specs=[row, wide, wide, wide],
        out_shape=[jax.ShapeDtypeStruct((S, D), F32)] + [jax.ShapeDtypeStruct((S, FF), BF)] * 3,
        compiler_params=_params(1, 56),
    )(x2, g2, wgt, wut, wd, *deps)


def final_loss(x, gf, tgt):
    S = x.shape[0]
    tm = min(S, SEQ_TILE)

    def body(x_ref, g_ref, t_ref, dx_ref, loss_ref, dg_ref):
        @pl.when(pl.program_id(0) == 0)
        def _():
            loss_ref[...] = jnp.zeros((8, 128), F32)
            dg_ref[...] = jnp.zeros((1, D), F32)

        xv = x_ref[...]
        r = lax.rsqrt(_mean(xv * xv) + EPS)
        xh = xv * r
        diff = xh * g_ref[...] - t_ref[...]
        loss_ref[...] += jnp.sum(jnp.sum(diff * diff, axis=-1, keepdims=True), axis=0, keepdims=True)
        dy = diff * (1.0 / D)
        dg_ref[...] += jnp.sum(dy * xh, axis=0, keepdims=True)
        dxh = dy * g_ref[...]
        dx_ref[...] = r * (dxh - xh * _mean(dxh * xh))

    return pl.pallas_call(
        body, name="final_loss", grid=(S // tm,),
        in_specs=[pl.BlockSpec((tm, D), lambda i: (i, 0)),
                  pl.BlockSpec((1, D), lambda i: (0, 0)),
                  pl.BlockSpec((tm, D), lambda i: (i, 0))],
        out_specs=[pl.BlockSpec((tm, D), lambda i: (i, 0)),
                   pl.BlockSpec((8, 128), lambda i: (0, 0)),
                   pl.BlockSpec((1, D), lambda i: (0, 0))],
        out_shape=[jax.ShapeDtypeStruct((S, D), F32), jax.ShapeDtypeStruct((8, 128), F32),
                   jax.ShapeDtypeStruct((1, D), F32)],
        compiler_params=_params(1, 40),
    )(x, gf, tgt)


def mlp_bwd(dx3, x2, g2, gs, us, wgt, wut, wd, l, deps=()):
    S = x2.shape[0]
    tm, tf = min(S, MLP_ROWS), MLP_COLS

    def body(dx_ref, x_ref, g_ref, gs_ref, us_ref, wg_ref, wu_ref, wd_ref,
             dx2_ref, dg_ref, du_ref, h_ref, dgain_ref):
        @pl.when(pl.program_id(0) == 0)
        def _():
            dgain_ref[...] = jnp.zeros((1, D), F32)

        dxv = dx_ref[...]
        dxb = dxv.astype(BF)
        for c0 in range(0, FF, tf):
            da = _dot_nt(dxb, wd_ref[c0:c0 + tf, :])
            gv = gs_ref[:, c0:c0 + tf].astype(F32)
            uv = us_ref[:, c0:c0 + tf].astype(F32)
            sg = _sigmoid(gv)
            dg_ref[:, c0:c0 + tf] = (da * uv * (sg * (1.0 + gv * (1.0 - sg)))).astype(BF)
            du_ref[:, c0:c0 + tf] = (da * (gv * sg)).astype(BF)
        dh = _dot(dg_ref[...], wg_ref[...]) + _dot(du_ref[...], wu_ref[...])
        dx2, dgain, hb = _rms_bwd(x_ref[...], g_ref[...], dh, dxv)
        dx2_ref[...] = dx2
        dgain_ref[...] += dgain
        h_ref[...] = hb

    wspec = pl.BlockSpec((FF, D), lambda i: (0, 0), pipeline_mode=pl.Buffered(1))
    row = pl.BlockSpec((tm, D), lambda i: (i, 0))
    wide = pl.BlockSpec((tm, FF), lambda i: (i, 0))
    vec = pl.BlockSpec((1, D), lambda i: (0, 0))
    return _ordered_call(
        body, deps, name=f"mlp_bwd_{l}", grid=(S // tm,),
        in_specs=[row, row, vec, wide, wide, wspec, wspec, wspec],
        out_specs=[row, wide, wide, row, vec],
        out_shape=[jax.ShapeDtypeStruct((S, D), F32), jax.ShapeDtypeStruct((S, FF), BF),
                   jax.ShapeDtypeStruct((S, FF), BF), jax.ShapeDtypeStruct((S, D), BF),
                   jax.ShapeDtypeStruct((1, D), F32)],
        compiler_params=_params(1, 56),
    )(dx3, x2, g2, gs, us, wgt, wut, wd, *deps)


def wgrad(a, b, l, name, deps=()):
    S, K = a.shape
    N = b.shape[1]
    tk = 1408 if K == FF else min(K, 1024)
    tn = min(N, 1024)
    ts = min(S, WGRAD_ROWS)
    ns = S // ts

    def body(a_ref, b_ref, o_ref, acc):
        s = pl.program_id(2)

        @pl.when(s == 0)
        def _():
            acc[...] = jnp.zeros((tk, tn), F32)

        acc[...] += _dot_tn(a_ref[...], b_ref[...].astype(BF))

        @pl.when(s == ns - 1)
        def _():
            o_ref[...] = acc[...].astype(BF)

    return _ordered_call(
        body, deps, name=f"{name}_{l}", grid=(K // tk, N // tn, ns),
        in_specs=[pl.BlockSpec((ts, tk), lambda i, j, s: (s, i)),
                  pl.BlockSpec((ts, tn), lambda i, j, s: (s, j))],
        out_specs=pl.BlockSpec((tk, tn), lambda i, j, s: (i, j)),
        out_shape=jax.ShapeDtypeStruct((K, N), BF),
        scratch_shapes=[pltpu.VMEM((tk, tn), F32)],
        compiler_params=_params(3, 48),
    )(a, b, *deps)


def out_proj_bwd(dx2, wout, r_raw, proj, gn, u1, lg, lb, l, deps=()):
    S = dx2.shape[0]
    tm = min(S, SEQ_TILE)
    gate_blk = (2 * CW + 3 * RW) // RW

    def body(dx_ref, w_ref, r_ref, gate_ref, gn_ref, u1_ref, lg_ref, lb_ref,
             dgate_ref, dr_ref, du1_ref, sums_ref):
        @pl.when(pl.program_id(0) == 0)
        def _():
            sums_ref[...] = jnp.zeros((8, CW), F32)

        dmix = _dot_nt(dx_ref[...].astype(BF), w_ref[...])
        gt = gate_ref[...].astype(F32)
        sg = _sigmoid(gt)
        sil = gt * sg
        dsil = sg * (1.0 + gt * (1.0 - sg))
        for h in range(NH):
            cols = slice(h * HD, (h + 1) * HD)
            rh = r_ref[:, cols]
            d = rh - _mean(rh)
            rs = lax.rsqrt(_mean(d * d) + EPS)
            rn = d * rs
            drr = dmix[:, CW + h * HD:CW + (h + 1) * HD]
            gnh = gn_ref[:, cols]
            sums_ref[0:1, cols] += jnp.sum(drr * rn * sil[:, cols], axis=0, keepdims=True)
            dgate_ref[:, cols] = (drr * rn * gnh * dsil[:, cols]).astype(BF)
            drn = drr * gnh * sil[:, cols]
            dr_ref[:, cols] = (rs * (drn - _mean(drn) - rn * _mean(drn * rn))).astype(BF)
        du = dmix[:, 0:CW]
        u1 = u1_ref[...]
        d = u1 - _mean(u1)
        rs = lax.rsqrt(_mean(d * d) + EPS)
        xh = d * rs
        u2 = xh * lg_ref[...] + lb_ref[...]
        sg2 = _sigmoid(u2)
        du2 = du * (sg2 * (1.0 + u2 * (1.0 - sg2)))
        sums_ref[1:2, :] += jnp.sum(du2 * xh, axis=0, keepdims=True)
        sums_ref[2:3, :] += jnp.sum(du2, axis=0, keepdims=True)
        dxh = du2 * lg_ref[...]
        du1_ref[...] = rs * (dxh - _mean(dxh) - xh * _mean(dxh * xh))

    vec = pl.BlockSpec((1, CW), lambda i: (0, 0))
    half = pl.BlockSpec((tm, CW), lambda i: (i, 0))
    return _ordered_call(
        body, deps, name=f"out_proj_bwd_{l}", grid=(S // tm,),
        in_specs=[pl.BlockSpec((tm, D), lambda i: (i, 0)),
                  pl.BlockSpec((D, D), lambda i: (0, 0)),
                  half, pl.BlockSpec((tm, RW), lambda i: (i, gate_blk)), vec, half, vec, vec],
        out_specs=[half, half, half, pl.BlockSpec((8, CW), lambda i: (0, 0))],
        out_shape=[jax.ShapeDtypeStruct((S, RW), BF), jax.ShapeDtypeStruct((S, RW), BF),
                   jax.ShapeDtypeStruct((S, CW), F32), jax.ShapeDtypeStruct((8, CW), F32)],
        compiler_params=_params(1, 40),
    )(dx2, wout, r_raw, proj, gn, u1, lg, lb, *deps)


def _conv_bwd_fill(tc, du1_ref, dwb_ref, buf, pacc):
    i = pl.program_id(0)

    @pl.when(i == 0)
    def _():
        buf[tc:tc + HALO, :] = jnp.zeros((HALO, CW), F32)
        dwb_ref[...] = jnp.zeros((CK + 1, CW), F32)
        pacc[...] = jnp.zeros((CK, 8, CW), F32)

    @pl.when(i > 0)
    def _():
        buf[tc:tc + HALO, :] = buf[0:HALO, :]

    buf[0:tc, :] = du1_ref[...]


def _conv_bwd_rows(r0, groups, pacc, a_ref, b_ref, w_ref, dab_ref, buf, win):
    for c0 in range(0, CW, CONV_COLS):
        cols = slice(c0, c0 + CONV_COLS)
        av = a_ref[r0:r0 + CONV_ROWS, cols].astype(F32)
        sgb = _sigmoid(b_ref[r0:r0 + CONV_ROWS, cols].astype(F32))
        u0 = av * sgb
        acc = jnp.zeros((CONV_ROWS, CONV_COLS), F32)
        for start, length, taps in groups:
            win[0:length, :] = buf[r0 + start:r0 + start + length, cols]
            for j, at in taps:
                sl = win[at:at + CONV_ROWS, :]
                acc = acc + w_ref[CK - 1 - j:CK - j, cols] * sl
                pacc[CK - 1 - j, :, cols] += _fold8(u0 * sl)
        dab_ref[r0:r0 + CONV_ROWS, c0:c0 + CONV_COLS] = (acc * sgb).astype(BF)
        dab_ref[r0:r0 + CONV_ROWS, CW + c0:CW + c0 + CONV_COLS] = (acc * av * sgb * (1.0 - sgb)).astype(BF)


def _conv_bwd_finish(nt, pacc, du1_ref, dwb_ref):
    dwb_ref[CK:CK + 1, :] += jnp.sum(du1_ref[...], axis=0, keepdims=True)

    @pl.when(pl.program_id(0) == nt - 1)
    def _():
        for k in range(CK):
            dwb_ref[k:k + 1, :] = jnp.sum(pacc[k], axis=0, keepdims=True)


def _ret_bwd_chunk(c, q_ref, k_ref, v_ref, cos_ref, sin_ref, dm_ref, qd_ref, kd_ref, cd_ref, dr_ref, st_ref,
                   dq_ref, dk_ref, dv_ref, gst):
    scale = HD ** -0.5
    rows = slice(c * CHUNK, (c + 1) * CHUNK)
    cs, sn = cos_ref[rows, :], sin_ref[rows, :]
    for h in range(NH):
        cols = slice(h * HD, (h + 1) * HD)
        qr = _rot(q_ref[rows, cols].astype(F32), cs, sn)
        kr = _rot(k_ref[rows, cols].astype(F32), cs, sn) * scale
        qb, kb = qr.astype(BF), kr.astype(BF)
        vb = v_ref[rows, cols].astype(BF)
        dob = dr_ref[rows, cols]
        sb = st_ref[h, c]
        gn1 = gst[h]
        gb = gn1.astype(BF)
        sc = (_dot_nt(qb, kb) * dm_ref[h]).astype(BF)
        dsc = (_dot_nt(dob, vb) * dm_ref[h]).astype(BF)
        dqr = _dot(dsc, kb) + _dot_nt(dob, sb) * qd_ref[h]
        dkr = _dot_tn(dsc, qb) + _dot_nt(vb, gb) * kd_ref[h]
        dvv = _dot_tn(sc, dob) + _dot((kr * kd_ref[h]).astype(BF), gb)
        gst[h] = cd_ref[h] * gn1 + _dot_tn((qr * qd_ref[h]).astype(BF), dob)
        dq_ref[rows, cols] = _rot_t(dqr, cs, sn).astype(BF)
        dk_ref[rows, cols] = _rot_t(dkr * scale, cs, sn).astype(BF)
        dv_ref[rows, cols] = dvv.astype(BF)


def mix_bwd(du1, dr, proj, cw, states, tables, l, deps=()):
    S = proj.shape[0]
    tt = min(S, SEQ_TILE)
    cpb = tt // CHUNK
    nt = S // tt
    groups = _tap_groups(0, CK)

    def body(du1_ref, a_ref, b_ref, w_ref, q_ref, k_ref, v_ref, cos_ref, sin_ref, dm_ref, qd_ref, kd_ref, cd_ref,
             dr_ref, st_ref, dab_ref, dwb_ref, dq_ref, dk_ref, dv_ref, buf, win, gst, pacc):
        @pl.when(pl.program_id(0) == 0)
        def _():
            gst[...] = jnp.zeros((NH, HD, HD), F32)

        _conv_bwd_fill(tt, du1_ref, dwb_ref, buf, pacc)
        for c in reversed(range(cpb)):
            _ret_bwd_chunk(c, q_ref, k_ref, v_ref, cos_ref, sin_ref, dm_ref, qd_ref, kd_ref, cd_ref, dr_ref, st_ref,
                           dq_ref, dk_ref, dv_ref, gst)
            for r0 in range(c * CHUNK, (c + 1) * CHUNK, CONV_ROWS):
                _conv_bwd_rows(r0, groups, pacc, a_ref, b_ref, w_ref, dab_ref, buf, win)
        _conv_bwd_finish(nt, pacc, du1_ref, dwb_ref)

    rev = lambda t: nt - 1 - t
    half = pl.BlockSpec((tt, CW), lambda t: (rev(t), 0))
    return _ordered_call(
        body, deps, name=f"mix_bwd_{l}", grid=(nt,),
        in_specs=[half, half, pl.BlockSpec((tt, CW), lambda t: (rev(t), 1)), pl.BlockSpec((CK, CW), lambda t: (0, 0))]
        + _ret_specs(tt, rev) + [half, pl.BlockSpec((NH, cpb, HD, HD), lambda t: (0, rev(t), 0, 0))],
        out_specs=[pl.BlockSpec((tt, 2 * CW), lambda t: (rev(t), 0)), pl.BlockSpec((CK + 1, CW), lambda t: (0, 0)),
                   half, half, half],
        out_shape=[jax.ShapeDtypeStruct((S, 2 * CW), BF), jax.ShapeDtypeStruct((CK + 1, CW), F32)]
        + [jax.ShapeDtypeStruct((S, RW), BF)] * 3,
        scratch_shapes=[pltpu.VMEM((tt + HALO, CW), F32), pltpu.VMEM((HALO + CONV_ROWS, CONV_COLS), F32),
                        pltpu.VMEM((NH, HD, HD), F32), pltpu.VMEM((CK, 8, CW), F32)],
        compiler_params=_params(1, 40),
    )(du1, proj, proj, cw, proj, proj, proj, *tables, dr, states, *deps)


def in_proj_bwd(parts, win, x, g, dx2, l, deps=()):
    S = x.shape[0]
    tm = min(S, SEQ_TILE)
    n = len(parts)

    def body(*refs):
        srcs = refs[:n]
        w_ref, x_ref, g_ref, dx2_ref, dx_ref, dgain_ref = refs[n:]

        @pl.when(pl.program_id(0) == 0)
        def _():
            dgain_ref[...] = jnp.zeros((1, D), F32)

        dh, col = None, 0
        for r in srcs:
            width = r.shape[1]
            term = _dot_nt(r[...], w_ref[:, col:col + width])
            dh = term if dh is None else dh + term
            col += width
        dx, dgain, _ = _rms_bwd(x_ref[...], g_ref[...], dh, dx2_ref[...])
        dx_ref[...] = dx
        dgain_ref[...] += dgain

    row = pl.BlockSpec((tm, D), lambda i: (i, 0))
    vec = pl.BlockSpec((1, D), lambda i: (0, 0))
    return _ordered_call(
        body, deps, name=f"in_proj_bwd_{l}", grid=(S // tm,),
        in_specs=[pl.BlockSpec((tm, p.shape[1]), lambda i: (i, 0)) for p in parts]
        + [pl.BlockSpec((D, INW), lambda i: (0, 0)), row, vec, row],
        out_specs=[row, vec],
        out_shape=[jax.ShapeDtypeStruct((S, D), F32), jax.ShapeDtypeStruct((1, D), F32)],
        compiler_params=_params(1, 48),
    )(*parts, win, x, g, dx2, *deps)


def wgrad_in(h, parts, l):
    S = h.shape[0]
    ts = min(S, SEQ_TILE)
    ns = S // ts

    def body(*refs):
        h_ref, srcs = refs[0], refs[1:1 + len(parts)]
        o_ref, acc = refs[-2], refs[-1]
        s = pl.program_id(0)

        @pl.when(s == 0)
        def _():
            acc[...] = jnp.zeros((D, INW), F32)

        hv = h_ref[...]
        col = 0
        for r in srcs:
            width = r.shape[1]
            acc[:, col:col + width] += _dot_tn(hv, r[...])
            col += width

        @pl.when(s == ns - 1)
        def _():
            o_ref[...] = acc[...].astype(BF)

    return pl.pallas_call(
        body, name=f"wgrad_in_{l}", grid=(ns,),
        in_specs=[pl.BlockSpec((ts, D), lambda s: (s, 0))]
        + [pl.BlockSpec((ts, p.shape[1]), lambda s: (s, 0)) for p in parts],
        out_specs=pl.BlockSpec((D, INW), lambda s: (0, 0)),
        out_shape=jax.ShapeDtypeStruct((D, INW), BF),
        scratch_shapes=[pltpu.VMEM((D, INW), F32)],
        compiler_params=_params(1, 48),
    )(h, *parts)


def sum_slots(recv, name, deps=()):
    _, R, C = recv.shape
    tr = 256 if R % 256 == 0 else R

    def body(r_ref, o_ref):
        acc = r_ref[0].astype(F32)
        for k in range(1, NCHIP):
            acc = acc + r_ref[k].astype(F32)
        o_ref[...] = acc

    return _ordered_call(
        body, deps, name=name, grid=(R // tr,),
        in_specs=[pl.BlockSpec((NCHIP, tr, C), lambda i: (0, i, 0))],
        out_specs=pl.BlockSpec((tr, C), lambda i: (i, 0)),
        out_shape=jax.ShapeDtypeStruct((R, C), F32),
        compiler_params=_params(1, 32),
    )(recv, *deps)


def adamw(w, ga, gb, m, v, name):
    R, C = w.shape
    tr = 256 if R % 256 == 0 else R
    c1 = 1.0 - ADAM_B1 ** ADAM_STEP
    c2 = 1.0 - ADAM_B2 ** ADAM_STEP

    def body(w_ref, ga_ref, gb_ref, m_ref, v_ref, g_out, d_out, m_out, v_out):
        g = ga_ref[...] + gb_ref[...]
        mn = ADAM_B1 * m_ref[...] + (1.0 - ADAM_B1) * g
        vn = ADAM_B2 * v_ref[...] + (1.0 - ADAM_B2) * (g * g)
        g_out[...] = g
        m_out[...] = mn
        v_out[...] = vn
        d_out[...] = -ADAM_LR * ((mn / c1) / (jnp.sqrt(vn / c2) + ADAM_EPS) + ADAM_WD * w_ref[...])

    blk = pl.BlockSpec((tr, C), lambda i: (i, 0))
    return pl.pallas_call(
        body, name=name, grid=(R // tr,),
        in_specs=[blk] * 5, out_specs=[blk] * 4,
        out_shape=[jax.ShapeDtypeStruct((R, C), F32)] * 4,
        compiler_params=_params(1, 40),
    )(w, ga, gb, m, v)


def _place():
    x, y, c = lax.axis_index("x"), lax.axis_index("y"), lax.axis_index("c")
    chips = [(1 - x, y), (x, 1 - y), (1 - x, 1 - y)]
    return x, y, c, chips


def _window(ref, axis, j, size):
    idx = [slice(None)] * len(ref.shape)
    idx[axis] = pl.ds(pl.multiple_of(j * size, 128 if axis == len(ref.shape) - 1 else 16), size)
    return ref.at[tuple(idx)]


def _hbm(a):
    return pltpu.with_memory_space_constraint(a, pltpu.HBM)


def _hbm_like(arrs):
    return [pltpu.HBM(a.shape, a.dtype) for a in arrs]


def gather_start(shards, axes, after, tag):
    n = len(shards)
    na = len(after)
    lands = []
    for s, ax in zip(shards, axes):
        shp = list(s.shape)
        shp[ax] *= NCHIP
        lands.append(lax.empty(tuple(shp), s.dtype))

    def body(*refs):
        ins, land = refs[:n], refs[n:2 * n]
        send, recv = refs[2 * n + na], refs[2 * n + na + 1]
        token = refs[-1]
        x, y, c, chips = _place()
        for a in range(n):
            for k, chip in enumerate(chips):
                pltpu.make_async_remote_copy(
                    src_ref=ins[a], dst_ref=_window(land[a], axes[a], 2 * x + y, ins[a].shape[axes[a]]),
                    send_sem=send.at[3 * a + k], recv_sem=recv.at[3 * a + k],
                    device_id=(chip[0], chip[1], c), device_id_type=MESH).start()
        token[...] = jnp.zeros_like(token)

    outs = pl.pallas_call(
        body, name=f"gather_start_{tag}",
        in_specs=[HBM_SPEC] * (2 * n) + [ANY] * na,
        out_specs=(SEM_SPEC, SEM_SPEC, *[HBM_SPEC] * (2 * n), VMEM_SPEC),
        out_shape=(pltpu.SemaphoreType.DMA((3 * n,)), pltpu.SemaphoreType.DMA((3 * n,)),
                   *_hbm_like(shards), *_hbm_like(lands), jax.ShapeDtypeStruct((8, 128), F32)),
        input_output_aliases={a: 2 + a for a in range(2 * n)},
        compiler_params=pltpu.CompilerParams(has_side_effects=DATAFLOW),
    )(*[_hbm(s) for s in shards], *[_hbm(b) for b in lands], *after)
    return (outs[0], outs[1], list(outs[2:2 + n]), list(outs[2 + n:2 + 2 * n]), list(axes)), outs[-1]


def gather_wait(groups, after, tag):
    sizes = [len(g[2]) for g in groups]
    total = sum(sizes)

    def body(*refs):
        x, y, c, chips = _place()
        stage, loc = refs[-1 - total:-1], refs[-1]
        pos = 2 * total
        off = 0
        mine = []
        for g, n in zip(groups, sizes):
            ins, land = refs[off:off + n], refs[total + off:total + off + n]
            send_ref, recv_ref = refs[pos], refs[pos + 1]
            axes = g[4]
            for a in range(n):
                fetch = pltpu.make_async_copy(ins[a], stage[off + a], loc.at[2 * (off + a)])
                fetch.start()
                put = pltpu.make_async_copy(
                    stage[off + a], _window(land[a], axes[a], 2 * x + y, ins[a].shape[axes[a]]),
                    loc.at[2 * (off + a) + 1])
                mine.append((fetch, put))
                for k, chip in enumerate(chips):
                    cp = pltpu.make_async_remote_copy(
                        src_ref=ins[a],
                        dst_ref=_window(land[a], axes[a], 2 * chip[0] + chip[1], ins[a].shape[axes[a]]),
                        send_sem=send_ref.at[3 * a + k], recv_sem=recv_ref.at[3 * a + k],
                        device_id=(chip[0], chip[1], c), device_id_type=MESH)
                    cp.wait_send()
                    cp.wait_recv()
            pos += 2
            off += n
        for fetch, put in mine:
            fetch.wait()
            put.start()
        for fetch, put in mine:
            put.wait()

    shards = [s for g in groups for s in g[2]]
    lands = [b for g in groups for b in g[3]]
    sems = [s for g in groups for s in (g[0], g[1])]
    outs = pl.pallas_call(
        body, name=f"gather_wait_{tag}",
        in_specs=[HBM_SPEC] * (2 * total) + [SEM_SPEC] * len(sems) + [ANY],
        out_specs=[HBM_SPEC] * (2 * total),
        out_shape=(*_hbm_like(shards), *_hbm_like(lands)),
        input_output_aliases={a: a for a in range(2 * total)},
        scratch_shapes=[pltpu.VMEM(s.shape, s.dtype) for s in shards] + [pltpu.SemaphoreType.DMA((2 * total,))],
        compiler_params=pltpu.CompilerParams(has_side_effects=DATAFLOW, vmem_limit_bytes=32 << 20),
    )(*shards, *lands, *sems, after)
    return list(outs[total:])


def scatter_start(grads, axes, sizes, lands, l, tag):
    n = len(grads)

    def body(*refs):
        ins, land = refs[:n], refs[n:2 * n]
        send, recv = refs[2 * n], refs[2 * n + 1]
        token = refs[2 * n + 2 + 2 * n]
        stage, loc = refs[-1 - n:-1], refs[-1]
        x, y, c, chips = _place()
        me = 2 * x + y
        fetches = [pltpu.make_async_copy(_window(ins[a], axes[a], me, sizes[a]), stage[a], loc.at[2 * a])
                   for a in range(n)]
        for cp in fetches:
            cp.start()
        for a in range(n):
            for k, chip in enumerate(chips):
                pltpu.make_async_remote_copy(
                    src_ref=_window(ins[a], axes[a], 2 * chip[0] + chip[1], sizes[a]), dst_ref=land[a].at[me, l],
                    send_sem=send.at[3 * a + k], recv_sem=recv.at[3 * a + k],
                    device_id=(chip[0], chip[1], c), device_id_type=MESH).start()
        puts = [pltpu.make_async_copy(stage[a], land[a].at[me, l], loc.at[2 * a + 1]) for a in range(n)]
        for fetch, put in zip(fetches, puts):
            fetch.wait()
            put.start()
        for put in puts:
            put.wait()
        token[...] = jnp.zeros_like(token)

    outs = pl.pallas_call(
        body, name=f"scatter_start_{tag}",
        in_specs=[HBM_SPEC] * (2 * n),
        out_specs=(SEM_SPEC, SEM_SPEC, *[HBM_SPEC] * (2 * n), VMEM_SPEC),
        out_shape=(pltpu.SemaphoreType.DMA((3 * n,)), pltpu.SemaphoreType.DMA((3 * n,)),
                   *_hbm_like(grads), *_hbm_like(lands), jax.ShapeDtypeStruct((8, 128), F32)),
        input_output_aliases={a: 2 + a for a in range(2 * n)},
        scratch_shapes=[pltpu.VMEM(b.shape[2:], b.dtype) for b in lands] + [pltpu.SemaphoreType.DMA((2 * n,))],
        compiler_params=pltpu.CompilerParams(has_side_effects=DATAFLOW, vmem_limit_bytes=32 << 20),
    )(*[_hbm(g) for g in grads], *[_hbm(b) for b in lands])
    group = (outs[0], outs[1], list(outs[2:2 + n]), list(axes), list(sizes), l)
    return group, list(outs[2 + n:2 + 2 * n]), outs[-1]


def scatter_wait(groups, lands, which, after):
    nl = len(lands)

    def body(*refs):
        land = refs[:nl]
        x, y, c, chips = _place()
        pos = nl
        for g, wh in zip(groups, which):
            n = len(g[2])
            ins = refs[pos:pos + n]
            send_ref, recv_ref = refs[pos + n], refs[pos + n + 1]
            axes, sizes, l = g[3], g[4], g[5]
            for a in range(n):
                for k, chip in enumerate(chips):
                    jp = 2 * chip[0] + chip[1]
                    cp = pltpu.make_async_remote_copy(
                        src_ref=_window(ins[a], axes[a], jp, sizes[a]), dst_ref=land[wh[a]].at[jp, l],
                        send_sem=send_ref.at[3 * a + k], recv_sem=recv_ref.at[3 * a + k],
                        device_id=(chip[0], chip[1], c), device_id_type=MESH)
                    cp.wait_send()
                    cp.wait_recv()
            pos += n + 2

    operands = list(lands)
    specs = [HBM_SPEC] * nl
    for g in groups:
        operands += list(g[2]) + [g[0], g[1]]
        specs += [HBM_SPEC] * len(g[2]) + [SEM_SPEC, SEM_SPEC]
    outs = pl.pallas_call(
        body, name="scatter_wait", in_specs=specs + [ANY], out_specs=[HBM_SPEC] * nl,
        out_shape=tuple(_hbm_like(lands)),
        input_output_aliases={a: a for a in range(nl)},
        compiler_params=pltpu.CompilerParams(has_side_effects=DATAFLOW),
    )(*operands, after)
    return list(outs)


def swap_start(part, tag):
    def body(p_ref, land_ref, send, recv, p_thru, land_thru, token):
        x, y, c, _ = _place()
        pltpu.make_async_remote_copy(src_ref=p_ref, dst_ref=land_ref, send_sem=send, recv_sem=recv,
                                     device_id=(x, y, 1 - c), device_id_type=MESH).start()
        token[...] = jnp.zeros_like(token)

    outs = pl.pallas_call(
        body, name=f"swap_start_{tag}", in_specs=[HBM_SPEC, HBM_SPEC],
        out_specs=(SEM_SPEC, SEM_SPEC, HBM_SPEC, HBM_SPEC, VMEM_SPEC),
        out_shape=(pltpu.SemaphoreType.DMA(()), pltpu.SemaphoreType.DMA(()), *_hbm_like([part, part]),
                   jax.ShapeDtypeStruct((8, 128), F32)),
        input_output_aliases={0: 2, 1: 3},
        compiler_params=pltpu.CompilerParams(has_side_effects=DATAFLOW),
    )(_hbm(part), _hbm(lax.empty(part.shape, part.dtype)))
    return tuple(outs[:4]), outs[4]


def swap_wait(group, after, tag):
    send, recv, part, land = group

    def body(p_ref, land_ref, send_ref, recv_ref, after_ref, p_out, land_out):
        x, y, c, _ = _place()
        cp = pltpu.make_async_remote_copy(src_ref=p_ref, dst_ref=land_ref, send_sem=send_ref, recv_sem=recv_ref,
                                          device_id=(x, y, 1 - c), device_id_type=MESH)
        cp.wait_send()
        cp.wait_recv()

    outs = pl.pallas_call(
        body, name=f"swap_wait_{tag}", in_specs=[HBM_SPEC, HBM_SPEC, SEM_SPEC, SEM_SPEC, ANY],
        out_specs=[HBM_SPEC, HBM_SPEC], out_shape=tuple(_hbm_like([part, land])),
        input_output_aliases={0: 0, 1: 1},
        compiler_params=pltpu.CompilerParams(has_side_effects=DATAFLOW),
    )(part, land, send, recv, after)
    return outs[0], outs[1]


def small_allreduce(p):
    R, C = p.shape
    ndev = 8

    def body(p_ref, o_ref, buf, send, recv):
        x, y, c, _ = _place()
        me = 4 * x + 2 * y + c
        buf[me] = p_ref[...]

        def peer(d):
            px = 1 - x if d & 4 else x
            py = 1 - y if d & 2 else y
            pc = 1 - c if d & 1 else c
            return px, py, pc

        def copy(d, slot):
            return pltpu.make_async_remote_copy(src_ref=p_ref, dst_ref=buf.at[slot], send_sem=send.at[d - 1],
                                                recv_sem=recv.at[d - 1], device_id=peer(d), device_id_type=MESH)

        sends = [copy(d, me) for d in range(1, ndev)]
        for cp in sends:
            cp.start()
        for d in range(1, ndev):
            px, py, pc = peer(d)
            copy(d, 4 * px + 2 * py + pc).wait_recv()
        for cp in sends:
            cp.wait_send()
        acc = buf[0]
        for k in range(1, ndev):
            acc = acc + buf[k]
        o_ref[...] = acc

    return pl.pallas_call(
        body, name="small_allreduce", in_specs=[VMEM_SPEC], out_specs=VMEM_SPEC,
        out_shape=jax.ShapeDtypeStruct((R, C), F32),
        scratch_shapes=[pltpu.VMEM((ndev, R, C), F32), pltpu.SemaphoreType.DMA((ndev - 1,)),
                        pltpu.SemaphoreType.DMA((ndev - 1,))],
        compiler_params=pltpu.CompilerParams(vmem_limit_bytes=32 << 20),
    )(p)


def kernel(x, norm1_g, w_in, conv_w, conv_b, conv_ln_g, conv_ln_b, ret_gn_g, w_out, norm2_g, w_gate, w_up, w_down, final_g, loss_target, m_norm1_g, m_w_in, m_conv_w, m_conv_b, m_conv_ln_g, m_conv_ln_b, m_ret_gn_g, m_w_out, m_norm2_g, m_w_gate, m_w_up, m_w_down, m_final_g, v_norm1_g, v_w_in, v_conv_w, v_conv_b, v_conv_ln_g, v_conv_ln_b, v_ret_gn_g, v_w_out, v_norm2_g, v_w_gate, v_w_up, v_w_down, v_final_g):
    S = x.shape[1]
    xs = x.reshape(S, D)
    tgt = loss_target.reshape(S, D)
    fsh = FF // NCHIP

    def shards_of(l):
        return [w_in[l].astype(BF), w_out[l].astype(BF), w_gate[l].T.astype(BF), w_up[l].T.astype(BF),
                w_down[l].astype(BF), conv_w[l]]

    gather_axes = [1, 0, 0, 0, 0, 1]
    shard_cache = [shards_of(l) for l in range(L)]
    tables = _ret_tables(S)
    row = lambda a, l: a[l].reshape(1, -1)
    groups = {}
    weights = [dict() for _ in range(L)]

    def begin(l, which, after):
        group, token = gather_start([shard_cache[l][i] for i in which], [gather_axes[i] for i in which],
                                    after, f"{l}_{which[0]}")
        groups[(l, which[0])] = (group, which)
        return token

    def finish(l, firsts, after, tag):
        gs = [groups[(l, f)] for f in firsts]
        outs = gather_wait([g for g, _ in gs], after, f"{l}_{tag}")
        k = 0
        for _, which in gs:
            for i in which:
                weights[l][i] = outs[k]
                k += 1

    first = begin(0, [0], [])
    second = begin(0, [1, 5], [first])
    after = begin(0, [2, 3, 4], [second])
    saved = []
    xc = xs
    for l in range(L):
        if l == 0:
            finish(l, [0], after, "a")
        else:
            finish(l, [0], after, "all")
        win = weights[l][0]
        proj, h1 = in_proj(xc, row(norm1_g, l), win, l)
        if l == 0:
            finish(l, [1], proj, "b")
        wout, cw = weights[l][1], weights[l][5]
        ahead = [begin(l + 1, [0, 1, 2, 3, 4, 5], [proj])] if l + 1 < L else []
        u1, u, r_raw, states = mix_fwd(proj, cw, row(conv_b, l), row(conv_ln_g, l), row(conv_ln_b, l), tables, l, ahead)
        x2, mixed = out_proj(u, r_raw, proj, row(ret_gn_g, l), wout, xc, l)
        if l == 0:
            finish(l, [2], x2, "c")
        wgt, wut, wd = weights[l][2], weights[l][3], weights[l][4]
        x3, gs, us, act = mlp_fwd(x2, row(norm2_g, l), wgt, wut, wd, l)
        saved.append((xc, proj, h1, u1, r_raw, states, mixed, x2, gs, us, act))
        xc = x3
        after = x3

    dx, loss_acc, d_final = final_loss(xc, final_g.reshape(1, D), tgt)
    loss = lax.psum(loss_acc[0, 0] * (0.5 / D), ("x", "y", "c"))

    scatter_axes = [1, 0, 0, 0, 0]
    scatter_sizes = [INW // NCHIP, D // NCHIP, fsh, fsh, fsh]
    lands = [lax.empty((NCHIP, L, D, INW // NCHIP), BF), lax.empty((NCHIP, L, D // NCHIP, D), BF),
             lax.empty((NCHIP, L, fsh, D), BF), lax.empty((NCHIP, L, fsh, D), BF), lax.empty((NCHIP, L, fsh, D), BF)]
    sent, sent_which = [], []

    def send_grad(g, a, l):
        group, new_land, token = scatter_start([g], [scatter_axes[a]], [scatter_sizes[a]], [lands[a]], l, f"{l}_{a}")
        lands[a] = new_land[0]
        sent.append(group)
        sent_which.append([a])
        return [token]

    small = [None] * L
    for l in reversed(range(L)):
        xin, proj, h1, u1, r_raw, states, mixed, x2, gs, us, act = saved[l]
        win, wout, wgt, wut, wd, cw = (weights[l][i] for i in range(6))
        dx2, dgs, dus, h2, d_n2 = mlp_bwd(dx, x2, row(norm2_g, l), gs, us, wgt, wut, wd, l)
        g_wd = wgrad(act, dx, l, "wgrad_down")
        g_wgt = wgrad(dgs, h2, l, "wgrad_gate", send_grad(g_wd, 4, l))
        g_wut = wgrad(dus, h2, l, "wgrad_up", send_grad(g_wgt, 2, l))
        dgate, dr, du1, sums = out_proj_bwd(dx2, wout, r_raw, proj, row(ret_gn_g, l), u1,
                                            row(conv_ln_g, l), row(conv_ln_b, l), l, send_grad(g_wut, 3, l))
        g_wout = wgrad(mixed, dx2, l, "wgrad_out")
        dab, dwb, dq, dk, dv = mix_bwd(du1, dr, proj, cw, states, tables, l, send_grad(g_wout, 1, l))
        dproj = [dab, dq, dk, dv, dgate]
        g_win = wgrad_in(h1, dproj, l)
        dx, d_n1 = in_proj_bwd(dproj, win, xin, row(norm1_g, l), dx2, l, send_grad(g_win, 0, l))
        small[l] = jnp.concatenate([dwb, sums, d_n1.reshape(2, CW), d_n2.reshape(2, CW)], axis=0)
    grad_x = dx.reshape(1, S, D)

    per = CK + 1 + 8 + 4
    packed = jnp.concatenate(small + [d_final.reshape(2, CW), jnp.zeros((6, CW), F32)], axis=0)
    tot = small_allreduce(packed)
    lay = tot[:L * per].reshape(L, per, CW)
    g_conv_w_full = lay[:, 0:CK, :]
    j = 2 * lax.axis_index("x") + lax.axis_index("y")
    g_conv_w = lax.dynamic_slice_in_dim(g_conv_w_full, j * (CW // NCHIP), CW // NCHIP, axis=2)
    g_small = {
        "conv_b": lay[:, CK, :], "ret_gn_g": lay[:, CK + 1, :], "conv_ln_g": lay[:, CK + 2, :],
        "conv_ln_b": lay[:, CK + 3, :], "norm1_g": lay[:, CK + 9:CK + 11, :].reshape(L, D),
        "norm2_g": lay[:, CK + 11:CK + 13, :].reshape(L, D), "final_g": tot[L * per:L * per + 2].reshape(D),
    }

    recv = scatter_wait(sent, lands, sent_which, tot)
    shard_shapes = [(L * D, INW // NCHIP), (L * D // NCHIP, D), (L * fsh, D), (L * fsh, D), (L * fsh, D)]
    names = ["w_in", "w_out", "w_gate", "w_up", "w_down"]
    swaps, token = [], []
    for r, shp, nm in zip(recv, shard_shapes, names):
        group, tk = swap_start(sum_slots(r.reshape((NCHIP,) + shp), f"sum_{nm}", token), nm)
        swaps.append(group)
        token = [tk]
    parts, theirs = zip(*[swap_wait(group, token[0], nm) for group, nm in zip(swaps, names)])

    def hidden_major(a):
        return jnp.swapaxes(a, 1, 2).reshape(L * fsh, D)

    big = {}
    wmv = {"w_in": (w_in, m_w_in, v_w_in), "w_out": (w_out, m_w_out, v_w_out),
           "w_gate": (w_gate, m_w_gate, v_w_gate), "w_up": (w_up, m_w_up, v_w_up),
           "w_down": (w_down, m_w_down, v_w_down)}
    for nm, mine, other in zip(names, parts, theirs):
        w, m, v = wmv[nm]
        if nm in ("w_gate", "w_up"):
            outs = adamw(hidden_major(w), mine, other, hidden_major(m), hidden_major(v), f"adamw_{nm}")
            big[nm] = [jnp.swapaxes(o.reshape(L, fsh, D), 1, 2) for o in outs]
        else:
            shp2 = (w.shape[0] * w.shape[1], w.shape[2])
            outs = adamw(w.reshape(shp2), mine, other, m.reshape(shp2), v.reshape(shp2), f"adamw_{nm}")
            big[nm] = [o.reshape(w.shape) for o in outs]

    cshape = (L * CK, CW // NCHIP)
    zc = jnp.zeros(cshape, F32)
    big["conv_w"] = [o.reshape(conv_w.shape) for o in adamw(
        conv_w.reshape(cshape), g_conv_w.reshape(cshape), zc, m_conv_w.reshape(cshape),
        v_conv_w.reshape(cshape), "adamw_conv_w")]
    vec_names = ["norm1_g", "conv_b", "conv_ln_g", "conv_ln_b", "ret_gn_g", "norm2_g", "final_g"]
    vec_w = {"norm1_g": (norm1_g, m_norm1_g, v_norm1_g), "conv_b": (conv_b, m_conv_b, v_conv_b),
             "conv_ln_g": (conv_ln_g, m_conv_ln_g, v_conv_ln_g), "conv_ln_b": (conv_ln_b, m_conv_ln_b, v_conv_ln_b),
             "ret_gn_g": (ret_gn_g, m_ret_gn_g, v_ret_gn_g), "norm2_g": (norm2_g, m_norm2_g, v_norm2_g),
             "final_g": (final_g, m_final_g, v_final_g)}
    cat = lambda arrs: jnp.concatenate([a.reshape(-1, CW) for a in arrs], axis=0)
    vw = cat([vec_w[nm][0] for nm in vec_names])
    vm = cat([vec_w[nm][1] for nm in vec_names])
    vv = cat([vec_w[nm][2] for nm in vec_names])
    vg = cat([g_small[nm] for nm in vec_names])
    vouts = adamw(vw, vg, jnp.zeros_like(vg), vm, vv, "adamw_vectors")
    off = 0
    for nm in vec_names:
        w = vec_w[nm][0]
        nrow = w.size // CW
        big[nm] = [o[off:off + nrow].reshape(w.shape) for o in vouts]
        off += nrow

    order = ["norm1_g", "w_in", "conv_w", "conv_b", "conv_ln_g", "conv_ln_b", "ret_gn_g", "w_out", "norm2_g",
             "w_gate", "w_up", "w_down", "final_g"]
    return (loss, grad_x, *[big[nm][0] for nm in order], *[big[nm][1] for nm in order],
            *[big[nm][2] for nm in order], *[big[nm][3] for nm in order])
```

```python
import math

import jax
import jax.numpy as jnp
from jax import lax
from jax.experimental import pallas as pl
from jax.experimental.pallas import tpu as pltpu

D = 1024
L = 4
CW = 512
RW = 512
NH = 4
HD = 128
CK = 31
CHUNK = 64
INW = 3072
FF = 2816
NCHIP = 4
EPS = 1e-6
ROPE_BASE = 10000.0
SEQ_TILE = 512
WGRAD_ROWS = 1024
MLP_ROWS = 256
MLP_COLS = 1408
HALO = 32
CONV_ROWS = 32
CONV_COLS = 256

ADAM_LR = 0.001
ADAM_B1 = 0.9
ADAM_B2 = 0.999
ADAM_EPS = 1e-08
ADAM_WD = 0.01
ADAM_STEP = 10

BF = jnp.bfloat16
F32 = jnp.float32
MESH = pl.DeviceIdType.MESH
ANY = pl.BlockSpec(memory_space=pl.ANY)
VMEM_SPEC = pl.BlockSpec(memory_space=pltpu.VMEM)
HBM_SPEC = pl.BlockSpec(memory_space=pltpu.HBM)
SEM_SPEC = pl.BlockSpec(memory_space=pltpu.SEMAPHORE)
DATAFLOW = pltpu.SideEffectType.DATAFLOW_SIDE_EFFECTING


def _params(n_grid, vmem_mb):
    return pltpu.CompilerParams(dimension_semantics=("arbitrary",) * n_grid,
                                vmem_limit_bytes=vmem_mb << 20)


def _ordered_call(body, deps, *, in_specs, **kw):
    n, nd = len(in_specs), len(deps)

    def with_deps(*refs):
        body(*refs[:n], *refs[n + nd:])

    return pl.pallas_call(with_deps, in_specs=list(in_specs) + [ANY] * nd, **kw)


def _dot(a, b):
    return jnp.dot(a, b, preferred_element_type=F32)


def _dot_nt(a, b):
    return lax.dot_general(a, b, (((1,), (1,)), ((), ())), preferred_element_type=F32)


def _dot_tn(a, b):
    return lax.dot_general(a, b, (((0,), (0,)), ((), ())), preferred_element_type=F32)


def _sigmoid(x):
    return 0.5 * jnp.tanh(0.5 * x) + 0.5


def _mean(x):
    return jnp.mean(x, axis=-1, keepdims=True)


def _fold8(x):
    out = x[0:8, :]
    for q in range(1, x.shape[0] // 8):
        out = out + x[8 * q:8 * q + 8, :]
    return out


def _tap_groups(first, count):
    groups = []
    for phase in range(8):
        taps = [(t, first + t - phase) for t in range(count) if (first + t) % 8 == phase]
        if taps:
            lo, hi = min(q for _, q in taps), max(q for _, q in taps)
            groups.append((lo + phase, hi - lo + CONV_ROWS, [(t, q - lo) for t, q in taps]))
    return groups


def _rot(t, cs, sn):
    return t * cs + pltpu.roll(t, HD // 2, 1) * sn


def _rot_t(dy, cs, sn):
    return dy * cs + pltpu.roll(dy * sn, HD // 2, 1)


def _rms_bwd(x, g, dh, dx_in):
    r = lax.rsqrt(_mean(x * x) + EPS)
    xh = x * r
    dxh = dh * g
    dx = dx_in + r * (dxh - xh * _mean(dxh * xh))
    return dx, jnp.sum(dh * xh, axis=0, keepdims=True), (xh * g).astype(BF)


def in_proj(x, g, win, l, deps=()):
    S = x.shape[0]
    tm = min(S, SEQ_TILE)

    def body(x_ref, g_ref, w_ref, o_ref, h_ref):
        xv = x_ref[...]
        h = (xv * lax.rsqrt(_mean(xv * xv) + EPS) * g_ref[...]).astype(BF)
        h_ref[...] = h
        o_ref[...] = _dot(h, w_ref[...]).astype(BF)

    return _ordered_call(
        body, deps, name=f"in_proj_{l}", grid=(S // tm,),
        in_specs=[pl.BlockSpec((tm, D), lambda i: (i, 0)),
                  pl.BlockSpec((1, D), lambda i: (0, 0)),
                  pl.BlockSpec((D, INW), lambda i: (0, 0))],
        out_specs=[pl.BlockSpec((tm, INW), lambda i: (i, 0)), pl.BlockSpec((tm, D), lambda i: (i, 0))],
        out_shape=[jax.ShapeDtypeStruct((S, INW), BF), jax.ShapeDtypeStruct((S, D), BF)],
        compiler_params=_params(1, 48),
    )(x, g, win, *deps)


def _conv_fwd_fill(tc, a_ref, b_ref, buf):
    i = pl.program_id(0)

    @pl.when(i == 0)
    def _():
        buf[0:HALO, :] = jnp.zeros((HALO, CW), F32)

    @pl.when(i > 0)
    def _():
        buf[0:HALO, :] = buf[tc:tc + HALO, :]

    buf[HALO:HALO + tc, :] = a_ref[...].astype(F32) * _sigmoid(b_ref[...].astype(F32))


def _conv_fwd_rows(r0, groups, w_ref, cb_ref, lg_ref, lb_ref, u1_ref, u_ref, buf, win):
    for c0 in range(0, CW, CONV_COLS):
        cols = slice(c0, c0 + CONV_COLS)
        acc = jnp.broadcast_to(cb_ref[:, cols], (CONV_ROWS, CONV_COLS))
        for start, length, taps in groups:
            win[0:length, :] = buf[r0 + start:r0 + start + length, cols]
            for k, at in taps:
                acc = acc + w_ref[k:k + 1, cols] * win[at:at + CONV_ROWS, :]
        u1_ref[r0:r0 + CONV_ROWS, cols] = acc
    acc = u1_ref[r0:r0 + CONV_ROWS, :]
    d = acc - _mean(acc)
    u2 = d * lax.rsqrt(_mean(d * d) + EPS) * lg_ref[...] + lb_ref[...]
    u_ref[r0:r0 + CONV_ROWS, :] = (u2 * _sigmoid(u2)).astype(BF)


def _ret_tables(S):
    half = HD // 2
    pos = jnp.arange(S, dtype=F32)
    freqs = ROPE_BASE ** (-jnp.arange(half, dtype=F32) / half)
    ang = pos[:, None] * freqs[None, :]
    cos, sin = jnp.cos(ang), jnp.sin(ang)
    cosf = jnp.concatenate([cos, cos], axis=-1)
    sinf = jnp.concatenate([-sin, sin], axis=-1)
    log_g = jnp.log(1.0 - 2.0 ** (-5.0 - jnp.arange(NH, dtype=F32)))
    idx = jnp.arange(CHUNK, dtype=F32)
    dmat = jnp.exp(log_g[:, None, None] * jnp.abs(idx[:, None] - idx[None, :]))
    qdec = jnp.broadcast_to(jnp.exp(log_g[:, None] * (idx + 1.0))[:, :, None], (NH, CHUNK, HD))
    kdec = jnp.broadcast_to(jnp.exp(log_g[:, None] * (CHUNK - 1 - idx))[:, :, None], (NH, CHUNK, HD))
    cdec = jnp.broadcast_to(jnp.exp(log_g * CHUNK)[:, None, None], (NH, HD, HD))
    return cosf, sinf, dmat, qdec, kdec, cdec


def _ret_specs(tr, tmap):
    q0 = (2 * CW) // RW
    return [pl.BlockSpec((tr, RW), lambda t: (tmap(t), q0)),
            pl.BlockSpec((tr, RW), lambda t: (tmap(t), q0 + 1)),
            pl.BlockSpec((tr, RW), lambda t: (tmap(t), q0 + 2)),
            pl.BlockSpec((tr, HD), lambda t: (tmap(t), 0)),
            pl.BlockSpec((tr, HD), lambda t: (tmap(t), 0)),
            pl.BlockSpec((NH, CHUNK, CHUNK), lambda t: (0, 0, 0)),
            pl.BlockSpec((NH, CHUNK, HD), lambda t: (0, 0, 0)),
            pl.BlockSpec((NH, CHUNK, HD), lambda t: (0, 0, 0)),
            pl.BlockSpec((NH, HD, HD), lambda t: (0, 0, 0))]


def _ret_fwd_chunk(c, q_ref, k_ref, v_ref, cos_ref, sin_ref, dm_ref, qd_ref, kd_ref, cd_ref, r_ref, st_ref, st):
    scale = HD ** -0.5
    rows = slice(c * CHUNK, (c + 1) * CHUNK)
    cs, sn = cos_ref[rows, :], sin_ref[rows, :]
    for h in range(NH):
        cols = slice(h * HD, (h + 1) * HD)
        qr = _rot(q_ref[rows, cols].astype(F32), cs, sn)
        kr = _rot(k_ref[rows, cols].astype(F32), cs, sn) * scale
        vb = v_ref[rows, cols].astype(BF)
        s = st[h]
        sb = s.astype(BF)
        st_ref[h, c] = sb
        sc = _dot_nt(qr.astype(BF), kr.astype(BF)) * dm_ref[h]
        r_ref[rows, cols] = _dot(sc.astype(BF), vb) + _dot((qr * qd_ref[h]).astype(BF), sb)
        st[h] = cd_ref[h] * s + _dot_tn((kr * kd_ref[h]).astype(BF), vb)


def mix_fwd(proj, cw, cb, lg, lb, tables, l, deps=()):
    S = proj.shape[0]
    tt = min(S, SEQ_TILE)
    cpb = tt // CHUNK
    groups = _tap_groups(HALO - (CK - 1), CK)

    def body(a_ref, b_ref, w_ref, cb_ref, lg_ref, lb_ref, q_ref, k_ref, v_ref, cos_ref, sin_ref,
             dm_ref, qd_ref, kd_ref, cd_ref, u1_ref, u_ref, r_ref, st_ref, buf, win, st):
        @pl.when(pl.program_id(0) == 0)
        def _():
            st[...] = jnp.zeros((NH, HD, HD), F32)

        _conv_fwd_fill(tt, a_ref, b_ref, buf)
        for c in range(cpb):
            _ret_fwd_chunk(c, q_ref, k_ref, v_ref, cos_ref, sin_ref, dm_ref, qd_ref, kd_ref, cd_ref,
                           r_ref, st_ref, st)
            for r0 in range(c * CHUNK, (c + 1) * CHUNK, CONV_ROWS):
                _conv_fwd_rows(r0, groups, w_ref, cb_ref, lg_ref, lb_ref, u1_ref, u_ref, buf, win)

    vec = pl.BlockSpec((1, CW), lambda t: (0, 0))
    half = pl.BlockSpec((tt, CW), lambda t: (t, 0))
    return _ordered_call(
        body, deps, name=f"mix_fwd_{l}", grid=(S // tt,),
        in_specs=[half, pl.BlockSpec((tt, CW), lambda t: (t, 1)),
                  pl.BlockSpec((CK, CW), lambda t: (0, 0)), vec, vec, vec] + _ret_specs(tt, lambda t: t),
        out_specs=[half, half, half, pl.BlockSpec((NH, cpb, HD, HD), lambda t: (0, t, 0, 0))],
        out_shape=[jax.ShapeDtypeStruct((S, CW), F32), jax.ShapeDtypeStruct((S, CW), BF),
                   jax.ShapeDtypeStruct((S, RW), F32), jax.ShapeDtypeStruct((NH, S // CHUNK, HD, HD), BF)],
        scratch_shapes=[pltpu.VMEM((tt + HALO, CW), F32), pltpu.VMEM((HALO + CONV_ROWS, CONV_COLS), F32),
                        pltpu.VMEM((NH, HD, HD), F32)],
        compiler_params=_params(1, 40),
    )(proj, proj, cw, cb, lg, lb, proj, proj, proj, *tables, *deps)


def out_proj(u, r_raw, proj, gn, wout, x, l, deps=()):
    S = x.shape[0]
    tm = min(S, SEQ_TILE)
    gate_blk = (2 * CW + 3 * RW) // RW

    def body(u_ref, r_ref, gate_ref, gn_ref, w_ref, x_ref, x2_ref, mix_ref):
        mix_ref[:, 0:CW] = u_ref[...]
        gt = gate_ref[...].astype(F32)
        sil = gt * _sigmoid(gt) * gn_ref[...]
        for h in range(NH):
            cols = slice(h * HD, (h + 1) * HD)
            rh = r_ref[:, cols]
            d = rh - _mean(rh)
            rn = d * lax.rsqrt(_mean(d * d) + EPS)
            mix_ref[:, CW + h * HD:CW + (h + 1) * HD] = (rn * sil[:, cols]).astype(BF)
        x2_ref[...] = x_ref[...] + _dot(mix_ref[...], w_ref[...])

    return _ordered_call(
        body, deps, name=f"out_proj_{l}", grid=(S // tm,),
        in_specs=[pl.BlockSpec((tm, CW), lambda i: (i, 0)),
                  pl.BlockSpec((tm, RW), lambda i: (i, 0)),
                  pl.BlockSpec((tm, RW), lambda i: (i, gate_blk)),
                  pl.BlockSpec((1, RW), lambda i: (0, 0)),
                  pl.BlockSpec((D, D), lambda i: (0, 0)),
                  pl.BlockSpec((tm, D), lambda i: (i, 0))],
        out_specs=[pl.BlockSpec((tm, D), lambda i: (i, 0)),
                   pl.BlockSpec((tm, D), lambda i: (i, 0))],
        out_shape=[jax.ShapeDtypeStruct((S, D), F32), jax.ShapeDtypeStruct((S, D), BF)],
        compiler_params=_params(1, 40),
    )(u, r_raw, proj, gn, wout, x, *deps)


def mlp_fwd(x2, g2, wgt, wut, wd, l, deps=()):
    S = x2.shape[0]
    tm, tf = min(S, MLP_ROWS), MLP_COLS

    def body(x_ref, g_ref, wg_ref, wu_ref, wd_ref, o_ref, gs_ref, us_ref, a_ref):
        xv = x_ref[...]
        h = (xv * lax.rsqrt(_mean(xv * xv) + EPS) * g_ref[...]).astype(BF)
        for c0 in range(0, FF, tf):
            gv = _dot_nt(h, wg_ref[c0:c0 + tf, :])
            uv = _dot_nt(h, wu_ref[c0:c0 + tf, :])
            gs_ref[:, c0:c0 + tf] = gv.astype(BF)
            us_ref[:, c0:c0 + tf] = uv.astype(BF)
            a_ref[:, c0:c0 + tf] = (gv * _sigmoid(gv) * uv).astype(BF)
        o_ref[...] = xv + _dot(a_ref[...], wd_ref[...])

    wspec = pl.BlockSpec((FF, D), lambda i: (0, 0), pipeline_mode=pl.Buffered(1))
    row = pl.BlockSpec((tm, D), lambda i: (i, 0))
    wide = pl.BlockSpec((tm, FF), lambda i: (i, 0))
    return _ordered_call(
        body, deps, name=f"mlp_fwd_{l}", grid=(S // tm,),
        in_specs=[row, pl.BlockSpec((1, D), lambda i: (0, 0)), wspec, wspec, wspec],
        out_specs=[row, wide, wide, wide],
        out_shape=[jax.ShapeDtypeStruct((S, D), F32)] + [jax.ShapeDtypeStruct((S, FF), BF)] * 3,
        compiler_params=_params(1, 56),
    )(x2, g2, wgt, wut, wd, *deps)


def final_loss(x, gf, tgt):
    S = x.shape[0]
    tm = min(S, SEQ_TILE)

    def body(x_ref, g_ref, t_ref, dx_ref, loss_ref, dg_ref):
        @pl.when(pl.program_id(0) == 0)
        def _():
            loss_ref[...] = jnp.zeros((8, 128), F32)
            dg_ref[...] = jnp.zeros((1, D), F32)

        xv = x_ref[...]
        r = lax.rsqrt(_mean(xv * xv) + EPS)
        xh = xv * r
        diff = xh * g_ref[...] - t_ref[...]
        loss_ref[...] += jnp.sum(jnp.sum(diff * diff, axis=-1, keepdims=True), axis=0, keepdims=True)
        dy = diff * (1.0 / D)
        dg_ref[...] += jnp.sum(dy * xh, axis=0, keepdims=True)
        dxh = dy * g_ref[...]
        dx_ref[...] = r * (dxh - xh * _mean(dxh * xh))

    return pl.pallas_call(
        body, name="final_loss", grid=(S // tm,),
        in_specs=[pl.BlockSpec((tm, D), lambda i: (i, 0)),
                  pl.BlockSpec((1, D), lambda i: (0, 0)),
                  pl.BlockSpec((tm, D), lambda i: (i, 0))],
        out_specs=[pl.BlockSpec((tm, D), lambda i: (i, 0)),
                   pl.BlockSpec((8, 128), lambda i: (0, 0)),
                   pl.BlockSpec((1, D), lambda i: (0, 0))],
        out_shape=[jax.ShapeDtypeStruct((S, D), F32), jax.ShapeDtypeStruct((8, 128), F32),
                   jax.ShapeDtypeStruct((1, D), F32)],
        compiler_params=_params(1, 40),
    )(x, gf, tgt)


def mlp_bwd(dx3, x2, g2, gs, us, wgt, wut, wd, l, deps=()):
    S = x2.shape[0]
    tm, tf = min(S, MLP_ROWS), MLP_COLS

    def body(dx_ref, x_ref, g_ref, gs_ref, us_ref, wg_ref, wu_ref, wd_ref,
             dx2_ref, dg_ref, du_ref, h_ref, dgain_ref):
        @pl.when(pl.program_id(0) == 0)
        def _():
            dgain_ref[...] = jnp.zeros((1, D), F32)

        dxv = dx_ref[...]
        dxb = dxv.astype(BF)
        for c0 in range(0, FF, tf):
            da = _dot_nt(dxb, wd_ref[c0:c0 + tf, :])
            gv = gs_ref[:, c0:c0 + tf].astype(F32)
            uv = us_ref[:, c0:c0 + tf].astype(F32)
            sg = _sigmoid(gv)
            dg_ref[:, c0:c0 + tf] = (da * uv * (sg * (1.0 + gv * (1.0 - sg)))).astype(BF)
            du_ref[:, c0:c0 + tf] = (da * (gv * sg)).astype(BF)
        dh = _dot(dg_ref[...], wg_ref[...]) + _dot(du_ref[...], wu_ref[...])
        dx2, dgain, hb = _rms_bwd(x_ref[...], g_ref[...], dh, dxv)
        dx2_ref[...] = dx2
        dgain_ref[...] += dgain
        h_ref[...] = hb

    wspec = pl.BlockSpec((FF, D), lambda i: (0, 0), pipeline_mode=pl.Buffered(1))
    row = pl.BlockSpec((tm, D), lambda i: (i, 0))
    wide = pl.BlockSpec((tm, FF), lambda i: (i, 0))
    vec = pl.BlockSpec((1, D), lambda i: (0, 0))
    return _ordered_call(
        body, deps, name=f"mlp_bwd_{l}", grid=(S // tm,),
        in_specs=[row, row, vec, wide, wide, wspec, wspec, wspec],
        out_specs=[row, wide, wide, row, vec],
        out_shape=[jax.ShapeDtypeStruct((S, D), F32), jax.ShapeDtypeStruct((S, FF), BF),
                   jax.ShapeDtypeStruct((S, FF), BF), jax.ShapeDtypeStruct((S, D), BF),
                   jax.ShapeDtypeStruct((1, D), F32)],
        compiler_params=_params(1, 56),
    )(dx3, x2, g2, gs, us, wgt, wut, wd, *deps)


def wgrad(a, b, l, name, deps=()):
    S, K = a.shape
    N = b.shape[1]
    tk = 1408 if K == FF else min(K, 1024)
    tn = min(N, 1024)
    ts = min(S, WGRAD_ROWS)
    ns = S // ts

    def body(a_ref, b_ref, o_ref, acc):
        s = pl.program_id(2)

        @pl.when(s == 0)
        def _():
            acc[...] = jnp.zeros((tk, tn), F32)

        acc[...] += _dot_tn(a_ref[...], b_ref[...].astype(BF))

        @pl.when(s == ns - 1)
        def _():
            o_ref[...] = acc[...].astype(BF)

    return _ordered_call(
        body, deps, name=f"{name}_{l}", grid=(K // tk, N // tn, ns),
        in_specs=[pl.BlockSpec((ts, tk), lambda i, j, s: (s, i)),
                  pl.BlockSpec((ts, tn), lambda i, j, s: (s, j))],
        out_specs=pl.BlockSpec((tk, tn), lambda i, j, s: (i, j)),
        out_shape=jax.ShapeDtypeStruct((K, N), BF),
        scratch_shapes=[pltpu.VMEM((tk, tn), F32)],
        compiler_params=_params(3, 48),
    )(a, b, *deps)


def out_proj_bwd(dx2, wout, r_raw, proj, gn, u1, lg, lb, l, deps=()):
    S = dx2.shape[0]
    tm = min(S, SEQ_TILE)
    gate_blk = (2 * CW + 3 * RW) // RW

    def body(dx_ref, w_ref, r_ref, gate_ref, gn_ref, u1_ref, lg_ref, lb_ref,
             dgate_ref, dr_ref, du1_ref, sums_ref):
        @pl.when(pl.program_id(0) == 0)
        def _():
            sums_ref[...] = jnp.zeros((8, CW), F32)

        dmix = _dot_nt(dx_ref[...].astype(BF), w_ref[...])
        gt = gate_ref[...].astype(F32)
        sg = _sigmoid(gt)
        sil = gt * sg
        dsil = sg * (1.0 + gt * (1.0 - sg))
        for h in range(NH):
            cols = slice(h * HD, (h + 1) * HD)
            rh = r_ref[:, cols]
            d = rh - _mean(rh)
            rs = lax.rsqrt(_mean(d * d) + EPS)
            rn = d * rs
            drr = dmix[:, CW + h * HD:CW + (h + 1) * HD]
            gnh = gn_ref[:, cols]
            sums_ref[0:1, cols] += jnp.sum(drr * rn * sil[:, cols], axis=0, keepdims=True)
            dgate_ref[:, cols] = (drr * rn * gnh * dsil[:, cols]).astype(BF)
            drn = drr * gnh * sil[:, cols]
            dr_ref[:, cols] = (rs * (drn - _mean(drn) - rn * _mean(drn * rn))).astype(BF)
        du = dmix[:, 0:CW]
        u1 = u1_ref[...]
        d = u1 - _mean(u1)
        rs = lax.rsqrt(_mean(d * d) + EPS)
        xh = d * rs
        u2 = xh * lg_ref[...] + lb_ref[...]
        sg2 = _sigmoid(u2)
        du2 = du * (sg2 * (1.0 + u2 * (1.0 - sg2)))
        sums_ref[1:2, :] += jnp.sum(du2 * xh, axis=0, keepdims=True)
        sums_ref[2:3, :] += jnp.sum(du2, axis=0, keepdims=True)
        dxh = du2 * lg_ref[...]
        du1_ref[...] = rs * (dxh - _mean(dxh) - xh * _mean(dxh * xh))

    vec = pl.BlockSpec((1, CW), lambda i: (0, 0))
    half = pl.BlockSpec((tm, CW), lambda i: (i, 0))
    return _ordered_call(
        body, deps, name=f"out_proj_bwd_{l}", grid=(S // tm,),
        in_specs=[pl.BlockSpec((tm, D), lambda i: (i, 0)),
                  pl.BlockSpec((D, D), lambda i: (0, 0)),
                  half, pl.BlockSpec((tm, RW), lambda i: (i, gate_blk)), vec, half, vec, vec],
        out_specs=[half, half, half, pl.BlockSpec((8, CW), lambda i: (0, 0))],
        out_shape=[jax.ShapeDtypeStruct((S, RW), BF), jax.ShapeDtypeStruct((S, RW), BF),
                   jax.ShapeDtypeStruct((S, CW), F32), jax.ShapeDtypeStruct((8, CW), F32)],
        compiler_params=_params(1, 40),
    )(dx2, wout, r_raw, proj, gn, u1, lg, lb, *deps)


def _conv_bwd_fill(tc, du1_ref, dwb_ref, buf, pacc):
    i = pl.program_id(0)

    @pl.when(i == 0)
    def _():
        buf[tc:tc + HALO, :] = jnp.zeros((HALO, CW), F32)
        dwb_ref[...] = jnp.zeros((CK + 1, CW), F32)
        pacc[...] = jnp.zeros((CK, 8, CW), F32)

    @pl.when(i > 0)
    def _():
        buf[tc:tc + HALO, :] = buf[0:HALO, :]

    buf[0:tc, :] = du1_ref[...]


def _conv_bwd_rows(r0, groups, pacc, a_ref, b_ref, w_ref, dab_ref, buf, win):
    for c0 in range(0, CW, CONV_COLS):
        cols = slice(c0, c0 + CONV_COLS)
        av = a_ref[r0:r0 + CONV_ROWS, cols].astype(F32)
        sgb = _sigmoid(b_ref[r0:r0 + CONV_ROWS, cols].astype(F32))
        u0 = av * sgb
        acc = jnp.zeros((CONV_ROWS, CONV_COLS), F32)
        for start, length, taps in groups:
            win[0:length, :] = buf[r0 + start:r0 + start + length, cols]
            for j, at in taps:
                sl = win[at:at + CONV_ROWS, :]
                acc = acc + w_ref[CK - 1 - j:CK - j, cols] * sl
                pacc[CK - 1 - j, :, cols] += _fold8(u0 * sl)
        dab_ref[r0:r0 + CONV_ROWS, c0:c0 + CONV_COLS] = (acc * sgb).astype(BF)
        dab_ref[r0:r0 + CONV_ROWS, CW + c0:CW + c0 + CONV_COLS] = (acc * av * sgb * (1.0 - sgb)).astype(BF)


def _conv_bwd_finish(nt, pacc, du1_ref, dwb_ref):
    dwb_ref[CK:CK + 1, :] += jnp.sum(du1_ref[...], axis=0, keepdims=True)

    @pl.when(pl.program_id(0) == nt - 1)
    def _():
        for k in range(CK):
            dwb_ref[k:k + 1, :] = jnp.sum(pacc[k], axis=0, keepdims=True)


def _ret_bwd_chunk(c, q_ref, k_ref, v_ref, cos_ref, sin_ref, dm_ref, qd_ref, kd_ref, cd_ref, dr_ref, st_ref,
                   dq_ref, dk_ref, dv_ref, gst):
    scale = HD ** -0.5
    rows = slice(c * CHUNK, (c + 1) * CHUNK)
    cs, sn = cos_ref[rows, :], sin_ref[rows, :]
    for h in range(NH):
        cols = slice(h * HD, (h + 1) * HD)
        qr = _rot(q_ref[rows, cols].astype(F32), cs, sn)
        kr = _rot(k_ref[rows, cols].astype(F32), cs, sn) * scale
        qb, kb = qr.astype(BF), kr.astype(BF)
        vb = v_ref[rows, cols].astype(BF)
        dob = dr_ref[rows, cols]
        sb = st_ref[h, c]
        gn1 = gst[h]
        gb = gn1.astype(BF)
        sc = (_dot_nt(qb, kb) * dm_ref[h]).astype(BF)
        dsc = (_dot_nt(dob, vb) * dm_ref[h]).astype(BF)
        dqr = _dot(dsc, kb) + _dot_nt(dob, sb) * qd_ref[h]
        dkr = _dot_tn(dsc, qb) + _dot_nt(vb, gb) * kd_ref[h]
        dvv = _dot_tn(sc, dob) + _dot((kr * kd_ref[h]).astype(BF), gb)
        gst[h] = cd_ref[h] * gn1 + _dot_tn((qr * qd_ref[h]).astype(BF), dob)
        dq_ref[rows, cols] = _rot_t(dqr, cs, sn).astype(BF)
        dk_ref[rows, cols] = _rot_t(dkr * scale, cs, sn).astype(BF)
        dv_ref[rows, cols] = dvv.astype(BF)


def mix_bwd(du1, dr, proj, cw, states, tables, l, deps=()):
    S = proj.shape[0]
    tt = min(S, SEQ_TILE)
    cpb = tt // CHUNK
    nt = S // tt
    groups = _tap_groups(0, CK)

    def body(du1_ref, a_ref, b_ref, w_ref, q_ref, k_ref, v_ref, cos_ref, sin_ref, dm_ref, qd_ref, kd_ref, cd_ref,
             dr_ref, st_ref, dab_ref, dwb_ref, dq_ref, dk_ref, dv_ref, buf, win, gst, pacc):
        @pl.when(pl.program_id(0) == 0)
        def _():
            gst[...] = jnp.zeros((NH, HD, HD), F32)

        _conv_bwd_fill(tt, du1_ref, dwb_ref, buf, pacc)
        for c in reversed(range(cpb)):
            _ret_bwd_chunk(c, q_ref, k_ref, v_ref, cos_ref, sin_ref, dm_ref, qd_ref, kd_ref, cd_ref, dr_ref, st_ref,
                           dq_ref, dk_ref, dv_ref, gst)
            for r0 in range(c * CHUNK, (c + 1) * CHUNK, CONV_ROWS):
                _conv_bwd_rows(r0, groups, pacc, a_ref, b_ref, w_ref, dab_ref, buf, win)
        _conv_bwd_finish(nt, pacc, du1_ref, dwb_ref)

    rev = lambda t: nt - 1 - t
    half = pl.BlockSpec((tt, CW), lambda t: (rev(t), 0))
    return _ordered_call(
        body, deps, name=f"mix_bwd_{l}", grid=(nt,),
        in_specs=[half, half, pl.BlockSpec((tt, CW), lambda t: (rev(t), 1)), pl.BlockSpec((CK, CW), lambda t: (0, 0))]
        + _ret_specs(tt, rev) + [half, pl.BlockSpec((NH, cpb, HD, HD), lambda t: (0, rev(t), 0, 0))],
        out_specs=[pl.BlockSpec((tt, 2 * CW), lambda t: (rev(t), 0)), pl.BlockSpec((CK + 1, CW), lambda t: (0, 0)),
                   half, half, half],
        out_shape=[jax.ShapeDtypeStruct((S, 2 * CW), BF), jax.ShapeDtypeStruct((CK + 1, CW), F32)]
        + [jax.ShapeDtypeStruct((S, RW), BF)] * 3,
        scratch_shapes=[pltpu.VMEM((tt + HALO, CW), F32), pltpu.VMEM((HALO + CONV_ROWS, CONV_COLS), F32),
                        pltpu.VMEM((NH, HD, HD), F32), pltpu.VMEM((CK, 8, CW), F32)],
        compiler_params=_params(1, 40),
    )(du1, proj, proj, cw, proj, proj, proj, *tables, dr, states, *deps)


def in_proj_bwd(parts, win, x, g, dx2, l, deps=()):
    S = x.shape[0]
    tm = min(S, SEQ_TILE)
    n = len(parts)

    def body(*refs):
        srcs = refs[:n]
        w_ref, x_ref, g_ref, dx2_ref, dx_ref, dgain_ref = refs[n:]

        @pl.when(pl.program_id(0) == 0)
        def _():
            dgain_ref[...] = jnp.zeros((1, D), F32)

        dh, col = None, 0
        for r in srcs:
            width = r.shape[1]
            term = _dot_nt(r[...], w_ref[:, col:col + width])
            dh = term if dh is None else dh + term
            col += width
        dx, dgain, _ = _rms_bwd(x_ref[...], g_ref[...], dh, dx2_ref[...])
        dx_ref[...] = dx
        dgain_ref[...] += dgain

    row = pl.BlockSpec((tm, D), lambda i: (i, 0))
    vec = pl.BlockSpec((1, D), lambda i: (0, 0))
    return _ordered_call(
        body, deps, name=f"in_proj_bwd_{l}", grid=(S // tm,),
        in_specs=[pl.BlockSpec((tm, p.shape[1]), lambda i: (i, 0)) for p in parts]
        + [pl.BlockSpec((D, INW), lambda i: (0, 0)), row, vec, row],
        out_specs=[row, vec],
        out_shape=[jax.ShapeDtypeStruct((S, D), F32), jax.ShapeDtypeStruct((1, D), F32)],
        compiler_params=_params(1, 48),
    )(*parts, win, x, g, dx2, *deps)


def wgrad_in(h, parts, l):
    S = h.shape[0]
    ts = min(S, SEQ_TILE)
    ns = S // ts

    def body(*refs):
        h_ref, srcs = refs[0], refs[1:1 + len(parts)]
        o_ref, acc = refs[-2], refs[-1]
        s = pl.program_id(0)

        @pl.when(s == 0)
        def _():
            acc[...] = jnp.zeros((D, INW), F32)

        hv = h_ref[...]
        col = 0
        for r in srcs:
            width = r.shape[1]
            acc[:, col:col + width] += _dot_tn(hv, r[...])
            col += width

        @pl.when(s == ns - 1)
        def _():
            o_ref[...] = acc[...].astype(BF)

    return pl.pallas_call(
        body, name=f"wgrad_in_{l}", grid=(ns,),
        in_specs=[pl.BlockSpec((ts, D), lambda s: (s, 0))]
        + [pl.BlockSpec((ts, p.shape[1]), lambda s: (s, 0)) for p in parts],
        out_specs=pl.BlockSpec((D, INW), lambda s: (0, 0)),
        out_shape=jax.ShapeDtypeStruct((D, INW), BF),
        scratch_shapes=[pltpu.VMEM((D, INW), F32)],
        compiler_params=_params(1, 48),
    )(h, *parts)


def sum_slots(recv, name, deps=()):
    _, R, C = recv.shape
    tr = 256 if R % 256 == 0 else R

    def body(r_ref, o_ref):
        acc = r_ref[0].astype(F32)
        for k in range(1, NCHIP):
            acc = acc + r_ref[k].astype(F32)
        o_ref[...] = acc

    return _ordered_call(
        body, deps, name=name, grid=(R // tr,),
        in_specs=[pl.BlockSpec((NCHIP, tr, C), lambda i: (0, i, 0))],
        out_specs=pl.BlockSpec((tr, C), lambda i: (i, 0)),
        out_shape=jax.ShapeDtypeStruct((R, C), F32),
        compiler_params=_params(1, 32),
    )(recv, *deps)


def adamw(w, ga, gb, m, v, name):
    R, C = w.shape
    tr = 256 if R % 256 == 0 else R
    c1 = 1.0 - ADAM_B1 ** ADAM_STEP
    c2 = 1.0 - ADAM_B2 ** ADAM_STEP

    def body(w_ref, ga_ref, gb_ref, m_ref, v_ref, g_out, d_out, m_out, v_out):
        g = ga_ref[...] + gb_ref[...]
        mn = ADAM_B1 * m_ref[...] + (1.0 - ADAM_B1) * g
        vn = ADAM_B2 * v_ref[...] + (1.0 - ADAM_B2) * (g * g)
        g_out[...] = g
        m_out[...] = mn
        v_out[...] = vn
        d_out[...] = -ADAM_LR * ((mn / c1) / (jnp.sqrt(vn / c2) + ADAM_EPS) + ADAM_WD * w_ref[...])

    blk = pl.BlockSpec((tr, C), lambda i: (i, 0))
    return pl.pallas_call(
        body, name=name, grid=(R // tr,),
        in_specs=[blk] * 5, out_specs=[blk] * 4,
        out_shape=[jax.ShapeDtypeStruct((R, C), F32)] * 4,
        compiler_params=_params(1, 40),
    )(w, ga, gb, m, v)


def _place():
    x, y, c = lax.axis_index("x"), lax.axis_index("y"), lax.axis_index("c")
    chips = [(1 - x, y), (x, 1 - y), (1 - x, 1 - y)]
    return x, y, c, chips


def _window(ref, axis, j, size):
    idx = [slice(None)] * len(ref.shape)
    idx[axis] = pl.ds(pl.multiple_of(j * size, 128 if axis == len(ref.shape) - 1 else 16), size)
    return ref.at[tuple(idx)]


def _hbm(a):
    return pltpu.with_memory_space_constraint(a, pltpu.HBM)


def _hbm_like(arrs):
    return [pltpu.HBM(a.shape, a.dtype) for a in arrs]


def gather_start(shards, axes, after, tag):
    n = len(shards)
    na = len(after)
    lands = []
    for s, ax in zip(shards, axes):
        shp = list(s.shape)
        shp[ax] *= NCHIP
        lands.append(lax.empty(tuple(shp), s.dtype))

    def body(*refs):
        ins, land = refs[:n], refs[n:2 * n]
        send, recv = refs[2 * n + na], refs[2 * n + na + 1]
        token = refs[-1]
        x, y, c, chips = _place()
        for a in range(n):
            for k, chip in enumerate(chips):
                pltpu.make_async_remote_copy(
                    src_ref=ins[a], dst_ref=_window(land[a], axes[a], 2 * x + y, ins[a].shape[axes[a]]),
                    send_sem=send.at[3 * a + k], recv_sem=recv.at[3 * a + k],
                    device_id=(chip[0], chip[1], c), device_id_type=MESH).start()
        token[...] = jnp.zeros_like(token)

    outs = pl.pallas_call(
        body, name=f"gather_start_{tag}",
        in_specs=[HBM_SPEC] * (2 * n) + [ANY] * na,
        out_specs=(SEM_SPEC, SEM_SPEC, *[HBM_SPEC] * (2 * n), VMEM_SPEC),
        out_shape=(pltpu.SemaphoreType.DMA((3 * n,)), pltpu.SemaphoreType.DMA((3 * n,)),
                   *_hbm_like(shards), *_hbm_like(lands), jax.ShapeDtypeStruct((8, 128), F32)),
        input_output_aliases={a: 2 + a for a in range(2 * n)},
        compiler_params=pltpu.CompilerParams(has_side_effects=DATAFLOW),
    )(*[_hbm(s) for s in shards], *[_hbm(b) for b in lands], *after)
    return (outs[0], outs[1], list(outs[2:2 + n]), list(outs[2 + n:2 + 2 * n]), list(axes)), outs[-1]


def gather_wait(groups, after, tag):
    sizes = [len(g[2]) for g in groups]
    total = sum(sizes)

    def body(*refs):
        x, y, c, chips = _place()
        stage, loc = refs[-1 - total:-1], refs[-1]
        pos = 2 * total
        off = 0
        mine = []
        for g, n in zip(groups, sizes):
            ins, land = refs[off:off + n], refs[total + off:total + off + n]
            send_ref, recv_ref = refs[pos], refs[pos + 1]
            axes = g[4]
            for a in range(n):
                fetch = pltpu.make_async_copy(ins[a], stage[off + a], loc.at[2 * (off + a)])
                fetch.start()
                put = pltpu.make_async_copy(
                    stage[off + a], _window(land[a], axes[a], 2 * x + y, ins[a].shape[axes[a]]),
                    loc.at[2 * (off + a) + 1])
                mine.append((fetch, put))
                for k, chip in enumerate(chips):
                    cp = pltpu.make_async_remote_copy(
                        src_ref=ins[a],
                        dst_ref=_window(land[a], axes[a], 2 * chip[0] + chip[1], ins[a].shape[axes[a]]),
                        send_sem=send_ref.at[3 * a + k], recv_sem=recv_ref.at[3 * a + k],
                        device_id=(chip[0], chip[1], c), device_id_type=MESH)
                    cp.wait_send()
                    cp.wait_recv()
            pos += 2
            off += n
        for fetch, put in mine:
            fetch.wait()
            put.start()
        for fetch, put in mine:
            put.wait()

    shards = [s for g in groups for s in g[2]]
    lands = [b for g in groups for b in g[3]]
    sems = [s for g in groups for s in (g[0], g[1])]
    outs = pl.pallas_call(
        body, name=f"gather_wait_{tag}",
        in_specs=[HBM_SPEC] * (2 * total) + [SEM_SPEC] * len(sems) + [ANY],
        out_specs=[HBM_SPEC] * (2 * total),
        out_shape=(*_hbm_like(shards), *_hbm_like(lands)),
        input_output_aliases={a: a for a in range(2 * total)},
        scratch_shapes=[pltpu.VMEM(s.shape, s.dtype) for s in shards] + [pltpu.SemaphoreType.DMA((2 * total,))],
        compiler_params=pltpu.CompilerParams(has_side_effects=DATAFLOW, vmem_limit_bytes=32 << 20),
    )(*shards, *lands, *sems, after)
    return list(outs[total:])


def scatter_start(grads, axes, sizes, lands, l, tag):
    n = len(grads)

    def body(*refs):
        ins, land = refs[:n], refs[n:2 * n]
        send, recv = refs[2 * n], refs[2 * n + 1]
        token = refs[2 * n + 2 + 2 * n]
        stage, loc = refs[-1 - n:-1], refs[-1]
        x, y, c, chips = _place()
        me = 2 * x + y
        fetches = [pltpu.make_async_copy(_window(ins[a], axes[a], me, sizes[a]), stage[a], loc.at[2 * a])
                   for a in range(n)]
        for cp in fetches:
            cp.start()
        for a in range(n):
            for k, chip in enumerate(chips):
                pltpu.make_async_remote_copy(
                    src_ref=_window(ins[a], axes[a], 2 * chip[0] + chip[1], sizes[a]), dst_ref=land[a].at[me, l],
                    send_sem=send.at[3 * a + k], recv_sem=recv.at[3 * a + k],
                    device_id=(chip[0], chip[1], c), device_id_type=MESH).start()
        puts = [pltpu.make_async_copy(stage[a], land[a].at[me, l], loc.at[2 * a + 1]) for a in range(n)]
        for fetch, put in zip(fetches, puts):
            fetch.wait()
            put.start()
        for put in puts:
            put.wait()
        token[...] = jnp.zeros_like(token)

    outs = pl.pallas_call(
        body, name=f"scatter_start_{tag}",
        in_specs=[HBM_SPEC] * (2 * n),
        out_specs=(SEM_SPEC, SEM_SPEC, *[HBM_SPEC] * (2 * n), VMEM_SPEC),
        out_shape=(pltpu.SemaphoreType.DMA((3 * n,)), pltpu.SemaphoreType.DMA((3 * n,)),
                   *_hbm_like(grads), *_hbm_like(lands), jax.ShapeDtypeStruct((8, 128), F32)),
        input_output_aliases={a: 2 + a for a in range(2 * n)},
        scratch_shapes=[pltpu.VMEM(b.shape[2:], b.dtype) for b in lands] + [pltpu.SemaphoreType.DMA((2 * n,))],
        compiler_params=pltpu.CompilerParams(has_side_effects=DATAFLOW, vmem_limit_bytes=32 << 20),
    )(*[_hbm(g) for g in grads], *[_hbm(b) for b in lands])
    group = (outs[0], outs[1], list(outs[2:2 + n]), list(axes), list(sizes), l)
    return group, list(outs[2 + n:2 + 2 * n]), outs[-1]


def scatter_wait(groups, lands, which, after):
    nl = len(lands)

    def body(*refs):
        land = refs[:nl]
        x, y, c, chips = _place()
        pos = nl
        for g, wh in zip(groups, which):
            n = len(g[2])
            ins = refs[pos:pos + n]
            send_ref, recv_ref = refs[pos + n], refs[pos + n + 1]
            axes, sizes, l = g[3], g[4], g[5]
            for a in range(n):
                for k, chip in enumerate(chips):
                    jp = 2 * chip[0] + chip[1]
                    cp = pltpu.make_async_remote_copy(
                        src_ref=_window(ins[a], axes[a], jp, sizes[a]), dst_ref=land[wh[a]].at[jp, l],
                        send_sem=send_ref.at[3 * a + k], recv_sem=recv_ref.at[3 * a + k],
                        device_id=(chip[0], chip[1], c), device_id_type=MESH)
                    cp.wait_send()
                    cp.wait_recv()
            pos += n + 2

    operands = list(lands)
    specs = [HBM_SPEC] * nl
    for g in groups:
        operands += list(g[2]) + [g[0], g[1]]
        specs += [HBM_SPEC] * len(g[2]) + [SEM_SPEC, SEM_SPEC]
    outs = pl.pallas_call(
        body, name="scatter_wait", in_specs=specs + [ANY], out_specs=[HBM_SPEC] * nl,
        out_shape=tuple(_hbm_like(lands)),
        input_output_aliases={a: a for a in range(nl)},
        compiler_params=pltpu.CompilerParams(has_side_effects=DATAFLOW),
    )(*operands, after)
    return list(outs)


def swap_start(part, tag):
    def body(p_ref, land_ref, send, recv, p_thru, land_thru, token):
        x, y, c, _ = _place()
        pltpu.make_async_remote_copy(src_ref=p_ref, dst_ref=land_ref, send_sem=send, recv_sem=recv,
                                     device_id=(x, y, 1 - c), device_id_type=MESH).start()
        token[...] = jnp.zeros_like(token)

    outs = pl.pallas_call(
        body, name=f"swap_start_{tag}", in_specs=[HBM_SPEC, HBM_SPEC],
        out_specs=(SEM_SPEC, SEM_SPEC, HBM_SPEC, HBM_SPEC, VMEM_SPEC),
        out_shape=(pltpu.SemaphoreType.DMA(()), pltpu.SemaphoreType.DMA(()), *_hbm_like([part, part]),
                   jax.ShapeDtypeStruct((8, 128), F32)),
        input_output_aliases={0: 2, 1: 3},
        compiler_params=pltpu.CompilerParams(has_side_effects=DATAFLOW),
    )(_hbm(part), _hbm(lax.empty(part.shape, part.dtype)))
    return tuple(outs[:4]), outs[4]


def swap_wait(group, after, tag):
    send, recv, part, land = group

    def body(p_ref, land_ref, send_ref, recv_ref, after_ref, p_out, land_out):
        x, y, c, _ = _place()
        cp = pltpu.make_async_remote_copy(src_ref=p_ref, dst_ref=land_ref, send_sem=send_ref, recv_sem=recv_ref,
                                          device_id=(x, y, 1 - c), device_id_type=MESH)
        cp.wait_send()
        cp.wait_recv()

    outs = pl.pallas_call(
        body, name=f"swap_wait_{tag}", in_specs=[HBM_SPEC, HBM_SPEC, SEM_SPEC, SEM_SPEC, ANY],
        out_specs=[HBM_SPEC, HBM_SPEC], out_shape=tuple(_hbm_like([part, land])),
        input_output_aliases={0: 0, 1: 1},
        compiler_params=pltpu.CompilerParams(has_side_effects=DATAFLOW),
    )(part, land, send, recv, after)
    return outs[0], outs[1]


def small_allreduce(p):
    R, C = p.shape
    ndev = 8

    def body(p_ref, o_ref, buf, send, recv):
        x, y, c, _ = _place()
        me = 4 * x + 2 * y + c
        buf[me] = p_ref[...]

        def peer(d):
            px = 1 - x if d & 4 else x
            py = 1 - y if d & 2 else y
            pc = 1 - c if d & 1 else c
            return px, py, pc

        def copy(d, slot):
            return pltpu.make_async_remote_copy(src_ref=p_ref, dst_ref=buf.at[slot], send_sem=send.at[d - 1],
                                                recv_sem=recv.at[d - 1], device_id=peer(d), device_id_type=MESH)

        sends = [copy(d, me) for d in range(1, ndev)]
        for cp in sends:
            cp.start()
        for d in range(1, ndev):
            px, py, pc = peer(d)
            copy(d, 4 * px + 2 * py + pc).wait_recv()
        for cp in sends:
            cp.wait_send()
        acc = buf[0]
        for k in range(1, ndev):
            acc = acc + buf[k]
        o_ref[...] = acc

    return pl.pallas_call(
        body, name="small_allreduce", in_specs=[VMEM_SPEC], out_specs=VMEM_SPEC,
        out_shape=jax.ShapeDtypeStruct((R, C), F32),
        scratch_shapes=[pltpu.VMEM((ndev, R, C), F32), pltpu.SemaphoreType.DMA((ndev - 1,)),
                        pltpu.SemaphoreType.DMA((ndev - 1,))],
        compiler_params=pltpu.CompilerParams(vmem_limit_bytes=32 << 20),
    )(p)


def kernel(x, norm1_g, w_in, conv_w, conv_b, conv_ln_g, conv_ln_b, ret_gn_g, w_out, norm2_g, w_gate, w_up, w_down, final_g, loss_target, m_norm1_g, m_w_in, m_conv_w, m_conv_b, m_conv_ln_g, m_conv_ln_b, m_ret_gn_g, m_w_out, m_norm2_g, m_w_gate, m_w_up, m_w_down, m_final_g, v_norm1_g, v_w_in, v_conv_w, v_conv_b, v_conv_ln_g, v_conv_ln_b, v_ret_gn_g, v_w_out, v_norm2_g, v_w_gate, v_w_up, v_w_down, v_final_g):
    S = x.shape[1]
    xs = x.reshape(S, D)
    tgt = loss_target.reshape(S, D)
    fsh = FF // NCHIP

    def shards_of(l):
        return [w_in[l].astype(BF), w_out[l].astype(BF), w_gate[l].T.astype(BF), w_up[l].T.astype(BF),
                w_down[l].astype(BF), conv_w[l]]

    gather_axes = [1, 0, 0, 0, 0, 1]
    shard_cache = [shards_of(l) for l in range(L)]
    tables = _ret_tables(S)
    row = lambda a, l: a[l].reshape(1, -1)
    groups = {}
    weights = [dict() for _ in range(L)]

    def begin(l, which, after):
        group, token = gather_start([shard_cache[l][i] for i in which], [gather_axes[i] for i in which],
                                    after, f"{l}_{which[0]}")
        groups[(l, which[0])] = (group, which)
        return token

    def finish(l, firsts, after, tag):
        gs = [groups[(l, f)] for f in firsts]
        outs = gather_wait([g for g, _ in gs], after, f"{l}_{tag}")
        k = 0
        for _, which in gs:
            for i in which:
                weights[l][i] = outs[k]
                k += 1

    first = begin(0, [0], [])
    second = begin(0, [1, 5], [first])
    after = begin(0, [2, 3, 4], [second])
    saved = []
    xc = xs
    for l in range(L):
        if l == 0:
            finish(l, [0], after, "a")
        else:
            finish(l, [0], after, "all")
        win = weights[l][0]
        proj, h1 = in_proj(xc, row(norm1_g, l), win, l)
        if l == 0:
            finish(l, [1], proj, "b")
        wout, cw = weights[l][1], weights[l][5]
        ahead = [begin(l + 1, [0, 1, 2, 3, 4, 5], [proj])] if l + 1 < L else []
        u1, u, r_raw, states = mix_fwd(proj, cw, row(conv_b, l), row(conv_ln_g, l), row(conv_ln_b, l), tables, l, ahead)
        x2, mixed = out_proj(u, r_raw, proj, row(ret_gn_g, l), wout, xc, l)
        if l == 0:
            finish(l, [2], x2, "c")
        wgt, wut, wd = weights[l][2], weights[l][3], weights[l][4]
        x3, gs, us, act = mlp_fwd(x2, row(norm2_g, l), wgt, wut, wd, l)
        saved.append((xc, proj, h1, u1, r_raw, states, mixed, x2, gs, us, act))
        xc = x3
        after = x3

    dx, loss_acc, d_final = final_loss(xc, final_g.reshape(1, D), tgt)
    loss = lax.psum(loss_acc[0, 0] * (0.5 / D), ("x", "y", "c"))

    scatter_axes = [1, 0, 0, 0, 0]
    scatter_sizes = [INW // NCHIP, D // NCHIP, fsh, fsh, fsh]
    lands = [lax.empty((NCHIP, L, D, INW // NCHIP), BF), lax.empty((NCHIP, L, D // NCHIP, D), BF),
             lax.empty((NCHIP, L, fsh, D), BF), lax.empty((NCHIP, L, fsh, D), BF), lax.empty((NCHIP, L, fsh, D), BF)]
    sent, sent_which = [], []

    def send_grads(gs, which, l):
        group, new_lands, token = scatter_start(gs, [scatter_axes[a] for a in which], [scatter_sizes[a] for a in which],
                                                [lands[a] for a in which], l, f"{l}_{which[0]}")
        for a, b in zip(which, new_lands):
            lands[a] = b
        sent.append(group)
        sent_which.append(list(which))
        return [token]

    small = [None] * L
    for l in reversed(range(L)):
        xin, proj, h1, u1, r_raw, states, mixed, x2, gs, us, act = saved[l]
        win, wout, wgt, wut, wd, cw = (weights[l][i] for i in range(6))
        dx2, dgs, dus, h2, d_n2 = mlp_bwd(dx, x2, row(norm2_g, l), gs, us, wgt, wut, wd, l)
        g_wd = wgrad(act, dx, l, "wgrad_down")
        g_wgt = wgrad(dgs, h2, l, "wgrad_gate", send_grads([g_wd], [4], l))
        g_wut = wgrad(dus, h2, l, "wgrad_up")
        dgate, dr, du1, sums = out_proj_bwd(dx2, wout, r_raw, proj, row(ret_gn_g, l), u1,
                                            row(conv_ln_g, l), row(conv_ln_b, l), l, send_grads([g_wgt, g_wut], [2, 3], l))
        g_wout = wgrad(mixed, dx2, l, "wgrad_out")
        dab, dwb, dq, dk, dv = mix_bwd(du1, dr, proj, cw, states, tables, l)
        dproj = [dab, dq, dk, dv, dgate]
        g_win = wgrad_in(h1, dproj, l)
        dx, d_n1 = in_proj_bwd(dproj, win, xin, row(norm1_g, l), dx2, l, send_grads([g_wout, g_win], [1, 0], l))
        small[l] = jnp.concatenate([dwb, sums, d_n1.reshape(2, CW), d_n2.reshape(2, CW)], axis=0)
    grad_x = dx.reshape(1, S, D)

    per = CK + 1 + 8 + 4
    packed = jnp.concatenate(small + [d_final.reshape(2, CW), jnp.zeros((6, CW), F32)], axis=0)
    tot = small_allreduce(packed)
    lay = tot[:L * per].reshape(L, per, CW)
    g_conv_w_full = lay[:, 0:CK, :]
    j = 2 * lax.axis_index("x") + lax.axis_index("y")
    g_conv_w = lax.dynamic_slice_in_dim(g_conv_w_full, j * (CW // NCHIP), CW // NCHIP, axis=2)
    g_small = {
        "conv_b": lay[:, CK, :], "ret_gn_g": lay[:, CK + 1, :], "conv_ln_g": lay[:, CK + 2, :],
        "conv_ln_b": lay[:, CK + 3, :], "norm1_g": lay[:, CK + 9:CK + 11, :].reshape(L, D),
        "norm2_g": lay[:, CK + 11:CK + 13, :].reshape(L, D), "final_g": tot[L * per:L * per + 2].reshape(D),
    }

    recv = scatter_wait(sent, lands, sent_which, tot)
    shard_shapes = [(L * D, INW // NCHIP), (L * D // NCHIP, D), (L * fsh, D), (L * fsh, D), (L * fsh, D)]
    names = ["w_in", "w_out", "w_gate", "w_up", "w_down"]
    swaps, token = [], []
    for r, shp, nm in zip(recv, shard_shapes, names):
        group, tk = swap_start(sum_slots(r.reshape((NCHIP,) + shp), f"sum_{nm}", token), nm)
        swaps.append(group)
        token = [tk]
    parts, theirs = zip(*[swap_wait(group, token[0], nm) for group, nm in zip(swaps, names)])

    def hidden_major(a):
        return jnp.swapaxes(a, 1, 2).reshape(L * fsh, D)

    big = {}
    wmv = {"w_in": (w_in, m_w_in, v_w_in), "w_out": (w_out, m_w_out, v_w_out),
           "w_gate": (w_gate, m_w_gate, v_w_gate), "w_up": (w_up, m_w_up, v_w_up),
           "w_down": (w_down, m_w_down, v_w_down)}
    for nm, mine, other in zip(names, parts, theirs):
        w, m, v = wmv[nm]
        if nm in ("w_gate", "w_up"):
            outs = adamw(hidden_major(w), mine, other, hidden_major(m), hidden_major(v), f"adamw_{nm}")
            big[nm] = [jnp.swapaxes(o.reshape(L, fsh, D), 1, 2) for o in outs]
        else:
            shp2 = (w.shape[0] * w.shape[1], w.shape[2])
            outs = adamw(w.reshape(shp2), mine, other, m.reshape(shp2), v.reshape(shp2), f"adamw_{nm}")
            big[nm] = [o.reshape(w.shape) for o in outs]

    cshape = (L * CK, CW // NCHIP)
    zc = jnp.zeros(cshape, F32)
    big["conv_w"] = [o.reshape(conv_w.shape) for o in adamw(
        conv_w.reshape(cshape), g_conv_w.reshape(cshape), zc, m_conv_w.reshape(cshape),
        v_conv_w.reshape(cshape), "adamw_conv_w")]
    vec_names = ["norm1_g", "conv_b", "conv_ln_g", "conv_ln_b", "ret_gn_g", "norm2_g", "final_g"]
    vec_w = {"norm1_g": (norm1_g, m_norm1_g, v_norm1_g), "conv_b": (conv_b, m_conv_b, v_conv_b),
             "conv_ln_g": (conv_ln_g, m_conv_ln_g, v_conv_ln_g), "conv_ln_b": (conv_ln_b, m_conv_ln_b, v_conv_ln_b),
             "ret_gn_g": (ret_gn_g, m_ret_gn_g, v_ret_gn_g), "norm2_g": (norm2_g, m_norm2_g, v_norm2_g),
             "final_g": (final_g, m_final_g, v_final_g)}
    cat = lambda arrs: jnp.concatenate([a.reshape(-1, CW) for a in arrs], axis=0)
    vw = cat([vec_w[nm][0] for nm in vec_names])
    vm = cat([vec_w[nm][1] for nm in vec_names])
    vv = cat([vec_w[nm][2] for nm in vec_names])
    vg = cat([g_small[nm] for nm in vec_names])
    vouts = adamw(vw, vg, jnp.zeros_like(vg), vm, vv, "adamw_vectors")
    off = 0
    for nm in vec_names:
        w = vec_w[nm][0]
        nrow = w.size // CW
        big[nm] = [o[off:off + nrow].reshape(w.shape) for o in vouts]
        off += nrow

    order = ["norm1_g", "w_in", "conv_w", "conv_b", "conv_ln_g", "conv_ln_b", "ret_gn_g", "w_out", "norm2_g",
             "w_gate", "w_up", "w_down", "final_g"]
    return (loss, grad_x, *[big[nm][0] for nm in order], *[big[nm][1] for nm in order],
            *[big[nm][2] for nm in order], *[big[nm][3] for nm in order])
```

```python
import math

import jax
import jax.numpy as jnp
from jax import lax
from jax.experimental import pallas as pl
from jax.experimental.pallas import tpu as pltpu

D = 1024
L = 4
CW = 512
RW = 512
NH = 4
HD = 128
CK = 31
CHUNK = 64
INW = 3072
FF = 2816
NCHIP = 4
EPS = 1e-6
ROPE_BASE = 10000.0
SEQ_TILE = 512
WGRAD_ROWS = 1024
MLP_ROWS = 256
MLP_COLS = 1408
HALO = 32
CONV_ROWS = 32
CONV_COLS = 256

ADAM_LR = 0.001
ADAM_B1 = 0.9
ADAM_B2 = 0.999
ADAM_EPS = 1e-08
ADAM_WD = 0.01
ADAM_STEP = 10

BF = jnp.bfloat16
F32 = jnp.float32
MESH = pl.DeviceIdType.MESH
ANY = pl.BlockSpec(memory_space=pl.ANY)
VMEM_SPEC = pl.BlockSpec(memory_space=pltpu.VMEM)
HBM_SPEC = pl.BlockSpec(memory_space=pltpu.HBM)
SEM_SPEC = pl.BlockSpec(memory_space=pltpu.SEMAPHORE)
DATAFLOW = pltpu.SideEffectType.DATAFLOW_SIDE_EFFECTING


def _params(n_grid, vmem_mb):
    return pltpu.CompilerParams(dimension_semantics=("arbitrary",) * n_grid,
                                vmem_limit_bytes=vmem_mb << 20)


def _ordered_call(body, deps, *, in_specs, **kw):
    n, nd = len(in_specs), len(deps)

    def with_deps(*refs):
        body(*refs[:n], *refs[n + nd:])

    return pl.pallas_call(with_deps, in_specs=list(in_specs) + [ANY] * nd, **kw)


def _dot(a, b):
    return jnp.dot(a, b, preferred_element_type=F32)


def _dot_nt(a, b):
    return lax.dot_general(a, b, (((1,), (1,)), ((), ())), preferred_element_type=F32)


def _dot_tn(a, b):
    return lax.dot_general(a, b, (((0,), (0,)), ((), ())), preferred_element_type=F32)


def _sigmoid(x):
    return 0.5 * jnp.tanh(0.5 * x) + 0.5


def _mean(x):
    return jnp.mean(x, axis=-1, keepdims=True)


def _fold8(x):
    out = x[0:8, :]
    for q in range(1, x.shape[0] // 8):
        out = out + x[8 * q:8 * q + 8, :]
    return out


def _tap_groups(first, count):
    groups = []
    for phase in range(8):
        taps = [(t, first + t - phase) for t in range(count) if (first + t) % 8 == phase]
        if taps:
            lo, hi = min(q for _, q in taps), max(q for _, q in taps)
            groups.append((lo + phase, hi - lo + CONV_ROWS, [(t, q - lo) for t, q in taps]))
    return groups


def _rot(t, cs, sn):
    return t * cs + pltpu.roll(t, HD // 2, 1) * sn


def _rot_t(dy, cs, sn):
    return dy * cs + pltpu.roll(dy * sn, HD // 2, 1)


def _rms_bwd(x, g, dh, dx_in):
    r = lax.rsqrt(_mean(x * x) + EPS)
    xh = x * r
    dxh = dh * g
    dx = dx_in + r * (dxh - xh * _mean(dxh * xh))
    return dx, jnp.sum(dh * xh, axis=0, keepdims=True), (xh * g).astype(BF)


def in_proj(x, g, win, l, deps=()):
    S = x.shape[0]
    tm = min(S, SEQ_TILE)

    def body(x_ref, g_ref, w_ref, o_ref, h_ref):
        xv = x_ref[...]
        h = (xv * lax.rsqrt(_mean(xv * xv) + EPS) * g_ref[...]).astype(BF)
        h_ref[...] = h
        o_ref[...] = _dot(h, w_ref[...]).astype(BF)

    return _ordered_call(
        body, deps, name=f"in_proj_{l}", grid=(S // tm,),
        in_specs=[pl.BlockSpec((tm, D), lambda i: (i, 0)),
                  pl.BlockSpec((1, D), lambda i: (0, 0)),
                  pl.BlockSpec((D, INW), lambda i: (0, 0))],
        out_specs=[pl.BlockSpec((tm, INW), lambda i: (i, 0)), pl.BlockSpec((tm, D), lambda i: (i, 0))],
        out_shape=[jax.ShapeDtypeStruct((S, INW), BF), jax.ShapeDtypeStruct((S, D), BF)],
        compiler_params=_params(1, 48),
    )(x, g, win, *deps)


def _conv_fwd_fill(tc, a_ref, b_ref, buf):
    i = pl.program_id(0)

    @pl.when(i == 0)
    def _():
        buf[0:HALO, :] = jnp.zeros((HALO, CW), F32)

    @pl.when(i > 0)
    def _():
        buf[0:HALO, :] = buf[tc:tc + HALO, :]

    buf[HALO:HALO + tc, :] = a_ref[...].astype(F32) * _sigmoid(b_ref[...].astype(F32))


def _conv_fwd_rows(r0, groups, w_ref, cb_ref, lg_ref, lb_ref, u1_ref, u_ref, buf, win):
    for c0 in range(0, CW, CONV_COLS):
        cols = slice(c0, c0 + CONV_COLS)
        acc = jnp.broadcast_to(cb_ref[:, cols], (CONV_ROWS, CONV_COLS))
        for start, length, taps in groups:
            win[0:length, :] = buf[r0 + start:r0 + start + length, cols]
            for k, at in taps:
                acc = acc + w_ref[k:k + 1, cols] * win[at:at + CONV_ROWS, :]
        u1_ref[r0:r0 + CONV_ROWS, cols] = acc
    acc = u1_ref[r0:r0 + CONV_ROWS, :]
    d = acc - _mean(acc)
    u2 = d * lax.rsqrt(_mean(d * d) + EPS) * lg_ref[...] + lb_ref[...]
    u_ref[r0:r0 + CONV_ROWS, :] = (u2 * _sigmoid(u2)).astype(BF)


def _ret_tables(S):
    half = HD // 2
    pos = jnp.arange(S, dtype=F32)
    freqs = ROPE_BASE ** (-jnp.arange(half, dtype=F32) / half)
    ang = pos[:, None] * freqs[None, :]
    cos, sin = jnp.cos(ang), jnp.sin(ang)
    cosf = jnp.concatenate([cos, cos], axis=-1)
    sinf = jnp.concatenate([-sin, sin], axis=-1)
    log_g = jnp.log(1.0 - 2.0 ** (-5.0 - jnp.arange(NH, dtype=F32)))
    idx = jnp.arange(CHUNK, dtype=F32)
    dmat = jnp.exp(log_g[:, None, None] * jnp.abs(idx[:, None] - idx[None, :]))
    qdec = jnp.broadcast_to(jnp.exp(log_g[:, None] * (idx + 1.0))[:, :, None], (NH, CHUNK, HD))
    kdec = jnp.broadcast_to(jnp.exp(log_g[:, None] * (CHUNK - 1 - idx))[:, :, None], (NH, CHUNK, HD))
    cdec = jnp.broadcast_to(jnp.exp(log_g * CHUNK)[:, None, None], (NH, HD, HD))
    return cosf, sinf, dmat, qdec, kdec, cdec


def _ret_specs(tr, tmap):
    q0 = (2 * CW) // RW
    return [pl.BlockSpec((tr, RW), lambda t: (tmap(t), q0)),
            pl.BlockSpec((tr, RW), lambda t: (tmap(t), q0 + 1)),
            pl.BlockSpec((tr, RW), lambda t: (tmap(t), q0 + 2)),
            pl.BlockSpec((tr, HD), lambda t: (tmap(t), 0)),
            pl.BlockSpec((tr, HD), lambda t: (tmap(t), 0)),
            pl.BlockSpec((NH, CHUNK, CHUNK), lambda t: (0, 0, 0)),
            pl.BlockSpec((NH, CHUNK, HD), lambda t: (0, 0, 0)),
            pl.BlockSpec((NH, CHUNK, HD), lambda t: (0, 0, 0)),
            pl.BlockSpec((NH, HD, HD), lambda t: (0, 0, 0))]


def _ret_fwd_chunk(c, q_ref, k_ref, v_ref, cos_ref, sin_ref, dm_ref, qd_ref, kd_ref, cd_ref, r_ref, st_ref, st):
    scale = HD ** -0.5
    rows = slice(c * CHUNK, (c + 1) * CHUNK)
    cs, sn = cos_ref[rows, :], sin_ref[rows, :]
    for h in range(NH):
        cols = slice(h * HD, (h + 1) * HD)
        qr = _rot(q_ref[rows, cols].astype(F32), cs, sn)
        kr = _rot(k_ref[rows, cols].astype(F32), cs, sn) * scale
        vb = v_ref[rows, cols].astype(BF)
        s = st[h]
        sb = s.astype(BF)
        st_ref[h, c] = sb
        sc = _dot_nt(qr.astype(BF), kr.astype(BF)) * dm_ref[h]
        r_ref[rows, cols] = _dot(sc.astype(BF), vb) + _dot((qr * qd_ref[h]).astype(BF), sb)
        st[h] = cd_ref[h] * s + _dot_tn((kr * kd_ref[h]).astype(BF), vb)


def mix_fwd(proj, cw, cb, lg, lb, tables, l, deps=()):
    S = proj.shape[0]
    tt = min(S, SEQ_TILE)
    cpb = tt // CHUNK
    groups = _tap_groups(HALO - (CK - 1), CK)

    def body(a_ref, b_ref, w_ref, cb_ref, lg_ref, lb_ref, q_ref, k_ref, v_ref, cos_ref, sin_ref,
             dm_ref, qd_ref, kd_ref, cd_ref, u1_ref, u_ref, r_ref, st_ref, buf, win, st):
        @pl.when(pl.program_id(0) == 0)
        def _():
            st[...] = jnp.zeros((NH, HD, HD), F32)

        _conv_fwd_fill(tt, a_ref, b_ref, buf)
        for c in range(cpb):
            _ret_fwd_chunk(c, q_ref, k_ref, v_ref, cos_ref, sin_ref, dm_ref, qd_ref, kd_ref, cd_ref,
                           r_ref, st_ref, st)
            for r0 in range(c * CHUNK, (c + 1) * CHUNK, CONV_ROWS):
                _conv_fwd_rows(r0, groups, w_ref, cb_ref, lg_ref, lb_ref, u1_ref, u_ref, buf, win)

    vec = pl.BlockSpec((1, CW), lambda t: (0, 0))
    half = pl.BlockSpec((tt, CW), lambda t: (t, 0))
    return _ordered_call(
        body, deps, name=f"mix_fwd_{l}", grid=(S // tt,),
        in_specs=[half, pl.BlockSpec((tt, CW), lambda t: (t, 1)),
                  pl.BlockSpec((CK, CW), lambda t: (0, 0)), vec, vec, vec] + _ret_specs(tt, lambda t: t),
        out_specs=[half, half, half, pl.BlockSpec((NH, cpb, HD, HD), lambda t: (0, t, 0, 0))],
        out_shape=[jax.ShapeDtypeStruct((S, CW), F32), jax.ShapeDtypeStruct((S, CW), BF),
                   jax.ShapeDtypeStruct((S, RW), F32), jax.ShapeDtypeStruct((NH, S // CHUNK, HD, HD), BF)],
        scratch_shapes=[pltpu.VMEM((tt + HALO, CW), F32), pltpu.VMEM((HALO + CONV_ROWS, CONV_COLS), F32),
                        pltpu.VMEM((NH, HD, HD), F32)],
        compiler_params=_params(1, 40),
    )(proj, proj, cw, cb, lg, lb, proj, proj, proj, *tables, *deps)


def out_proj(u, r_raw, proj, gn, wout, x, l, deps=()):
    S = x.shape[0]
    tm = min(S, SEQ_TILE)
    gate_blk = (2 * CW + 3 * RW) // RW

    def body(u_ref, r_ref, gate_ref, gn_ref, w_ref, x_ref, x2_ref, mix_ref):
        mix_ref[:, 0:CW] = u_ref[...]
        gt = gate_ref[...].astype(F32)
        sil = gt * _sigmoid(gt) * gn_ref[...]
        for h in range(NH):
            cols = slice(h * HD, (h + 1) * HD)
            rh = r_ref[:, cols]
            d = rh - _mean(rh)
            rn = d * lax.rsqrt(_mean(d * d) + EPS)
            mix_ref[:, CW + h * HD:CW + (h + 1) * HD] = (rn * sil[:, cols]).astype(BF)
        x2_ref[...] = x_ref[...] + _dot(mix_ref[...], w_ref[...])

    return _ordered_call(
        body, deps, name=f"out_proj_{l}", grid=(S // tm,),
        in_specs=[pl.BlockSpec((tm, CW), lambda i: (i, 0)),
                  pl.BlockSpec((tm, RW), lambda i: (i, 0)),
                  pl.BlockSpec((tm, RW), lambda i: (i, gate_blk)),
                  pl.BlockSpec((1, RW), lambda i: (0, 0)),
                  pl.BlockSpec((D, D), lambda i: (0, 0)),
                  pl.BlockSpec((tm, D), lambda i: (i, 0))],
        out_specs=[pl.BlockSpec((tm, D), lambda i: (i, 0)),
                   pl.BlockSpec((tm, D), lambda i: (i, 0))],
        out_shape=[jax.ShapeDtypeStruct((S, D), F32), jax.ShapeDtypeStruct((S, D), BF)],
        compiler_params=_params(1, 40),
    )(u, r_raw, proj, gn, wout, x, *deps)


def mlp_fwd(x2, g2, wgt, wut, wd, l, deps=()):
    S = x2.shape[0]
    tm, tf = min(S, MLP_ROWS), MLP_COLS

    def body(x_ref, g_ref, wg_ref, wu_ref, wd_ref, o_ref, gs_ref, us_ref, a_ref):
        xv = x_ref[...]
        h = (xv * lax.rsqrt(_mean(xv * xv) + EPS) * g_ref[...]).astype(BF)
        for c0 in range(0, FF, tf):
            gv = _dot_nt(h, wg_ref[c0:c0 + tf, :])
            uv = _dot_nt(h, wu_ref[c0:c0 + tf, :])
            gs_ref[:, c0:c0 + tf] = gv.astype(BF)
            us_ref[:, c0:c0 + tf] = uv.astype(BF)
            a_ref[:, c0:c0 + tf] = (gv * _sigmoid(gv) * uv).astype(BF)
        o_ref[...] = xv + _dot(a_ref[...], wd_ref[...])

    wspec = pl.BlockSpec((FF, D), lambda i: (0, 0), pipeline_mode=pl.Buffered(1))
    row = pl.BlockSpec((tm, D), lambda i: (i, 0))
    wide = pl.BlockSpec((tm, FF), lambda i: (i, 0))
    return _ordered_call(
        body, deps, name=f"mlp_fwd_{l}", grid=(S // tm,),
        in_specs=[row, pl.BlockSpec((1, D), lambda i: (0, 0)), wspec, wspec, wspec],
        out_specs=[row, wide, wide, wide],
        out_shape=[jax.ShapeDtypeStruct((S, D), F32)] + [jax.ShapeDtypeStruct((S, FF), BF)] * 3,
        compiler_params=_params(1, 56),
    )(x2, g2, wgt, wut, wd, *deps)


def final_loss(x, gf, tgt):
    S = x.shape[0]
    tm = min(S, SEQ_TILE)

    def body(x_ref, g_ref, t_ref, dx_ref, loss_ref, dg_ref):
        @pl.when(pl.program_id(0) == 0)
        def _():
            loss_ref[...] = jnp.zeros((8, 128), F32)
            dg_ref[...] = jnp.zeros((1, D), F32)

        xv = x_ref[...]
        r = lax.rsqrt(_mean(xv * xv) + EPS)
        xh = xv * r
        diff = xh * g_ref[...] - t_ref[...]
        loss_ref[...] += jnp.sum(jnp.sum(diff * diff, axis=-1, keepdims=True), axis=0, keepdims=True)
        dy = diff * (1.0 / D)
        dg_ref[...] += jnp.sum(dy * xh, axis=0, keepdims=True)
        dxh = dy * g_ref[...]
        dx_ref[...] = r * (dxh - xh * _mean(dxh * xh))

    return pl.pallas_call(
        body, name="final_loss", grid=(S // tm,),
        in_specs=[pl.BlockSpec((tm, D), lambda i: (i, 0)),
                  pl.BlockSpec((1, D), lambda i: (0, 0)),
                  pl.BlockSpec((tm, D), lambda i: (i, 0))],
        out_specs=[pl.BlockSpec((tm, D), lambda i: (i, 0)),
                   pl.BlockSpec((8, 128), lambda i: (0, 0)),
                   pl.BlockSpec((1, D), lambda i: (0, 0))],
        out_shape=[jax.ShapeDtypeStruct((S, D), F32), jax.ShapeDtypeStruct((8, 128), F32),
                   jax.ShapeDtypeStruct((1, D), F32)],
        compiler_params=_params(1, 40),
    )(x, gf, tgt)


def mlp_bwd(dx3, x2, g2, gs, us, wgt, wut, wd, l, deps=()):
    S = x2.shape[0]
    tm, tf = min(S, MLP_ROWS), MLP_COLS

    def body(dx_ref, x_ref, g_ref, gs_ref, us_ref, wg_ref, wu_ref, wd_ref,
             dx2_ref, dg_ref, du_ref, h_ref, dgain_ref):
        @pl.when(pl.program_id(0) == 0)
        def _():
            dgain_ref[...] = jnp.zeros((1, D), F32)

        dxv = dx_ref[...]
        dxb = dxv.astype(BF)
        for c0 in range(0, FF, tf):
            da = _dot_nt(dxb, wd_ref[c0:c0 + tf, :])
            gv = gs_ref[:, c0:c0 + tf].astype(F32)
            uv = us_ref[:, c0:c0 + tf].astype(F32)
            sg = _sigmoid(gv)
            dg_ref[:, c0:c0 + tf] = (da * uv * (sg * (1.0 + gv * (1.0 - sg)))).astype(BF)
            du_ref[:, c0:c0 + tf] = (da * (gv * sg)).astype(BF)
        dh = _dot(dg_ref[...], wg_ref[...]) + _dot(du_ref[...], wu_ref[...])
        dx2, dgain, hb = _rms_bwd(x_ref[...], g_ref[...], dh, dxv)
        dx2_ref[...] = dx2
        dgain_ref[...] += dgain
        h_ref[...] = hb

    wspec = pl.BlockSpec((FF, D), lambda i: (0, 0), pipeline_mode=pl.Buffered(1))
    row = pl.BlockSpec((tm, D), lambda i: (i, 0))
    wide = pl.BlockSpec((tm, FF), lambda i: (i, 0))
    vec = pl.BlockSpec((1, D), lambda i: (0, 0))
    return _ordered_call(
        body, deps, name=f"mlp_bwd_{l}", grid=(S // tm,),
        in_specs=[row, row, vec, wide, wide, wspec, wspec, wspec],
        out_specs=[row, wide, wide, row, vec],
        out_shape=[jax.ShapeDtypeStruct((S, D), F32), jax.ShapeDtypeStruct((S, FF), BF),
                   jax.ShapeDtypeStruct((S, FF), BF), jax.ShapeDtypeStruct((S, D), BF),
                   jax.ShapeDtypeStruct((1, D), F32)],
        compiler_params=_params(1, 56),
    )(dx3, x2, g2, gs, us, wgt, wut, wd, *deps)


def wgrad(a, b, l, name, deps=()):
    S, K = a.shape
    N = b.shape[1]
    tk = 1408 if K == FF else min(K, 1024)
    tn = min(N, 1024)
    ts = min(S, WGRAD_ROWS)
    ns = S // ts

    def body(a_ref, b_ref, o_ref, acc):
        s = pl.program_id(2)

        @pl.when(s == 0)
        def _():
            acc[...] = jnp.zeros((tk, tn), F32)

        acc[...] += _dot_tn(a_ref[...], b_ref[...].astype(BF))

        @pl.when(s == ns - 1)
        def _():
            o_ref[...] = acc[...].astype(BF)

    return _ordered_call(
        body, deps, name=f"{name}_{l}", grid=(K // tk, N // tn, ns),
        in_specs=[pl.BlockSpec((ts, tk), lambda i, j, s: (s, i)),
                  pl.BlockSpec((ts, tn), lambda i, j, s: (s, j))],
        out_specs=pl.BlockSpec((tk, tn), lambda i, j, s: (i, j)),
        out_shape=jax.ShapeDtypeStruct((K, N), BF),
        scratch_shapes=[pltpu.VMEM((tk, tn), F32)],
        compiler_params=_params(3, 48),
    )(a, b, *deps)


def out_proj_bwd(dx2, wout, r_raw, proj, gn, u1, lg, lb, l, deps=()):
    S = dx2.shape[0]
    tm = min(S, SEQ_TILE)
    gate_blk = (2 * CW + 3 * RW) // RW

    def body(dx_ref, w_ref, r_ref, gate_ref, gn_ref, u1_ref, lg_ref, lb_ref,
             dgate_ref, dr_ref, du1_ref, sums_ref):
        @pl.when(pl.program_id(0) == 0)
        def _():
            sums_ref[...] = jnp.zeros((8, CW), F32)

        dmix = _dot_nt(dx_ref[...].astype(BF), w_ref[...])
        gt = gate_ref[...].astype(F32)
        sg = _sigmoid(gt)
        sil = gt * sg
        dsil = sg * (1.0 + gt * (1.0 - sg))
        for h in range(NH):
            cols = slice(h * HD, (h + 1) * HD)
            rh = r_ref[:, cols]
            d = rh - _mean(rh)
            rs = lax.rsqrt(_mean(d * d) + EPS)
            rn = d * rs
            drr = dmix[:, CW + h * HD:CW + (h + 1) * HD]
            gnh = gn_ref[:, cols]
            sums_ref[0:1, cols] += jnp.sum(drr * rn * sil[:, cols], axis=0, keepdims=True)
            dgate_ref[:, cols] = (drr * rn * gnh * dsil[:, cols]).astype(BF)
            drn = drr * gnh * sil[:, cols]
            dr_ref[:, cols] = (rs * (drn - _mean(drn) - rn * _mean(drn * rn))).astype(BF)
        du = dmix[:, 0:CW]
        u1 = u1_ref[...]
        d = u1 - _mean(u1)
        rs = lax.rsqrt(_mean(d * d) + EPS)
        xh = d * rs
        u2 = xh * lg_ref[...] + lb_ref[...]
        sg2 = _sigmoid(u2)
        du2 = du * (sg2 * (1.0 + u2 * (1.0 - sg2)))
        sums_ref[1:2, :] += jnp.sum(du2 * xh, axis=0, keepdims=True)
        sums_ref[2:3, :] += jnp.sum(du2, axis=0, keepdims=True)
        dxh = du2 * lg_ref[...]
        du1_ref[...] = rs * (dxh - _mean(dxh) - xh * _mean(dxh * xh))

    vec = pl.BlockSpec((1, CW), lambda i: (0, 0))
    half = pl.BlockSpec((tm, CW), lambda i: (i, 0))
    return _ordered_call(
        body, deps, name=f"out_proj_bwd_{l}", grid=(S // tm,),
        in_specs=[pl.BlockSpec((tm, D), lambda i: (i, 0)),
                  pl.BlockSpec((D, D), lambda i: (0, 0)),
                  half, pl.BlockSpec((tm, RW), lambda i: (i, gate_blk)), vec, half, vec, vec],
        out_specs=[half, half, half, pl.BlockSpec((8, CW), lambda i: (0, 0))],
        out_shape=[jax.ShapeDtypeStruct((S, RW), BF), jax.ShapeDtypeStruct((S, RW), BF),
                   jax.ShapeDtypeStruct((S, CW), F32), jax.ShapeDtypeStruct((8, CW), F32)],
        compiler_params=_params(1, 40),
    )(dx2, wout, r_raw, proj, gn, u1, lg, lb, *deps)


def _conv_bwd_fill(tc, du1_ref, dwb_ref, buf, pacc):
    i = pl.program_id(0)

    @pl.when(i == 0)
    def _():
        buf[tc:tc + HALO, :] = jnp.zeros((HALO, CW), F32)
        dwb_ref[...] = jnp.zeros((CK + 1, CW), F32)
        pacc[...] = jnp.zeros((CK, 8, CW), F32)

    @pl.when(i > 0)
    def _():
        buf[tc:tc + HALO, :] = buf[0:HALO, :]

    buf[0:tc, :] = du1_ref[...]


def _conv_bwd_rows(r0, groups, pacc, a_ref, b_ref, w_ref, dab_ref, buf, win):
    for c0 in range(0, CW, CONV_COLS):
        cols = slice(c0, c0 + CONV_COLS)
        av = a_ref[r0:r0 + CONV_ROWS, cols].astype(F32)
        sgb = _sigmoid(b_ref[r0:r0 + CONV_ROWS, cols].astype(F32))
        u0 = av * sgb
        acc = jnp.zeros((CONV_ROWS, CONV_COLS), F32)
        for start, length, taps in groups:
            win[0:length, :] = buf[r0 + start:r0 + start + length, cols]
            for j, at in taps:
                sl = win[at:at + CONV_ROWS, :]
                acc = acc + w_ref[CK - 1 - j:CK - j, cols] * sl
                pacc[CK - 1 - j, :, cols] += _fold8(u0 * sl)
        dab_ref[r0:r0 + CONV_ROWS, c0:c0 + CONV_COLS] = (acc * sgb).astype(BF)
        dab_ref[r0:r0 + CONV_ROWS, CW + c0:CW + c0 + CONV_COLS] = (acc * av * sgb * (1.0 - sgb)).astype(BF)


def _conv_bwd_finish(nt, pacc, du1_ref, dwb_ref):
    dwb_ref[CK:CK + 1, :] += jnp.sum(du1_ref[...], axis=0, keepdims=True)

    @pl.when(pl.program_id(0) == nt - 1)
    def _():
        for k in range(CK):
            dwb_ref[k:k + 1, :] = jnp.sum(pacc[k], axis=0, keepdims=True)


def _ret_bwd_chunk(c, q_ref, k_ref, v_ref, cos_ref, sin_ref, dm_ref, qd_ref, kd_ref, cd_ref, dr_ref, st_ref,
                   dq_ref, dk_ref, dv_ref, gst):
    scale = HD ** -0.5
    rows = slice(c * CHUNK, (c + 1) * CHUNK)
    cs, sn = cos_ref[rows, :], sin_ref[rows, :]
    for h in range(NH):
        cols = slice(h * HD, (h + 1) * HD)
        qr = _rot(q_ref[rows, cols].astype(F32), cs, sn)
        kr = _rot(k_ref[rows, cols].astype(F32), cs, sn) * scale
        qb, kb = qr.astype(BF), kr.astype(BF)
        vb = v_ref[rows, cols].astype(BF)
        dob = dr_ref[rows, cols]
        sb = st_ref[h, c]
        gn1 = gst[h]
        gb = gn1.astype(BF)
        sc = (_dot_nt(qb, kb) * dm_ref[h]).astype(BF)
        dsc = (_dot_nt(dob, vb) * dm_ref[h]).astype(BF)
        dqr = _dot(dsc, kb) + _dot_nt(dob, sb) * qd_ref[h]
        dkr = _dot_tn(dsc, qb) + _dot_nt(vb, gb) * kd_ref[h]
        dvv = _dot_tn(sc, dob) + _dot((kr * kd_ref[h]).astype(BF), gb)
        gst[h] = cd_ref[h] * gn1 + _dot_tn((qr * qd_ref[h]).astype(BF), dob)
        dq_ref[rows, cols] = _rot_t(dqr, cs, sn).astype(BF)
        dk_ref[rows, cols] = _rot_t(dkr * scale, cs, sn).astype(BF)
        dv_ref[rows, cols] = dvv.astype(BF)


def mix_bwd(du1, dr, proj, cw, states, tables, l, deps=()):
    S = proj.shape[0]
    tt = min(S, SEQ_TILE)
    cpb = tt // CHUNK
    nt = S // tt
    groups = _tap_groups(0, CK)

    def body(du1_ref, a_ref, b_ref, w_ref, q_ref, k_ref, v_ref, cos_ref, sin_ref, dm_ref, qd_ref, kd_ref, cd_ref,
             dr_ref, st_ref, dab_ref, dwb_ref, dq_ref, dk_ref, dv_ref, buf, win, gst, pacc):
        @pl.when(pl.program_id(0) == 0)
        def _():
            gst[...] = jnp.zeros((NH, HD, HD), F32)

        _conv_bwd_fill(tt, du1_ref, dwb_ref, buf, pacc)
        for c in reversed(range(cpb)):
            _ret_bwd_chunk(c, q_ref, k_ref, v_ref, cos_ref, sin_ref, dm_ref, qd_ref, kd_ref, cd_ref, dr_ref, st_ref,
                           dq_ref, dk_ref, dv_ref, gst)
            for r0 in range(c * CHUNK, (c + 1) * CHUNK, CONV_ROWS):
                _conv_bwd_rows(r0, groups, pacc, a_ref, b_ref, w_ref, dab_ref, buf, win)
        _conv_bwd_finish(nt, pacc, du1_ref, dwb_ref)

    rev = lambda t: nt - 1 - t
    half = pl.BlockSpec((tt, CW), lambda t: (rev(t), 0))
    return _ordered_call(
        body, deps, name=f"mix_bwd_{l}", grid=(nt,),
        in_specs=[half, half, pl.BlockSpec((tt, CW), lambda t: (rev(t), 1)), pl.BlockSpec((CK, CW), lambda t: (0, 0))]
        + _ret_specs(tt, rev) + [half, pl.BlockSpec((NH, cpb, HD, HD), lambda t: (0, rev(t), 0, 0))],
        out_specs=[pl.BlockSpec((tt, 2 * CW), lambda t: (rev(t), 0)), pl.BlockSpec((CK + 1, CW), lambda t: (0, 0)),
                   half, half, half],
        out_shape=[jax.ShapeDtypeStruct((S, 2 * CW), BF), jax.ShapeDtypeStruct((CK + 1, CW), F32)]
        + [jax.ShapeDtypeStruct((S, RW), BF)] * 3,
        scratch_shapes=[pltpu.VMEM((tt + HALO, CW), F32), pltpu.VMEM((HALO + CONV_ROWS, CONV_COLS), F32),
                        pltpu.VMEM((NH, HD, HD), F32), pltpu.VMEM((CK, 8, CW), F32)],
        compiler_params=_params(1, 40),
    )(du1, proj, proj, cw, proj, proj, proj, *tables, dr, states, *deps)


def in_proj_bwd(parts, win, x, g, dx2, l, deps=()):
    S = x.shape[0]
    tm = min(S, SEQ_TILE)
    n = len(parts)

    def body(*refs):
        srcs = refs[:n]
        w_ref, x_ref, g_ref, dx2_ref, dx_ref, dgain_ref = refs[n:]

        @pl.when(pl.program_id(0) == 0)
        def _():
            dgain_ref[...] = jnp.zeros((1, D), F32)

        dh, col = None, 0
        for r in srcs:
            width = r.shape[1]
            term = _dot_nt(r[...], w_ref[:, col:col + width])
            dh = term if dh is None else dh + term
            col += width
        dx, dgain, _ = _rms_bwd(x_ref[...], g_ref[...], dh, dx2_ref[...])
        dx_ref[...] = dx
        dgain_ref[...] += dgain

    row = pl.BlockSpec((tm, D), lambda i: (i, 0))
    vec = pl.BlockSpec((1, D), lambda i: (0, 0))
    return _ordered_call(
        body, deps, name=f"in_proj_bwd_{l}", grid=(S // tm,),
        in_specs=[pl.BlockSpec((tm, p.shape[1]), lambda i: (i, 0)) for p in parts]
        + [pl.BlockSpec((D, INW), lambda i: (0, 0)), row, vec, row],
        out_specs=[row, vec],
        out_shape=[jax.ShapeDtypeStruct((S, D), F32), jax.ShapeDtypeStruct((1, D), F32)],
        compiler_params=_params(1, 48),
    )(*parts, win, x, g, dx2, *deps)


def wgrad_in(h, parts, l):
    S = h.shape[0]
    ts = min(S, SEQ_TILE)
    ns = S // ts

    def body(*refs):
        h_ref, srcs = refs[0], refs[1:1 + len(parts)]
        o_ref, acc = refs[-2], refs[-1]
        s = pl.program_id(0)

        @pl.when(s == 0)
        def _():
            acc[...] = jnp.zeros((D, INW), F32)

        hv = h_ref[...]
        col = 0
        for r in srcs:
            width = r.shape[1]
            acc[:, col:col + width] += _dot_tn(hv, r[...])
            col += width

        @pl.when(s == ns - 1)
        def _():
            o_ref[...] = acc[...].astype(BF)

    return pl.pallas_call(
        body, name=f"wgrad_in_{l}", grid=(ns,),
        in_specs=[pl.BlockSpec((ts, D), lambda s: (s, 0))]
        + [pl.BlockSpec((ts, p.shape[1]), lambda s: (s, 0)) for p in parts],
        out_specs=pl.BlockSpec((D, INW), lambda s: (0, 0)),
        out_shape=jax.ShapeDtypeStruct((D, INW), BF),
        scratch_shapes=[pltpu.VMEM((D, INW), F32)],
        compiler_params=_params(1, 48),
    )(h, *parts)


def sum_slots(recv, name, deps=()):
    _, R, C = recv.shape
    tr = 256 if R % 256 == 0 else R

    def body(r_ref, o_ref):
        acc = r_ref[0].astype(F32)
        for k in range(1, NCHIP):
            acc = acc + r_ref[k].astype(F32)
        o_ref[...] = acc

    return _ordered_call(
        body, deps, name=name, grid=(R // tr,),
        in_specs=[pl.BlockSpec((NCHIP, tr, C), lambda i: (0, i, 0))],
        out_specs=pl.BlockSpec((tr, C), lambda i: (i, 0)),
        out_shape=jax.ShapeDtypeStruct((R, C), F32),
        compiler_params=_params(1, 32),
    )(recv, *deps)


def adamw(w, ga, gb, m, v, name):
    R, C = w.shape
    tr = 256 if R % 256 == 0 else R
    c1 = 1.0 - ADAM_B1 ** ADAM_STEP
    c2 = 1.0 - ADAM_B2 ** ADAM_STEP

    def body(w_ref, ga_ref, gb_ref, m_ref, v_ref, g_out, d_out, m_out, v_out):
        g = ga_ref[...] + gb_ref[...]
        mn = ADAM_B1 * m_ref[...] + (1.0 - ADAM_B1) * g
        vn = ADAM_B2 * v_ref[...] + (1.0 - ADAM_B2) * (g * g)
        g_out[...] = g
        m_out[...] = mn
        v_out[...] = vn
        d_out[...] = -ADAM_LR * ((mn / c1) / (jnp.sqrt(vn / c2) + ADAM_EPS) + ADAM_WD * w_ref[...])

    blk = pl.BlockSpec((tr, C), lambda i: (i, 0))
    return pl.pallas_call(
        body, name=name, grid=(R // tr,),
        in_specs=[blk] * 5, out_specs=[blk] * 4,
        out_shape=[jax.ShapeDtypeStruct((R, C), F32)] * 4,
        compiler_params=_params(1, 40),
    )(w, ga, gb, m, v)


def _place():
    x, y, c = lax.axis_index("x"), lax.axis_index("y"), lax.axis_index("c")
    chips = [(1 - x, y), (x, 1 - y), (1 - x, 1 - y)]
    return x, y, c, chips


def _window(ref, axis, j, size):
    idx = [slice(None)] * len(ref.shape)
    idx[axis] = pl.ds(pl.multiple_of(j * size, 128 if axis == len(ref.shape) - 1 else 16), size)
    return ref.at[tuple(idx)]


def _hbm(a):
    return pltpu.with_memory_space_constraint(a, pltpu.HBM)


def _hbm_like(arrs):
    return [pltpu.HBM(a.shape, a.dtype) for a in arrs]


def gather_start(shards, axes, after, tag):
    n = len(shards)
    na = len(after)
    lands = []
    for s, ax in zip(shards, axes):
        shp = list(s.shape)
        shp[ax] *= NCHIP
        lands.append(lax.empty(tuple(shp), s.dtype))

    def body(*refs):
        ins, land = refs[:n], refs[n:2 * n]
        send, recv = refs[2 * n + na], refs[2 * n + na + 1]
        token = refs[-1]
        x, y, c, chips = _place()
        for a in range(n):
            for k, chip in enumerate(chips):
                pltpu.make_async_remote_copy(
                    src_ref=ins[a], dst_ref=_window(land[a], axes[a], 2 * x + y, ins[a].shape[axes[a]]),
                    send_sem=send.at[3 * a + k], recv_sem=recv.at[3 * a + k],
                    device_id=(chip[0], chip[1], c), device_id_type=MESH).start()
        token[...] = jnp.zeros_like(token)

    outs = pl.pallas_call(
        body, name=f"gather_start_{tag}",
        in_specs=[HBM_SPEC] * (2 * n) + [ANY] * na,
        out_specs=(SEM_SPEC, SEM_SPEC, *[HBM_SPEC] * (2 * n), VMEM_SPEC),
        out_shape=(pltpu.SemaphoreType.DMA((3 * n,)), pltpu.SemaphoreType.DMA((3 * n,)),
                   *_hbm_like(shards), *_hbm_like(lands), jax.ShapeDtypeStruct((8, 128), F32)),
        input_output_aliases={a: 2 + a for a in range(2 * n)},
        compiler_params=pltpu.CompilerParams(has_side_effects=DATAFLOW),
    )(*[_hbm(s) for s in shards], *[_hbm(b) for b in lands], *after)
    return (outs[0], outs[1], list(outs[2:2 + n]), list(outs[2 + n:2 + 2 * n]), list(axes)), outs[-1]


def gather_wait(groups, after, tag):
    sizes = [len(g[2]) for g in groups]
    total = sum(sizes)

    def body(*refs):
        x, y, c, chips = _place()
        stage, loc = refs[-1 - total:-1], refs[-1]
        pos = 2 * total
        off = 0
        mine = []
        for g, n in zip(groups, sizes):
            ins, land = refs[off:off + n], refs[total + off:total + off + n]
            send_ref, recv_ref = refs[pos], refs[pos + 1]
            axes = g[4]
            for a in range(n):
                fetch = pltpu.make_async_copy(ins[a], stage[off + a], loc.at[2 * (off + a)])
                fetch.start()
                put = pltpu.make_async_copy(
                    stage[off + a], _window(land[a], axes[a], 2 * x + y, ins[a].shape[axes[a]]),
                    loc.at[2 * (off + a) + 1])
                mine.append((fetch, put))
                for k, chip in enumerate(chips):
                    cp = pltpu.make_async_remote_copy(
                        src_ref=ins[a],
                        dst_ref=_window(land[a], axes[a], 2 * chip[0] + chip[1], ins[a].shape[axes[a]]),
                        send_sem=send_ref.at[3 * a + k], recv_sem=recv_ref.at[3 * a + k],
                        device_id=(chip[0], chip[1], c), device_id_type=MESH)
                    cp.wait_send()
                    cp.wait_recv()
            pos += 2
            off += n
        for fetch, put in mine:
            fetch.wait()
            put.start()
        for fetch, put in mine:
            put.wait()

    shards = [s for g in groups for s in g[2]]
    lands = [b for g in groups for b in g[3]]
    sems = [s for g in groups for s in (g[0], g[1])]
    outs = pl.pallas_call(
        body, name=f"gather_wait_{tag}",
        in_specs=[HBM_SPEC] * (2 * total) + [SEM_SPEC] * len(sems) + [ANY],
        out_specs=[HBM_SPEC] * (2 * total),
        out_shape=(*_hbm_like(shards), *_hbm_like(lands)),
        input_output_aliases={a: a for a in range(2 * total)},
        scratch_shapes=[pltpu.VMEM(s.shape, s.dtype) for s in shards] + [pltpu.SemaphoreType.DMA((2 * total,))],
        compiler_params=pltpu.CompilerParams(has_side_effects=DATAFLOW, vmem_limit_bytes=32 << 20),
    )(*shards, *lands, *sems, after)
    return list(outs[total:])


def scatter_start(grads, axes, sizes, lands, l, tag):
    n = len(grads)

    def body(*refs):
        ins, land = refs[:n], refs[n:2 * n]
        send, recv = refs[2 * n], refs[2 * n + 1]
        token = refs[2 * n + 2 + 2 * n]
        stage, loc = refs[-1 - n:-1], refs[-1]
        x, y, c, chips = _place()
        me = 2 * x + y
        fetches = [pltpu.make_async_copy(_window(ins[a], axes[a], me, sizes[a]), stage[a], loc.at[2 * a])
                   for a in range(n)]
        for cp in fetches:
            cp.start()
        for a in range(n):
            for k, chip in enumerate(chips):
                pltpu.make_async_remote_copy(
                    src_ref=_window(ins[a], axes[a], 2 * chip[0] + chip[1], sizes[a]), dst_ref=land[a].at[me, l],
                    send_sem=send.at[3 * a + k], recv_sem=recv.at[3 * a + k],
                    device_id=(chip[0], chip[1], c), device_id_type=MESH).start()
        puts = [pltpu.make_async_copy(stage[a], land[a].at[me, l], loc.at[2 * a + 1]) for a in range(n)]
        for fetch, put in zip(fetches, puts):
            fetch.wait()
            put.start()
        for put in puts:
            put.wait()
        token[...] = jnp.zeros_like(token)

    outs = pl.pallas_call(
        body, name=f"scatter_start_{tag}",
        in_specs=[HBM_SPEC] * (2 * n),
        out_specs=(SEM_SPEC, SEM_SPEC, *[HBM_SPEC] * (2 * n), VMEM_SPEC),
        out_shape=(pltpu.SemaphoreType.DMA((3 * n,)), pltpu.SemaphoreType.DMA((3 * n,)),
                   *_hbm_like(grads), *_hbm_like(lands), jax.ShapeDtypeStruct((8, 128), F32)),
        input_output_aliases={a: 2 + a for a in range(2 * n)},
        scratch_shapes=[pltpu.VMEM(b.shape[2:], b.dtype) for b in lands] + [pltpu.SemaphoreType.DMA((2 * n,))],
        compiler_params=pltpu.CompilerParams(has_side_effects=DATAFLOW, vmem_limit_bytes=32 << 20),
    )(*[_hbm(g) for g in grads], *[_hbm(b) for b in lands])
    group = (outs[0], outs[1], list(outs[2:2 + n]), list(axes), list(sizes), l)
    return group, list(outs[2 + n:2 + 2 * n]), outs[-1]


def scatter_wait(groups, lands, which, after):
    nl = len(lands)

    def body(*refs):
        land = refs[:nl]
        x, y, c, chips = _place()
        pos = nl
        for g, wh in zip(groups, which):
            n = len(g[2])
            ins = refs[pos:pos + n]
            send_ref, recv_ref = refs[pos + n], refs[pos + n + 1]
            axes, sizes, l = g[3], g[4], g[5]
            for a in range(n):
                for k, chip in enumerate(chips):
                    jp = 2 * chip[0] + chip[1]
                    cp = pltpu.make_async_remote_copy(
                        src_ref=_window(ins[a], axes[a], jp, sizes[a]), dst_ref=land[wh[a]].at[jp, l],
                        send_sem=send_ref.at[3 * a + k], recv_sem=recv_ref.at[3 * a + k],
                        device_id=(chip[0], chip[1], c), device_id_type=MESH)
                    cp.wait_send()
                    cp.wait_recv()
            pos += n + 2

    operands = list(lands)
    specs = [HBM_SPEC] * nl
    for g in groups:
        operands += list(g[2]) + [g[0], g[1]]
        specs += [HBM_SPEC] * len(g[2]) + [SEM_SPEC, SEM_SPEC]
    outs = pl.pallas_call(
        body, name="scatter_wait", in_specs=specs + [ANY], out_specs=[HBM_SPEC] * nl,
        out_shape=tuple(_hbm_like(lands)),
        input_output_aliases={a: a for a in range(nl)},
        compiler_params=pltpu.CompilerParams(has_side_effects=DATAFLOW),
    )(*operands, after)
    return list(outs)


def swap_start(part, tag):
    def body(p_ref, land_ref, send, recv, p_thru, land_thru, token):
        x, y, c, _ = _place()
        pltpu.make_async_remote_copy(src_ref=p_ref, dst_ref=land_ref, send_sem=send, recv_sem=recv,
                                     device_id=(x, y, 1 - c), device_id_type=MESH).start()
        token[...] = jnp.zeros_like(token)

    outs = pl.pallas_call(
        body, name=f"swap_start_{tag}", in_specs=[HBM_SPEC, HBM_SPEC],
        out_specs=(SEM_SPEC, SEM_SPEC, HBM_SPEC, HBM_SPEC, VMEM_SPEC),
        out_shape=(pltpu.SemaphoreType.DMA(()), pltpu.SemaphoreType.DMA(()), *_hbm_like([part, part]),
                   jax.ShapeDtypeStruct((8, 128), F32)),
        input_output_aliases={0: 2, 1: 3},
        compiler_params=pltpu.CompilerParams(has_side_effects=DATAFLOW),
    )(_hbm(part), _hbm(lax.empty(part.shape, part.dtype)))
    return tuple(outs[:4]), outs[4]


def swap_wait(group, after, tag):
    send, recv, part, land = group

    def body(p_ref, land_ref, send_ref, recv_ref, after_ref, p_out, land_out):
        x, y, c, _ = _place()
        cp = pltpu.make_async_remote_copy(src_ref=p_ref, dst_ref=land_ref, send_sem=send_ref, recv_sem=recv_ref,
                                          device_id=(x, y, 1 - c), device_id_type=MESH)
        cp.wait_send()
        cp.wait_recv()

    outs = pl.pallas_call(
        body, name=f"swap_wait_{tag}", in_specs=[HBM_SPEC, HBM_SPEC, SEM_SPEC, SEM_SPEC, ANY],
        out_specs=[HBM_SPEC, HBM_SPEC], out_shape=tuple(_hbm_like([part, land])),
        input_output_aliases={0: 0, 1: 1},
        compiler_params=pltpu.CompilerParams(has_side_effects=DATAFLOW),
    )(part, land, send, recv, after)
    return outs[0], outs[1]


def small_allreduce(p):
    R, C = p.shape
    ndev = 8

    def body(p_ref, o_ref, buf, send, recv):
        x, y, c, _ = _place()
        me = 4 * x + 2 * y + c
        buf[me] = p_ref[...]

        def peer(d):
            px = 1 - x if d & 4 else x
            py = 1 - y if d & 2 else y
            pc = 1 - c if d & 1 else c
            return px, py, pc

        def copy(d, slot):
            return pltpu.make_async_remote_copy(src_ref=p_ref, dst_ref=buf.at[slot], send_sem=send.at[d - 1],
                                                recv_sem=recv.at[d - 1], device_id=peer(d), device_id_type=MESH)

        sends = [copy(d, me) for d in range(1, ndev)]
        for cp in sends:
            cp.start()
        for d in range(1, ndev):
            px, py, pc = peer(d)
            copy(d, 4 * px + 2 * py + pc).wait_recv()
        for cp in sends:
            cp.wait_send()
        acc = buf[0]
        for k in range(1, ndev):
            acc = acc + buf[k]
        o_ref[...] = acc

    return pl.pallas_call(
        body, name="small_allreduce", in_specs=[VMEM_SPEC], out_specs=VMEM_SPEC,
        out_shape=jax.ShapeDtypeStruct((R, C), F32),
        scratch_shapes=[pltpu.VMEM((ndev, R, C), F32), pltpu.SemaphoreType.DMA((ndev - 1,)),
                        pltpu.SemaphoreType.DMA((ndev - 1,))],
        compiler_params=pltpu.CompilerParams(vmem_limit_bytes=32 << 20),
    )(p)


def kernel(x, norm1_g, w_in, conv_w, conv_b, conv_ln_g, conv_ln_b, ret_gn_g, w_out, norm2_g, w_gate, w_up, w_down, final_g, loss_target, m_norm1_g, m_w_in, m_conv_w, m_conv_b, m_conv_ln_g, m_conv_ln_b, m_ret_gn_g, m_w_out, m_norm2_g, m_w_gate, m_w_up, m_w_down, m_final_g, v_norm1_g, v_w_in, v_conv_w, v_conv_b, v_conv_ln_g, v_conv_ln_b, v_ret_gn_g, v_w_out, v_norm2_g, v_w_gate, v_w_up, v_w_down, v_final_g):
    S = x.shape[1]
    xs = x.reshape(S, D)
    tgt = loss_target.reshape(S, D)
    fsh = FF // NCHIP

    def shards_of(l):
        return [w_in[l].astype(BF), w_out[l].astype(BF), w_gate[l].T.astype(BF), w_up[l].T.astype(BF),
                w_down[l].astype(BF), conv_w[l]]

    gather_axes = [1, 0, 0, 0, 0, 1]
    shard_cache = [shards_of(l) for l in range(L)]
    tables = _ret_tables(S)
    row = lambda a, l: a[l].reshape(1, -1)
    groups = {}
    weights = [dict() for _ in range(L)]

    def begin(l, which, after):
        group, token = gather_start([shard_cache[l][i] for i in which], [gather_axes[i] for i in which],
                                    after, f"{l}_{which[0]}")
        groups[(l, which[0])] = (group, which)
        return token

    def finish(l, firsts, after, tag):
        gs = [groups[(l, f)] for f in firsts]
        outs = gather_wait([g for g, _ in gs], after, f"{l}_{tag}")
        k = 0
        for _, which in gs:
            for i in which:
                weights[l][i] = outs[k]
                k += 1

    first = begin(0, [0], [])
    second = begin(0, [1, 5], [first])
    after = begin(0, [2, 3, 4], [second])
    saved = []
    xc = xs
    for l in range(L):
        if l == 0:
            finish(l, [0], after, "a")
        else:
            finish(l, [0], after, "all")
        win = weights[l][0]
        proj, h1 = in_proj(xc, row(norm1_g, l), win, l)
        if l == 0:
            finish(l, [1], proj, "b")
        wout, cw = weights[l][1], weights[l][5]
        ahead = [begin(l + 1, [0, 1, 2, 3, 4, 5], [proj])] if l + 1 < L else []
        u1, u, r_raw, states = mix_fwd(proj, cw, row(conv_b, l), row(conv_ln_g, l), row(conv_ln_b, l), tables, l, ahead)
        x2, mixed = out_proj(u, r_raw, proj, row(ret_gn_g, l), wout, xc, l)
        if l == 0:
            finish(l, [2], x2, "c")
        wgt, wut, wd = weights[l][2], weights[l][3], weights[l][4]
        x3, gs, us, act = mlp_fwd(x2, row(norm2_g, l), wgt, wut, wd, l)
        saved.append((xc, proj, h1, u1, r_raw, states, mixed, x2, gs, us, act))
        xc = x3
        after = x3

    dx, loss_acc, d_final = final_loss(xc, final_g.reshape(1, D), tgt)
    loss = lax.psum(loss_acc[0, 0] * (0.5 / D), ("x", "y", "c"))

    scatter_axes = [1, 0, 0, 0, 0]
    scatter_sizes = [INW // NCHIP, D // NCHIP, fsh, fsh, fsh]
    lands = [lax.empty((NCHIP, L, D, INW // NCHIP), BF), lax.empty((NCHIP, L, D // NCHIP, D), BF),
             lax.empty((NCHIP, L, fsh, D), BF), lax.empty((NCHIP, L, fsh, D), BF), lax.empty((NCHIP, L, fsh, D), BF)]
    sent, sent_which = [], []

    def send_grads(gs, which, l):
        group, new_lands, token = scatter_start(gs, [scatter_axes[a] for a in which], [scatter_sizes[a] for a in which],
                                                [lands[a] for a in which], l, f"{l}_{which[0]}")
        for a, b in zip(which, new_lands):
            lands[a] = b
        sent.append(group)
        sent_which.append(list(which))
        return [token]

    small = [None] * L
    for l in reversed(range(L)):
        xin, proj, h1, u1, r_raw, states, mixed, x2, gs, us, act = saved[l]
        win, wout, wgt, wut, wd, cw = (weights[l][i] for i in range(6))
        dx2, dgs, dus, h2, d_n2 = mlp_bwd(dx, x2, row(norm2_g, l), gs, us, wgt, wut, wd, l)
        g_wd = wgrad(act, dx, l, "wgrad_down")
        g_wgt = wgrad(dgs, h2, l, "wgrad_gate")
        g_wut = wgrad(dus, h2, l, "wgrad_up")
        dgate, dr, du1, sums = out_proj_bwd(dx2, wout, r_raw, proj, row(ret_gn_g, l), u1,
                                            row(conv_ln_g, l), row(conv_ln_b, l), l,
                                            send_grads([g_wd, g_wgt, g_wut], [4, 2, 3], l))
        g_wout = wgrad(mixed, dx2, l, "wgrad_out")
        dab, dwb, dq, dk, dv = mix_bwd(du1, dr, proj, cw, states, tables, l)
        dproj = [dab, dq, dk, dv, dgate]
        g_win = wgrad_in(h1, dproj, l)
        dx, d_n1 = in_proj_bwd(dproj, win, xin, row(norm1_g, l), dx2, l, send_grads([g_wout, g_win], [1, 0], l))
        small[l] = jnp.concatenate([dwb, sums, d_n1.reshape(2, CW), d_n2.reshape(2, CW)], axis=0)
    grad_x = dx.reshape(1, S, D)

    per = CK + 1 + 8 + 4
    packed = jnp.concatenate(small + [d_final.reshape(2, CW), jnp.zeros((6, CW), F32)], axis=0)
    tot = small_allreduce(packed)
    lay = tot[:L * per].reshape(L, per, CW)
    g_conv_w_full = lay[:, 0:CK, :]
    j = 2 * lax.axis_index("x") + lax.axis_index("y")
    g_conv_w = lax.dynamic_slice_in_dim(g_conv_w_full, j * (CW // NCHIP), CW // NCHIP, axis=2)
    g_small = {
        "conv_b": lay[:, CK, :], "ret_gn_g": lay[:, CK + 1, :], "conv_ln_g": lay[:, CK + 2, :],
        "conv_ln_b": lay[:, CK + 3, :], "norm1_g": lay[:, CK + 9:CK + 11, :].reshape(L, D),
        "norm2_g": lay[:, CK + 11:CK + 13, :].reshape(L, D), "final_g": tot[L * per:L * per + 2].reshape(D),
    }

    recv = scatter_wait(sent, lands, sent_which, tot)
    shard_shapes = [(L * D, INW // NCHIP), (L * D // NCHIP, D), (L * fsh, D), (L * fsh, D), (L * fsh, D)]
    names = ["w_in", "w_out", "w_gate", "w_up", "w_down"]
    swaps, token = [], []
    for r, shp, nm in zip(recv, shard_shapes, names):
        group, tk = swap_start(sum_slots(r.reshape((NCHIP,) + shp), f"sum_{nm}", token), nm)
        swaps.append(group)
        token = [tk]
    parts, theirs = zip(*[swap_wait(group, token[0], nm) for group, nm in zip(swaps, names)])

    def hidden_major(a):
        return jnp.swapaxes(a, 1, 2).reshape(L * fsh, D)

    big = {}
    wmv = {"w_in": (w_in, m_w_in, v_w_in), "w_out": (w_out, m_w_out, v_w_out),
           "w_gate": (w_gate, m_w_gate, v_w_gate), "w_up": (w_up, m_w_up, v_w_up),
           "w_down": (w_down, m_w_down, v_w_down)}
    for nm, mine, other in zip(names, parts, theirs):
        w, m, v = wmv[nm]
        if nm in ("w_gate", "w_up"):
            outs = adamw(hidden_major(w), mine, other, hidden_major(m), hidden_major(v), f"adamw_{nm}")
            big[nm] = [jnp.swapaxes(o.reshape(L, fsh, D), 1, 2) for o in outs]
        else:
            shp2 = (w.shape[0] * w.shape[1], w.shape[2])
            outs = adamw(w.reshape(shp2), mine, other, m.reshape(shp2), v.reshape(shp2), f"adamw_{nm}")
            big[nm] = [o.reshape(w.shape) for o in outs]

    cshape = (L * CK, CW // NCHIP)
    zc = jnp.zeros(cshape, F32)
    big["conv_w"] = [o.reshape(conv_w.shape) for o in adamw(
        conv_w.reshape(cshape), g_conv_w.reshape(cshape), zc, m_conv_w.reshape(cshape),
        v_conv_w.reshape(cshape), "adamw_conv_w")]
    vec_names = ["norm1_g", "conv_b", "conv_ln_g", "conv_ln_b", "ret_gn_g", "norm2_g", "final_g"]
    vec_w = {"norm1_g": (norm1_g, m_norm1_g, v_norm1_g), "conv_b": (conv_b, m_conv_b, v_conv_b),
             "conv_ln_g": (conv_ln_g, m_conv_ln_g, v_conv_ln_g), "conv_ln_b": (conv_ln_b, m_conv_ln_b, v_conv_ln_b),
             "ret_gn_g": (ret_gn_g, m_ret_gn_g, v_ret_gn_g), "norm2_g": (norm2_g, m_norm2_g, v_norm2_g),
             "final_g": (final_g, m_final_g, v_final_g)}
    cat = lambda arrs: jnp.concatenate([a.reshape(-1, CW) for a in arrs], axis=0)
    vw = cat([vec_w[nm][0] for nm in vec_names])
    vm = cat([vec_w[nm][1] for nm in vec_names])
    vv = cat([vec_w[nm][2] for nm in vec_names])
    vg = cat([g_small[nm] for nm in vec_names])
    vouts = adamw(vw, vg, jnp.zeros_like(vg), vm, vv, "adamw_vectors")
    off = 0
    for nm in vec_names:
        w = vec_w[nm][0]
        nrow = w.size // CW
        big[nm] = [o[off:off + nrow].reshape(w.shape) for o in vouts]
        off += nrow

    order = ["norm1_g", "w_in", "conv_w", "conv_b", "conv_ln_g", "conv_ln_b", "ret_gn_g", "w_out", "norm2_g",
             "w_gate", "w_up", "w_down", "final_g"]
    return (loss, grad_x, *[big[nm][0] for nm in order], *[big[nm][1] for nm in order],
            *[big[nm][2] for nm in order], *[big[nm][3] for nm in order])
```
